```python
import math, functools
import jax, jax.numpy as jnp
from jax import lax
import numpy as np

D_MODEL = 1024
BATCH = 8
SEQ = 2048
DEPTH = 1
DEC_BATCH = 32
DEC_SEQ = 8
PAST_LEN = 16384
PAGE_SIZE = 128

GLA_HEADS = 4
GLA_DK = 64
GLA_DV = 128
GLA_GATE_RANK = 16
GLA_GATE_NORM = 16.0
GLA_CHUNK = 32
MLA_HEADS = 8
MLA_Q_LORA = 256
MLA_KV_LORA = 128
MLA_NOPE = 64
MLA_ROPE = 32
MLA_V = 64
MLA_SCALE = (MLA_NOPE + MLA_ROPE) ** -0.5
ROPE_THETA = 10000.0
Q_BLOCK = 128
D_FF = 2816
CONV_W = 3
PLE_DIM = 256
EPS = 1e-6
NEG = -1e30

IN_SIZES = (GLA_HEADS * GLA_DK, GLA_HEADS * GLA_DK, GLA_HEADS * GLA_DV, GLA_HEADS * GLA_DV,
            GLA_GATE_RANK, MLA_Q_LORA, MLA_KV_LORA, MLA_ROPE)
IN_COLS = sum(IN_SIZES)
MIX_OUT = GLA_HEADS * GLA_DV + MLA_HEADS * MLA_V

kernel_name = "hymba_gla_mla_convffn_ple_step"


def rmsnorm(x, g):
    xf = x.astype(jnp.float32)
    y = xf * lax.rsqrt(jnp.mean(xf * xf, axis=-1, keepdims=True) + EPS)
    return (y * g.astype(jnp.float32)).astype(x.dtype)


def rope(x, pos):
    half = MLA_ROPE // 2
    inv = ROPE_THETA ** (-jnp.arange(half, dtype=jnp.float32) * 2.0 / MLA_ROPE)
    ang = pos.astype(jnp.float32)[:, None] * inv[None, :]
    shape = (1, pos.shape[0]) + (1,) * (x.ndim - 3) + (half,)
    cos = jnp.cos(ang).reshape(shape).astype(x.dtype)
    sin = jnp.sin(ang).reshape(shape).astype(x.dtype)
    x1, x2 = x[..., :half], x[..., half:]
    return jnp.concatenate([x1 * cos - x2 * sin, x2 * cos + x1 * sin], axis=-1)


def gla_chunked(q, k, v, log_a, s0):
    B, T, H, DK = q.shape
    DV = v.shape[-1]
    L = math.gcd(T, GLA_CHUNK)
    N = T // L
    f32 = jnp.float32
    q = q.astype(f32).reshape(B, N, L, H, DK)
    k = k.astype(f32).reshape(B, N, L, H, DK)
    v = v.astype(f32).reshape(B, N, L, H, DV)
    b = jnp.cumsum(log_a.astype(f32).reshape(B, N, L, H, DK), axis=2)
    b_last = b[:, :, -1:]
    q_dec = q * jnp.exp(b)
    k_inv = k * jnp.exp(-b)
    k_end = k * jnp.exp(b_last - b)
    causal = jnp.tril(jnp.ones((L, L), dtype=bool))
    A = jnp.einsum('bnthd,bnshd->bnhts', q_dec, k_inv)
    A = jnp.where(causal, A, 0.0)
    o_intra = jnp.einsum('bnhts,bnshv->bnthv', A, v)

    def step(S, xs):
        qd, ke, vv, bl = xs
        o = jnp.einsum('bthd,bhdv->bthv', qd, S)
        S = jnp.exp(bl)[..., None] * S + jnp.einsum('bshd,bshv->bhdv', ke, vv)
        return S, o

    xs = (q_dec.swapaxes(0, 1), k_end.swapaxes(0, 1), v.swapaxes(0, 1),
          b_last[:, :, 0].swapaxes(0, 1))
    S, o_inter = lax.scan(step, s0.astype(f32), xs)
    o = o_intra + o_inter.swapaxes(0, 1)
    return o.reshape(B, T, H, DV), S


def latent_to_kv(ckv, krope, w_kvup, g_kn):
    kv = (ckv @ w_kvup).reshape(ckv.shape[:-1] + (MLA_HEADS, MLA_NOPE + MLA_V))
    k_nope = rmsnorm(kv[..., :MLA_NOPE], g_kn)
    v = kv[..., MLA_NOPE:]
    kr = jnp.broadcast_to(krope[..., None, :], krope.shape[:-1] + (MLA_HEADS, MLA_ROPE))
    k = jnp.concatenate([k_nope, kr.astype(k_nope.dtype)], axis=-1)
    return k, v


def prompt_attention(q, k, v):
    B, S, H, DQK = q.shape
    nb = S // Q_BLOCK
    qb = q.reshape(B, nb, Q_BLOCK, H, DQK).transpose(1, 0, 2, 3, 4)
    starts = jnp.arange(nb) * Q_BLOCK
    kpos = jnp.arange(S)

    def one(args):
        qi, start = args
        s = jnp.einsum('bqhd,bkhd->bhqk', qi, k, preferred_element_type=jnp.float32) * MLA_SCALE
        qpos = start + jnp.arange(Q_BLOCK)
        s = jnp.where(kpos[None, :] <= qpos[:, None], s, NEG)
        pr = jax.nn.softmax(s, axis=-1)
        return jnp.einsum('bhqk,bkhd->bqhd', pr.astype(v.dtype), v)

    out = lax.map(one, (qb, starts))
    return out.transpose(1, 0, 2, 3, 4).reshape(B, S, H * MLA_V)


def block_stats(q, k, v, mask):
    s = jnp.einsum('bthd,bkhd->bhtk', q, k, preferred_element_type=jnp.float32) * MLA_SCALE
    if mask is not None:
        s = jnp.where(mask, s, NEG)
    m = jnp.max(s, axis=-1)
    p = jnp.exp(s - m[..., None])
    l = jnp.sum(p, axis=-1)
    acc = jnp.einsum('bhtk,bkhd->bhtd', p, v.astype(jnp.float32))
    return m, l, acc


def combine(a, b):
    m1, l1, acc1 = a
    m2, l2, acc2 = b
    m = jnp.maximum(m1, m2)
    c1 = jnp.exp(m1 - m)
    c2 = jnp.exp(m2 - m)
    return m, l1 * c1 + l2 * c2, acc1 * c1[..., None] + acc2 * c2[..., None]


def sample_attention(q, k_new, v_new, pool_ckv, pool_krope, page_table, w_kvup, g_kn):
    Bd, T = q.shape[0], q.shape[1]
    causal = jnp.arange(T)[None, :] <= jnp.arange(T)[:, None]
    stats = block_stats(q, k_new, v_new, causal)

    def body(carry, phys):
        c = pool_ckv[phys]
        r = pool_krope[phys]
        k, v = latent_to_kv(c, r, w_kvup, g_kn)
        return combine(carry, block_stats(q, k, v, None)), None

    (m, l, acc), _ = lax.scan(body, stats, page_table.T)
    out = acc / l[..., None]
    return out.transpose(0, 2, 1, 3).reshape(Bd, T, MLA_HEADS * MLA_V).astype(q.dtype)


def conv_ffn(n, prev, lw):
    a = n @ lw['ffn_w_gate']
    c = n @ lw['ffn_w_up']
    T = a.shape[1]
    full = jnp.concatenate([prev.astype(a.dtype), a], axis=1)
    w = lw['ffn_conv_w']
    conv = lw['ffn_conv_b'] + sum(w[j] * full[:, j:j + T] for j in range(CONV_W))
    y = (jax.nn.silu(conv) * c) @ lw['ffn_w_down']
    return y, full[:, full.shape[1] - (CONV_W - 1):]


def layer_forward(h, p, pos, gla_s0, conv_prev, attend, lw):
    B, T, _ = h.shape
    n = rmsnorm(h, lw['g_mix'])
    u = n @ lw['w_in']
    split_points = [int(s) for s in np.cumsum(IN_SIZES)[:-1]]
    gq, gk, gv, gg, ga, mq, mkv, mkr = jnp.split(u, split_points, axis=-1)
    q = gq.reshape(B, T, GLA_HEADS, GLA_DK) * (GLA_DK ** -0.5)
    k = gk.reshape(B, T, GLA_HEADS, GLA_DK)
    v = gv.reshape(B, T, GLA_HEADS, GLA_DV)
    z = (ga @ lw['gla_w_a2'] + lw['gla_b_a']).astype(jnp.float32)
    log_a = (jax.nn.log_sigmoid(z) / GLA_GATE_NORM).reshape(B, T, GLA_HEADS, GLA_DK)
    o_gla, s_new = gla_chunked(q, k, v, log_a, gla_s0)
    o_gla = rmsnorm(o_gla.astype(h.dtype), lw['gla_g_out']).reshape(B, T, -1) * jax.nn.silu(gg)
    cq = rmsnorm(mq, lw['mla_g_qa'])
    qf = (cq @ lw['mla_w_qup']).reshape(B, T, MLA_HEADS, MLA_NOPE + MLA_ROPE)
    q_nope = rmsnorm(qf[..., :MLA_NOPE], lw['mla_g_qn'])
    q_rope = rope(rmsnorm(qf[..., MLA_NOPE:], lw['mla_g_qr']), pos)
    q_m = jnp.concatenate([q_nope, q_rope], axis=-1)
    ckv = rmsnorm(mkv, lw['mla_g_kva'])
    krope = rope(rmsnorm(mkr, lw['mla_g_kr']), pos)
    k_m, v_m = latent_to_kv(ckv, krope, lw['mla_w_kvup'], lw['mla_g_kn'])
    o_mla = attend(q_m, k_m, v_m).astype(h.dtype)
    h = h + jnp.concatenate([o_gla, o_mla], axis=-1) @ lw['w_o']
    y, conv_new = conv_ffn(rmsnorm(h, lw['g_ffn']), conv_prev, lw)
    h = h + y
    h = h + (p.astype(h.dtype) @ lw['ple_w_proj']) * jax.nn.sigmoid(rmsnorm(h, lw['g_ple']) @ lw['ple_w_gate'])
    return h, ckv, krope, s_new, conv_new


def setup_inputs(seed: int = 0) -> dict:
    key = jax.random.key(seed)
    ks = jax.random.split(key, 40)
    f32 = jnp.float32
    n_pages = PAST_LEN // PAGE_SIZE
    n_phys = (DEC_BATCH * n_pages * 5) // 4

    def nrm(k, shape, scale):
        return jax.random.normal(k, shape, f32) * scale

    def gain(k, n):
        return 1.0 + 0.05 * jax.random.normal(k, (DEPTH, n), f32)

    page_table = jax.random.permutation(ks[0], n_phys)[:DEC_BATCH * n_pages]
    page_table = page_table.reshape(DEC_BATCH, n_pages).astype(jnp.int32)
    return {
        'x_prompt': nrm(ks[1], (BATCH, SEQ, D_MODEL), 1.0),
        'x_sample': nrm(ks[2], (DEC_BATCH, DEC_SEQ, D_MODEL), 1.0),
        'cache_ckv': nrm(ks[3], (DEPTH, n_phys, PAGE_SIZE, MLA_KV_LORA), 1.0),
        'cache_krope': nrm(ks[4], (DEPTH, n_phys, PAGE_SIZE, MLA_ROPE), 1.0),
        'state_gla': nrm(ks[5], (DEPTH, DEC_BATCH, GLA_HEADS, GLA_DK, GLA_DV), 0.3),
        'state_conv': nrm(ks[6], (DEPTH, DEC_BATCH, CONV_W - 1, D_FF), 1.0),
        'page_table': page_table,
        'p_prompt': nrm(ks[7], (DEPTH, BATCH, SEQ, PLE_DIM), 1.0),
        'p_sample': nrm(ks[8], (DEPTH, DEC_BATCH, DEC_SEQ, PLE_DIM), 1.0),
        'g_mix': gain(ks[9], D_MODEL),
        'w_in': nrm(ks[10], (DEPTH, D_MODEL, IN_COLS), D_MODEL ** -0.5),
        'gla_w_a2': nrm(ks[11], (DEPTH, GLA_GATE_RANK, GLA_HEADS * GLA_DK), GLA_GATE_RANK ** -0.5),
        'gla_b_a': nrm(ks[12], (DEPTH, GLA_HEADS * GLA_DK), 0.5),
        'gla_g_out': gain(ks[13], GLA_DV),
        'mla_g_qa': gain(ks[14], MLA_Q_LORA),
        'mla_w_qup': nrm(ks[15], (DEPTH, MLA_Q_LORA, MLA_HEADS * (MLA_NOPE + MLA_ROPE)), MLA_Q_LORA ** -0.5),
        'mla_g_qn': gain(ks[16], MLA_NOPE),
        'mla_g_qr': gain(ks[17], MLA_ROPE),
        'mla_g_kva': gain(ks[18], MLA_KV_LORA),
        'mla_g_kr': gain(ks[19], MLA_ROPE),
        'mla_w_kvup': nrm(ks[20], (DEPTH, MLA_KV_LORA, MLA_HEADS * (MLA_NOPE + MLA_V)), MLA_KV_LORA ** -0.5),
        'mla_g_kn': gain(ks[21], MLA_NOPE),
        'w_o': nrm(ks[22], (DEPTH, MIX_OUT, D_MODEL), MIX_OUT ** -0.5),
        'g_ffn': gain(ks[23], D_MODEL),
        'ffn_w_gate': nrm(ks[24], (DEPTH, D_MODEL, D_FF), D_MODEL ** -0.5),
        'ffn_w_up': nrm(ks[25], (DEPTH, D_MODEL, D_FF), D_MODEL ** -0.5),
        'ffn_conv_w': nrm(ks[26], (DEPTH, CONV_W, D_FF), CONV_W ** -0.5),
        'ffn_conv_b': nrm(ks[27], (DEPTH, D_FF), 0.02),
        'ffn_w_down': nrm(ks[28], (DEPTH, D_FF, D_MODEL), D_FF ** -0.5),
        'g_ple': gain(ks[29], D_MODEL),
        'ple_w_gate': nrm(ks[30], (DEPTH, D_MODEL, D_MODEL), D_MODEL ** -0.5),
        'ple_w_proj': nrm(ks[31], (DEPTH, PLE_DIM, D_MODEL), PLE_DIM ** -0.5),
    }


def reference(x_prompt, x_sample, cache_ckv, cache_krope, state_gla, state_conv, page_table,
              p_prompt, p_sample, g_mix, w_in, gla_w_a2, gla_b_a, gla_g_out, mla_g_qa, mla_w_qup,
              mla_g_qn, mla_g_qr, mla_g_kva, mla_g_kr, mla_w_kvup, mla_g_kn, w_o, g_ffn,
              ffn_w_gate, ffn_w_up, ffn_conv_w, ffn_conv_b, ffn_w_down, g_ple, ple_w_gate,
              ple_w_proj):
    B, S = x_prompt.shape[0], x_prompt.shape[1]
    pos_p = jnp.arange(S)
    pos_s = PAST_LEN + jnp.arange(x_sample.shape[1])
    y_p, y_s = x_prompt, x_sample
    ckv_p, kr_p, gla_p, conv_p = [], [], [], []
    ckv_s, kr_s, gla_s, conv_s = [], [], [], []
    for i in range(DEPTH):
        lw = dict(g_mix=g_mix[i], w_in=w_in[i], gla_w_a2=gla_w_a2[i], gla_b_a=gla_b_a[i],
                  gla_g_out=gla_g_out[i], mla_g_qa=mla_g_qa[i], mla_w_qup=mla_w_qup[i],
                  mla_g_qn=mla_g_qn[i], mla_g_qr=mla_g_qr[i], mla_g_kva=mla_g_kva[i],
                  mla_g_kr=mla_g_kr[i], mla_w_kvup=mla_w_kvup[i], mla_g_kn=mla_g_kn[i],
                  w_o=w_o[i], g_ffn=g_ffn[i], ffn_w_gate=ffn_w_gate[i], ffn_w_up=ffn_w_up[i],
                  ffn_conv_w=ffn_conv_w[i], ffn_conv_b=ffn_conv_b[i], ffn_w_down=ffn_w_down[i],
                  g_ple=g_ple[i], ple_w_gate=ple_w_gate[i], ple_w_proj=ple_w_proj[i])
        s0 = jnp.zeros((B, GLA_HEADS, GLA_DK, GLA_DV), jnp.float32)
        c0 = jnp.zeros((B, CONV_W - 1, D_FF), x_prompt.dtype)
        y_p, ckv, kr, s_new, c_new = layer_forward(y_p, p_prompt[i], pos_p, s0, c0,
                                                   prompt_attention, lw)
        ckv_p.append(ckv)
        kr_p.append(kr)
        gla_p.append(s_new.astype(x_prompt.dtype))
        conv_p.append(c_new)
        attend_s = functools.partial(sample_attention, pool_ckv=cache_ckv[i],
                                     pool_krope=cache_krope[i], page_table=page_table,
                                     w_kvup=lw['mla_w_kvup'], g_kn=lw['mla_g_kn'])
        y_s, ckv, kr, s_new, c_new = layer_forward(y_s, p_sample[i], pos_s, state_gla[i],
                                                   state_conv[i], attend_s, lw)
        ckv_s.append(ckv)
        kr_s.append(kr)
        gla_s.append(s_new.astype(state_gla.dtype))
        conv_s.append(c_new.astype(state_conv.dtype))
    return (y_p, y_s, jnp.stack(ckv_p), jnp.stack(kr_p), jnp.stack(gla_p), jnp.stack(conv_p),
            jnp.stack(ckv_s), jnp.stack(kr_s), jnp.stack(gla_s), jnp.stack(conv_s))
```

```python
import functools
import math

import jax
import jax.numpy as jnp
import numpy as np
from jax import lax
from jax.experimental import pallas as pl
from jax.experimental.pallas import tpu as pltpu

D_MODEL = 1024
PAST_LEN = 16384
PAGE_SIZE = 128
GLA_HEADS = 4
GLA_DK = 64
GLA_DV = 128
GLA_GATE_RANK = 16
GLA_GATE_NORM = 16.0
GLA_CHUNK = 32
MLA_HEADS = 8
MLA_Q_LORA = 256
MLA_KV_LORA = 128
MLA_NOPE = 64
MLA_ROPE = 32
MLA_V = 64
MLA_SCALE = (MLA_NOPE + MLA_ROPE) ** -0.5
ROPE_THETA = 10000.0
D_FF = 2816
CONV_W = 3
PLE_DIM = 256
EPS = 1e-6
NEG = -1e30

LANES = 128
SUBLANES = 8
HEAD_SLAB = LANES
ROPE_LO = MLA_NOPE
ROPE_HALF = MLA_ROPE // 2
GLA_QK = GLA_HEADS * GLA_DK
GLA_V = GLA_HEADS * GLA_DV
MLA_SLABS = MLA_HEADS * HEAD_SLAB
MLA_VALL = MLA_HEADS * MLA_V
C_GQ, C_GK, C_GV, C_GG = 0, 256, 512, 1024
C_MQ, C_MKV, C_MISC = 1536, 1792, 1920
IN_COLS_P = 2048
FFN_CHUNK = 256
VMEM_LIMIT = 56 * 1024 * 1024

BF16 = jnp.bfloat16
F32 = jnp.float32


def _dot(a, b):
    return jnp.dot(a, b, preferred_element_type=F32)


def _dot_nt(a, b):
    return lax.dot_general(a, b, (((1,), (1,)), ((), ())), preferred_element_type=F32)


def _dot_tn(a, b):
    return lax.dot_general(a, b, (((0,), (0,)), ((), ())), preferred_element_type=F32)


def _rms(x, g):
    return x * lax.rsqrt(jnp.mean(x * x, axis=-1, keepdims=True) + EPS) * g


def _const_spec(shape):
    nd = len(shape)
    return pl.BlockSpec(shape, lambda *_: (0,) * nd, pipeline_mode=pl.Buffered(1))


def _rope_slab(y, cos_t, sin_t):
    lane = lax.broadcasted_iota(jnp.int32, y.shape, 1)
    swapped = jnp.where(lane < ROPE_LO + ROPE_HALF,
                        pltpu.roll(y, LANES - ROPE_HALF, axis=1),
                        pltpu.roll(y, ROPE_HALF, axis=1))
    return y * cos_t + swapped * sin_t


def _in_proj_kernel(x_ref, cos_ref, sin_ref, gmix_ref, win_ref, wa2_ref, ba_ref, gqa_ref, wqup_ref,
                    segm_ref, gq_ref, gkva_ref, wkvup_ref, gk_ref, gkr_ref, *rest, sample):
    if sample:
        (wkabs_ref, gq_o, gk_o, gv_o, gg_o, la_o, qa_o, ka_o, va_o, ckv_o, kr_o, qabs_o) = rest
    else:
        (gq_o, gk_o, gv_o, gg_o, la_o, qa_o, ka_o, va_o, ckv_o, kr_o) = rest
    x = x_ref[...]
    n = _rms(x, gmix_ref[...]).astype(BF16)
    u = _dot(n, win_ref[...])
    gq_o[...] = u[:, C_GQ:C_GQ + GLA_QK] * (GLA_DK ** -0.5)
    gk_o[...] = u[:, C_GK:C_GK + GLA_QK]
    gv_o[...] = u[:, C_GV:C_GV + GLA_V].astype(BF16)
    gg_o[...] = u[:, C_GG:C_GG + GLA_V]
    misc = u[:, C_MISC:C_MISC + LANES]
    z = _dot(misc.astype(BF16), wa2_ref[...]) + ba_ref[...]
    la_o[...] = (jnp.minimum(z, 0.0) - jnp.log1p(jnp.exp(-jnp.abs(z)))) * (1.0 / GLA_GATE_NORM)

    cos_t = cos_ref[...]
    sin_t = sin_ref[...]
    segm = segm_ref[...]

    lane = lax.broadcasted_iota(jnp.int32, misc.shape, 1)
    is_rope = (lane >= ROPE_LO) & (lane < ROPE_LO + MLA_ROPE)
    ms_r = jnp.sum(jnp.where(is_rope, misc * misc, 0.0), axis=-1, keepdims=True) * (1.0 / MLA_ROPE)
    kr_slab = _rope_slab(misc * lax.rsqrt(ms_r + EPS) * gkr_ref[...], cos_t, sin_t)
    kr_o[...] = kr_slab[:, ROPE_LO:ROPE_LO + MLA_ROPE]

    mkv = u[:, C_MKV:C_MKV + MLA_KV_LORA]
    ckv = _rms(mkv, gkva_ref[...])
    ckv_o[...] = ckv
    kv = _dot(ckv.astype(BF16), wkvup_ref[...])
    va_o[...] = kv[:, MLA_SLABS:MLA_SLABS + MLA_VALL].astype(BF16)

    cq = _rms(u[:, C_MQ:C_MQ + MLA_Q_LORA], gqa_ref[...]).astype(BF16)
    qf = _dot(cq, wqup_ref[...])

    for pair in range(MLA_HEADS // 2):
        lo = pair * 2 * HEAD_SLAB
        q2 = qf[:, lo:lo + 2 * HEAD_SLAB]
        k2 = kv[:, lo:lo + 2 * HEAD_SLAB]
        q2 = q2 * lax.rsqrt(_dot((q2 * q2).astype(BF16), segm) + EPS) * gq_ref[:, lo:lo + 2 * HEAD_SLAB]
        k2 = k2 * lax.rsqrt(_dot((k2 * k2).astype(BF16), segm) + EPS) * gk_ref[:, lo:lo + 2 * HEAD_SLAB]
        for j in range(2):
            sl = slice(lo + j * HEAD_SLAB, lo + (j + 1) * HEAD_SLAB)
            qh = _rope_slab(q2[:, j * HEAD_SLAB:(j + 1) * HEAD_SLAB], cos_t, sin_t) * MLA_SCALE
            qa_o[:, sl] = qh.astype(BF16)
            ka_o[:, sl] = (k2[:, j * HEAD_SLAB:(j + 1) * HEAD_SLAB] + kr_slab).astype(BF16)
    if sample:
        qabs_o[...] = _dot(qa_o[...], wkabs_ref[...]).astype(BF16)


def _in_proj(x2d, cos_t, sin_t, w, *, tm, tiles_per_seq, sample):
    T = x2d.shape[0]
    nt = T // tm
    row = lambda i: (i, 0)
    pos = lambda i: (i % tiles_per_seq, 0)
    consts = [w['g_mix'], w['w_in'], w['w_a2'], w['b_a'], w['g_qa'], w['w_qup'], w['segm'], w['gq_slab'],
              w['g_kva'], w['w_kvup'], w['gk_slab'], w['gkr_slab']]
    if sample:
        consts.append(w['w_kabs'])
    in_specs = [pl.BlockSpec((tm, D_MODEL), row), pl.BlockSpec((tm, LANES), pos), pl.BlockSpec((tm, LANES), pos)]
    in_specs += [_const_spec(c.shape) for c in consts]
    outs = [(GLA_QK, F32), (GLA_QK, F32), (GLA_V, BF16), (GLA_V, F32), (GLA_QK, F32),
            (MLA_SLABS, BF16), (MLA_SLABS, BF16), (MLA_VALL, BF16), (MLA_KV_LORA, F32), (MLA_ROPE, F32)]
    if sample:
        outs.append((MLA_SLABS, BF16))
    return pl.pallas_call(
        functools.partial(_in_proj_kernel, sample=sample),
        grid=(nt,),
        in_specs=in_specs,
        out_specs=[pl.BlockSpec((tm, c), row) for c, _ in outs],
        out_shape=[jax.ShapeDtypeStruct((T, c), d) for c, d in outs],
        compiler_params=pltpu.CompilerParams(dimension_semantics=("parallel",), vmem_limit_bytes=VMEM_LIMIT),
        name="in_proj_sample" if sample else "in_proj_prompt",
    )(x2d, cos_t, sin_t, *consts)


def _gla_kernel(q_ref, k_ref, v_ref, gg_ref, la_ref, s0_ref, tril_ref, ones_ref, gout_ref,
                o_ref, sT_ref, st_scr, *, chunk, n_chunks):
    step = pl.program_id(1)

    @pl.when(step == 0)
    def _():
        st_scr[...] = s0_ref[0]

    tc = chunk * n_chunks
    la = la_ref[...]
    la_hi = la.astype(BF16)
    la_lo = (la - la_hi.astype(F32)).astype(BF16)
    tril = tril_ref[...]
    ones = ones_ref[...]
    b = _dot(tril, la_hi) + _dot(tril, la_lo)
    bl = _dot(ones, la_hi) + _dot(ones, la_lo)
    q = q_ref[...]
    k = k_ref[...]
    v = v_ref[...]
    q_dec = (q * jnp.exp(b)).astype(BF16)
    k_inv = (k * jnp.exp(-b)).astype(BF16)
    k_end = (k * jnp.exp(bl - b)).astype(BF16)
    dec = jnp.exp(bl)

    r = lax.broadcasted_iota(jnp.int32, (tc, tc), 0)
    c = lax.broadcasted_iota(jnp.int32, (tc, tc), 1)
    causal = (r >= c) & ((r // chunk) == (c // chunk))
    sr = lax.broadcasted_iota(jnp.int32, (GLA_V, GLA_QK), 0) // GLA_DV
    sc = lax.broadcasted_iota(jnp.int32, (GLA_V, GLA_QK), 1) // GLA_DK
    head_diag = sr == sc

    o_intra = []
    for h in range(GLA_HEADS):
        qh = q_dec[:, h * GLA_DK:(h + 1) * GLA_DK]
        kh = k_inv[:, h * GLA_DK:(h + 1) * GLA_DK]
        a = jnp.where(causal, _dot_nt(qh, kh), 0.0).astype(BF16)
        o_intra.append(_dot(a, v[:, h * GLA_DV:(h + 1) * GLA_DV]))
    o_intra = jnp.concatenate(o_intra, axis=1)

    st = st_scr[...]
    o_inter = []
    for ci in range(n_chunks):
        rows = slice(ci * chunk, (ci + 1) * chunk)
        o_inter.append(_dot_nt(q_dec[rows], st.astype(BF16)))
        xt = _dot_tn(v[rows], k_end[rows])
        st = st * dec[ci * chunk:ci * chunk + 1, :] + jnp.where(head_diag, xt, 0.0)
    st_scr[...] = st
    o = o_intra + (jnp.concatenate(o_inter, axis=0) if n_chunks > 1 else o_inter[0])

    gg = gg_ref[...]
    gate = gg / (1.0 + jnp.exp(-gg))
    gout = gout_ref[...]
    for h in range(GLA_HEADS):
        sl = slice(h * GLA_DV, (h + 1) * GLA_DV)
        o_ref[:, sl] = (_rms(o[:, sl], gout) * gate[:, sl]).astype(o_ref.dtype)

    @pl.when(step == pl.num_programs(1) - 1)
    def _():
        for h in range(GLA_HEADS):
            sT_ref[0, h] = st[h * GLA_DV:(h + 1) * GLA_DV, h * GLA_DK:(h + 1) * GLA_DK]


def _gla(q, k, v, gg, la, s0_bd, g_out, *, n_seq, seq_len, chunk, n_chunks):
    tc = chunk * n_chunks
    steps = seq_len // tc
    idx = np.arange(tc)
    same = (idx[:, None] // chunk) == (idx[None, :] // chunk)
    tril = jnp.asarray(same & (idx[:, None] >= idx[None, :]), BF16)
    ones = jnp.asarray(same, BF16)
    row = lambda b, s: (b * steps + s, 0)
    return pl.pallas_call(
        functools.partial(_gla_kernel, chunk=chunk, n_chunks=n_chunks),
        grid=(n_seq, steps),
        in_specs=[pl.BlockSpec((tc, GLA_QK), row), pl.BlockSpec((tc, GLA_QK), row), pl.BlockSpec((tc, GLA_V), row),
                  pl.BlockSpec((tc, GLA_V), row), pl.BlockSpec((tc, GLA_QK), row),
                  pl.BlockSpec((1, GLA_V, GLA_QK), lambda b, s: (b, 0, 0)),
                  _const_spec((tc, tc)), _const_spec((tc, tc)), _const_spec((1, GLA_DV))],
        out_specs=[pl.BlockSpec((tc, GLA_V), row),
                   pl.BlockSpec((1, GLA_HEADS, GLA_DV, GLA_DK), lambda b, s: (b, 0, 0, 0))],
        out_shape=[jax.ShapeDtypeStruct((n_seq * seq_len, GLA_V), BF16),
                   jax.ShapeDtypeStruct((n_seq, GLA_HEADS, GLA_DV, GLA_DK), F32)],
        scratch_shapes=[pltpu.VMEM((GLA_V, GLA_QK), F32)],
        compiler_params=pltpu.CompilerParams(dimension_semantics=("parallel", "arbitrary"),
                                             vmem_limit_bytes=VMEM_LIMIT),
        name=f"gla_c{chunk}",
    )(q, k, v, gg, la, s0_bd, tril, ones, g_out)


def _prompt_attn_kernel(q_ref, k_ref, v_ref, o_ref, *, tq, tk):
    qi = pl.program_id(2)
    r = lax.broadcasted_iota(jnp.int32, (tq, tk), 0)
    c = lax.broadcasted_iota(jnp.int32, (tq, tk), 1)
    n_diag = tq // tk
    outs = []
    for j in range(2):
        q = q_ref[:, j * HEAD_SLAB:(j + 1) * HEAD_SLAB]

        def block(kb, carry, masked):
            m, l, acc = carry
            ks = pl.multiple_of(kb * tk, tk)
            kblk = k_ref[pl.ds(ks, tk), j * HEAD_SLAB:(j + 1) * HEAD_SLAB]
            vblk = v_ref[pl.ds(ks, tk), j * MLA_V:(j + 1) * MLA_V]
            s = _dot_nt(q, kblk)
            if masked:
                s = jnp.where(c + (kb * tk - qi * tq) <= r, s, NEG)
            m_new = jnp.maximum(m, jnp.max(s, axis=-1, keepdims=True))
            alpha = jnp.exp(m - m_new)
            p = jnp.exp(s - m_new)
            l = alpha * l + jnp.sum(p, axis=-1, keepdims=True)
            acc = alpha * acc + _dot(p.astype(BF16), vblk)
            return m_new, l, acc

        carry = (jnp.full((tq, 1), NEG, F32), jnp.zeros((tq, 1), F32), jnp.zeros((tq, MLA_V), F32))
        carry = lax.fori_loop(0, qi * n_diag, functools.partial(block, masked=False), carry)
        for d in range(n_diag):
            carry = block(qi * n_diag + d, carry, True)
        m, l, acc = carry
        outs.append(acc / l)
    o_ref[...] = jnp.concatenate(outs, axis=1).astype(o_ref.dtype)


def _prompt_attn(qa, ka, va, *, n_seq, seq_len, tq, tk):
    nq = seq_len // tq
    return pl.pallas_call(
        functools.partial(_prompt_attn_kernel, tq=tq, tk=tk),
        grid=(n_seq, MLA_HEADS // 2, nq),
        in_specs=[pl.BlockSpec((tq, 2 * HEAD_SLAB), lambda b, hp, i: (b * nq + i, hp)),
                  pl.BlockSpec((seq_len, 2 * HEAD_SLAB), lambda b, hp, i: (b, hp)),
                  pl.BlockSpec((seq_len, 2 * MLA_V), lambda b, hp, i: (b, hp))],
        out_specs=pl.BlockSpec((tq, 2 * MLA_V), lambda b, hp, i: (b * nq + i, hp)),
        out_shape=jax.ShapeDtypeStruct((n_seq * seq_len, MLA_VALL), BF16),
        compiler_params=pltpu.CompilerParams(dimension_semantics=("parallel", "parallel", "arbitrary"),
                                             vmem_limit_bytes=VMEM_LIMIT),
        name="prompt_attn",
    )(qa, ka, va)


def _sample_attn_kernel(pt_ref, qabs_ref, qrope_ref, cnew_ref, krnew_ref, wkT_ref, wv_ref, *rest,
                        n_tok, pages_per_step):
    G = pages_per_step
    c_refs = rest[:G]
    r_refs = rest[G:2 * G]
    o_ref, lhs_scr, m_scr, l_scr, acc_scr = rest[2 * G:]
    g = pl.program_id(1)
    rows = n_tok * MLA_HEADS
    n_k = MLA_HEADS * MLA_NOPE

    def attend(c, kr, mask):
        cb = c.astype(BF16)
        big = _dot_nt(lhs_scr[...], cb)
        kvn = big[:n_k]
        s_nope = big[n_k:n_k + rows]
        ss = jnp.sum((kvn * kvn).reshape(MLA_HEADS, MLA_NOPE, PAGE_SIZE), axis=1)
        inv = lax.rsqrt(ss * (1.0 / MLA_NOPE) + EPS)
        s_rope = _dot_nt(qrope_ref[0], kr.astype(BF16))
        s = (s_nope.reshape(n_tok, MLA_HEADS, PAGE_SIZE) * inv[None]).reshape(rows, PAGE_SIZE) + s_rope
        if mask is not None:
            s = jnp.where(mask, s, NEG)
        m = m_scr[...]
        m_new = jnp.maximum(m, jnp.max(s, axis=-1, keepdims=True))
        alpha = jnp.exp(m - m_new)
        p = jnp.exp(s - m_new)
        l_scr[...] = alpha * l_scr[...] + jnp.sum(p, axis=-1, keepdims=True)
        acc_scr[...] = alpha * acc_scr[...] + _dot(p.astype(BF16), cb)
        m_scr[...] = m_new

    @pl.when(g == 0)
    def _():
        lhs_scr[:n_k, :] = wkT_ref[...]
        lhs_scr[n_k:n_k + rows, :] = qabs_ref[0]
        m_scr[...] = jnp.full(m_scr.shape, NEG, F32)
        l_scr[...] = jnp.zeros(l_scr.shape, F32)
        acc_scr[...] = jnp.zeros(acc_scr.shape, F32)
        c_new = jnp.concatenate([cnew_ref[0], jnp.zeros((PAGE_SIZE - n_tok, MLA_KV_LORA), F32)], axis=0)
        kr_new = jnp.concatenate([krnew_ref[0], jnp.zeros((PAGE_SIZE - n_tok, MLA_ROPE), F32)], axis=0)
        key = lax.broadcasted_iota(jnp.int32, (rows, PAGE_SIZE), 1)
        tok = lax.broadcasted_iota(jnp.int32, (rows, PAGE_SIZE), 0) // MLA_HEADS
        attend(c_new, kr_new, key <= tok)

    for j in range(G):
        attend(c_refs[j][0], r_refs[j][0], None)

    @pl.when(g == pl.num_programs(1) - 1)
    def _():
        o_lat = (acc_scr[...] / l_scr[...]).astype(BF16)
        full = _dot(o_lat, wv_ref[...])
        full = full.reshape(n_tok, MLA_HEADS, MLA_VALL)
        hr = lax.broadcasted_iota(jnp.int32, (MLA_HEADS, MLA_VALL), 0)
        hc = lax.broadcasted_iota(jnp.int32, (MLA_HEADS, MLA_VALL), 1) // MLA_V
        o_ref[...] = jnp.sum(jnp.where((hr == hc)[None], full, 0.0), axis=1).astype(o_ref.dtype)


def _sample_attn(page_table, qabs, qrope, c_new, kr_new, w_kT, w_v, pool_c, pool_r, *, pages_per_step):
    n_b, n_pages = page_table.shape
    n_tok = c_new.shape[1]
    rows = n_tok * MLA_HEADS
    G = pages_per_step
    n_k = MLA_HEADS * MLA_NOPE

    def page_spec(width, j):
        return pl.BlockSpec((1, PAGE_SIZE, width), lambda b, g, pt: (pt[b, g * G + j], 0, 0))

    per_b = lambda b, g, pt: (b, 0, 0)
    grid_spec = pltpu.PrefetchScalarGridSpec(
        num_scalar_prefetch=1,
        grid=(n_b, n_pages // G),
        in_specs=[pl.BlockSpec((1, rows, MLA_KV_LORA), per_b), pl.BlockSpec((1, rows, MLA_ROPE), per_b),
                  pl.BlockSpec((1, n_tok, MLA_KV_LORA), per_b), pl.BlockSpec((1, n_tok, MLA_ROPE), per_b),
                  pl.BlockSpec((n_k, MLA_KV_LORA), lambda b, g, pt: (0, 0)),
                  pl.BlockSpec((MLA_KV_LORA, MLA_VALL), lambda b, g, pt: (0, 0))]
                 + [page_spec(MLA_KV_LORA, j) for j in range(G)]
                 + [page_spec(MLA_ROPE, j) for j in range(G)],
        out_specs=pl.BlockSpec((n_tok, MLA_VALL), lambda b, g, pt: (b, 0)),
        scratch_shapes=[pltpu.VMEM((n_k + rows, MLA_KV_LORA), BF16), pltpu.VMEM((rows, 1), F32),
                        pltpu.VMEM((rows, 1), F32), pltpu.VMEM((rows, MLA_KV_LORA), F32)])
    return pl.pallas_call(
        functools.partial(_sample_attn_kernel, n_tok=n_tok, pages_per_step=G),
        grid_spec=grid_spec,
        out_shape=jax.ShapeDtypeStruct((n_b * n_tok, MLA_VALL), BF16),
        compiler_params=pltpu.CompilerParams(dimension_semantics=("parallel", "arbitrary"),
                                             vmem_limit_bytes=VMEM_LIMIT),
        name="sample_attn",
    )(page_table, qabs, qrope, c_new, kr_new, w_kT, w_v, *([pool_c] * G), *([pool_r] * G))


def _post_kernel(x_ref, og_ref, om_ref, p_ref, wo_ref, gffn_ref, wg_ref, wu_ref, cw_ref, cb_ref, wd_ref,
                 gple_ref, wpg_ref, wpp_ref, *rest, tiles_per_seq, seq_rows):
    paged_prev = seq_rows is not None
    if paged_prev:
        p1_ref, p2_ref, y_ref, tail_ref, acc_scr = rest
    else:
        y_ref, tail_ref, acc_scr, carry_scr = rest
    tm = x_ref.shape[0]
    i = pl.program_id(0)
    h1 = x_ref[...] + _dot(og_ref[...], wo_ref[:GLA_V, :]) + _dot(om_ref[...], wo_ref[GLA_V:, :])
    n2 = _rms(h1, gffn_ref[...]).astype(BF16)
    row = lax.broadcasted_iota(jnp.int32, (tm, FFN_CHUNK), 0)
    if not paged_prev:
        @pl.when(i % tiles_per_seq == 0)
        def _():
            carry_scr[...] = jnp.zeros(carry_scr.shape, F32)

    acc_scr[...] = jnp.zeros(acc_scr.shape, F32)
    for f in range(0, D_FF, FFN_CHUNK):
        cols = slice(f, f + FFN_CHUNK)
        a = _dot(n2, wg_ref[:, cols])
        up = _dot(n2, wu_ref[:, cols])
        r1 = pltpu.roll(a, 1, axis=0)
        r2 = pltpu.roll(a, 2, axis=0)
        if paged_prev:
            t = row % seq_rows
            a1 = jnp.where(t >= 1, r1, p1_ref[:, cols])
            a2 = jnp.where(t >= 2, r2, p2_ref[:, cols])
        else:
            prev = carry_scr[:, cols]
            pm1 = prev[SUBLANES - 1:SUBLANES, :]
            pm2 = prev[SUBLANES - 2:SUBLANES - 1, :]
            a1 = jnp.where(row >= 1, r1, pm1)
            a2 = jnp.where(row >= 2, r2, jnp.where(row == 0, pm2, pm1))
            carry_scr[:, cols] = a[tm - SUBLANES:, :]
        tail_ref[:, cols] = a[tm - SUBLANES:, :] if not paged_prev else a
        conv = cb_ref[:, cols] + cw_ref[0:1, cols] * a2 + cw_ref[1:2, cols] * a1 + cw_ref[2:3, cols] * a
        gact = (conv / (1.0 + jnp.exp(-conv)) * up).astype(BF16)
        acc_scr[...] += _dot(gact, wd_ref[cols, :])
    h2 = h1 + acc_scr[...]
    n3 = _rms(h2, gple_ref[...]).astype(BF16)
    gate = 1.0 / (1.0 + jnp.exp(-_dot(n3, wpg_ref[...])))
    y_ref[...] = h2 + _dot(p_ref[...].astype(BF16), wpp_ref[...]) * gate


def _post(x2d, og, om, p2d, w, prev=None, *, tm, tiles_per_seq, seq_rows):
    T = x2d.shape[0]
    nt = T // tm
    row = lambda i: (i, 0)
    consts = [w['w_o'], w['g_ffn'], w['w_gate'], w['w_up'], w['conv_w'], w['conv_b'], w['w_down'],
              w['g_ple'], w['w_pgate'], w['w_pproj']]
    in_specs = [pl.BlockSpec((tm, D_MODEL), row), pl.BlockSpec((tm, GLA_V), row),
                pl.BlockSpec((tm, MLA_VALL), row), pl.BlockSpec((tm, PLE_DIM), row)]
    in_specs += [_const_spec(c.shape) for c in consts]
    args = [x2d, og, om, p2d, *consts]
    scratch = [pltpu.VMEM((tm, D_MODEL), F32)]
    if prev is not None:
        in_specs += [pl.BlockSpec((tm, D_FF), row)] * 2
        args += list(prev)
        tail_rows = tm
    else:
        scratch.append(pltpu.VMEM((SUBLANES, D_FF), F32))
        tail_rows = SUBLANES
    return pl.pallas_call(
        functools.partial(_post_kernel, tiles_per_seq=tiles_per_seq, seq_rows=seq_rows),
        grid=(nt,),
        in_specs=in_specs,
        out_specs=[pl.BlockSpec((tm, D_MODEL), row), pl.BlockSpec((tail_rows, D_FF), row)],
        out_shape=[jax.ShapeDtypeStruct((T, D_MODEL), F32), jax.ShapeDtypeStruct((nt * tail_rows, D_FF), F32)],
        scratch_shapes=scratch,
        compiler_params=pltpu.CompilerParams(dimension_semantics=("arbitrary",), vmem_limit_bytes=VMEM_LIMIT),
        name="post_sample" if prev is not None else "post_prompt",
    )(*args)


def _prep_weights(g_mix, w_in, gla_w_a2, gla_b_a, gla_g_out, mla_g_qa, mla_w_qup, mla_g_qn, mla_g_qr,
                  mla_g_kva, mla_g_kr, mla_w_kvup, mla_g_kn, w_o, g_ffn, ffn_w_gate, ffn_w_up, ffn_conv_w,
                  ffn_conv_b, ffn_w_down, g_ple, ple_w_gate, ple_w_proj):
    sizes = (GLA_QK, GLA_QK, GLA_V, GLA_V, GLA_GATE_RANK, MLA_Q_LORA, MLA_KV_LORA, MLA_ROPE)
    offs = np.concatenate([[0], np.cumsum(sizes)])
    piece = lambda i: w_in[:, offs[i]:offs[i + 1]]
    zeros = lambda n: jnp.zeros((D_MODEL, n), w_in.dtype)
    misc = jnp.concatenate([piece(4), zeros(ROPE_LO - GLA_GATE_RANK), piece(7),
                            zeros(LANES - ROPE_LO - MLA_ROPE)], axis=1)
    w_in_p = jnp.concatenate([piece(0), piece(1), piece(2), piece(3), piece(5), piece(6), misc], axis=1)
    w_a2 = jnp.concatenate([gla_w_a2, jnp.zeros((LANES - GLA_GATE_RANK, GLA_QK), gla_w_a2.dtype)], axis=0)

    def slab_vec(nope, rope_):
        one = jnp.concatenate([nope, rope_, jnp.zeros((HEAD_SLAB - MLA_NOPE - MLA_ROPE,), F32)])
        return jnp.tile(one, MLA_HEADS)[None, :]

    wq = mla_w_qup.reshape(MLA_Q_LORA, MLA_HEADS, MLA_NOPE + MLA_ROPE)
    wq = jnp.pad(wq, ((0, 0), (0, 0), (0, HEAD_SLAB - MLA_NOPE - MLA_ROPE))).reshape(MLA_Q_LORA, MLA_SLABS)
    wkv = mla_w_kvup.reshape(MLA_KV_LORA, MLA_HEADS, MLA_NOPE + MLA_V)
    wk = wkv[:, :, :MLA_NOPE]
    wv = wkv[:, :, MLA_NOPE:].reshape(MLA_KV_LORA, MLA_VALL)
    wk_slab = jnp.pad(wk, ((0, 0), (0, 0), (0, HEAD_SLAB - MLA_NOPE))).reshape(MLA_KV_LORA, MLA_SLABS)
    w_kvup_p = jnp.concatenate([wk_slab, wv], axis=1)
    wk_g = jnp.pad(wk * mla_g_kn[None, None, :], ((0, 0), (0, 0), (0, HEAD_SLAB - MLA_NOPE)))
    eye = jnp.eye(MLA_HEADS, dtype=F32)
    w_kabs = jnp.einsum('nhd,hg->hdgn', wk_g, eye).reshape(MLA_SLABS, MLA_HEADS * MLA_KV_LORA)
    w_kT = wk.transpose(1, 2, 0).reshape(MLA_HEADS * MLA_NOPE, MLA_KV_LORA)

    lane = np.arange(2 * HEAD_SLAB)
    seg = np.where(lane % HEAD_SLAB < MLA_NOPE, 0, np.where(lane % HEAD_SLAB < MLA_NOPE + MLA_ROPE, 1, 2))
    same = (lane[:, None] // HEAD_SLAB == lane[None, :] // HEAD_SLAB) & (seg[:, None] == seg[None, :])
    segm = np.where(same & (seg[:, None] == 0), 1.0 / MLA_NOPE, np.where(same & (seg[:, None] == 1), 1.0 / MLA_ROPE, 0.0))

    gkr = jnp.concatenate([jnp.zeros((ROPE_LO,), F32), mla_g_kr, jnp.zeros((LANES - ROPE_LO - MLA_ROPE,), F32)])
    return dict(
        g_mix=g_mix[None, :], w_in=w_in_p.astype(BF16), w_a2=w_a2.astype(BF16), b_a=gla_b_a[None, :],
        g_out=gla_g_out[None, :], g_qa=mla_g_qa[None, :], w_qup=wq.astype(BF16), segm=jnp.asarray(segm, BF16),
        gq_slab=slab_vec(mla_g_qn, mla_g_qr), g_kva=mla_g_kva[None, :], w_kvup=w_kvup_p.astype(BF16),
        gk_slab=slab_vec(mla_g_kn, jnp.zeros((MLA_ROPE,), F32)),
        g_kn=mla_g_kn, gkr_slab=gkr[None, :], w_kabs=w_kabs.astype(BF16), w_kT=w_kT.astype(BF16),
        w_v=wv.astype(BF16), w_o=w_o.astype(BF16), g_ffn=g_ffn[None, :], w_gate=ffn_w_gate.astype(BF16),
        w_up=ffn_w_up.astype(BF16), conv_w=ffn_conv_w, conv_b=ffn_conv_b[None, :],
        w_down=ffn_w_down.astype(BF16), g_ple=g_ple[None, :], w_pgate=ple_w_gate.astype(BF16),
        w_pproj=ple_w_proj.astype(BF16))


def _rope_tables(pos):
    inv = ROPE_THETA ** (-jnp.arange(ROPE_HALF, dtype=F32) * 2.0 / MLA_ROPE)
    ang = pos.astype(F32)[:, None] * inv[None, :]
    cos, sin = jnp.cos(ang), jnp.sin(ang)
    T = pos.shape[0]
    pad = jnp.zeros((T, HEAD_SLAB - MLA_NOPE - MLA_ROPE), F32)
    cos_t = jnp.concatenate([jnp.ones((T, MLA_NOPE), F32), cos, cos, pad], axis=1)
    sin_t = jnp.concatenate([jnp.zeros((T, MLA_NOPE), F32), -sin, sin, pad], axis=1)
    return cos_t, sin_t


def _pick_tile(n, cap):
    t = math.gcd(n, cap)
    return t


def _state_to_bd(s):
    eye = jnp.eye(GLA_HEADS, dtype=s.dtype)
    return jnp.einsum('bhdv,hg->bhvgd', s, eye).reshape(s.shape[0], GLA_V, GLA_QK)


def _layer(w, x_p, x_s, p_p, p_s, pool_c, pool_r, state_gla, state_conv, page_table):
    B, S, _ = x_p.shape
    Bd, Td, _ = x_s.shape
    tm = _pick_tile(S, 512)
    tps = S // tm
    cos_p, sin_p = _rope_tables(jnp.arange(S))
    xp2 = x_p.reshape(B * S, D_MODEL)
    gq, gk, gv, gg, la, qa, ka, va, ckv_p, kr_p = _in_proj(xp2, cos_p, sin_p, w, tm=tm, tiles_per_seq=tps,
                                                           sample=False)
    chunk_p = math.gcd(S, GLA_CHUNK)
    n_chunks = math.gcd(S // chunk_p, 8)
    og_p, sT_p = _gla(gq, gk, gv, gg, la, jnp.zeros((B, GLA_V, GLA_QK), F32), w['g_out'],
                      n_seq=B, seq_len=S, chunk=chunk_p, n_chunks=n_chunks)
    tq = _pick_tile(S, 512)
    om_p = _prompt_attn(qa, ka, va, n_seq=B, seq_len=S, tq=tq, tk=tq)
    y_p, tail_p = _post(xp2, og_p, om_p, p_p.reshape(B * S, PLE_DIM), w, tm=tm, tiles_per_seq=tps, seq_rows=None)
    conv_p = tail_p.reshape(B, tps, SUBLANES, D_FF)[:, -1, SUBLANES - (CONV_W - 1):, :]

    Ts = Bd * Td
    pos_s = PAST_LEN + jnp.arange(Td)
    cos_s, sin_s = _rope_tables(jnp.tile(pos_s, Bd))
    xs2 = x_s.reshape(Ts, D_MODEL)
    gq, gk, gv, gg, la, qa, ka, va, ckv_s, kr_s, qabs = _in_proj(xs2, cos_s, sin_s, w, tm=Ts, tiles_per_seq=1,
                                                                 sample=True)
    chunk_s = math.gcd(Td, GLA_CHUNK)
    og_s, sT_s = _gla(gq, gk, gv, gg, la, _state_to_bd(state_gla), w['g_out'],
                      n_seq=Bd, seq_len=Td, chunk=chunk_s, n_chunks=Td // chunk_s)
    rows = Td * MLA_HEADS
    qrope = qa.reshape(Ts, MLA_HEADS, HEAD_SLAB)[:, :, ROPE_LO:ROPE_LO + MLA_ROPE].reshape(Bd, rows, MLA_ROPE)
    om_s = _sample_attn(page_table, qabs.reshape(Bd, rows, MLA_KV_LORA), qrope,
                        ckv_s.reshape(Bd, Td, MLA_KV_LORA), kr_s.reshape(Bd, Td, MLA_ROPE),
                        w['w_kT'], w['w_v'], pool_c, pool_r, pages_per_step=math.gcd(page_table.shape[1], 8))
    zpad = lambda a, lo: jnp.pad(a, ((0, 0), (lo, Td - lo - a.shape[1]), (0, 0))).reshape(Ts, D_FF)
    prev1 = zpad(state_conv[:, 1:2], 0)
    prev2 = zpad(state_conv, 0)
    y_s, a_s = _post(xs2, og_s, om_s, p_s.reshape(Ts, PLE_DIM), w, prev=(prev1, prev2), tm=Ts, tiles_per_seq=1,
                     seq_rows=Td)
    full = jnp.concatenate([state_conv, a_s.reshape(Bd, Td, D_FF)], axis=1)
    conv_s = full[:, full.shape[1] - (CONV_W - 1):]
    return (y_p.reshape(B, S, D_MODEL), y_s.reshape(Bd, Td, D_MODEL),
            ckv_p.reshape(B, S, MLA_KV_LORA), kr_p.reshape(B, S, MLA_ROPE),
            sT_p.transpose(0, 1, 3, 2), conv_p,
            ckv_s.reshape(Bd, Td, MLA_KV_LORA), kr_s.reshape(Bd, Td, MLA_ROPE),
            sT_s.transpose(0, 1, 3, 2), conv_s)


def kernel(x_prompt, x_sample, cache_ckv, cache_krope, state_gla, state_conv, page_table, p_prompt, p_sample, g_mix, w_in, gla_w_a2, gla_b_a, gla_g_out, mla_g_qa, mla_w_qup, mla_g_qn, mla_g_qr, mla_g_kva, mla_g_kr, mla_w_kvup, mla_g_kn, w_o, g_ffn, ffn_w_gate, ffn_w_up, ffn_conv_w, ffn_conv_b, ffn_w_down, g_ple, ple_w_gate, ple_w_proj):
    depth = w_in.shape[0]
    per_layer = (g_mix, w_in, gla_w_a2, gla_b_a, gla_g_out, mla_g_qa, mla_w_qup, mla_g_qn, mla_g_qr, mla_g_kva,
                 mla_g_kr, mla_w_kvup, mla_g_kn, w_o, g_ffn, ffn_w_gate, ffn_w_up, ffn_conv_w, ffn_conv_b,
                 ffn_w_down, g_ple, ple_w_gate, ple_w_proj)
    y_p, y_s = x_prompt, x_sample
    outs = [[] for _ in range(8)]
    for i in range(depth):
        w = _prep_weights(*(a[i] for a in per_layer))
        res = _layer(w, y_p, y_s, p_prompt[i], p_sample[i], cache_ckv[i], cache_krope[i], state_gla[i],
                     state_conv[i], page_table)
        y_p, y_s = res[0], res[1]
        for lst, r in zip(outs, res[2:]):
            lst.append(r)
    return (y_p, y_s) + tuple(jnp.stack(lst) for lst in outs)
```

```python
import functools
import math

import jax
import jax.numpy as jnp
import numpy as np
from jax import lax
from jax.experimental import pallas as pl
from jax.experimental.pallas import tpu as pltpu

D_MODEL = 1024
PAST_LEN = 16384
PAGE_SIZE = 128
GLA_HEADS = 4
GLA_DK = 64
GLA_DV = 128
GLA_GATE_RANK = 16
GLA_GATE_NORM = 16.0
GLA_CHUNK = 32
MLA_HEADS = 8
MLA_Q_LORA = 256
MLA_KV_LORA = 128
MLA_NOPE = 64
MLA_ROPE = 32
MLA_V = 64
MLA_SCALE = (MLA_NOPE + MLA_ROPE) ** -0.5
LOG2E = math.log2(math.e)
ROPE_THETA = 10000.0
D_FF = 2816
CONV_W = 3
PLE_DIM = 256
EPS = 1e-6
NEG = -1e30

LANES = 128
SUBLANES = 8
HEAD_SLAB = LANES
ROPE_LO = MLA_NOPE
ROPE_HALF = MLA_ROPE // 2
GLA_QK = GLA_HEADS * GLA_DK
GLA_V = GLA_HEADS * GLA_DV
MLA_SLABS = MLA_HEADS * HEAD_SLAB
MLA_VALL = MLA_HEADS * MLA_V
C_GQ, C_GK, C_GV, C_GG = 0, 256, 512, 1024
C_MQ, C_MKV, C_MISC = 1536, 1792, 1920
IN_COLS_P = 2048
FFN_CHUNK = 256
SAMPLE_PAGES_PER_STEP = 16
VMEM_LIMIT = 56 * 1024 * 1024

BF16 = jnp.bfloat16
F32 = jnp.float32


def _dot(a, b):
    return jnp.dot(a, b, preferred_element_type=F32)


def _dot_nt(a, b):
    return lax.dot_general(a, b, (((1,), (1,)), ((), ())), preferred_element_type=F32)


def _dot_tn(a, b):
    return lax.dot_general(a, b, (((0,), (0,)), ((), ())), preferred_element_type=F32)


def _rms(x, g):
    return x * lax.rsqrt(jnp.mean(x * x, axis=-1, keepdims=True) + EPS) * g


def _const_spec(shape):
    nd = len(shape)
    return pl.BlockSpec(shape, lambda *_: (0,) * nd, pipeline_mode=pl.Buffered(1))


def _rope_slab(y, cos_t, sin_t):
    lane = lax.broadcasted_iota(jnp.int32, y.shape, 1)
    swapped = jnp.where(lane < ROPE_LO + ROPE_HALF,
                        pltpu.roll(y, LANES - ROPE_HALF, axis=1),
                        pltpu.roll(y, ROPE_HALF, axis=1))
    return y * cos_t + swapped * sin_t


def _in_proj_kernel(x_ref, cos_ref, sin_ref, gmix_ref, win_ref, wa2_ref, ba_ref, gqa_ref, wqup_ref,
                    segm_ref, gq_ref, gkva_ref, gkr_ref, *rest, sample):
    if sample:
        (wkabs_ref, gq_o, gk_o, gv_o, gg_o, la_o, qa_o, ckv_o, kr_o, qabs_o) = rest
    else:
        (wk_ref, gk_ref, wvT_ref, gq_o, gk_o, gv_o, gg_o, la_o, qa_o, ckv_o, kr_o, ka_o, vaT_o) = rest
    x = x_ref[...]
    n = _rms(x, gmix_ref[...]).astype(BF16)
    u = _dot(n, win_ref[...])
    gq_o[...] = u[:, C_GQ:C_GQ + GLA_QK] * (GLA_DK ** -0.5)
    gk_o[...] = u[:, C_GK:C_GK + GLA_QK]
    gv_o[...] = u[:, C_GV:C_GV + GLA_V].astype(BF16)
    gg_o[...] = u[:, C_GG:C_GG + GLA_V]
    misc = u[:, C_MISC:C_MISC + LANES]
    z = _dot(misc.astype(BF16), wa2_ref[...]) + ba_ref[...]
    la_o[...] = (jnp.minimum(z, 0.0) - jnp.log1p(jnp.exp(-jnp.abs(z)))) * (1.0 / GLA_GATE_NORM)

    cos_t = cos_ref[...]
    sin_t = sin_ref[...]
    segm = segm_ref[...]

    lane = lax.broadcasted_iota(jnp.int32, misc.shape, 1)
    is_rope = (lane >= ROPE_LO) & (lane < ROPE_LO + MLA_ROPE)
    ms_r = jnp.sum(jnp.where(is_rope, misc * misc, 0.0), axis=-1, keepdims=True) * (1.0 / MLA_ROPE)
    kr_slab = _rope_slab(misc * lax.rsqrt(ms_r + EPS) * gkr_ref[...], cos_t, sin_t)
    kr_o[...] = kr_slab[:, ROPE_LO:ROPE_LO + MLA_ROPE]

    mkv = u[:, C_MKV:C_MKV + MLA_KV_LORA]
    ckv = _rms(mkv, gkva_ref[...])
    ckv_o[...] = ckv
    ckv_b = ckv.astype(BF16)
    if not sample:
        kn = _dot(ckv_b, wk_ref[...])
        vaT_o[...] = _dot_nt(wvT_ref[...], ckv_b).astype(BF16)

    cq = _rms(u[:, C_MQ:C_MQ + MLA_Q_LORA], gqa_ref[...]).astype(BF16)
    qf = _dot(cq, wqup_ref[...])

    for pair in range(MLA_HEADS // 2):
        lo = pair * 2 * HEAD_SLAB
        q2 = qf[:, lo:lo + 2 * HEAD_SLAB]
        q2 = q2 * lax.rsqrt(_dot((q2 * q2).astype(BF16), segm) + EPS) * gq_ref[:, lo:lo + 2 * HEAD_SLAB]
        if not sample:
            k2 = kn[:, lo:lo + 2 * HEAD_SLAB]
            k2 = k2 * lax.rsqrt(_dot((k2 * k2).astype(BF16), segm) + EPS) * gk_ref[:, lo:lo + 2 * HEAD_SLAB]
        for j in range(2):
            sl = slice(lo + j * HEAD_SLAB, lo + (j + 1) * HEAD_SLAB)
            qh = _rope_slab(q2[:, j * HEAD_SLAB:(j + 1) * HEAD_SLAB], cos_t, sin_t) * (MLA_SCALE * LOG2E)
            qa_o[:, sl] = qh.astype(BF16)
            if not sample:
                ka_o[:, sl] = (k2[:, j * HEAD_SLAB:(j + 1) * HEAD_SLAB] + kr_slab).astype(BF16)
    if sample:
        qabs_o[...] = _dot(qa_o[...], wkabs_ref[...]).astype(BF16)


def _in_proj(x2d, cos_t, sin_t, w, *, tm, tiles_per_seq, sample):
    T = x2d.shape[0]
    nt = T // tm
    row = lambda i: (i, 0)
    pos = lambda i: (i % tiles_per_seq, 0)
    consts = [w['g_mix'], w['w_in'], w['w_a2'], w['b_a'], w['g_qa'], w['w_qup'], w['segm'], w['gq_slab'],
              w['g_kva'], w['gkr_slab']]
    consts += [w['w_kabs']] if sample else [w['w_kslab'], w['gk_slab'], w['w_vT']]
    in_specs = [pl.BlockSpec((tm, D_MODEL), row), pl.BlockSpec((tm, LANES), pos), pl.BlockSpec((tm, LANES), pos)]
    in_specs += [_const_spec(c.shape) for c in consts]
    outs = [(GLA_QK, F32), (GLA_QK, F32), (GLA_V, BF16), (GLA_V, F32), (GLA_QK, F32),
            (MLA_SLABS, BF16), (MLA_KV_LORA, F32), (MLA_ROPE, F32), (MLA_SLABS, BF16)]
    out_specs = [pl.BlockSpec((tm, c), row) for c, _ in outs]
    out_shape = [jax.ShapeDtypeStruct((T, c), d) for c, d in outs]
    if not sample:
        out_specs.append(pl.BlockSpec((MLA_VALL, tm), lambda i: (0, i)))
        out_shape.append(jax.ShapeDtypeStruct((MLA_VALL, T), BF16))
    return pl.pallas_call(
        functools.partial(_in_proj_kernel, sample=sample),
        grid=(nt,),
        in_specs=in_specs,
        out_specs=out_specs,
        out_shape=out_shape,
        compiler_params=pltpu.CompilerParams(dimension_semantics=("parallel",), vmem_limit_bytes=VMEM_LIMIT),
        name="in_proj_sample" if sample else "in_proj_prompt",
    )(x2d, cos_t, sin_t, *consts)


def _gla_kernel(q_ref, k_ref, v_ref, gg_ref, la_ref, s0_ref, tril_ref, ones_ref, gout_ref,
                o_ref, sT_ref, st_scr, *, chunk, n_chunks):
    step = pl.program_id(1)

    @pl.when(step == 0)
    def _():
        st_scr[...] = s0_ref[0]

    tc = chunk * n_chunks
    la = la_ref[...]
    la_hi = la.astype(BF16)
    la_lo = (la - la_hi.astype(F32)).astype(BF16)
    tril = tril_ref[...]
    ones = ones_ref[...]
    b = _dot(tril, la_hi) + _dot(tril, la_lo)
    bl = _dot(ones, la_hi) + _dot(ones, la_lo)
    q = q_ref[...]
    k = k_ref[...]
    v = v_ref[...]
    q_dec = (q * jnp.exp(b)).astype(BF16)
    k_inv = (k * jnp.exp(-b)).astype(BF16)
    k_end = (k * jnp.exp(bl - b)).astype(BF16)
    dec = jnp.exp(bl)

    r = lax.broadcasted_iota(jnp.int32, (tc, tc), 0)
    c = lax.broadcasted_iota(jnp.int32, (tc, tc), 1)
    causal = (r >= c) & ((r // chunk) == (c // chunk))
    sr = lax.broadcasted_iota(jnp.int32, (GLA_V, GLA_QK), 0) // GLA_DV
    sc = lax.broadcasted_iota(jnp.int32, (GLA_V, GLA_QK), 1) // GLA_DK
    head_diag = sr == sc

    o_intra = []
    for h in range(GLA_HEADS):
        qh = q_dec[:, h * GLA_DK:(h + 1) * GLA_DK]
        kh = k_inv[:, h * GLA_DK:(h + 1) * GLA_DK]
        a = jnp.where(causal, _dot_nt(qh, kh), 0.0).astype(BF16)
        o_intra.append(_dot(a, v[:, h * GLA_DV:(h + 1) * GLA_DV]))
    o_intra = jnp.concatenate(o_intra, axis=1)

    st = st_scr[...]
    o_inter = []
    for ci in range(n_chunks):
        rows = slice(ci * chunk, (ci + 1) * chunk)
        o_inter.append(_dot_nt(q_dec[rows], st.astype(BF16)))
        xt = _dot_tn(v[rows], k_end[rows])
        st = st * dec[ci * chunk:ci * chunk + 1, :] + jnp.where(head_diag, xt, 0.0)
    st_scr[...] = st
    o = o_intra + (jnp.concatenate(o_inter, axis=0) if n_chunks > 1 else o_inter[0])

    gg = gg_ref[...]
    gate = gg / (1.0 + jnp.exp(-gg))
    gout = gout_ref[...]
    for h in range(GLA_HEADS):
        sl = slice(h * GLA_DV, (h + 1) * GLA_DV)
        o_ref[:, sl] = (_rms(o[:, sl], gout) * gate[:, sl]).astype(o_ref.dtype)

    @pl.when(step == pl.num_programs(1) - 1)
    def _():
        for h in range(GLA_HEADS):
            sT_ref[0, h] = st[h * GLA_DV:(h + 1) * GLA_DV, h * GLA_DK:(h + 1) * GLA_DK]


def _gla(q, k, v, gg, la, s0_bd, g_out, *, n_seq, seq_len, chunk, n_chunks):
    tc = chunk * n_chunks
    steps = seq_len // tc
    idx = np.arange(tc)
    same = (idx[:, None] // chunk) == (idx[None, :] // chunk)
    tril = jnp.asarray(same & (idx[:, None] >= idx[None, :]), BF16)
    ones = jnp.asarray(same, BF16)
    row = lambda b, s: (b * steps + s, 0)
    return pl.pallas_call(
        functools.partial(_gla_kernel, chunk=chunk, n_chunks=n_chunks),
        grid=(n_seq, steps),
        in_specs=[pl.BlockSpec((tc, GLA_QK), row), pl.BlockSpec((tc, GLA_QK), row), pl.BlockSpec((tc, GLA_V), row),
                  pl.BlockSpec((tc, GLA_V), row), pl.BlockSpec((tc, GLA_QK), row),
                  pl.BlockSpec((1, GLA_V, GLA_QK), lambda b, s: (b, 0, 0)),
                  _const_spec((tc, tc)), _const_spec((tc, tc)), _const_spec((1, GLA_DV))],
        out_specs=[pl.BlockSpec((tc, GLA_V), row),
                   pl.BlockSpec((1, GLA_HEADS, GLA_DV, GLA_DK), lambda b, s: (b, 0, 0, 0))],
        out_shape=[jax.ShapeDtypeStruct((n_seq * seq_len, GLA_V), BF16),
                   jax.ShapeDtypeStruct((n_seq, GLA_HEADS, GLA_DV, GLA_DK), F32)],
        scratch_shapes=[pltpu.VMEM((GLA_V, GLA_QK), F32)],
        compiler_params=pltpu.CompilerParams(dimension_semantics=("parallel", "arbitrary"),
                                             vmem_limit_bytes=VMEM_LIMIT),
        name=f"gla_c{chunk}",
    )(q, k, v, gg, la, s0_bd, tril, ones, g_out)


def _prompt_attn_kernel(q_ref, k_ref, vT_ref, o_ref, s_scr, *, tq, tk):
    assert tq == tk
    qi = pl.program_id(2)
    key = lax.broadcasted_iota(jnp.int32, (tk, tq), 0)
    qry = lax.broadcasted_iota(jnp.int32, (tk, tq), 1)

    def scores(j, kb, masked):
        ks = pl.multiple_of(kb * tk, tk)
        s = _dot_nt(k_ref[pl.ds(ks, tk), j * HEAD_SLAB:(j + 1) * HEAD_SLAB],
                    q_ref[:, j * HEAD_SLAB:(j + 1) * HEAD_SLAB])
        if masked:
            s = jnp.where(key <= qry, s, NEG)
        s_scr[j] = s
        return jnp.max(s, axis=0, keepdims=True)

    def update(j, kb, m_blk, state):
        m, l, acc = state
        ks = pl.multiple_of(kb * tk, tk)
        m_new = jnp.maximum(m, m_blk)
        alpha = jnp.exp2(m - m_new)
        p = jnp.exp2(s_scr[j] - m_new)
        l = alpha * l + jnp.sum(p, axis=0, keepdims=True)
        acc = alpha * acc + _dot(vT_ref[j * MLA_V:(j + 1) * MLA_V, pl.ds(ks, tk)], p.astype(BF16))
        return m_new, l, acc

    def body(t, carry):
        blk1, st0, st1 = carry
        kb = qi - 1 - t
        blk0 = scores(0, kb, False)
        st1 = update(1, kb + 1, blk1, st1)
        blk1 = scores(1, kb, False)
        st0 = update(0, kb, blk0, st0)
        return blk1, st0, st1

    init = (jnp.full((1, tq), NEG, F32), jnp.zeros((1, tq), F32), jnp.zeros((MLA_V, tq), F32))
    blk0 = scores(0, qi, True)
    blk1 = scores(1, qi, True)
    st0 = update(0, qi, blk0, init)
    blk1, st0, st1 = lax.fori_loop(0, qi, body, (blk1, st0, init))
    st1 = update(1, 0, blk1, st1)
    o_ref[...] = jnp.concatenate([acc / l for _, l, acc in (st0, st1)], axis=0).T.astype(o_ref.dtype)


def _prompt_attn(qa, ka, va, *, n_seq, seq_len, tq, tk):
    nq = seq_len // tq
    return pl.pallas_call(
        functools.partial(_prompt_attn_kernel, tq=tq, tk=tk),
        grid=(n_seq, MLA_HEADS // 2, nq),
        in_specs=[pl.BlockSpec((tq, 2 * HEAD_SLAB), lambda b, hp, i: (b * nq + i, hp)),
                  pl.BlockSpec((seq_len, 2 * HEAD_SLAB), lambda b, hp, i: (b, hp)),
                  pl.BlockSpec((2 * MLA_V, seq_len), lambda b, hp, i: (hp, b))],
        out_specs=pl.BlockSpec((tq, 2 * MLA_V), lambda b, hp, i: (b * nq + i, hp)),
        out_shape=jax.ShapeDtypeStruct((n_seq * seq_len, MLA_VALL), BF16),
        scratch_shapes=[pltpu.VMEM((2, tk, tq), F32)],
        compiler_params=pltpu.CompilerParams(dimension_semantics=("parallel", "parallel", "arbitrary"),
                                             vmem_limit_bytes=VMEM_LIMIT),
        name="prompt_attn",
    )(qa, ka, va)


def _sample_attn_kernel(pt_ref, qabs_ref, qrope_ref, cnew_ref, krnew_ref, wkT_ref, wv_ref, *rest,
                        n_tok, pages_per_step):
    G = pages_per_step
    c_refs = rest[:G]
    r_refs = rest[G:2 * G]
    o_ref, lhs_scr, m_scr, l_scr, acc_scr, cb_scr, kb_scr = rest[2 * G:]
    g = pl.program_id(1)
    rows = n_tok * MLA_HEADS
    n_k = MLA_HEADS * MLA_NOPE

    def attend(cb, kb, mask):
        nk = cb.shape[0]
        big = _dot_nt(lhs_scr[...], cb)
        kvn = big[:n_k]
        s_nope = big[n_k:n_k + rows]
        ss = jnp.sum((kvn * kvn).reshape(MLA_HEADS, MLA_NOPE, nk), axis=1)
        inv = lax.rsqrt(ss * (1.0 / MLA_NOPE) + EPS)
        s_rope = _dot_nt(qrope_ref[0], kb)
        s = (s_nope.reshape(n_tok, MLA_HEADS, nk) * inv[None]).reshape(rows, nk) + s_rope
        if mask is not None:
            s = jnp.where(mask, s, NEG)
        m = m_scr[...]
        m_new = jnp.maximum(m, jnp.max(s, axis=-1, keepdims=True))
        alpha = jnp.exp2(m - m_new)
        p = jnp.exp2(s - m_new)
        l_scr[...] = alpha * l_scr[...] + jnp.sum(p, axis=-1, keepdims=True)
        acc_scr[...] = alpha * acc_scr[...] + _dot(p.astype(BF16), cb)
        m_scr[...] = m_new

    @pl.when(g == 0)
    def _():
        lhs_scr[:n_k, :] = wkT_ref[...]
        lhs_scr[n_k:n_k + rows, :] = qabs_ref[0]
        m_scr[...] = jnp.full(m_scr.shape, NEG, F32)
        l_scr[...] = jnp.zeros(l_scr.shape, F32)
        acc_scr[...] = jnp.zeros(acc_scr.shape, F32)
        c_new = jnp.concatenate([cnew_ref[0], jnp.zeros((PAGE_SIZE - n_tok, MLA_KV_LORA), F32)], axis=0)
        kr_new = jnp.concatenate([krnew_ref[0], jnp.zeros((PAGE_SIZE - n_tok, MLA_ROPE), F32)], axis=0)
        key = lax.broadcasted_iota(jnp.int32, (rows, PAGE_SIZE), 1)
        tok = lax.broadcasted_iota(jnp.int32, (rows, PAGE_SIZE), 0) // MLA_HEADS
        attend(c_new.astype(BF16), kr_new.astype(BF16), key <= tok)

    for j in range(G):
        cb_scr[j * PAGE_SIZE:(j + 1) * PAGE_SIZE, :] = c_refs[j][0, 0].astype(BF16)
        kb_scr[j * PAGE_SIZE:(j + 1) * PAGE_SIZE, :] = r_refs[j][0, 0].astype(BF16)
    attend(cb_scr[...], kb_scr[...], None)

    @pl.when(g == pl.num_programs(1) - 1)
    def _():
        o_lat = (acc_scr[...] / l_scr[...]).astype(BF16)
        full = _dot(o_lat, wv_ref[...])
        full = full.reshape(n_tok, MLA_HEADS, MLA_VALL)
        hr = lax.broadcasted_iota(jnp.int32, (MLA_HEADS, MLA_VALL), 0)
        hc = lax.broadcasted_iota(jnp.int32, (MLA_HEADS, MLA_VALL), 1) // MLA_V
        o_ref[...] = jnp.sum(jnp.where((hr == hc)[None], full, 0.0), axis=1).astype(o_ref.dtype)


def _sample_attn(page_table, qabs, qrope, c_new, kr_new, w_kT, w_v, pool_c, pool_r, layer, *, pages_per_step):
    n_b, n_pages = page_table.shape
    n_tok = c_new.shape[1]
    rows = n_tok * MLA_HEADS
    G = pages_per_step
    n_k = MLA_HEADS * MLA_NOPE

    def page_spec(width, j):
        return pl.BlockSpec((1, 1, PAGE_SIZE, width), lambda b, g, pt: (layer, pt[b, g * G + j], 0, 0))

    per_b = lambda b, g, pt: (b, 0, 0)
    grid_spec = pltpu.PrefetchScalarGridSpec(
        num_scalar_prefetch=1,
        grid=(n_b, n_pages // G),
        in_specs=[pl.BlockSpec((1, rows, MLA_KV_LORA), per_b), pl.BlockSpec((1, rows, MLA_ROPE), per_b),
                  pl.BlockSpec((1, n_tok, MLA_KV_LORA), per_b), pl.BlockSpec((1, n_tok, MLA_ROPE), per_b),
                  pl.BlockSpec((n_k, MLA_KV_LORA), lambda b, g, pt: (0, 0)),
                  pl.BlockSpec((MLA_KV_LORA, MLA_VALL), lambda b, g, pt: (0, 0))]
                 + [page_spec(MLA_KV_LORA, j) for j in range(G)]
                 + [page_spec(MLA_ROPE, j) for j in range(G)],
        out_specs=pl.BlockSpec((n_tok, MLA_VALL), lambda b, g, pt: (b, 0)),
        scratch_shapes=[pltpu.VMEM((n_k + rows, MLA_KV_LORA), BF16), pltpu.VMEM((rows, 1), F32),
                        pltpu.VMEM((rows, 1), F32), pltpu.VMEM((rows, MLA_KV_LORA), F32),
                        pltpu.VMEM((G * PAGE_SIZE, MLA_KV_LORA), BF16), pltpu.VMEM((G * PAGE_SIZE, MLA_ROPE), BF16)])
    return pl.pallas_call(
        functools.partial(_sample_attn_kernel, n_tok=n_tok, pages_per_step=G),
        grid_spec=grid_spec,
        out_shape=jax.ShapeDtypeStruct((n_b * n_tok, MLA_VALL), BF16),
        compiler_params=pltpu.CompilerParams(dimension_semantics=("parallel", "arbitrary"),
                                             vmem_limit_bytes=VMEM_LIMIT),
        name="sample_attn",
    )(page_table, qabs, qrope, c_new, kr_new, w_kT, w_v, *([pool_c] * G), *([pool_r] * G))


def _post_kernel(x_ref, og_ref, om_ref, p_ref, wo_ref, gffn_ref, wg_ref, wu_ref, cw_ref, cb_ref, wd_ref,
                 gple_ref, wpg_ref, wpp_ref, *rest, tiles_per_seq, seq_rows):
    paged_prev = seq_rows is not None
    if paged_prev:
        p1_ref, p2_ref, y_ref, tail_ref, acc_scr = rest
    else:
        y_ref, tail_ref, acc_scr, carry_scr = rest
    tm = x_ref.shape[0]
    i = pl.program_id(0)
    h1 = x_ref[...] + _dot(og_ref[...], wo_ref[:GLA_V, :]) + _dot(om_ref[...], wo_ref[GLA_V:, :])
    n2 = _rms(h1, gffn_ref[...]).astype(BF16)
    row = lax.broadcasted_iota(jnp.int32, (tm, FFN_CHUNK), 0)
    if not paged_prev:
        @pl.when(i % tiles_per_seq == 0)
        def _():
            carry_scr[...] = jnp.zeros(carry_scr.shape, F32)

    acc_scr[...] = jnp.zeros(acc_scr.shape, F32)
    for f in range(0, D_FF, FFN_CHUNK):
        cols = slice(f, f + FFN_CHUNK)
        a = _dot(n2, wg_ref[:, cols])
        up = _dot(n2, wu_ref[:, cols])
        r1 = pltpu.roll(a, 1, axis=0)
        r2 = pltpu.roll(a, 2, axis=0)
        if paged_prev:
            t = row % seq_rows
            a1 = jnp.where(t >= 1, r1, p1_ref[:, cols])
            a2 = jnp.where(t >= 2, r2, p2_ref[:, cols])
        else:
            prev = carry_scr[:, cols]
            pm1 = prev[SUBLANES - 1:SUBLANES, :]
            pm2 = prev[SUBLANES - 2:SUBLANES - 1, :]
            a1 = jnp.where(row >= 1, r1, pm1)
            a2 = jnp.where(row >= 2, r2, jnp.where(row == 0, pm2, pm1))
            carry_scr[:, cols] = a[tm - SUBLANES:, :]
        tail_ref[:, cols] = a[tm - SUBLANES:, :] if not paged_prev else a
        conv = cb_ref[:, cols] + cw_ref[0:1, cols] * a2 + cw_ref[1:2, cols] * a1 + cw_ref[2:3, cols] * a
        gact = (conv / (1.0 + jnp.exp(-conv)) * up).astype(BF16)
        acc_scr[...] += _dot(gact, wd_ref[cols, :])
    h2 = h1 + acc_scr[...]
    n3 = _rms(h2, gple_ref[...]).astype(BF16)
    gate = 1.0 / (1.0 + jnp.exp(-_dot(n3, wpg_ref[...])))
    y_ref[...] = h2 + _dot(p_ref[...].astype(BF16), wpp_ref[...]) * gate


def _post(x2d, og, om, p2d, w, prev=None, *, tm, tiles_per_seq, seq_rows):
    T = x2d.shape[0]
    nt = T // tm
    row = lambda i: (i, 0)
    consts = [w['w_o'], w['g_ffn'], w['w_gate'], w['w_up'], w['conv_w'], w['conv_b'], w['w_down'],
              w['g_ple'], w['w_pgate'], w['w_pproj']]
    in_specs = [pl.BlockSpec((tm, D_MODEL), row), pl.BlockSpec((tm, GLA_V), row),
                pl.BlockSpec((tm, MLA_VALL), row), pl.BlockSpec((tm, PLE_DIM), row)]
    in_specs += [_const_spec(c.shape) for c in consts]
    args = [x2d, og, om, p2d, *consts]
    scratch = [pltpu.VMEM((tm, D_MODEL), F32)]
    if prev is not None:
        in_specs += [pl.BlockSpec((tm, D_FF), row)] * 2
        args += list(prev)
        tail_rows = tm
    else:
        scratch.append(pltpu.VMEM((SUBLANES, D_FF), F32))
        tail_rows = SUBLANES
    return pl.pallas_call(
        functools.partial(_post_kernel, tiles_per_seq=tiles_per_seq, seq_rows=seq_rows),
        grid=(nt,),
        in_specs=in_specs,
        out_specs=[pl.BlockSpec((tm, D_MODEL), row), pl.BlockSpec((tail_rows, D_FF), row)],
        out_shape=[jax.ShapeDtypeStruct((T, D_MODEL), F32), jax.ShapeDtypeStruct((nt * tail_rows, D_FF), F32)],
        scratch_shapes=scratch,
        compiler_params=pltpu.CompilerParams(dimension_semantics=("arbitrary",), vmem_limit_bytes=VMEM_LIMIT),
        name="post_sample" if prev is not None else "post_prompt",
    )(*args)


def _prep_weights(g_mix, w_in, gla_w_a2, gla_b_a, gla_g_out, mla_g_qa, mla_w_qup, mla_g_qn, mla_g_qr,
                  mla_g_kva, mla_g_kr, mla_w_kvup, mla_g_kn, w_o, g_ffn, ffn_w_gate, ffn_w_up, ffn_conv_w,
                  ffn_conv_b, ffn_w_down, g_ple, ple_w_gate, ple_w_proj):
    sizes = (GLA_QK, GLA_QK, GLA_V, GLA_V, GLA_GATE_RANK, MLA_Q_LORA, MLA_KV_LORA, MLA_ROPE)
    offs = np.concatenate([[0], np.cumsum(sizes)])
    piece = lambda i: w_in[:, offs[i]:offs[i + 1]]
    zeros = lambda n: jnp.zeros((D_MODEL, n), w_in.dtype)
    misc = jnp.concatenate([piece(4), zeros(ROPE_LO - GLA_GATE_RANK), piece(7),
                            zeros(LANES - ROPE_LO - MLA_ROPE)], axis=1)
    w_in_p = jnp.concatenate([piece(0), piece(1), piece(2), piece(3), piece(5), piece(6), misc], axis=1)
    w_a2 = jnp.concatenate([gla_w_a2, jnp.zeros((LANES - GLA_GATE_RANK, GLA_QK), gla_w_a2.dtype)], axis=0)

    def slab_vec(nope, rope_):
        one = jnp.concatenate([nope, rope_, jnp.zeros((HEAD_SLAB - MLA_NOPE - MLA_ROPE,), F32)])
        return jnp.tile(one, MLA_HEADS)[None, :]

    wq = mla_w_qup.reshape(MLA_Q_LORA, MLA_HEADS, MLA_NOPE + MLA_ROPE)
    wq = jnp.pad(wq, ((0, 0), (0, 0), (0, HEAD_SLAB - MLA_NOPE - MLA_ROPE))).reshape(MLA_Q_LORA, MLA_SLABS)
    wkv = mla_w_kvup.reshape(MLA_KV_LORA, MLA_HEADS, MLA_NOPE + MLA_V)
    wk = wkv[:, :, :MLA_NOPE]
    wv = wkv[:, :, MLA_NOPE:].reshape(MLA_KV_LORA, MLA_VALL)
    wk_slab = jnp.pad(wk, ((0, 0), (0, 0), (0, HEAD_SLAB - MLA_NOPE))).reshape(MLA_KV_LORA, MLA_SLABS)
    wk_g = jnp.pad(wk * mla_g_kn[None, None, :], ((0, 0), (0, 0), (0, HEAD_SLAB - MLA_NOPE)))
    eye = jnp.eye(MLA_HEADS, dtype=F32)
    w_kabs = jnp.einsum('nhd,hg->hdgn', wk_g, eye).reshape(MLA_SLABS, MLA_HEADS * MLA_KV_LORA)
    w_kT = wk.transpose(1, 2, 0).reshape(MLA_HEADS * MLA_NOPE, MLA_KV_LORA)

    lane = np.arange(2 * HEAD_SLAB)
    seg = np.where(lane % HEAD_SLAB < MLA_NOPE, 0, np.where(lane % HEAD_SLAB < MLA_NOPE + MLA_ROPE, 1, 2))
    same = (lane[:, None] // HEAD_SLAB == lane[None, :] // HEAD_SLAB) & (seg[:, None] == seg[None, :])
    segm = np.where(same & (seg[:, None] == 0), 1.0 / MLA_NOPE, np.where(same & (seg[:, None] == 1), 1.0 / MLA_ROPE, 0.0))

    gkr = jnp.concatenate([jnp.zeros((ROPE_LO,), F32), mla_g_kr, jnp.zeros((LANES - ROPE_LO - MLA_ROPE,), F32)])
    return dict(
        g_mix=g_mix[None, :], w_in=w_in_p.astype(BF16), w_a2=w_a2.astype(BF16), b_a=gla_b_a[None, :],
        g_out=gla_g_out[None, :], g_qa=mla_g_qa[None, :], w_qup=wq.astype(BF16), segm=jnp.asarray(segm, BF16),
        gq_slab=slab_vec(mla_g_qn, mla_g_qr), g_kva=mla_g_kva[None, :], w_kslab=wk_slab.astype(BF16),
        w_vT=wv.T.astype(BF16),
        gk_slab=slab_vec(mla_g_kn, jnp.zeros((MLA_ROPE,), F32)),
        g_kn=mla_g_kn, gkr_slab=gkr[None, :], w_kabs=w_kabs.astype(BF16), w_kT=w_kT.astype(BF16),
        w_v=wv.astype(BF16), w_o=w_o.astype(BF16), g_ffn=g_ffn[None, :], w_gate=ffn_w_gate.astype(BF16),
        w_up=ffn_w_up.astype(BF16), conv_w=ffn_conv_w, conv_b=ffn_conv_b[None, :],
        w_down=ffn_w_down.astype(BF16), g_ple=g_ple[None, :], w_pgate=ple_w_gate.astype(BF16),
        w_pproj=ple_w_proj.astype(BF16))


def _rope_tables(pos):
    inv = ROPE_THETA ** (-jnp.arange(ROPE_HALF, dtype=F32) * 2.0 / MLA_ROPE)
    ang = pos.astype(F32)[:, None] * inv[None, :]
    cos, sin = jnp.cos(ang), jnp.sin(ang)
    T = pos.shape[0]
    pad = jnp.zeros((T, HEAD_SLAB - MLA_NOPE - MLA_ROPE), F32)
    cos_t = jnp.concatenate([jnp.ones((T, MLA_NOPE), F32), cos, cos, pad], axis=1)
    sin_t = jnp.concatenate([jnp.zeros((T, MLA_NOPE), F32), -sin, sin, pad], axis=1)
    return cos_t, sin_t


def _pick_tile(n, cap):
    t = math.gcd(n, cap)
    return t


def _state_to_bd(s):
    eye = jnp.eye(GLA_HEADS, dtype=s.dtype)
    return jnp.einsum('bhdv,hg->bhvgd', s, eye).reshape(s.shape[0], GLA_V, GLA_QK)


def _layer(w, x_p, x_s, p_p, p_s, pool_c, pool_r, layer, state_gla, state_conv, page_table):
    B, S, _ = x_p.shape
    Bd, Td, _ = x_s.shape
    tm = _pick_tile(S, 512)
    tps = S // tm
    cos_p, sin_p = _rope_tables(jnp.arange(S))
    xp2 = x_p.reshape(B * S, D_MODEL)
    gq, gk, gv, gg, la, qa, ckv_p, kr_p, ka, vaT = _in_proj(xp2, cos_p, sin_p, w, tm=tm, tiles_per_seq=tps,
                                                            sample=False)
    chunk_p = math.gcd(S, GLA_CHUNK)
    n_chunks = math.gcd(S // chunk_p, 8)
    og_p, sT_p = _gla(gq, gk, gv, gg, la, jnp.zeros((B, GLA_V, GLA_QK), F32), w['g_out'],
                      n_seq=B, seq_len=S, chunk=chunk_p, n_chunks=n_chunks)
    tq = _pick_tile(S, 512)
    om_p = _prompt_attn(qa, ka, vaT, n_seq=B, seq_len=S, tq=tq, tk=tq)
    y_p, tail_p = _post(xp2, og_p, om_p, p_p.reshape(B * S, PLE_DIM), w, tm=tm, tiles_per_seq=tps, seq_rows=None)
    conv_p = tail_p.reshape(B, tps, SUBLANES, D_FF)[:, -1, SUBLANES - (CONV_W - 1):, :]

    Ts = Bd * Td
    pos_s = PAST_LEN + jnp.arange(Td)
    cos_s, sin_s = _rope_tables(jnp.tile(pos_s, Bd))
    xs2 = x_s.reshape(Ts, D_MODEL)
    gq, gk, gv, gg, la, qa, ckv_s, kr_s, qabs = _in_proj(xs2, cos_s, sin_s, w, tm=Ts, tiles_per_seq=1, sample=True)
    chunk_s = math.gcd(Td, GLA_CHUNK)
    og_s, sT_s = _gla(gq, gk, gv, gg, la, _state_to_bd(state_gla), w['g_out'],
                      n_seq=Bd, seq_len=Td, chunk=chunk_s, n_chunks=Td // chunk_s)
    rows = Td * MLA_HEADS
    qrope = qa.reshape(Ts, MLA_HEADS, HEAD_SLAB)[:, :, ROPE_LO:ROPE_LO + MLA_ROPE].reshape(Bd, rows, MLA_ROPE)
    om_s = _sample_attn(page_table, qabs.reshape(Bd, rows, MLA_KV_LORA), qrope,
                        ckv_s.reshape(Bd, Td, MLA_KV_LORA), kr_s.reshape(Bd, Td, MLA_ROPE),
                        w['w_kT'], w['w_v'], pool_c, pool_r, layer,
                        pages_per_step=math.gcd(page_table.shape[1], SAMPLE_PAGES_PER_STEP))
    zpad = lambda a, lo: jnp.pad(a, ((0, 0), (lo, Td - lo - a.shape[1]), (0, 0))).reshape(Ts, D_FF)
    prev1 = zpad(state_conv[:, 1:2], 0)
    prev2 = zpad(state_conv, 0)
    y_s, a_s = _post(xs2, og_s, om_s, p_s.reshape(Ts, PLE_DIM), w, prev=(prev1, prev2), tm=Ts, tiles_per_seq=1,
                     seq_rows=Td)
    full = jnp.concatenate([state_conv, a_s.reshape(Bd, Td, D_FF)], axis=1)
    conv_s = full[:, full.shape[1] - (CONV_W - 1):]
    return (y_p.reshape(B, S, D_MODEL), y_s.reshape(Bd, Td, D_MODEL),
            ckv_p.reshape(B, S, MLA_KV_LORA), kr_p.reshape(B, S, MLA_ROPE),
            sT_p.transpose(0, 1, 3, 2), conv_p,
            ckv_s.reshape(Bd, Td, MLA_KV_LORA), kr_s.reshape(Bd, Td, MLA_ROPE),
            sT_s.transpose(0, 1, 3, 2), conv_s)


def kernel(x_prompt, x_sample, cache_ckv, cache_krope, state_gla, state_conv, page_table, p_prompt, p_sample, g_mix, w_in, gla_w_a2, gla_b_a, gla_g_out, mla_g_qa, mla_w_qup, mla_g_qn, mla_g_qr, mla_g_kva, mla_g_kr, mla_w_kvup, mla_g_kn, w_o, g_ffn, ffn_w_gate, ffn_w_up, ffn_conv_w, ffn_conv_b, ffn_w_down, g_ple, ple_w_gate, ple_w_proj):
    depth = w_in.shape[0]
    per_layer = (g_mix, w_in, gla_w_a2, gla_b_a, gla_g_out, mla_g_qa, mla_w_qup, mla_g_qn, mla_g_qr, mla_g_kva,
                 mla_g_kr, mla_w_kvup, mla_g_kn, w_o, g_ffn, ffn_w_gate, ffn_w_up, ffn_conv_w, ffn_conv_b,
                 ffn_w_down, g_ple, ple_w_gate, ple_w_proj)
    y_p, y_s = x_prompt, x_sample
    outs = [[] for _ in range(8)]
    for i in range(depth):
        w = _prep_weights(*(a[i] for a in per_layer))
        res = _layer(w, y_p, y_s, p_prompt[i], p_sample[i], cache_ckv, cache_krope, i, state_gla[i],
                     state_conv[i], page_table)
        y_p, y_s = res[0], res[1]
        for lst, r in zip(outs, res[2:]):
            lst.append(r)
    return (y_p, y_s) + tuple(jnp.stack(lst) for lst in outs)
```

```python
import functools
import math

import jax
import jax.numpy as jnp
import numpy as np
from jax import lax
from jax.experimental import pallas as pl
from jax.experimental.pallas import tpu as pltpu

D_MODEL = 1024
PAST_LEN = 16384
PAGE_SIZE = 128
GLA_HEADS = 4
GLA_DK = 64
GLA_DV = 128
GLA_GATE_RANK = 16
GLA_GATE_NORM = 16.0
GLA_CHUNK = 32
MLA_HEADS = 8
MLA_Q_LORA = 256
MLA_KV_LORA = 128
MLA_NOPE = 64
MLA_ROPE = 32
MLA_V = 64
MLA_SCALE = (MLA_NOPE + MLA_ROPE) ** -0.5
LOG2E = math.log2(math.e)
ROPE_THETA = 10000.0
D_FF = 2816
CONV_W = 3
PLE_DIM = 256
EPS = 1e-6
NEG = -1e30

LANES = 128
SUBLANES = 8
HEAD_SLAB = LANES
ROPE_LO = MLA_NOPE
ROPE_HALF = MLA_ROPE // 2
GLA_QK = GLA_HEADS * GLA_DK
GLA_V = GLA_HEADS * GLA_DV
MLA_SLABS = MLA_HEADS * HEAD_SLAB
MLA_VALL = MLA_HEADS * MLA_V
C_GQ, C_GK, C_GV, C_GG = 0, 256, 512, 1024
C_MQ, C_MKV, C_MISC = 1536, 1792, 1920
IN_COLS_P = 2048
FFN_CHUNK = 256
SAMPLE_PAGES_PER_STEP = 16
VMEM_LIMIT = 56 * 1024 * 1024

BF16 = jnp.bfloat16
F32 = jnp.float32


def _dot(a, b):
    return jnp.dot(a, b, preferred_element_type=F32)


def _dot_nt(a, b):
    return lax.dot_general(a, b, (((1,), (1,)), ((), ())), preferred_element_type=F32)


def _dot_tn(a, b):
    return lax.dot_general(a, b, (((0,), (0,)), ((), ())), preferred_element_type=F32)


def _rms(x, g):
    return x * lax.rsqrt(jnp.mean(x * x, axis=-1, keepdims=True) + EPS) * g


def _const_spec(shape):
    nd = len(shape)
    return pl.BlockSpec(shape, lambda *_: (0,) * nd, pipeline_mode=pl.Buffered(1))


def _rope_slab(y, cos_t, sin_t):
    lane = lax.broadcasted_iota(jnp.int32, y.shape, 1)
    swapped = jnp.where(lane < ROPE_LO + ROPE_HALF,
                        pltpu.roll(y, LANES - ROPE_HALF, axis=1),
                        pltpu.roll(y, ROPE_HALF, axis=1))
    return y * cos_t + swapped * sin_t


def _in_proj_kernel(x_ref, cos_ref, sin_ref, gmix_ref, win_ref, wa2_ref, ba_ref, gqa_ref, wqup_ref,
                    segm_ref, gq_ref, gkva_ref, gkr_ref, *rest, sample):
    if sample:
        (wkabs_ref, gq_o, gk_o, gv_o, gg_o, la_o, qa_o, ckv_o, kr_o, qabs_o) = rest
    else:
        (wk_ref, gk_ref, wvT_ref, gq_o, gk_o, gv_o, gg_o, la_o, qa_o, ckv_o, kr_o, ka_o, vaT_o) = rest
    x = x_ref[...]
    n = _rms(x, gmix_ref[...]).astype(BF16)
    u = _dot(n, win_ref[...])
    gq_o[...] = u[:, C_GQ:C_GQ + GLA_QK] * (GLA_DK ** -0.5)
    gk_o[...] = u[:, C_GK:C_GK + GLA_QK]
    gv_o[...] = u[:, C_GV:C_GV + GLA_V].astype(BF16)
    gg_o[...] = u[:, C_GG:C_GG + GLA_V]
    misc = u[:, C_MISC:C_MISC + LANES]
    z = _dot(misc.astype(BF16), wa2_ref[...]) + ba_ref[...]
    la_o[...] = (jnp.minimum(z, 0.0) - jnp.log1p(jnp.exp(-jnp.abs(z)))) * (1.0 / GLA_GATE_NORM)

    cos_t = cos_ref[...]
    sin_t = sin_ref[...]
    segm = segm_ref[...]

    lane = lax.broadcasted_iota(jnp.int32, misc.shape, 1)
    is_rope = (lane >= ROPE_LO) & (lane < ROPE_LO + MLA_ROPE)
    ms_r = jnp.sum(jnp.where(is_rope, misc * misc, 0.0), axis=-1, keepdims=True) * (1.0 / MLA_ROPE)
    kr_slab = _rope_slab(misc * lax.rsqrt(ms_r + EPS) * gkr_ref[...], cos_t, sin_t)
    kr_o[...] = kr_slab[:, ROPE_LO:ROPE_LO + MLA_ROPE]

    mkv = u[:, C_MKV:C_MKV + MLA_KV_LORA]
    ckv = _rms(mkv, gkva_ref[...])
    ckv_o[...] = ckv
    ckv_b = ckv.astype(BF16)
    if not sample:
        kn = _dot(ckv_b, wk_ref[...])
        vaT_o[...] = _dot_nt(wvT_ref[...], ckv_b).astype(BF16)

    cq = _rms(u[:, C_MQ:C_MQ + MLA_Q_LORA], gqa_ref[...]).astype(BF16)
    qf = _dot(cq, wqup_ref[...])
    q_gain = gq_ref[...]
    q_cos = cos_t * (q_gain[0:1] * (MLA_SCALE * LOG2E))
    q_sin = sin_t * (q_gain[1:2] * (MLA_SCALE * LOG2E))

    for pair in range(MLA_HEADS // 2):
        lo = pair * 2 * HEAD_SLAB
        q2 = qf[:, lo:lo + 2 * HEAD_SLAB]
        q_inv = lax.rsqrt(_dot((q2 * q2).astype(BF16), segm) + EPS)
        if not sample:
            k2 = kn[:, lo:lo + 2 * HEAD_SLAB]
            k2 = k2 * lax.rsqrt(_dot((k2 * k2).astype(BF16), segm) + EPS) * gk_ref[:, lo:lo + 2 * HEAD_SLAB]
        for j in range(2):
            sl = slice(lo + j * HEAD_SLAB, lo + (j + 1) * HEAD_SLAB)
            swapped = qf[:, MLA_SLABS + lo + j * HEAD_SLAB:MLA_SLABS + lo + (j + 1) * HEAD_SLAB]
            qh = q_inv[:, j * HEAD_SLAB:(j + 1) * HEAD_SLAB] * (qf[:, sl] * q_cos + swapped * q_sin)
            qa_o[:, sl] = qh.astype(BF16)
            if not sample:
                ka_o[:, sl] = (k2[:, j * HEAD_SLAB:(j + 1) * HEAD_SLAB] + kr_slab).astype(BF16)
    if sample:
        qabs_o[...] = _dot(qa_o[...], wkabs_ref[...]).astype(BF16)


def _in_proj(x2d, cos_t, sin_t, w, *, tm, tiles_per_seq, sample):
    T = x2d.shape[0]
    nt = T // tm
    row = lambda i: (i, 0)
    pos = lambda i: (i % tiles_per_seq, 0)
    consts = [w['g_mix'], w['w_in'], w['w_a2'], w['b_a'], w['g_qa'], w['w_qup'], w['segm'], w['gq_slab'],
              w['g_kva'], w['gkr_slab']]
    consts += [w['w_kabs']] if sample else [w['w_kslab'], w['gk_slab'], w['w_vT']]
    in_specs = [pl.BlockSpec((tm, D_MODEL), row), pl.BlockSpec((tm, LANES), pos), pl.BlockSpec((tm, LANES), pos)]
    in_specs += [_const_spec(c.shape) for c in consts]
    outs = [(GLA_QK, F32), (GLA_QK, F32), (GLA_V, BF16), (GLA_V, F32), (GLA_QK, F32),
            (MLA_SLABS, BF16), (MLA_KV_LORA, F32), (MLA_ROPE, F32), (MLA_SLABS, BF16)]
    out_specs = [pl.BlockSpec((tm, c), row) for c, _ in outs]
    out_shape = [jax.ShapeDtypeStruct((T, c), d) for c, d in outs]
    if not sample:
        out_specs.append(pl.BlockSpec((MLA_VALL, tm), lambda i: (0, i)))
        out_shape.append(jax.ShapeDtypeStruct((MLA_VALL, T), BF16))
    return pl.pallas_call(
        functools.partial(_in_proj_kernel, sample=sample),
        grid=(nt,),
        in_specs=in_specs,
        out_specs=out_specs,
        out_shape=out_shape,
        compiler_params=pltpu.CompilerParams(dimension_semantics=("parallel",), vmem_limit_bytes=VMEM_LIMIT),
        name="in_proj_sample" if sample else "in_proj_prompt",
    )(x2d, cos_t, sin_t, *consts)


def _gla_kernel(q_ref, k_ref, v_ref, gg_ref, la_ref, s0_ref, tril_ref, ones_ref, gout_ref,
                o_ref, sT_ref, st_scr, *, chunk, n_chunks):
    step = pl.program_id(1)

    @pl.when(step == 0)
    def _():
        st_scr[...] = s0_ref[0]

    tc = chunk * n_chunks
    la = la_ref[...]
    la_hi = la.astype(BF16)
    la_lo = (la - la_hi.astype(F32)).astype(BF16)
    tril = tril_ref[...]
    ones = ones_ref[...]
    b = _dot(tril, la_hi) + _dot(tril, la_lo)
    bl = _dot(ones, la_hi) + _dot(ones, la_lo)
    q = q_ref[...]
    k = k_ref[...]
    v = v_ref[...]
    q_dec = (q * jnp.exp(b)).astype(BF16)
    k_inv = (k * jnp.exp(-b)).astype(BF16)
    k_end = (k * jnp.exp(bl - b)).astype(BF16)
    dec = jnp.exp(bl)

    r = lax.broadcasted_iota(jnp.int32, (tc, tc), 0)
    c = lax.broadcasted_iota(jnp.int32, (tc, tc), 1)
    causal = (r >= c) & ((r // chunk) == (c // chunk))
    sr = lax.broadcasted_iota(jnp.int32, (GLA_V, GLA_QK), 0) // GLA_DV
    sc = lax.broadcasted_iota(jnp.int32, (GLA_V, GLA_QK), 1) // GLA_DK
    head_diag = sr == sc

    o_intra = []
    for h in range(GLA_HEADS):
        qh = q_dec[:, h * GLA_DK:(h + 1) * GLA_DK]
        kh = k_inv[:, h * GLA_DK:(h + 1) * GLA_DK]
        a = jnp.where(causal, _dot_nt(qh, kh), 0.0).astype(BF16)
        o_intra.append(_dot(a, v[:, h * GLA_DV:(h + 1) * GLA_DV]))
    o_intra = jnp.concatenate(o_intra, axis=1)

    st = st_scr[...]
    o_inter = []
    for ci in range(n_chunks):
        rows = slice(ci * chunk, (ci + 1) * chunk)
        o_inter.append(_dot_nt(q_dec[rows], st.astype(BF16)))
        xt = _dot_tn(v[rows], k_end[rows])
        st = st * dec[ci * chunk:ci * chunk + 1, :] + jnp.where(head_diag, xt, 0.0)
    st_scr[...] = st
    o = o_intra + (jnp.concatenate(o_inter, axis=0) if n_chunks > 1 else o_inter[0])

    gg = gg_ref[...]
    gate = gg / (1.0 + jnp.exp(-gg))
    gout = gout_ref[...]
    for h in range(GLA_HEADS):
        sl = slice(h * GLA_DV, (h + 1) * GLA_DV)
        o_ref[:, sl] = (_rms(o[:, sl], gout) * gate[:, sl]).astype(o_ref.dtype)

    @pl.when(step == pl.num_programs(1) - 1)
    def _():
        for h in range(GLA_HEADS):
            sT_ref[0, h] = st[h * GLA_DV:(h + 1) * GLA_DV, h * GLA_DK:(h + 1) * GLA_DK]


def _gla(q, k, v, gg, la, s0_bd, g_out, *, n_seq, seq_len, chunk, n_chunks):
    tc = chunk * n_chunks
    steps = seq_len // tc
    idx = np.arange(tc)
    same = (idx[:, None] // chunk) == (idx[None, :] // chunk)
    tril = jnp.asarray(same & (idx[:, None] >= idx[None, :]), BF16)
    ones = jnp.asarray(same, BF16)
    row = lambda b, s: (b * steps + s, 0)
    return pl.pallas_call(
        functools.partial(_gla_kernel, chunk=chunk, n_chunks=n_chunks),
        grid=(n_seq, steps),
        in_specs=[pl.BlockSpec((tc, GLA_QK), row), pl.BlockSpec((tc, GLA_QK), row), pl.BlockSpec((tc, GLA_V), row),
                  pl.BlockSpec((tc, GLA_V), row), pl.BlockSpec((tc, GLA_QK), row),
                  pl.BlockSpec((1, GLA_V, GLA_QK), lambda b, s: (b, 0, 0)),
                  _const_spec((tc, tc)), _const_spec((tc, tc)), _const_spec((1, GLA_DV))],
        out_specs=[pl.BlockSpec((tc, GLA_V), row),
                   pl.BlockSpec((1, GLA_HEADS, GLA_DV, GLA_DK), lambda b, s: (b, 0, 0, 0))],
        out_shape=[jax.ShapeDtypeStruct((n_seq * seq_len, GLA_V), BF16),
                   jax.ShapeDtypeStruct((n_seq, GLA_HEADS, GLA_DV, GLA_DK), F32)],
        scratch_shapes=[pltpu.VMEM((GLA_V, GLA_QK), F32)],
        compiler_params=pltpu.CompilerParams(dimension_semantics=("parallel", "arbitrary"),
                                             vmem_limit_bytes=VMEM_LIMIT),
        name=f"gla_c{chunk}",
    )(q, k, v, gg, la, s0_bd, tril, ones, g_out)


def _prompt_attn_kernel(q_ref, k_ref, vT_ref, o_ref, s_scr, *, tq, tk):
    assert tq == tk
    qi = pl.program_id(2)
    key = lax.broadcasted_iota(jnp.int32, (tk, tq), 0)
    qry = lax.broadcasted_iota(jnp.int32, (tk, tq), 1)

    def scores(j, kb, masked):
        ks = pl.multiple_of(kb * tk, tk)
        s = _dot_nt(k_ref[pl.ds(ks, tk), j * HEAD_SLAB:(j + 1) * HEAD_SLAB],
                    q_ref[:, j * HEAD_SLAB:(j + 1) * HEAD_SLAB])
        if masked:
            s = jnp.where(key <= qry, s, NEG)
        s_scr[j] = s
        return jnp.max(s, axis=0, keepdims=True)

    def update(j, kb, m_blk, state):
        m, l, acc = state
        ks = pl.multiple_of(kb * tk, tk)
        m_new = jnp.maximum(m, m_blk)
        alpha = jnp.exp2(m - m_new)
        p = jnp.exp2(s_scr[j] - m_new)
        l = alpha * l + jnp.sum(p, axis=0, keepdims=True)
        acc = alpha * acc + _dot(vT_ref[j * MLA_V:(j + 1) * MLA_V, pl.ds(ks, tk)], p.astype(BF16))
        return m_new, l, acc

    def body(t, carry):
        blk1, st0, st1 = carry
        kb = qi - 1 - t
        blk0 = scores(0, kb, False)
        st1 = update(1, kb + 1, blk1, st1)
        blk1 = scores(1, kb, False)
        st0 = update(0, kb, blk0, st0)
        return blk1, st0, st1

    init = (jnp.full((1, tq), NEG, F32), jnp.zeros((1, tq), F32), jnp.zeros((MLA_V, tq), F32))
    blk0 = scores(0, qi, True)
    blk1 = scores(1, qi, True)
    st0 = update(0, qi, blk0, init)
    blk1, st0, st1 = lax.fori_loop(0, qi, body, (blk1, st0, init))
    st1 = update(1, 0, blk1, st1)
    o_ref[...] = jnp.concatenate([acc / l for _, l, acc in (st0, st1)], axis=0).T.astype(o_ref.dtype)


def _prompt_attn(qa, ka, va, *, n_seq, seq_len, tq, tk):
    nq = seq_len // tq
    return pl.pallas_call(
        functools.partial(_prompt_attn_kernel, tq=tq, tk=tk),
        grid=(n_seq, MLA_HEADS // 2, nq),
        in_specs=[pl.BlockSpec((tq, 2 * HEAD_SLAB), lambda b, hp, i: (b * nq + i, hp)),
                  pl.BlockSpec((seq_len, 2 * HEAD_SLAB), lambda b, hp, i: (b, hp)),
                  pl.BlockSpec((2 * MLA_V, seq_len), lambda b, hp, i: (hp, b))],
        out_specs=pl.BlockSpec((tq, 2 * MLA_V), lambda b, hp, i: (b * nq + i, hp)),
        out_shape=jax.ShapeDtypeStruct((n_seq * seq_len, MLA_VALL), BF16),
        scratch_shapes=[pltpu.VMEM((2, tk, tq), F32)],
        compiler_params=pltpu.CompilerParams(dimension_semantics=("parallel", "parallel", "arbitrary"),
                                             vmem_limit_bytes=VMEM_LIMIT),
        name="prompt_attn",
    )(qa, ka, va)


def _sample_attn_kernel(pt_ref, qabs_ref, qrope_ref, cnew_ref, krnew_ref, wkT_ref, wv_ref, poolc_hbm, poolr_hbm,
                        o_ref, lhs_scr, m_scr, l_scr, acc_scr, cb_scr, krT_scr, cbuf, rbuf, sem,
                        *, n_tok, pages_per_step, layer):
    G = pages_per_step
    b = pl.program_id(0)
    g = pl.program_id(1)
    n_g = pl.num_programs(1)
    step = b * n_g + g
    last = step == pl.num_programs(0) * n_g - 1
    slot = step % 2
    rows = n_tok * MLA_HEADS
    n_k = MLA_HEADS * MLA_NOPE

    def page_copies(sl, group=None):
        copies = []
        for j in range(G):
            page = 0 if group is None else pt_ref[group[0], group[1] * G + j]
            copies.append(pltpu.make_async_copy(poolc_hbm.at[layer, page], cbuf.at[sl, j], sem.at[0, sl]))
            copies.append(pltpu.make_async_copy(poolr_hbm.at[layer, page], rbuf.at[sl, j], sem.at[1, sl]))
        return copies

    @pl.when(step == 0)
    def _():
        for cp in page_copies(slot, (b, g)):
            cp.start()

    for cp in page_copies(slot):
        cp.wait()

    def attend(cb, krT, mask):
        nk = cb.shape[0]
        big = _dot_nt(lhs_scr[...], cb)
        kvn = big[:n_k]
        s_nope = big[n_k:n_k + rows]
        ss = jnp.sum((kvn * kvn).reshape(MLA_HEADS, MLA_NOPE, nk), axis=1)
        inv = lax.rsqrt(ss * (1.0 / MLA_NOPE) + EPS)
        s_rope = _dot(qrope_ref[0], krT)
        s = (s_nope.reshape(n_tok, MLA_HEADS, nk) * inv[None]).reshape(rows, nk) + s_rope
        if mask is not None:
            s = jnp.where(mask, s, NEG)
        m = m_scr[...]
        m_new = jnp.maximum(m, jnp.max(s, axis=-1, keepdims=True))
        alpha = jnp.exp2(m - m_new)
        p = jnp.exp2(s - m_new)
        l_scr[...] = alpha * l_scr[...] + jnp.sum(p, axis=-1, keepdims=True)
        acc_scr[...] = alpha * acc_scr[...] + _dot(p.astype(BF16), cb)
        m_scr[...] = m_new

    @pl.when(g == 0)
    def _():
        lhs_scr[:n_k, :] = wkT_ref[...]
        lhs_scr[n_k:n_k + rows, :] = qabs_ref[0]
        m_scr[...] = jnp.full(m_scr.shape, NEG, F32)
        l_scr[...] = jnp.zeros(l_scr.shape, F32)
        acc_scr[...] = jnp.zeros(acc_scr.shape, F32)
        c_new = jnp.concatenate([cnew_ref[0], jnp.zeros((PAGE_SIZE - n_tok, MLA_KV_LORA), F32)], axis=0)
        key =lax.broadcasted_iota(jnp.int32, (rows, PAGE_SIZE), 1)
        tok = lax.broadcasted_iota(jnp.int32, (rows, PAGE_SIZE), 0) // MLA_HEADS
        attend(c_new.astype(BF16), krnew_ref[0].astype(BF16), key <= tok)

    for j in range(G):
        cb_scr[j * PAGE_SIZE:(j + 1) * PAGE_SIZE, :] = cbuf[slot, j].astype(BF16)
        krT_scr[:, j * PAGE_SIZE:(j + 1) * PAGE_SIZE] = rbuf[slot, j].astype(BF16)

    wrap = g + 1 == n_g
    nb = jnp.where(last, b, jnp.where(wrap, b + 1, b))
    ngr = jnp.where(last, g, jnp.where(wrap, 0, g + 1))
    for cp in page_copies(1 - slot, (nb, ngr)):
        cp.start()

    attend(cb_scr[...], krT_scr[...], None)

    @pl.when(last)
    def _():
        for cp in page_copies(1 - slot):
            cp.wait()

    @pl.when(g == n_g - 1)
    def _():
        o_lat = (acc_scr[...] / l_scr[...]).astype(BF16)
        full = _dot(o_lat, wv_ref[...])
        full = full.reshape(n_tok, MLA_HEADS, MLA_VALL)
        hr = lax.broadcasted_iota(jnp.int32, (MLA_HEADS, MLA_VALL), 0)
        hc = lax.broadcasted_iota(jnp.int32, (MLA_HEADS, MLA_VALL), 1) // MLA_V
        o_ref[...] = jnp.sum(jnp.where((hr == hc)[None], full, 0.0), axis=1).astype(o_ref.dtype)


def _sample_attn(page_table, qabs, qrope, c_new, kr_new, w_kT, w_v, pool_c, pool_r, layer, *, pages_per_step):
    n_b, n_pages = page_table.shape
    n_tok = c_new.shape[1]
    rows = n_tok * MLA_HEADS
    G = pages_per_step
    n_k = MLA_HEADS * MLA_NOPE

    per_b = lambda b, g, pt: (b, 0, 0)
    grid_spec = pltpu.PrefetchScalarGridSpec(
        num_scalar_prefetch=1,
        grid=(n_b, n_pages // G),
        in_specs=[pl.BlockSpec((1, rows, MLA_KV_LORA), per_b), pl.BlockSpec((1, rows, MLA_ROPE), per_b),
                  pl.BlockSpec((1, n_tok, MLA_KV_LORA), per_b), pl.BlockSpec((1, MLA_ROPE, PAGE_SIZE), per_b),
                  pl.BlockSpec((n_k, MLA_KV_LORA), lambda b, g, pt: (0, 0)),
                  pl.BlockSpec((MLA_KV_LORA, MLA_VALL), lambda b, g, pt: (0, 0)),
                  pl.BlockSpec(memory_space=pl.ANY), pl.BlockSpec(memory_space=pl.ANY)],
        out_specs=pl.BlockSpec((n_tok, MLA_VALL), lambda b, g, pt: (b, 0)),
        scratch_shapes=[pltpu.VMEM((n_k + rows, MLA_KV_LORA), BF16), pltpu.VMEM((rows, 1), F32),
                        pltpu.VMEM((rows, 1), F32), pltpu.VMEM((rows, MLA_KV_LORA), F32),
                        pltpu.VMEM((G * PAGE_SIZE, MLA_KV_LORA), BF16), pltpu.VMEM((MLA_ROPE, G * PAGE_SIZE), BF16),
                        pltpu.VMEM((2, G, PAGE_SIZE, MLA_KV_LORA), F32), pltpu.VMEM((2, G, MLA_ROPE, PAGE_SIZE), F32),
                        pltpu.SemaphoreType.DMA((2, 2))])
    return pl.pallas_call(
        functools.partial(_sample_attn_kernel, n_tok=n_tok, pages_per_step=G, layer=layer),
        grid_spec=grid_spec,
        out_shape=jax.ShapeDtypeStruct((n_b * n_tok, MLA_VALL), BF16),
        compiler_params=pltpu.CompilerParams(dimension_semantics=("arbitrary", "arbitrary"),
                                             vmem_limit_bytes=VMEM_LIMIT),
        name="sample_attn",
    )(page_table, qabs, qrope, c_new, kr_new, w_kT, w_v, pool_c, pool_r)


def _post_kernel(x_ref, og_ref, om_ref, p_ref, wo_ref, gffn_ref, wg_ref, wu_ref, cw_ref, cb_ref, wd_ref,
                 gple_ref, wpg_ref, wpp_ref, *rest, tiles_per_seq, seq_rows):
    paged_prev = seq_rows is not None
    if paged_prev:
        p1_ref, p2_ref, y_ref, tail_ref, acc_scr = rest
    else:
        y_ref, tail_ref, acc_scr, carry_scr = rest
    tm = x_ref.shape[0]
    i = pl.program_id(0)
    h1 = x_ref[...] + _dot(og_ref[...], wo_ref[:GLA_V, :]) + _dot(om_ref[...], wo_ref[GLA_V:, :])
    n2 = _rms(h1, gffn_ref[...]).astype(BF16)
    row = lax.broadcasted_iota(jnp.int32, (tm, FFN_CHUNK), 0)
    if not paged_prev:
        @pl.when(i % tiles_per_seq == 0)
        def _():
            carry_scr[...] = jnp.zeros(carry_scr.shape, F32)

    def gate_up(f):
        return _dot(n2, wg_ref[:, f:f + FFN_CHUNK]), _dot(n2, wu_ref[:, f:f + FFN_CHUNK])

    acc_scr[...] = jnp.zeros(acc_scr.shape, F32)
    nxt = gate_up(0)
    for f in range(0, D_FF, FFN_CHUNK):
        cols = slice(f, f + FFN_CHUNK)
        a, up = nxt
        if f + FFN_CHUNK < D_FF:
            nxt = gate_up(f + FFN_CHUNK)
        r1 = pltpu.roll(a, 1, axis=0)
        r2 = pltpu.roll(a, 2, axis=0)
        if paged_prev:
            t = row % seq_rows
            a1 = jnp.where(t >= 1, r1, p1_ref[:, cols])
            a2 = jnp.where(t >= 2, r2, p2_ref[:, cols])
        else:
            prev = carry_scr[:, cols]
            pm1 = prev[SUBLANES - 1:SUBLANES, :]
            pm2 = prev[SUBLANES - 2:SUBLANES - 1, :]
            a1 = jnp.where(row >= 1, r1, pm1)
            a2 = jnp.where(row >= 2, r2, jnp.where(row == 0, pm2, pm1))
            carry_scr[:, cols] = a[tm - SUBLANES:, :]
        tail_ref[:, cols] = a[tm - SUBLANES:, :] if not paged_prev else a
        conv = cb_ref[:, cols] + cw_ref[0:1, cols] * a2 + cw_ref[1:2, cols] * a1 + cw_ref[2:3, cols] * a
        gact = (conv / (1.0 + jnp.exp(-conv)) * up).astype(BF16)
        acc_scr[...] += _dot(gact, wd_ref[cols, :])
    h2 = h1 + acc_scr[...]
    n3 = _rms(h2, gple_ref[...]).astype(BF16)
    gate = 1.0 / (1.0 + jnp.exp(-_dot(n3, wpg_ref[...])))
    y_ref[...] = h2 + _dot(p_ref[...].astype(BF16), wpp_ref[...]) * gate


def _post(x2d, og, om, p2d, w, prev=None, *, tm, tiles_per_seq, seq_rows):
    T = x2d.shape[0]
    nt = T // tm
    row = lambda i: (i, 0)
    consts = [w['w_o'], w['g_ffn'], w['w_gate'], w['w_up'], w['conv_w'], w['conv_b'], w['w_down'],
              w['g_ple'], w['w_pgate'], w['w_pproj']]
    in_specs = [pl.BlockSpec((tm, D_MODEL), row), pl.BlockSpec((tm, GLA_V), row),
                pl.BlockSpec((tm, MLA_VALL), row), pl.BlockSpec((tm, PLE_DIM), row)]
    in_specs += [_const_spec(c.shape) for c in consts]
    args = [x2d, og, om, p2d, *consts]
    scratch = [pltpu.VMEM((tm, D_MODEL), F32)]
    if prev is not None:
        in_specs += [pl.BlockSpec((tm, D_FF), row)] * 2
        args += list(prev)
        tail_rows = tm
    else:
        scratch.append(pltpu.VMEM((SUBLANES, D_FF), F32))
        tail_rows = SUBLANES
    return pl.pallas_call(
        functools.partial(_post_kernel, tiles_per_seq=tiles_per_seq, seq_rows=seq_rows),
        grid=(nt,),
        in_specs=in_specs,
        out_specs=[pl.BlockSpec((tm, D_MODEL), row), pl.BlockSpec((tail_rows, D_FF), row)],
        out_shape=[jax.ShapeDtypeStruct((T, D_MODEL), F32), jax.ShapeDtypeStruct((nt * tail_rows, D_FF), F32)],
        scratch_shapes=scratch,
        compiler_params=pltpu.CompilerParams(dimension_semantics=("arbitrary",), vmem_limit_bytes=VMEM_LIMIT),
        name="post_sample" if prev is not None else "post_prompt",
    )(*args)


def _prep_weights(g_mix, w_in, gla_w_a2, gla_b_a, gla_g_out, mla_g_qa, mla_w_qup, mla_g_qn, mla_g_qr,
                  mla_g_kva, mla_g_kr, mla_w_kvup, mla_g_kn, w_o, g_ffn, ffn_w_gate, ffn_w_up, ffn_conv_w,
                  ffn_conv_b, ffn_w_down, g_ple, ple_w_gate, ple_w_proj):
    sizes = (GLA_QK, GLA_QK, GLA_V, GLA_V, GLA_GATE_RANK, MLA_Q_LORA, MLA_KV_LORA, MLA_ROPE)
    offs = np.concatenate([[0], np.cumsum(sizes)])
    piece = lambda i: w_in[:, offs[i]:offs[i + 1]]
    zeros = lambda n: jnp.zeros((D_MODEL, n), w_in.dtype)
    misc = jnp.concatenate([piece(4), zeros(ROPE_LO - GLA_GATE_RANK), piece(7),
                            zeros(LANES - ROPE_LO - MLA_ROPE)], axis=1)
    w_in_p = jnp.concatenate([piece(0), piece(1), piece(2), piece(3), piece(5), piece(6), misc], axis=1)
    w_a2 = jnp.concatenate([gla_w_a2, jnp.zeros((LANES - GLA_GATE_RANK, GLA_QK), gla_w_a2.dtype)], axis=0)

    def slab_vec(nope, rope_):
        one = jnp.concatenate([nope, rope_, jnp.zeros((HEAD_SLAB - MLA_NOPE - MLA_ROPE,), F32)])
        return jnp.tile(one, MLA_HEADS)[None, :]

    wq = mla_w_qup.reshape(MLA_Q_LORA, MLA_HEADS, MLA_NOPE + MLA_ROPE)
    swap = np.arange(MLA_NOPE + MLA_ROPE)
    swap[MLA_NOPE:] = np.concatenate([swap[MLA_NOPE + ROPE_HALF:], swap[MLA_NOPE:MLA_NOPE + ROPE_HALF]])
    slab_pad = ((0, 0), (0, 0), (0, HEAD_SLAB - MLA_NOPE - MLA_ROPE))
    wq = jnp.concatenate([jnp.pad(wq, slab_pad).reshape(MLA_Q_LORA, MLA_SLABS),
                          jnp.pad(wq[:, :, swap], slab_pad).reshape(MLA_Q_LORA, MLA_SLABS)], axis=1)
    g_q = jnp.concatenate([mla_g_qn, mla_g_qr])
    gq_slab = jnp.pad(jnp.stack([g_q, g_q[swap]]), ((0, 0), (0, HEAD_SLAB - MLA_NOPE - MLA_ROPE)))
    wkv = mla_w_kvup.reshape(MLA_KV_LORA, MLA_HEADS, MLA_NOPE + MLA_V)
    wk = wkv[:, :, :MLA_NOPE]
    wv = wkv[:, :, MLA_NOPE:].reshape(MLA_KV_LORA, MLA_VALL)
    wk_slab = jnp.pad(wk, ((0, 0), (0, 0), (0, HEAD_SLAB - MLA_NOPE))).reshape(MLA_KV_LORA, MLA_SLABS)
    wk_g = jnp.pad(wk * mla_g_kn[None, None, :], ((0, 0), (0, 0), (0, HEAD_SLAB - MLA_NOPE)))
    eye = jnp.eye(MLA_HEADS, dtype=F32)
    w_kabs = jnp.einsum('nhd,hg->hdgn', wk_g, eye).reshape(MLA_SLABS, MLA_HEADS * MLA_KV_LORA)
    w_kT = wk.transpose(1, 2, 0).reshape(MLA_HEADS * MLA_NOPE, MLA_KV_LORA)

    lane = np.arange(2 * HEAD_SLAB)
    seg = np.where(lane % HEAD_SLAB < MLA_NOPE, 0, np.where(lane % HEAD_SLAB < MLA_NOPE + MLA_ROPE, 1, 2))
    same = (lane[:, None] // HEAD_SLAB == lane[None, :] // HEAD_SLAB) & (seg[:, None] == seg[None, :])
    segm = np.where(same & (seg[:, None] == 0), 1.0 / MLA_NOPE, np.where(same & (seg[:, None] == 1), 1.0 / MLA_ROPE, 0.0))

    gkr = jnp.concatenate([jnp.zeros((ROPE_LO,), F32), mla_g_kr, jnp.zeros((LANES - ROPE_LO - MLA_ROPE,), F32)])
    return dict(
        g_mix=g_mix[None, :], w_in=w_in_p.astype(BF16), w_a2=w_a2.astype(BF16), b_a=gla_b_a[None, :],
        g_out=gla_g_out[None, :], g_qa=mla_g_qa[None, :], w_qup=wq.astype(BF16), segm=jnp.asarray(segm, BF16),
        gq_slab=gq_slab, g_kva=mla_g_kva[None, :], w_kslab=wk_slab.astype(BF16),
        w_vT=wv.T.astype(BF16),
        gk_slab=slab_vec(mla_g_kn, jnp.zeros((MLA_ROPE,), F32)),
        g_kn=mla_g_kn, gkr_slab=gkr[None, :], w_kabs=w_kabs.astype(BF16), w_kT=w_kT.astype(BF16),
        w_v=wv.astype(BF16), w_o=w_o.astype(BF16), g_ffn=g_ffn[None, :], w_gate=ffn_w_gate.astype(BF16),
        w_up=ffn_w_up.astype(BF16), conv_w=ffn_conv_w, conv_b=ffn_conv_b[None, :],
        w_down=ffn_w_down.astype(BF16), g_ple=g_ple[None, :], w_pgate=ple_w_gate.astype(BF16),
        w_pproj=ple_w_proj.astype(BF16))


def _rope_tables(pos):
    inv = ROPE_THETA ** (-jnp.arange(ROPE_HALF, dtype=F32) * 2.0 / MLA_ROPE)
    ang = pos.astype(F32)[:, None] * inv[None, :]
    cos, sin = jnp.cos(ang), jnp.sin(ang)
    T = pos.shape[0]
    pad = jnp.zeros((T, HEAD_SLAB - MLA_NOPE - MLA_ROPE), F32)
    cos_t = jnp.concatenate([jnp.ones((T, MLA_NOPE), F32), cos, cos, pad], axis=1)
    sin_t = jnp.concatenate([jnp.zeros((T, MLA_NOPE), F32), -sin, sin, pad], axis=1)
    return cos_t, sin_t


def _pick_tile(n, cap):
    t = math.gcd(n, cap)
    return t


def _state_to_bd(s):
    eye = jnp.eye(GLA_HEADS, dtype=s.dtype)
    return jnp.einsum('bhdv,hg->bhvgd', s, eye).reshape(s.shape[0], GLA_V, GLA_QK)


def _layer(w, x_p, x_s, p_p, p_s, pool_c, pool_r, layer, state_gla, state_conv, page_table):
    B, S, _ = x_p.shape
    Bd, Td, _ = x_s.shape
    tm = _pick_tile(S, 512)
    tps = S // tm
    cos_p, sin_p = _rope_tables(jnp.arange(S))
    xp2 = x_p.reshape(B * S, D_MODEL)
    gq, gk, gv, gg, la, qa, ckv_p, kr_p, ka, vaT = _in_proj(xp2, cos_p, sin_p, w, tm=tm, tiles_per_seq=tps,
                                                            sample=False)
    chunk_p = math.gcd(S, GLA_CHUNK)
    n_chunks = math.gcd(S // chunk_p, 8)
    og_p, sT_p = _gla(gq, gk, gv, gg, la, jnp.zeros((B, GLA_V, GLA_QK), F32), w['g_out'],
                      n_seq=B, seq_len=S, chunk=chunk_p, n_chunks=n_chunks)
    tq = _pick_tile(S, 512)
    om_p = _prompt_attn(qa, ka, vaT, n_seq=B, seq_len=S, tq=tq, tk=tq)
    y_p, tail_p = _post(xp2, og_p, om_p, p_p.reshape(B * S, PLE_DIM), w, tm=tm, tiles_per_seq=tps, seq_rows=None)
    conv_p = tail_p.reshape(B, tps, SUBLANES, D_FF)[:, -1, SUBLANES - (CONV_W - 1):, :]

    Ts = Bd * Td
    pos_s = PAST_LEN + jnp.arange(Td)
    cos_s, sin_s = _rope_tables(jnp.tile(pos_s, Bd))
    xs2 = x_s.reshape(Ts, D_MODEL)
    gq, gk, gv, gg, la, qa, ckv_s, kr_s, qabs = _in_proj(xs2, cos_s, sin_s, w, tm=Ts, tiles_per_seq=1, sample=True)
    chunk_s = math.gcd(Td, GLA_CHUNK)
    og_s, sT_s = _gla(gq, gk, gv, gg, la, _state_to_bd(state_gla), w['g_out'],
                      n_seq=Bd, seq_len=Td, chunk=chunk_s, n_chunks=Td // chunk_s)
    rows = Td * MLA_HEADS
    qrope = qa.reshape(Ts, MLA_HEADS, HEAD_SLAB)[:, :, ROPE_LO:ROPE_LO + MLA_ROPE].reshape(Bd, rows, MLA_ROPE)
    krT_new = jnp.pad(jnp.swapaxes(kr_s.reshape(Bd, Td, MLA_ROPE), 1, 2), ((0, 0), (0, 0), (0, PAGE_SIZE - Td)))
    om_s = _sample_attn(page_table, qabs.reshape(Bd, rows, MLA_KV_LORA), qrope,
                        ckv_s.reshape(Bd, Td, MLA_KV_LORA), krT_new,
                        w['w_kT'], w['w_v'], pool_c, jnp.swapaxes(pool_r, 2, 3), layer,
                        pages_per_step=math.gcd(page_table.shape[1], SAMPLE_PAGES_PER_STEP))
    zpad = lambda a, lo: jnp.pad(a, ((0, 0), (lo, Td - lo - a.shape[1]), (0, 0))).reshape(Ts, D_FF)
    prev1 = zpad(state_conv[:, 1:2], 0)
    prev2 = zpad(state_conv, 0)
    y_s, a_s = _post(xs2, og_s, om_s, p_s.reshape(Ts, PLE_DIM), w, prev=(prev1, prev2), tm=Ts, tiles_per_seq=1,
                     seq_rows=Td)
    full = jnp.concatenate([state_conv, a_s.reshape(Bd, Td, D_FF)], axis=1)
    conv_s = full[:, full.shape[1] - (CONV_W - 1):]
    return (y_p.reshape(B, S, D_MODEL), y_s.reshape(Bd, Td, D_MODEL),
            ckv_p.reshape(B, S, MLA_KV_LORA), kr_p.reshape(B, S, MLA_ROPE),
            sT_p.transpose(0, 1, 3, 2), conv_p,
            ckv_s.reshape(Bd, Td, MLA_KV_LORA), kr_s.reshape(Bd, Td, MLA_ROPE),
            sT_s.transpose(0, 1, 3, 2), conv_s)


def kernel(x_prompt, x_sample, cache_ckv, cache_krope, state_gla, state_conv, page_table, p_prompt, p_sample, g_mix, w_in, gla_w_a2, gla_b_a, gla_g_out, mla_g_qa, mla_w_qup, mla_g_qn, mla_g_qr, mla_g_kva, mla_g_kr, mla_w_kvup, mla_g_kn, w_o, g_ffn, ffn_w_gate, ffn_w_up, ffn_conv_w, ffn_conv_b, ffn_w_down, g_ple, ple_w_gate, ple_w_proj):
    depth = w_in.shape[0]
    per_layer = (g_mix, w_in, gla_w_a2, gla_b_a, gla_g_out, mla_g_qa, mla_w_qup, mla_g_qn, mla_g_qr, mla_g_kva,
                 mla_g_kr, mla_w_kvup, mla_g_kn, w_o, g_ffn, ffn_w_gate, ffn_w_up, ffn_conv_w, ffn_conv_b,
                 ffn_w_down, g_ple, ple_w_gate, ple_w_proj)
    y_p, y_s = x_prompt, x_sample
    outs = [[] for _ in range(8)]
    for i in range(depth):
        w = _prep_weights(*(a[i] for a in per_layer))
        res = _layer(w, y_p, y_s, p_prompt[i], p_sample[i], cache_ckv, cache_krope, i, state_gla[i],
                     state_conv[i], page_table)
        y_p, y_s = res[0], res[1]
        for lst, r in zip(outs, res[2:]):
            lst.append(r)
    return (y_p, y_s) + tuple(jnp.stack(lst) for lst in outs)
```

```python
import functools
import math

import jax
import jax.numpy as jnp
import numpy as np
from jax import lax
from jax.experimental import pallas as pl
from jax.experimental.pallas import tpu as pltpu

D_MODEL = 1024
PAST_LEN = 16384
PAGE_SIZE = 128
GLA_HEADS = 4
GLA_DK = 64
GLA_DV = 128
GLA_GATE_RANK = 16
GLA_GATE_NORM = 16.0
GLA_CHUNK = 32
MLA_HEADS = 8
MLA_Q_LORA = 256
MLA_KV_LORA = 128
MLA_NOPE = 64
MLA_ROPE = 32
MLA_V = 64
MLA_SCALE = (MLA_NOPE + MLA_ROPE) ** -0.5
LOG2E = math.log2(math.e)
ROPE_THETA = 10000.0
D_FF = 2816
CONV_W = 3
PLE_DIM = 256
EPS = 1e-6
NEG = -1e30

LANES = 128
SUBLANES = 8
HEAD_SLAB = LANES
ROPE_LO = MLA_NOPE
ROPE_HALF = MLA_ROPE // 2
GLA_QK = GLA_HEADS * GLA_DK
GLA_V = GLA_HEADS * GLA_DV
MLA_SLABS = MLA_HEADS * HEAD_SLAB
MLA_VALL = MLA_HEADS * MLA_V
C_GQ, C_GK, C_GV, C_GG = 0, 256, 512, 1024
C_MQ, C_MKV, C_MISC = 1536, 1792, 1920
IN_COLS_P = 2048
FFN_CHUNK = 256
SAMPLE_PAGES_PER_STEP = 16
PAGE_RING = 3
VMEM_LIMIT = 56 * 1024 * 1024

BF16 = jnp.bfloat16
F32 = jnp.float32


def _dot(a, b):
    return jnp.dot(a, b, preferred_element_type=F32)


def _dot_nt(a, b):
    return lax.dot_general(a, b, (((1,), (1,)), ((), ())), preferred_element_type=F32)


def _dot_tn(a, b):
    return lax.dot_general(a, b, (((0,), (0,)), ((), ())), preferred_element_type=F32)


def _rms(x, g):
    return x * lax.rsqrt(jnp.mean(x * x, axis=-1, keepdims=True) + EPS) * g


def _const_spec(shape):
    nd = len(shape)
    return pl.BlockSpec(shape, lambda *_: (0,) * nd, pipeline_mode=pl.Buffered(1))


def _rope_slab(y, cos_t, sin_t):
    lane = lax.broadcasted_iota(jnp.int32, y.shape, 1)
    swapped = jnp.where(lane < ROPE_LO + ROPE_HALF,
                        pltpu.roll(y, LANES - ROPE_HALF, axis=1),
                        pltpu.roll(y, ROPE_HALF, axis=1))
    return y * cos_t + swapped * sin_t


def _in_proj_kernel(x_ref, cos_ref, sin_ref, gmix_ref, win_ref, wa2_ref, ba_ref, gqa_ref, wqup_ref,
                    segm_ref, gq_ref, gkva_ref, gkr_ref, *rest, sample):
    if sample:
        (wkabs_ref, gq_o, gk_o, gv_o, gg_o, la_o, qa_o, ckv_o, kr_o, qabs_o) = rest
    else:
        (wk_ref, gk_ref, wvT_ref, gq_o, gk_o, gv_o, gg_o, la_o, qa_o, ckv_o, kr_o, ka_o, vaT_o) = rest
    x = x_ref[...]
    n = _rms(x, gmix_ref[...]).astype(BF16)
    u = _dot(n, win_ref[...])
    gq_o[...] = u[:, C_GQ:C_GQ + GLA_QK] * (GLA_DK ** -0.5)
    gk_o[...] = u[:, C_GK:C_GK + GLA_QK]
    gv_o[...] = u[:, C_GV:C_GV + GLA_V].astype(BF16)
    gg_o[...] = u[:, C_GG:C_GG + GLA_V]
    misc = u[:, C_MISC:C_MISC + LANES]
    z = _dot(misc.astype(BF16), wa2_ref[...]) + ba_ref[...]
    la_o[...] = (jnp.minimum(z, 0.0) - jnp.log1p(jnp.exp(-jnp.abs(z)))) * (1.0 / GLA_GATE_NORM)

    cos_t = cos_ref[...]
    sin_t = sin_ref[...]
    segm = segm_ref[...]

    lane = lax.broadcasted_iota(jnp.int32, misc.shape, 1)
    is_rope = (lane >= ROPE_LO) & (lane < ROPE_LO + MLA_ROPE)
    ms_r = jnp.sum(jnp.where(is_rope, misc * misc, 0.0), axis=-1, keepdims=True) * (1.0 / MLA_ROPE)
    kr_slab = _rope_slab(misc * lax.rsqrt(ms_r + EPS) * gkr_ref[...], cos_t, sin_t)
    kr_o[...] = kr_slab[:, ROPE_LO:ROPE_LO + MLA_ROPE]

    mkv = u[:, C_MKV:C_MKV + MLA_KV_LORA]
    ckv = _rms(mkv, gkva_ref[...])
    ckv_o[...] = ckv
    ckv_b = ckv.astype(BF16)
    if not sample:
        kn = _dot(ckv_b, wk_ref[...])
        vaT_o[...] = _dot_nt(wvT_ref[...], ckv_b).astype(BF16)

    cq = _rms(u[:, C_MQ:C_MQ + MLA_Q_LORA], gqa_ref[...]).astype(BF16)
    qf = _dot(cq, wqup_ref[...])
    q_gain = gq_ref[...]
    q_cos = cos_t * (q_gain[0:1] * (MLA_SCALE * LOG2E))
    q_sin = sin_t * (q_gain[1:2] * (MLA_SCALE * LOG2E))

    for pair in range(MLA_HEADS // 2):
        lo = pair * 2 * HEAD_SLAB
        q2 = qf[:, lo:lo + 2 * HEAD_SLAB]
        q_inv = lax.rsqrt(_dot((q2 * q2).astype(BF16), segm) + EPS)
        if not sample:
            k2 = kn[:, lo:lo + 2 * HEAD_SLAB]
            k2 = k2 * lax.rsqrt(_dot((k2 * k2).astype(BF16), segm) + EPS) * gk_ref[:, lo:lo + 2 * HEAD_SLAB]
        for j in range(2):
            sl = slice(lo + j * HEAD_SLAB, lo + (j + 1) * HEAD_SLAB)
            swapped = qf[:, MLA_SLABS + lo + j * HEAD_SLAB:MLA_SLABS + lo + (j + 1) * HEAD_SLAB]
            qh = q_inv[:, j * HEAD_SLAB:(j + 1) * HEAD_SLAB] * (qf[:, sl] * q_cos + swapped * q_sin)
            qa_o[:, sl] = qh.astype(BF16)
            if not sample:
                ka_o[:, sl] = (k2[:, j * HEAD_SLAB:(j + 1) * HEAD_SLAB] + kr_slab).astype(BF16)
    if sample:
        qabs_o[...] = _dot(qa_o[...], wkabs_ref[...]).astype(BF16)


def _in_proj(x2d, cos_t, sin_t, w, *, tm, tiles_per_seq, sample):
    T = x2d.shape[0]
    nt = T // tm
    row = lambda i: (i, 0)
    pos = lambda i: (i % tiles_per_seq, 0)
    consts = [w['g_mix'], w['w_in'], w['w_a2'], w['b_a'], w['g_qa'], w['w_qup'], w['segm'], w['gq_slab'],
              w['g_kva'], w['gkr_slab']]
    consts += [w['w_kabs']] if sample else [w['w_kslab'], w['gk_slab'], w['w_vT']]
    in_specs = [pl.BlockSpec((tm, D_MODEL), row), pl.BlockSpec((tm, LANES), pos), pl.BlockSpec((tm, LANES), pos)]
    in_specs += [_const_spec(c.shape) for c in consts]
    outs = [(GLA_QK, F32), (GLA_QK, F32), (GLA_V, BF16), (GLA_V, F32), (GLA_QK, F32),
            (MLA_SLABS, BF16), (MLA_KV_LORA, F32), (MLA_ROPE, F32), (MLA_SLABS, BF16)]
    out_specs = [pl.BlockSpec((tm, c), row) for c, _ in outs]
    out_shape = [jax.ShapeDtypeStruct((T, c), d) for c, d in outs]
    if not sample:
        out_specs.append(pl.BlockSpec((MLA_VALL, tm), lambda i: (0, i)))
        out_shape.append(jax.ShapeDtypeStruct((MLA_VALL, T), BF16))
    return pl.pallas_call(
        functools.partial(_in_proj_kernel, sample=sample),
        grid=(nt,),
        in_specs=in_specs,
        out_specs=out_specs,
        out_shape=out_shape,
        compiler_params=pltpu.CompilerParams(dimension_semantics=("parallel",), vmem_limit_bytes=VMEM_LIMIT),
        name="in_proj_sample" if sample else "in_proj_prompt",
    )(x2d, cos_t, sin_t, *consts)


def _gla_kernel(q_ref, k_ref, v_ref, gg_ref, la_ref, s0_ref, tril_ref, ones_ref, gout_ref,
                o_ref, sT_ref, st_scr, *, chunk, n_chunks):
    step = pl.program_id(1)

    @pl.when(step == 0)
    def _():
        st_scr[...] = s0_ref[0]

    tc = chunk * n_chunks
    la = la_ref[...]
    la_hi = la.astype(BF16)
    la_lo = (la - la_hi.astype(F32)).astype(BF16)
    tril = tril_ref[...]
    ones = ones_ref[...]
    b = _dot(tril, la_hi) + _dot(tril, la_lo)
    bl = _dot(ones, la_hi) + _dot(ones, la_lo)
    q = q_ref[...]
    k = k_ref[...]
    v = v_ref[...]
    q_dec = (q * jnp.exp(b)).astype(BF16)
    k_inv = (k * jnp.exp(-b)).astype(BF16)
    k_end = (k * jnp.exp(bl - b)).astype(BF16)
    dec = jnp.exp(bl)

    r = lax.broadcasted_iota(jnp.int32, (tc, tc), 0)
    c = lax.broadcasted_iota(jnp.int32, (tc, tc), 1)
    causal = (r >= c) & ((r // chunk) == (c // chunk))
    sr = lax.broadcasted_iota(jnp.int32, (GLA_V, GLA_QK), 0) // GLA_DV
    sc = lax.broadcasted_iota(jnp.int32, (GLA_V, GLA_QK), 1) // GLA_DK
    head_diag = sr == sc

    o_intra = []
    for h in range(GLA_HEADS):
        qh = q_dec[:, h * GLA_DK:(h + 1) * GLA_DK]
        kh = k_inv[:, h * GLA_DK:(h + 1) * GLA_DK]
        a = jnp.where(causal, _dot_nt(qh, kh), 0.0).astype(BF16)
        o_intra.append(_dot(a, v[:, h * GLA_DV:(h + 1) * GLA_DV]))
    o_intra = jnp.concatenate(o_intra, axis=1)

    st = st_scr[...]
    o_inter = []
    for ci in range(n_chunks):
        rows = slice(ci * chunk, (ci + 1) * chunk)
        o_inter.append(_dot_nt(q_dec[rows], st.astype(BF16)))
        xt = _dot_tn(v[rows], k_end[rows])
        st = st * dec[ci * chunk:ci * chunk + 1, :] + jnp.where(head_diag, xt, 0.0)
    st_scr[...] = st
    o = o_intra + (jnp.concatenate(o_inter, axis=0) if n_chunks > 1 else o_inter[0])

    gg = gg_ref[...]
    gate = gg / (1.0 + jnp.exp(-gg))
    gout = gout_ref[...]
    for h in range(GLA_HEADS):
        sl = slice(h * GLA_DV, (h + 1) * GLA_DV)
        o_ref[:, sl] = (_rms(o[:, sl], gout) * gate[:, sl]).astype(o_ref.dtype)

    @pl.when(step == pl.num_programs(1) - 1)
    def _():
        for h in range(GLA_HEADS):
            sT_ref[0, h] = st[h * GLA_DV:(h + 1) * GLA_DV, h * GLA_DK:(h + 1) * GLA_DK]


def _gla(q, k, v, gg, la, s0_bd, g_out, *, n_seq, seq_len, chunk, n_chunks):
    tc = chunk * n_chunks
    steps = seq_len // tc
    idx = np.arange(tc)
    same = (idx[:, None] // chunk) == (idx[None, :] // chunk)
    tril = jnp.asarray(same & (idx[:, None] >= idx[None, :]), BF16)
    ones = jnp.asarray(same, BF16)
    row = lambda b, s: (b * steps + s, 0)
    return pl.pallas_call(
        functools.partial(_gla_kernel, chunk=chunk, n_chunks=n_chunks),
        grid=(n_seq, steps),
        in_specs=[pl.BlockSpec((tc, GLA_QK), row), pl.BlockSpec((tc, GLA_QK), row), pl.BlockSpec((tc, GLA_V), row),
                  pl.BlockSpec((tc, GLA_V), row), pl.BlockSpec((tc, GLA_QK), row),
                  pl.BlockSpec((1, GLA_V, GLA_QK), lambda b, s: (b, 0, 0)),
                  _const_spec((tc, tc)), _const_spec((tc, tc)), _const_spec((1, GLA_DV))],
        out_specs=[pl.BlockSpec((tc, GLA_V), row),
                   pl.BlockSpec((1, GLA_HEADS, GLA_DV, GLA_DK), lambda b, s: (b, 0, 0, 0))],
        out_shape=[jax.ShapeDtypeStruct((n_seq * seq_len, GLA_V), BF16),
                   jax.ShapeDtypeStruct((n_seq, GLA_HEADS, GLA_DV, GLA_DK), F32)],
        scratch_shapes=[pltpu.VMEM((GLA_V, GLA_QK), F32)],
        compiler_params=pltpu.CompilerParams(dimension_semantics=("parallel", "arbitrary"),
                                             vmem_limit_bytes=VMEM_LIMIT),
        name=f"gla_c{chunk}",
    )(q, k, v, gg, la, s0_bd, tril, ones, g_out)


def _prompt_attn_kernel(q_ref, k_ref, vT_ref, o_ref, s_scr, *, tq, tk):
    assert tq == tk
    qi = pl.program_id(2)
    key = lax.broadcasted_iota(jnp.int32, (tk, tq), 0)
    qry = lax.broadcasted_iota(jnp.int32, (tk, tq), 1)

    def scores(j, kb, masked):
        ks = pl.multiple_of(kb * tk, tk)
        s = _dot_nt(k_ref[pl.ds(ks, tk), j * HEAD_SLAB:(j + 1) * HEAD_SLAB],
                    q_ref[:, j * HEAD_SLAB:(j + 1) * HEAD_SLAB])
        if masked:
            s = jnp.where(key <= qry, s, NEG)
        s_scr[j] = s
        return jnp.max(s, axis=0, keepdims=True)

    def update(j, kb, m_blk, state):
        m, l, acc = state
        ks = pl.multiple_of(kb * tk, tk)
        m_new = jnp.maximum(m, m_blk)
        alpha = jnp.exp2(m - m_new)
        p = jnp.exp2(s_scr[j] - m_new)
        l = alpha * l + jnp.sum(p, axis=0, keepdims=True)
        acc = alpha * acc + _dot(vT_ref[j * MLA_V:(j + 1) * MLA_V, pl.ds(ks, tk)], p.astype(BF16))
        return m_new, l, acc

    def body(t, carry):
        blk1, st0, st1 = carry
        kb = qi - 1 - t
        blk0 = scores(0, kb, False)
        st1 = update(1, kb + 1, blk1, st1)
        blk1 = scores(1, kb, False)
        st0 = update(0, kb, blk0, st0)
        return blk1, st0, st1

    init = (jnp.full((1, tq), NEG, F32), jnp.zeros((1, tq), F32), jnp.zeros((MLA_V, tq), F32))
    blk0 = scores(0, qi, True)
    blk1 = scores(1, qi, True)
    st0 = update(0, qi, blk0, init)
    blk1, st0, st1 = lax.fori_loop(0, qi, body, (blk1, st0, init))
    st1 = update(1, 0, blk1, st1)
    o_ref[...] = jnp.concatenate([acc / l for _, l, acc in (st0, st1)], axis=0).T.astype(o_ref.dtype)


def _prompt_attn(qa, ka, va, *, n_seq, seq_len, tq, tk):
    nq = seq_len // tq
    return pl.pallas_call(
        functools.partial(_prompt_attn_kernel, tq=tq, tk=tk),
        grid=(n_seq, MLA_HEADS // 2, nq),
        in_specs=[pl.BlockSpec((tq, 2 * HEAD_SLAB), lambda b, hp, i: (b * nq + i, hp)),
                  pl.BlockSpec((seq_len, 2 * HEAD_SLAB), lambda b, hp, i: (b, hp)),
                  pl.BlockSpec((2 * MLA_V, seq_len), lambda b, hp, i: (hp, b))],
        out_specs=pl.BlockSpec((tq, 2 * MLA_V), lambda b, hp, i: (b * nq + i, hp)),
        out_shape=jax.ShapeDtypeStruct((n_seq * seq_len, MLA_VALL), BF16),
        scratch_shapes=[pltpu.VMEM((2, tk, tq), F32)],
        compiler_params=pltpu.CompilerParams(dimension_semantics=("parallel", "parallel", "arbitrary"),
                                             vmem_limit_bytes=VMEM_LIMIT),
        name="prompt_attn",
    )(qa, ka, va)


def _sample_attn_kernel(pt_ref, qabs_ref, qrope_ref, cnew_ref, krnew_ref, wkT_ref, wv_ref, poolc_hbm, poolr_hbm,
                        o_ref, lhs_scr, m_scr, l_scr, acc_scr, cb_scr, krT_scr, s_scr, cbuf, rbuf, sem,
                        *, n_tok, pages_per_step, layer):
    G = pages_per_step
    b = pl.program_id(0)
    g = pl.program_id(1)
    n_g = pl.num_programs(1)
    step = b * n_g + g
    n_steps = pl.num_programs(0) * n_g
    last = step == n_steps - 1
    slot = step % PAGE_RING

    def group_of(s):
        s = jnp.minimum(s, n_steps - 1)
        return s // n_g, s % n_g
    rows = n_tok * MLA_HEADS
    n_k = MLA_HEADS * MLA_NOPE

    def page_copies(sl, group=None):
        copies = []
        for j in range(G):
            page = 0 if group is None else pt_ref[group[0], group[1] * G + j]
            copies.append(pltpu.make_async_copy(poolc_hbm.at[layer, page], cbuf.at[sl, j], sem.at[0, sl]))
            copies.append(pltpu.make_async_copy(poolr_hbm.at[layer, page], rbuf.at[sl, j], sem.at[1, sl]))
        return copies

    @pl.when(step == 0)
    def _():
        for d in range(PAGE_RING - 1):
            for cp in page_copies(d, group_of(d)):
                cp.start()

    for cp in page_copies(slot):
        cp.wait()

    def scores(cb, krT):
        nk = cb.shape[0]
        big = _dot_nt(lhs_scr[...], cb)
        kvn = big[:n_k]
        s_nope = big[n_k:n_k + rows]
        ss = jnp.sum((kvn * kvn).reshape(MLA_HEADS, MLA_NOPE, nk), axis=1)
        inv = lax.rsqrt(ss * (1.0 / MLA_NOPE) + EPS)
        s_rope = _dot(qrope_ref[0], krT)
        return (s_nope.reshape(n_tok, MLA_HEADS, nk) * inv[None]).reshape(rows, nk) + s_rope

    def weights(s):
        m = m_scr[...]
        m_new = jnp.maximum(m, jnp.max(s, axis=-1, keepdims=True))
        alpha = jnp.exp2(m - m_new)
        p = jnp.exp2(s - m_new)
        l_scr[...] = alpha * l_scr[...] + jnp.sum(p, axis=-1, keepdims=True)
        m_scr[...] = m_new
        return alpha, p.astype(BF16)

    def accumulate(alpha, p, cb):
        acc_scr[...] = alpha * acc_scr[...] + _dot(p, cb)

    def absorb(s, cb):
        accumulate(*weights(s), cb)

    cur = g % 2
    prv = 1 - cur

    @pl.when(g == 0)
    def _():
        lhs_scr[:n_k, :] = wkT_ref[...]
        lhs_scr[n_k:n_k + rows, :] = qabs_ref[0]
        m_scr[...] = jnp.full(m_scr.shape, NEG, F32)
        l_scr[...] = jnp.zeros(l_scr.shape, F32)
        acc_scr[...] = jnp.zeros(acc_scr.shape, F32)
        s_scr[prv] = jnp.full(s_scr.shape[1:], NEG, F32)
        cb_scr[prv] = jnp.zeros(cb_scr.shape[1:], BF16)
        c_new = jnp.concatenate([cnew_ref[0], jnp.zeros((PAGE_SIZE - n_tok, MLA_KV_LORA), F32)],
                                axis=0).astype(BF16)
        key = lax.broadcasted_iota(jnp.int32, (rows, PAGE_SIZE), 1)
        tok = lax.broadcasted_iota(jnp.int32, (rows, PAGE_SIZE), 0) // MLA_HEADS
        absorb(jnp.where(key <= tok, scores(c_new, krnew_ref[0].astype(BF16)), NEG), c_new)

    for j in range(G):
        cb_scr[cur, j * PAGE_SIZE:(j + 1) * PAGE_SIZE, :] = cbuf[slot, j].astype(BF16)
        krT_scr[:, j * PAGE_SIZE:(j + 1) * PAGE_SIZE] = rbuf[slot, j].astype(BF16)

    ahead = step + (PAGE_RING - 1)
    for cp in page_copies(ahead % PAGE_RING, group_of(ahead)):
        cp.start()

    alpha, p = weights(s_scr[prv])
    s_scr[cur] = scores(cb_scr[cur], krT_scr[...])
    accumulate(alpha, p, cb_scr[prv])

    @pl.when(last)
    def _():
        for d in range(1, PAGE_RING):
            for cp in page_copies((step + d) % PAGE_RING):
                cp.wait()

    @pl.when(g == n_g - 1)
    def _():
        absorb(s_scr[cur], cb_scr[cur])
        o_lat =(acc_scr[...] / l_scr[...]).astype(BF16)
        full = _dot(o_lat, wv_ref[...])
        full = full.reshape(n_tok, MLA_HEADS, MLA_VALL)
        hr = lax.broadcasted_iota(jnp.int32, (MLA_HEADS, MLA_VALL), 0)
        hc = lax.broadcasted_iota(jnp.int32, (MLA_HEADS, MLA_VALL), 1) // MLA_V
        o_ref[...] = jnp.sum(jnp.where((hr == hc)[None], full, 0.0), axis=1).astype(o_ref.dtype)


def _sample_attn(page_table, qabs, qrope, c_new, kr_new, w_kT, w_v, pool_c, pool_r, layer, *, pages_per_step):
    n_b, n_pages = page_table.shape
    n_tok = c_new.shape[1]
    rows = n_tok * MLA_HEADS
    G = pages_per_step
    n_k = MLA_HEADS * MLA_NOPE

    per_b = lambda b, g, pt: (b, 0, 0)
    grid_spec = pltpu.PrefetchScalarGridSpec(
        num_scalar_prefetch=1,
        grid=(n_b, n_pages // G),
        in_specs=[pl.BlockSpec((1, rows, MLA_KV_LORA), per_b), pl.BlockSpec((1, rows, MLA_ROPE), per_b),
                  pl.BlockSpec((1, n_tok, MLA_KV_LORA), per_b), pl.BlockSpec((1, MLA_ROPE, PAGE_SIZE), per_b),
                  pl.BlockSpec((n_k, MLA_KV_LORA), lambda b, g, pt: (0, 0)),
                  pl.BlockSpec((MLA_KV_LORA, MLA_VALL), lambda b, g, pt: (0, 0)),
                  pl.BlockSpec(memory_space=pl.ANY), pl.BlockSpec(memory_space=pl.ANY)],
        out_specs=pl.BlockSpec((n_tok, MLA_VALL), lambda b, g, pt: (b, 0)),
        scratch_shapes=[pltpu.VMEM((n_k + rows, MLA_KV_LORA), BF16), pltpu.VMEM((rows, 1), F32),
                        pltpu.VMEM((rows, 1), F32), pltpu.VMEM((rows, MLA_KV_LORA), F32),
                        pltpu.VMEM((2, G * PAGE_SIZE, MLA_KV_LORA), BF16), pltpu.VMEM((MLA_ROPE, G * PAGE_SIZE), BF16),
                        pltpu.VMEM((2, rows, G * PAGE_SIZE), F32),
                        pltpu.VMEM((PAGE_RING, G, PAGE_SIZE, MLA_KV_LORA), F32),
                        pltpu.VMEM((PAGE_RING, G, MLA_ROPE, PAGE_SIZE), F32),
                        pltpu.SemaphoreType.DMA((2, PAGE_RING))])
    return pl.pallas_call(
        functools.partial(_sample_attn_kernel, n_tok=n_tok, pages_per_step=G, layer=layer),
        grid_spec=grid_spec,
        out_shape=jax.ShapeDtypeStruct((n_b * n_tok, MLA_VALL), BF16),
        compiler_params=pltpu.CompilerParams(dimension_semantics=("arbitrary", "arbitrary"),
                                             vmem_limit_bytes=VMEM_LIMIT),
        name="sample_attn",
    )(page_table, qabs, qrope, c_new, kr_new, w_kT, w_v, pool_c, pool_r)


def _post_kernel(x_ref, og_ref, om_ref, p_ref, wo_ref, gffn_ref, wg_ref, wu_ref, cw_ref, cb_ref, wd_ref,
                 gple_ref, wpg_ref, wpp_ref, *rest, tiles_per_seq, seq_rows):
    paged_prev = seq_rows is not None
    if paged_prev:
        p1_ref, p2_ref, y_ref, tail_ref, acc_scr = rest
    else:
        y_ref, tail_ref, acc_scr, carry_scr = rest
    tm = x_ref.shape[0]
    i = pl.program_id(0)
    h1 = x_ref[...] + _dot(og_ref[...], wo_ref[:GLA_V, :]) + _dot(om_ref[...], wo_ref[GLA_V:, :])
    n2 = _rms(h1, gffn_ref[...]).astype(BF16)
    row = lax.broadcasted_iota(jnp.int32, (tm, FFN_CHUNK), 0)
    if not paged_prev:
        @pl.when(i % tiles_per_seq == 0)
        def _():
            carry_scr[...] = jnp.zeros(carry_scr.shape, F32)

    def gate_up(f):
        return _dot(n2, wg_ref[:, f:f + FFN_CHUNK]), _dot(n2, wu_ref[:, f:f + FFN_CHUNK])

    acc_scr[...] = jnp.zeros(acc_scr.shape, F32)
    nxt = gate_up(0)
    for f in range(0, D_FF, FFN_CHUNK):
        cols = slice(f, f + FFN_CHUNK)
        a, up = nxt
        if f + FFN_CHUNK < D_FF:
            nxt = gate_up(f + FFN_CHUNK)
        r1 = pltpu.roll(a, 1, axis=0)
        r2 = pltpu.roll(a, 2, axis=0)
        if paged_prev:
            t = row % seq_rows
            a1 = jnp.where(t >= 1, r1, p1_ref[:, cols])
            a2 = jnp.where(t >= 2, r2, p2_ref[:, cols])
        else:
            prev = carry_scr[:, cols]
            pm1 = prev[SUBLANES - 1:SUBLANES, :]
            pm2 = prev[SUBLANES - 2:SUBLANES - 1, :]
            a1 = jnp.where(row >= 1, r1, pm1)
            a2 = jnp.where(row >= 2, r2, jnp.where(row == 0, pm2, pm1))
            carry_scr[:, cols] = a[tm - SUBLANES:, :]
        tail_ref[:, cols] = a[tm - SUBLANES:, :] if not paged_prev else a
        conv = cb_ref[:, cols] + cw_ref[0:1, cols] * a2 + cw_ref[1:2, cols] * a1 + cw_ref[2:3, cols] * a
        gact = (conv / (1.0 + jnp.exp(-conv)) * up).astype(BF16)
        acc_scr[...] += _dot(gact, wd_ref[cols, :])
    h2 = h1 + acc_scr[...]
    n3 = _rms(h2, gple_ref[...]).astype(BF16)
    gate = 1.0 / (1.0 + jnp.exp(-_dot(n3, wpg_ref[...])))
    y_ref[...] = h2 + _dot(p_ref[...].astype(BF16), wpp_ref[...]) * gate


def _post(x2d, og, om, p2d, w, prev=None, *, tm, tiles_per_seq, seq_rows):
    T = x2d.shape[0]
    nt = T // tm
    row = lambda i: (i, 0)
    consts = [w['w_o'], w['g_ffn'], w['w_gate'], w['w_up'], w['conv_w'], w['conv_b'], w['w_down'],
              w['g_ple'], w['w_pgate'], w['w_pproj']]
    in_specs = [pl.BlockSpec((tm, D_MODEL), row), pl.BlockSpec((tm, GLA_V), row),
                pl.BlockSpec((tm, MLA_VALL), row), pl.BlockSpec((tm, PLE_DIM), row)]
    in_specs += [_const_spec(c.shape) for c in consts]
    args = [x2d, og, om, p2d, *consts]
    scratch = [pltpu.VMEM((tm, D_MODEL), F32)]
    if prev is not None:
        in_specs += [pl.BlockSpec((tm, D_FF), row)] * 2
        args += list(prev)
        tail_rows = tm
    else:
        scratch.append(pltpu.VMEM((SUBLANES, D_FF), F32))
        tail_rows = SUBLANES
    return pl.pallas_call(
        functools.partial(_post_kernel, tiles_per_seq=tiles_per_seq, seq_rows=seq_rows),
        grid=(nt,),
        in_specs=in_specs,
        out_specs=[pl.BlockSpec((tm, D_MODEL), row), pl.BlockSpec((tail_rows, D_FF), row)],
        out_shape=[jax.ShapeDtypeStruct((T, D_MODEL), F32), jax.ShapeDtypeStruct((nt * tail_rows, D_FF), F32)],
        scratch_shapes=scratch,
        compiler_params=pltpu.CompilerParams(dimension_semantics=("arbitrary",), vmem_limit_bytes=VMEM_LIMIT),
        name="post_sample" if prev is not None else "post_prompt",
    )(*args)


def _prep_weights(g_mix, w_in, gla_w_a2, gla_b_a, gla_g_out, mla_g_qa, mla_w_qup, mla_g_qn, mla_g_qr,
                  mla_g_kva, mla_g_kr, mla_w_kvup, mla_g_kn, w_o, g_ffn, ffn_w_gate, ffn_w_up, ffn_conv_w,
                  ffn_conv_b, ffn_w_down, g_ple, ple_w_gate, ple_w_proj):
    sizes = (GLA_QK, GLA_QK, GLA_V, GLA_V, GLA_GATE_RANK, MLA_Q_LORA, MLA_KV_LORA, MLA_ROPE)
    offs = np.concatenate([[0], np.cumsum(sizes)])
    piece = lambda i: w_in[:, offs[i]:offs[i + 1]]
    zeros = lambda n: jnp.zeros((D_MODEL, n), w_in.dtype)
    misc = jnp.concatenate([piece(4), zeros(ROPE_LO - GLA_GATE_RANK), piece(7),
                            zeros(LANES - ROPE_LO - MLA_ROPE)], axis=1)
    w_in_p = jnp.concatenate([piece(0), piece(1), piece(2), piece(3), piece(5), piece(6), misc], axis=1)
    w_a2 = jnp.concatenate([gla_w_a2, jnp.zeros((LANES - GLA_GATE_RANK, GLA_QK), gla_w_a2.dtype)], axis=0)

    def slab_vec(nope, rope_):
        one = jnp.concatenate([nope, rope_, jnp.zeros((HEAD_SLAB - MLA_NOPE - MLA_ROPE,), F32)])
        return jnp.tile(one, MLA_HEADS)[None, :]

    wq = mla_w_qup.reshape(MLA_Q_LORA, MLA_HEADS, MLA_NOPE + MLA_ROPE)
    swap = np.arange(MLA_NOPE + MLA_ROPE)
    swap[MLA_NOPE:] = np.concatenate([swap[MLA_NOPE + ROPE_HALF:], swap[MLA_NOPE:MLA_NOPE + ROPE_HALF]])
    slab_pad = ((0, 0), (0, 0), (0, HEAD_SLAB - MLA_NOPE - MLA_ROPE))
    wq = jnp.concatenate([jnp.pad(wq, slab_pad).reshape(MLA_Q_LORA, MLA_SLABS),
                          jnp.pad(wq[:, :, swap], slab_pad).reshape(MLA_Q_LORA, MLA_SLABS)], axis=1)
    g_q = jnp.concatenate([mla_g_qn, mla_g_qr])
    gq_slab = jnp.pad(jnp.stack([g_q, g_q[swap]]), ((0, 0), (0, HEAD_SLAB - MLA_NOPE - MLA_ROPE)))
    wkv = mla_w_kvup.reshape(MLA_KV_LORA, MLA_HEADS, MLA_NOPE + MLA_V)
    wk = wkv[:, :, :MLA_NOPE]
    wv = wkv[:, :, MLA_NOPE:].reshape(MLA_KV_LORA, MLA_VALL)
    wk_slab = jnp.pad(wk, ((0, 0), (0, 0), (0, HEAD_SLAB - MLA_NOPE))).reshape(MLA_KV_LORA, MLA_SLABS)
    wk_g = jnp.pad(wk * mla_g_kn[None, None, :], ((0, 0), (0, 0), (0, HEAD_SLAB - MLA_NOPE)))
    eye = jnp.eye(MLA_HEADS, dtype=F32)
    w_kabs = jnp.einsum('nhd,hg->hdgn', wk_g, eye).reshape(MLA_SLABS, MLA_HEADS * MLA_KV_LORA)
    w_kT = wk.transpose(1, 2, 0).reshape(MLA_HEADS * MLA_NOPE, MLA_KV_LORA)

    lane = np.arange(2 * HEAD_SLAB)
    seg = np.where(lane % HEAD_SLAB < MLA_NOPE, 0, np.where(lane % HEAD_SLAB < MLA_NOPE + MLA_ROPE, 1, 2))
    same = (lane[:, None] // HEAD_SLAB == lane[None, :] // HEAD_SLAB) & (seg[:, None] == seg[None, :])
    segm = np.where(same & (seg[:, None] == 0), 1.0 / MLA_NOPE, np.where(same & (seg[:, None] == 1), 1.0 / MLA_ROPE, 0.0))

    gkr = jnp.concatenate([jnp.zeros((ROPE_LO,), F32), mla_g_kr, jnp.zeros((LANES - ROPE_LO - MLA_ROPE,), F32)])
    return dict(
        g_mix=g_mix[None, :], w_in=w_in_p.astype(BF16), w_a2=w_a2.astype(BF16), b_a=gla_b_a[None, :],
        g_out=gla_g_out[None, :], g_qa=mla_g_qa[None, :], w_qup=wq.astype(BF16), segm=jnp.asarray(segm, BF16),
        gq_slab=gq_slab, g_kva=mla_g_kva[None, :], w_kslab=wk_slab.astype(BF16),
        w_vT=wv.T.astype(BF16),
        gk_slab=slab_vec(mla_g_kn, jnp.zeros((MLA_ROPE,), F32)),
        g_kn=mla_g_kn, gkr_slab=gkr[None, :], w_kabs=w_kabs.astype(BF16), w_kT=w_kT.astype(BF16),
        w_v=wv.astype(BF16), w_o=w_o.astype(BF16), g_ffn=g_ffn[None, :], w_gate=ffn_w_gate.astype(BF16),
        w_up=ffn_w_up.astype(BF16), conv_w=ffn_conv_w, conv_b=ffn_conv_b[None, :],
        w_down=ffn_w_down.astype(BF16), g_ple=g_ple[None, :], w_pgate=ple_w_gate.astype(BF16),
        w_pproj=ple_w_proj.astype(BF16))


def _rope_tables(pos):
    inv = ROPE_THETA ** (-jnp.arange(ROPE_HALF, dtype=F32) * 2.0 / MLA_ROPE)
    ang = pos.astype(F32)[:, None] * inv[None, :]
    cos, sin = jnp.cos(ang), jnp.sin(ang)
    T = pos.shape[0]
    pad = jnp.zeros((T, HEAD_SLAB - MLA_NOPE - MLA_ROPE), F32)
    cos_t = jnp.concatenate([jnp.ones((T, MLA_NOPE), F32), cos, cos, pad], axis=1)
    sin_t = jnp.concatenate([jnp.zeros((T, MLA_NOPE), F32), -sin, sin, pad], axis=1)
    return cos_t, sin_t


def _pick_tile(n, cap):
    t = math.gcd(n, cap)
    return t


def _state_to_bd(s):
    eye = jnp.eye(GLA_HEADS, dtype=s.dtype)
    return jnp.einsum('bhdv,hg->bhvgd', s, eye).reshape(s.shape[0], GLA_V, GLA_QK)


def _layer(w, x_p, x_s, p_p, p_s, pool_c, pool_r, layer, state_gla, state_conv, page_table):
    B, S, _ = x_p.shape
    Bd, Td, _ = x_s.shape
    tm = _pick_tile(S, 512)
    tps = S // tm
    cos_p, sin_p = _rope_tables(jnp.arange(S))
    xp2 = x_p.reshape(B * S, D_MODEL)
    gq, gk, gv, gg, la, qa, ckv_p, kr_p, ka, vaT = _in_proj(xp2, cos_p, sin_p, w, tm=tm, tiles_per_seq=tps,
                                                            sample=False)
    chunk_p = math.gcd(S, GLA_CHUNK)
    n_chunks = math.gcd(S // chunk_p, 8)
    og_p, sT_p = _gla(gq, gk, gv, gg, la, jnp.zeros((B, GLA_V, GLA_QK), F32), w['g_out'],
                      n_seq=B, seq_len=S, chunk=chunk_p, n_chunks=n_chunks)
    tq = _pick_tile(S, 512)
    om_p = _prompt_attn(qa, ka, vaT, n_seq=B, seq_len=S, tq=tq, tk=tq)
    y_p, tail_p = _post(xp2, og_p, om_p, p_p.reshape(B * S, PLE_DIM), w, tm=tm, tiles_per_seq=tps, seq_rows=None)
    conv_p = tail_p.reshape(B, tps, SUBLANES, D_FF)[:, -1, SUBLANES - (CONV_W - 1):, :]

    Ts = Bd * Td
    pos_s = PAST_LEN + jnp.arange(Td)
    cos_s, sin_s = _rope_tables(jnp.tile(pos_s, Bd))
    xs2 = x_s.reshape(Ts, D_MODEL)
    gq, gk, gv, gg, la, qa, ckv_s, kr_s, qabs = _in_proj(xs2, cos_s, sin_s, w, tm=Ts, tiles_per_seq=1, sample=True)
    chunk_s = math.gcd(Td, GLA_CHUNK)
    og_s, sT_s = _gla(gq, gk, gv, gg, la, _state_to_bd(state_gla), w['g_out'],
                      n_seq=Bd, seq_len=Td, chunk=chunk_s, n_chunks=Td // chunk_s)
    rows = Td * MLA_HEADS
    qrope = qa.reshape(Ts, MLA_HEADS, HEAD_SLAB)[:, :, ROPE_LO:ROPE_LO + MLA_ROPE].reshape(Bd, rows, MLA_ROPE)
    krT_new = jnp.pad(jnp.swapaxes(kr_s.reshape(Bd, Td, MLA_ROPE), 1, 2), ((0, 0), (0, 0), (0, PAGE_SIZE - Td)))
    om_s = _sample_attn(page_table, qabs.reshape(Bd, rows, MLA_KV_LORA), qrope,
                        ckv_s.reshape(Bd, Td, MLA_KV_LORA), krT_new,
                        w['w_kT'], w['w_v'], pool_c, jnp.swapaxes(pool_r, 2, 3), layer,
                        pages_per_step=math.gcd(page_table.shape[1], SAMPLE_PAGES_PER_STEP))
    zpad = lambda a, lo: jnp.pad(a, ((0, 0), (lo, Td - lo - a.shape[1]), (0, 0))).reshape(Ts, D_FF)
    prev1 = zpad(state_conv[:, 1:2], 0)
    prev2 = zpad(state_conv, 0)
    y_s, a_s = _post(xs2, og_s, om_s, p_s.reshape(Ts, PLE_DIM), w, prev=(prev1, prev2), tm=Ts, tiles_per_seq=1,
                     seq_rows=Td)
    full = jnp.concatenate([state_conv, a_s.reshape(Bd, Td, D_FF)], axis=1)
    conv_s = full[:, full.shape[1] - (CONV_W - 1):]
    return (y_p.reshape(B, S, D_MODEL), y_s.reshape(Bd, Td, D_MODEL),
            ckv_p.reshape(B, S, MLA_KV_LORA), kr_p.reshape(B, S, MLA_ROPE),
            sT_p.transpose(0, 1, 3, 2), conv_p,
            ckv_s.reshape(Bd, Td, MLA_KV_LORA), kr_s.reshape(Bd, Td, MLA_ROPE),
            sT_s.transpose(0, 1, 3, 2), conv_s)


def kernel(x_prompt, x_sample, cache_ckv, cache_krope, state_gla, state_conv, page_table, p_prompt, p_sample, g_mix, w_in, gla_w_a2, gla_b_a, gla_g_out, mla_g_qa, mla_w_qup, mla_g_qn, mla_g_qr, mla_g_kva, mla_g_kr, mla_w_kvup, mla_g_kn, w_o, g_ffn, ffn_w_gate, ffn_w_up, ffn_conv_w, ffn_conv_b, ffn_w_down, g_ple, ple_w_gate, ple_w_proj):
    depth = w_in.shape[0]
    per_layer = (g_mix, w_in, gla_w_a2, gla_b_a, gla_g_out, mla_g_qa, mla_w_qup, mla_g_qn, mla_g_qr, mla_g_kva,
                 mla_g_kr, mla_w_kvup, mla_g_kn, w_o, g_ffn, ffn_w_gate, ffn_w_up, ffn_conv_w, ffn_conv_b,
                 ffn_w_down, g_ple, ple_w_gate, ple_w_proj)
    y_p, y_s = x_prompt, x_sample
    outs = [[] for _ in range(8)]
    for i in range(depth):
        w = _prep_weights(*(a[i] for a in per_layer))
        res = _layer(w, y_p, y_s, p_prompt[i], p_sample[i], cache_ckv, cache_krope, i, state_gla[i],
                     state_conv[i], page_table)
        y_p, y_s = res[0], res[1]
        for lst, r in zip(outs, res[2:]):
            lst.append(r)
    return (y_p, y_s) + tuple(jnp.stack(lst) for lst in outs)
```

```python
import functools
import math

import jax
import jax.numpy as jnp
import numpy as np
from jax import lax
from jax.experimental import pallas as pl
from jax.experimental.pallas import tpu as pltpu

D_MODEL = 1024
PAST_LEN = 16384
PAGE_SIZE = 128
GLA_HEADS = 4
GLA_DK = 64
GLA_DV = 128
GLA_GATE_RANK = 16
GLA_GATE_NORM = 16.0
GLA_CHUNK = 32
MLA_HEADS = 8
MLA_Q_LORA = 256
MLA_KV_LORA = 128
MLA_NOPE = 64
MLA_ROPE = 32
MLA_V = 64
MLA_SCALE = (MLA_NOPE + MLA_ROPE) ** -0.5
LOG2E = math.log2(math.e)
ROPE_THETA = 10000.0
D_FF = 2816
CONV_W = 3
PLE_DIM = 256
EPS = 1e-6
NEG = -1e30

LANES = 128
SUBLANES = 8
HEAD_SLAB = LANES
ROPE_LO = MLA_NOPE
ROPE_HALF = MLA_ROPE // 2
GLA_QK = GLA_HEADS * GLA_DK
GLA_V = GLA_HEADS * GLA_DV
MLA_SLABS = MLA_HEADS * HEAD_SLAB
MLA_VALL = MLA_HEADS * MLA_V
C_GQ, C_GK, C_GV, C_GG = 0, 256, 512, 1024
C_MQ, C_MKV, C_MISC = 1536, 1792, 1920
IN_COLS_P = 2048
FFN_CHUNK = 256
SAMPLE_PAGES_PER_STEP = 32
PAGE_RING = 3
VMEM_LIMIT = 56 * 1024 * 1024

BF16 = jnp.bfloat16
F32 = jnp.float32


def _dot(a, b):
    return jnp.dot(a, b, preferred_element_type=F32)


def _dot_nt(a, b):
    return lax.dot_general(a, b, (((1,), (1,)), ((), ())), preferred_element_type=F32)


def _dot_tn(a, b):
    return lax.dot_general(a, b, (((0,), (0,)), ((), ())), preferred_element_type=F32)


def _rms(x, g):
    return x * lax.rsqrt(jnp.mean(x * x, axis=-1, keepdims=True) + EPS) * g


def _const_spec(shape):
    nd = len(shape)
    return pl.BlockSpec(shape, lambda *_: (0,) * nd, pipeline_mode=pl.Buffered(1))


def _rope_slab(y, cos_t, sin_t):
    lane = lax.broadcasted_iota(jnp.int32, y.shape, 1)
    swapped = jnp.where(lane < ROPE_LO + ROPE_HALF,
                        pltpu.roll(y, LANES - ROPE_HALF, axis=1),
                        pltpu.roll(y, ROPE_HALF, axis=1))
    return y * cos_t + swapped * sin_t


def _in_proj_kernel(x_ref, cos_ref, sin_ref, gmix_ref, win_ref, wa2_ref, ba_ref, gqa_ref, wqup_ref,
                    segm_ref, gq_ref, gkva_ref, gkr_ref, *rest, sample):
    if sample:
        (wkabs_ref, gq_o, gk_o, gv_o, gg_o, la_o, qa_o, ckv_o, kr_o, qabs_o) = rest
    else:
        (wk_ref, gk_ref, wvT_ref, gq_o, gk_o, gv_o, gg_o, la_o, qa_o, ckv_o, kr_o, ka_o, vaT_o) = rest
    x = x_ref[...]
    n = _rms(x, gmix_ref[...]).astype(BF16)
    u = _dot(n, win_ref[...])
    gq_o[...] = u[:, C_GQ:C_GQ + GLA_QK] * (GLA_DK ** -0.5)
    gk_o[...] = u[:, C_GK:C_GK + GLA_QK]
    gv_o[...] = u[:, C_GV:C_GV + GLA_V].astype(BF16)
    gg_o[...] = u[:, C_GG:C_GG + GLA_V]
    misc = u[:, C_MISC:C_MISC + LANES]
    z = _dot(misc.astype(BF16), wa2_ref[...]) + ba_ref[...]
    la_o[...] = (jnp.minimum(z, 0.0) - jnp.log1p(jnp.exp(-jnp.abs(z)))) * (1.0 / GLA_GATE_NORM)

    cos_t = cos_ref[...]
    sin_t = sin_ref[...]
    segm = segm_ref[...]

    lane = lax.broadcasted_iota(jnp.int32, misc.shape, 1)
    is_rope = (lane >= ROPE_LO) & (lane < ROPE_LO + MLA_ROPE)
    ms_r = jnp.sum(jnp.where(is_rope, misc * misc, 0.0), axis=-1, keepdims=True) * (1.0 / MLA_ROPE)
    kr_slab = _rope_slab(misc * lax.rsqrt(ms_r + EPS) * gkr_ref[...], cos_t, sin_t)
    kr_o[...] = kr_slab[:, ROPE_LO:ROPE_LO + MLA_ROPE]

    mkv = u[:, C_MKV:C_MKV + MLA_KV_LORA]
    ckv = _rms(mkv, gkva_ref[...])
    ckv_o[...] = ckv
    ckv_b = ckv.astype(BF16)
    if not sample:
        kn = _dot(ckv_b, wk_ref[...])
        vaT_o[...] = _dot_nt(wvT_ref[...], ckv_b).astype(BF16)

    cq = _rms(u[:, C_MQ:C_MQ + MLA_Q_LORA], gqa_ref[...]).astype(BF16)
    qf = _dot(cq, wqup_ref[...])
    q_gain = gq_ref[...]
    q_cos = cos_t * (q_gain[0:1] * (MLA_SCALE * LOG2E))
    q_sin = sin_t * (q_gain[1:2] * (MLA_SCALE * LOG2E))

    for pair in range(MLA_HEADS // 2):
        lo = pair * 2 * HEAD_SLAB
        q2 = qf[:, lo:lo + 2 * HEAD_SLAB]
        q_inv = lax.rsqrt(_dot((q2 * q2).astype(BF16), segm) + EPS)
        if not sample:
            k2 = kn[:, lo:lo + 2 * HEAD_SLAB]
            k2 = k2 * lax.rsqrt(_dot((k2 * k2).astype(BF16), segm) + EPS) * gk_ref[:, lo:lo + 2 * HEAD_SLAB]
        for j in range(2):
            sl = slice(lo + j * HEAD_SLAB, lo + (j + 1) * HEAD_SLAB)
            swapped = qf[:, MLA_SLABS + lo + j * HEAD_SLAB:MLA_SLABS + lo + (j + 1) * HEAD_SLAB]
            qh = q_inv[:, j * HEAD_SLAB:(j + 1) * HEAD_SLAB] * (qf[:, sl] * q_cos + swapped * q_sin)
            qa_o[:, sl] = qh.astype(BF16)
            if not sample:
                ka_o[:, sl] = (k2[:, j * HEAD_SLAB:(j + 1) * HEAD_SLAB] + kr_slab).astype(BF16)
    if sample:
        qabs_o[...] = _dot(qa_o[...], wkabs_ref[...]).astype(BF16)


def _in_proj(x2d, cos_t, sin_t, w, *, tm, tiles_per_seq, sample):
    T = x2d.shape[0]
    nt = T // tm
    row = lambda i: (i, 0)
    pos = lambda i: (i % tiles_per_seq, 0)
    consts = [w['g_mix'], w['w_in'], w['w_a2'], w['b_a'], w['g_qa'], w['w_qup'], w['segm'], w['gq_slab'],
              w['g_kva'], w['gkr_slab']]
    consts += [w['w_kabs']] if sample else [w['w_kslab'], w['gk_slab'], w['w_vT']]
    in_specs = [pl.BlockSpec((tm, D_MODEL), row), pl.BlockSpec((tm, LANES), pos), pl.BlockSpec((tm, LANES), pos)]
    in_specs += [_const_spec(c.shape) for c in consts]
    outs = [(GLA_QK, F32), (GLA_QK, F32), (GLA_V, BF16), (GLA_V, F32), (GLA_QK, F32),
            (MLA_SLABS, BF16), (MLA_KV_LORA, F32), (MLA_ROPE, F32), (MLA_SLABS, BF16)]
    out_specs = [pl.BlockSpec((tm, c), row) for c, _ in outs]
    out_shape = [jax.ShapeDtypeStruct((T, c), d) for c, d in outs]
    if not sample:
        out_specs.append(pl.BlockSpec((MLA_VALL, tm), lambda i: (0, i)))
        out_shape.append(jax.ShapeDtypeStruct((MLA_VALL, T), BF16))
    return pl.pallas_call(
        functools.partial(_in_proj_kernel, sample=sample),
        grid=(nt,),
        in_specs=in_specs,
        out_specs=out_specs,
        out_shape=out_shape,
        compiler_params=pltpu.CompilerParams(dimension_semantics=("parallel",), vmem_limit_bytes=VMEM_LIMIT),
        name="in_proj_sample" if sample else "in_proj_prompt",
    )(x2d, cos_t, sin_t, *consts)


def _gla_kernel(q_ref, k_ref, v_ref, gg_ref, la_ref, s0_ref, tril_ref, ones_ref, gout_ref,
                o_ref, sT_ref, st_scr, *, chunk, n_chunks):
    step = pl.program_id(1)

    @pl.when(step == 0)
    def _():
        st_scr[...] = jnp.zeros(st_scr.shape, F32)
        for h in range(GLA_HEADS):
            st_scr[h * GLA_DV:(h + 1) * GLA_DV, h * GLA_DK:(h + 1) * GLA_DK] = s0_ref[0, h]

    tc = chunk * n_chunks
    la = la_ref[...]
    la_hi = la.astype(BF16)
    la_lo = (la - la_hi.astype(F32)).astype(BF16)
    tril = tril_ref[...]
    ones = ones_ref[...]
    b = _dot(tril, la_hi) + _dot(tril, la_lo)
    bl = _dot(ones, la_hi) + _dot(ones, la_lo)
    q = q_ref[...]
    k = k_ref[...]
    v = v_ref[...]
    q_dec = (q * jnp.exp(b)).astype(BF16)
    k_inv = (k * jnp.exp(-b)).astype(BF16)
    k_end = (k * jnp.exp(bl - b)).astype(BF16)
    dec = jnp.exp(bl)

    r = lax.broadcasted_iota(jnp.int32, (tc, tc), 0)
    c = lax.broadcasted_iota(jnp.int32, (tc, tc), 1)
    causal = (r >= c) & ((r // chunk) == (c // chunk))
    sr = lax.broadcasted_iota(jnp.int32, (GLA_V, GLA_QK), 0) // GLA_DV
    sc = lax.broadcasted_iota(jnp.int32, (GLA_V, GLA_QK), 1) // GLA_DK
    head_diag = sr == sc

    o_intra = []
    for h in range(GLA_HEADS):
        qh = q_dec[:, h * GLA_DK:(h + 1) * GLA_DK]
        kh = k_inv[:, h * GLA_DK:(h + 1) * GLA_DK]
        a = jnp.where(causal, _dot_nt(qh, kh), 0.0).astype(BF16)
        o_intra.append(_dot(a, v[:, h * GLA_DV:(h + 1) * GLA_DV]))
    o_intra = jnp.concatenate(o_intra, axis=1)

    st = st_scr[...]
    o_inter = []
    for ci in range(n_chunks):
        rows = slice(ci * chunk, (ci + 1) * chunk)
        o_inter.append(_dot_nt(q_dec[rows], st.astype(BF16)))
        xt = _dot_tn(v[rows], k_end[rows])
        st = st * dec[ci * chunk:ci * chunk + 1, :] + jnp.where(head_diag, xt, 0.0)
    st_scr[...] = st
    o = o_intra + (jnp.concatenate(o_inter, axis=0) if n_chunks > 1 else o_inter[0])

    gg = gg_ref[...]
    gate = gg / (1.0 + jnp.exp(-gg))
    gout = gout_ref[...]
    for h in range(GLA_HEADS):
        sl = slice(h * GLA_DV, (h + 1) * GLA_DV)
        o_ref[:, sl] = (_rms(o[:, sl], gout) * gate[:, sl]).astype(o_ref.dtype)

    @pl.when(step == pl.num_programs(1) - 1)
    def _():
        for h in range(GLA_HEADS):
            sT_ref[0, h] = st[h * GLA_DV:(h + 1) * GLA_DV, h * GLA_DK:(h + 1) * GLA_DK]


def _gla(q, k, v, gg, la, s0T, g_out, *, n_seq, seq_len, chunk, n_chunks):
    tc = chunk * n_chunks
    steps = seq_len // tc
    idx = np.arange(tc)
    same = (idx[:, None] // chunk) == (idx[None, :] // chunk)
    tril = jnp.asarray(same & (idx[:, None] >= idx[None, :]), BF16)
    ones = jnp.asarray(same, BF16)
    row = lambda b, s: (b * steps + s, 0)
    return pl.pallas_call(
        functools.partial(_gla_kernel, chunk=chunk, n_chunks=n_chunks),
        grid=(n_seq, steps),
        in_specs=[pl.BlockSpec((tc, GLA_QK), row), pl.BlockSpec((tc, GLA_QK), row), pl.BlockSpec((tc, GLA_V), row),
                  pl.BlockSpec((tc, GLA_V), row), pl.BlockSpec((tc, GLA_QK), row),
                  pl.BlockSpec((1, GLA_HEADS, GLA_DV, GLA_DK), lambda b, s: (b, 0, 0, 0)),
                  _const_spec((tc, tc)), _const_spec((tc, tc)), _const_spec((1, GLA_DV))],
        out_specs=[pl.BlockSpec((tc, GLA_V), row),
                   pl.BlockSpec((1, GLA_HEADS, GLA_DV, GLA_DK), lambda b, s: (b, 0, 0, 0))],
        out_shape=[jax.ShapeDtypeStruct((n_seq * seq_len, GLA_V), BF16),
                   jax.ShapeDtypeStruct((n_seq, GLA_HEADS, GLA_DV, GLA_DK), F32)],
        scratch_shapes=[pltpu.VMEM((GLA_V, GLA_QK), F32)],
        compiler_params=pltpu.CompilerParams(dimension_semantics=("parallel", "arbitrary"),
                                             vmem_limit_bytes=VMEM_LIMIT),
        name=f"gla_c{chunk}",
    )(q, k, v, gg, la, s0T, tril, ones, g_out)


def _prompt_attn_kernel(q_ref, k_ref, vT_ref, o_ref, s_scr, *, tq, tk):
    assert tq == tk
    qi = pl.program_id(2)
    key = lax.broadcasted_iota(jnp.int32, (tk, tq), 0)
    qry = lax.broadcasted_iota(jnp.int32, (tk, tq), 1)

    def scores(j, kb, masked):
        ks = pl.multiple_of(kb * tk, tk)
        s = _dot_nt(k_ref[pl.ds(ks, tk), j * HEAD_SLAB:(j + 1) * HEAD_SLAB],
                    q_ref[:, j * HEAD_SLAB:(j + 1) * HEAD_SLAB])
        if masked:
            s = jnp.where(key <= qry, s, NEG)
        s_scr[j] = s
        return jnp.max(s, axis=0, keepdims=True)

    def update(j, kb, m_blk, state):
        m, l, acc = state
        ks = pl.multiple_of(kb * tk, tk)
        m_new = jnp.maximum(m, m_blk)
        alpha = jnp.exp2(m - m_new)
        p = jnp.exp2(s_scr[j] - m_new)
        l = alpha * l + jnp.sum(p, axis=0, keepdims=True)
        acc = alpha * acc + _dot(vT_ref[j * MLA_V:(j + 1) * MLA_V, pl.ds(ks, tk)], p.astype(BF16))
        return m_new, l, acc

    def body(t, carry):
        blk1, st0, st1 = carry
        kb = qi - 1 - t
        blk0 = scores(0, kb, False)
        st1 = update(1, kb + 1, blk1, st1)
        blk1 = scores(1, kb, False)
        st0 = update(0, kb, blk0, st0)
        return blk1, st0, st1

    init = (jnp.full((1, tq), NEG, F32), jnp.zeros((1, tq), F32), jnp.zeros((MLA_V, tq), F32))
    blk0 = scores(0, qi, True)
    blk1 = scores(1, qi, True)
    st0 = update(0, qi, blk0, init)
    blk1, st0, st1 = lax.fori_loop(0, qi, body, (blk1, st0, init))
    st1 = update(1, 0, blk1, st1)
    o_ref[...] = jnp.concatenate([acc / l for _, l, acc in (st0, st1)], axis=0).T.astype(o_ref.dtype)


def _prompt_attn(qa, ka, va, *, n_seq, seq_len, tq, tk):
    nq = seq_len // tq
    return pl.pallas_call(
        functools.partial(_prompt_attn_kernel, tq=tq, tk=tk),
        grid=(n_seq, MLA_HEADS // 2, nq),
        in_specs=[pl.BlockSpec((tq, 2 * HEAD_SLAB), lambda b, hp, i: (b * nq + i, hp)),
                  pl.BlockSpec((seq_len, 2 * HEAD_SLAB), lambda b, hp, i: (b, hp)),
                  pl.BlockSpec((2 * MLA_V, seq_len), lambda b, hp, i: (hp, b))],
        out_specs=pl.BlockSpec((tq, 2 * MLA_V), lambda b, hp, i: (b * nq + i, hp)),
        out_shape=jax.ShapeDtypeStruct((n_seq * seq_len, MLA_VALL), BF16),
        scratch_shapes=[pltpu.VMEM((2, tk, tq), F32)],
        compiler_params=pltpu.CompilerParams(dimension_semantics=("parallel", "parallel", "arbitrary"),
                                             vmem_limit_bytes=VMEM_LIMIT),
        name="prompt_attn",
    )(qa, ka, va)


def _sample_attn_kernel(pt_ref, qabs_ref, qrope_ref, cnew_ref, krnew_ref, wkT_ref, wv_ref, poolc_hbm, poolr_hbm,
                        o_ref, lhs_scr, m_scr, l_scr, acc_scr, cb_scr, krT_scr, s_scr, cbuf, rbuf, sem,
                        *, n_tok, pages_per_step, layer):
    G = pages_per_step
    b = pl.program_id(0)
    g = pl.program_id(1)
    n_g = pl.num_programs(1)
    step = b * n_g + g
    n_steps = pl.num_programs(0) * n_g
    last = step == n_steps - 1
    slot = step % PAGE_RING

    def group_of(s):
        s = jnp.minimum(s, n_steps - 1)
        return s // n_g, s % n_g
    rows = n_tok * MLA_HEADS
    n_k = MLA_HEADS * MLA_NOPE

    def page_copies(sl, group=None):
        copies = []
        for j in range(G):
            page = 0 if group is None else pt_ref[group[0], group[1] * G + j]
            copies.append(pltpu.make_async_copy(poolc_hbm.at[layer, page], cbuf.at[sl, j], sem.at[0, sl]))
            copies.append(pltpu.make_async_copy(poolr_hbm.at[layer, page], rbuf.at[sl, j], sem.at[1, sl]))
        return copies

    @pl.when(step == 0)
    def _():
        for d in range(PAGE_RING - 1):
            for cp in page_copies(d, group_of(d)):
                cp.start()

    for cp in page_copies(slot):
        cp.wait()

    def scores(cb, krT):
        nk = cb.shape[0]
        big = _dot_nt(lhs_scr[...], cb)
        kvn = big[:n_k]
        s_nope = big[n_k:n_k + rows]
        ss = jnp.sum((kvn * kvn).reshape(MLA_HEADS, MLA_NOPE, nk), axis=1)
        inv = lax.rsqrt(ss * (1.0 / MLA_NOPE) + EPS)
        s_rope = _dot(qrope_ref[0], krT)
        return (s_nope.reshape(n_tok, MLA_HEADS, nk) * inv[None]).reshape(rows, nk) + s_rope

    def weights(s):
        m = m_scr[...]
        m_new = jnp.maximum(m, jnp.max(s, axis=-1, keepdims=True))
        alpha = jnp.exp2(m - m_new)
        p = jnp.exp2(s - m_new)
        l_scr[...] = alpha * l_scr[...] + jnp.sum(p, axis=-1, keepdims=True)
        m_scr[...] = m_new
        return alpha, p.astype(BF16)

    def accumulate(alpha, p, cb):
        acc_scr[...] = alpha * acc_scr[...] + _dot(p, cb)

    def absorb(s, cb):
        accumulate(*weights(s), cb)

    cur = g % 2
    prv = 1 - cur

    @pl.when(g == 0)
    def _():
        lhs_scr[:n_k, :] = wkT_ref[...]
        lhs_scr[n_k:n_k + rows, :] = qabs_ref[0]
        m_scr[...] = jnp.full(m_scr.shape, NEG, F32)
        l_scr[...] = jnp.zeros(l_scr.shape, F32)
        acc_scr[...] = jnp.zeros(acc_scr.shape, F32)
        s_scr[prv] = jnp.full(s_scr.shape[1:], NEG, F32)
        cb_scr[prv] = jnp.zeros(cb_scr.shape[1:], BF16)
        c_new = jnp.concatenate([cnew_ref[0], jnp.zeros((PAGE_SIZE - n_tok, MLA_KV_LORA), F32)],
                                axis=0).astype(BF16)
        key = lax.broadcasted_iota(jnp.int32, (rows, PAGE_SIZE), 1)
        tok = lax.broadcasted_iota(jnp.int32, (rows, PAGE_SIZE), 0) // MLA_HEADS
        absorb(jnp.where(key <= tok, scores(c_new, krnew_ref[0].astype(BF16)), NEG), c_new)

    for j in range(G):
        cb_scr[cur, j * PAGE_SIZE:(j + 1) * PAGE_SIZE, :] = cbuf[slot, j].astype(BF16)
        krT_scr[:, j * PAGE_SIZE:(j + 1) * PAGE_SIZE] = rbuf[slot, j].astype(BF16)

    ahead = step + (PAGE_RING - 1)
    for cp in page_copies(ahead % PAGE_RING, group_of(ahead)):
        cp.start()

    alpha, p = weights(s_scr[prv])
    s_scr[cur] = scores(cb_scr[cur], krT_scr[...])
    accumulate(alpha, p, cb_scr[prv])

    @pl.when(last)
    def _():
        for d in range(1, PAGE_RING):
            for cp in page_copies((step + d) % PAGE_RING):
                cp.wait()

    @pl.when(g == n_g - 1)
    def _():
        absorb(s_scr[cur], cb_scr[cur])
        o_lat =(acc_scr[...] / l_scr[...]).astype(BF16)
        full = _dot(o_lat, wv_ref[...])
        full = full.reshape(n_tok, MLA_HEADS, MLA_VALL)
        hr = lax.broadcasted_iota(jnp.int32, (MLA_HEADS, MLA_VALL), 0)
        hc = lax.broadcasted_iota(jnp.int32, (MLA_HEADS, MLA_VALL), 1) // MLA_V
        o_ref[...] = jnp.sum(jnp.where((hr == hc)[None], full, 0.0), axis=1).astype(o_ref.dtype)


def _sample_attn(page_table, qabs, qrope, c_new, kr_new, w_kT, w_v, pool_c, pool_r, layer, *, pages_per_step):
    n_b, n_pages = page_table.shape
    n_tok = c_new.shape[1]
    rows = n_tok * MLA_HEADS
    G = pages_per_step
    n_k = MLA_HEADS * MLA_NOPE

    per_b = lambda b, g, pt: (b, 0, 0)
    grid_spec = pltpu.PrefetchScalarGridSpec(
        num_scalar_prefetch=1,
        grid=(n_b, n_pages // G),
        in_specs=[pl.BlockSpec((1, rows, MLA_KV_LORA), per_b), pl.BlockSpec((1, rows, MLA_ROPE), per_b),
                  pl.BlockSpec((1, n_tok, MLA_KV_LORA), per_b), pl.BlockSpec((1, MLA_ROPE, PAGE_SIZE), per_b),
                  pl.BlockSpec((n_k, MLA_KV_LORA), lambda b, g, pt: (0, 0)),
                  pl.BlockSpec((MLA_KV_LORA, MLA_VALL), lambda b, g, pt: (0, 0)),
                  pl.BlockSpec(memory_space=pl.ANY), pl.BlockSpec(memory_space=pl.ANY)],
        out_specs=pl.BlockSpec((n_tok, MLA_VALL), lambda b, g, pt: (b, 0)),
        scratch_shapes=[pltpu.VMEM((n_k + rows, MLA_KV_LORA), BF16), pltpu.VMEM((rows, 1), F32),
                        pltpu.VMEM((rows, 1), F32), pltpu.VMEM((rows, MLA_KV_LORA), F32),
                        pltpu.VMEM((2, G * PAGE_SIZE, MLA_KV_LORA), BF16), pltpu.VMEM((MLA_ROPE, G * PAGE_SIZE), BF16),
                        pltpu.VMEM((2, rows, G * PAGE_SIZE), F32),
                        pltpu.VMEM((PAGE_RING, G, PAGE_SIZE, MLA_KV_LORA), F32),
                        pltpu.VMEM((PAGE_RING, G, MLA_ROPE, PAGE_SIZE), F32),
                        pltpu.SemaphoreType.DMA((2, PAGE_RING))])
    return pl.pallas_call(
        functools.partial(_sample_attn_kernel, n_tok=n_tok, pages_per_step=G, layer=layer),
        grid_spec=grid_spec,
        out_shape=jax.ShapeDtypeStruct((n_b * n_tok, MLA_VALL), BF16),
        compiler_params=pltpu.CompilerParams(dimension_semantics=("arbitrary", "arbitrary"),
                                             vmem_limit_bytes=VMEM_LIMIT),
        name="sample_attn",
    )(page_table, qabs, qrope, c_new, kr_new, w_kT, w_v, pool_c, pool_r)


def _post_kernel(x_ref, og_ref, om_ref, p_ref, wo_ref, gffn_ref, wg_ref, wu_ref, cw_ref, cb_ref, wd_ref,
                 gple_ref, wpg_ref, wpp_ref, *rest, tiles_per_seq, seq_rows):
    paged_prev = seq_rows is not None
    if paged_prev:
        p1_ref, p2_ref, y_ref, tail_ref, acc_scr = rest
    else:
        y_ref, tail_ref, acc_scr, carry_scr = rest
    tm = x_ref.shape[0]
    i = pl.program_id(0)
    h1 = x_ref[...] + _dot(og_ref[...], wo_ref[:GLA_V, :]) + _dot(om_ref[...], wo_ref[GLA_V:, :])
    n2 = _rms(h1, gffn_ref[...]).astype(BF16)
    row = lax.broadcasted_iota(jnp.int32, (tm, FFN_CHUNK), 0)
    if not paged_prev:
        @pl.when(i % tiles_per_seq == 0)
        def _():
            carry_scr[...] = jnp.zeros(carry_scr.shape, F32)

    def gate_up(f):
        return _dot(n2, wg_ref[:, f:f + FFN_CHUNK]), _dot(n2, wu_ref[:, f:f + FFN_CHUNK])

    acc_scr[...] = jnp.zeros(acc_scr.shape, F32)
    nxt = gate_up(0)
    for f in range(0, D_FF, FFN_CHUNK):
        cols = slice(f, f + FFN_CHUNK)
        a, up = nxt
        if f + FFN_CHUNK < D_FF:
            nxt = gate_up(f + FFN_CHUNK)
        r1 = pltpu.roll(a, 1, axis=0)
        r2 = pltpu.roll(a, 2, axis=0)
        if paged_prev:
            t = row % seq_rows
            a1 = jnp.where(t >= 1, r1, p1_ref[:, cols])
            a2 = jnp.where(t >= 2, r2, p2_ref[:, cols])
        else:
            prev = carry_scr[:, cols]
            pm1 = prev[SUBLANES - 1:SUBLANES, :]
            pm2 = prev[SUBLANES - 2:SUBLANES - 1, :]
            a1 = jnp.where(row >= 1, r1, pm1)
            a2 = jnp.where(row >= 2, r2, jnp.where(row == 0, pm2, pm1))
            carry_scr[:, cols] = a[tm - SUBLANES:, :]
        tail_ref[:, cols] = a[tm - SUBLANES:, :] if not paged_prev else a
        conv = cb_ref[:, cols] + cw_ref[0:1, cols] * a2 + cw_ref[1:2, cols] * a1 + cw_ref[2:3, cols] * a
        gact = (conv / (1.0 + jnp.exp(-conv)) * up).astype(BF16)
        acc_scr[...] += _dot(gact, wd_ref[cols, :])
    h2 = h1 + acc_scr[...]
    n3 = _rms(h2, gple_ref[...]).astype(BF16)
    gate = 1.0 / (1.0 + jnp.exp(-_dot(n3, wpg_ref[...])))
    y_ref[...] = h2 + _dot(p_ref[...].astype(BF16), wpp_ref[...]) * gate


def _post(x2d, og, om, p2d, w, prev=None, *, tm, tiles_per_seq, seq_rows):
    T = x2d.shape[0]
    nt = T // tm
    row = lambda i: (i, 0)
    consts = [w['w_o'], w['g_ffn'], w['w_gate'], w['w_up'], w['conv_w'], w['conv_b'], w['w_down'],
              w['g_ple'], w['w_pgate'], w['w_pproj']]
    in_specs = [pl.BlockSpec((tm, D_MODEL), row), pl.BlockSpec((tm, GLA_V), row),
                pl.BlockSpec((tm, MLA_VALL), row), pl.BlockSpec((tm, PLE_DIM), row)]
    in_specs += [_const_spec(c.shape) for c in consts]
    args = [x2d, og, om, p2d, *consts]
    scratch = [pltpu.VMEM((tm, D_MODEL), F32)]
    if prev is not None:
        in_specs += [pl.BlockSpec((tm, D_FF), row)] * 2
        args += list(prev)
        tail_rows = tm
    else:
        scratch.append(pltpu.VMEM((SUBLANES, D_FF), F32))
        tail_rows = SUBLANES
    return pl.pallas_call(
        functools.partial(_post_kernel, tiles_per_seq=tiles_per_seq, seq_rows=seq_rows),
        grid=(nt,),
        in_specs=in_specs,
        out_specs=[pl.BlockSpec((tm, D_MODEL), row), pl.BlockSpec((tail_rows, D_FF), row)],
        out_shape=[jax.ShapeDtypeStruct((T, D_MODEL), F32), jax.ShapeDtypeStruct((nt * tail_rows, D_FF), F32)],
        scratch_shapes=scratch,
        compiler_params=pltpu.CompilerParams(dimension_semantics=("arbitrary",), vmem_limit_bytes=VMEM_LIMIT),
        name="post_sample" if prev is not None else "post_prompt",
    )(*args)


def _prep_weights(g_mix, w_in, gla_w_a2, gla_b_a, gla_g_out, mla_g_qa, mla_w_qup, mla_g_qn, mla_g_qr,
                  mla_g_kva, mla_g_kr, mla_w_kvup, mla_g_kn, w_o, g_ffn, ffn_w_gate, ffn_w_up, ffn_conv_w,
                  ffn_conv_b, ffn_w_down, g_ple, ple_w_gate, ple_w_proj):
    sizes = (GLA_QK, GLA_QK, GLA_V, GLA_V, GLA_GATE_RANK, MLA_Q_LORA, MLA_KV_LORA, MLA_ROPE)
    offs = np.concatenate([[0], np.cumsum(sizes)])
    piece = lambda i: w_in[:, offs[i]:offs[i + 1]]
    zeros = lambda n: jnp.zeros((D_MODEL, n), w_in.dtype)
    misc = jnp.concatenate([piece(4), zeros(ROPE_LO - GLA_GATE_RANK), piece(7),
                            zeros(LANES - ROPE_LO - MLA_ROPE)], axis=1)
    w_in_p = jnp.concatenate([piece(0), piece(1), piece(2), piece(3), piece(5), piece(6), misc], axis=1)
    w_a2 = jnp.concatenate([gla_w_a2, jnp.zeros((LANES - GLA_GATE_RANK, GLA_QK), gla_w_a2.dtype)], axis=0)

    def slab_vec(nope, rope_):
        one = jnp.concatenate([nope, rope_, jnp.zeros((HEAD_SLAB - MLA_NOPE - MLA_ROPE,), F32)])
        return jnp.tile(one, MLA_HEADS)[None, :]

    wq = mla_w_qup.reshape(MLA_Q_LORA, MLA_HEADS, MLA_NOPE + MLA_ROPE)
    swap = np.arange(MLA_NOPE + MLA_ROPE)
    swap[MLA_NOPE:] = np.concatenate([swap[MLA_NOPE + ROPE_HALF:], swap[MLA_NOPE:MLA_NOPE + ROPE_HALF]])
    slab_pad = ((0, 0), (0, 0), (0, HEAD_SLAB - MLA_NOPE - MLA_ROPE))
    wq = jnp.concatenate([jnp.pad(wq, slab_pad).reshape(MLA_Q_LORA, MLA_SLABS),
                          jnp.pad(wq[:, :, swap], slab_pad).reshape(MLA_Q_LORA, MLA_SLABS)], axis=1)
    g_q = jnp.concatenate([mla_g_qn, mla_g_qr])
    gq_slab = jnp.pad(jnp.stack([g_q, g_q[swap]]), ((0, 0), (0, HEAD_SLAB - MLA_NOPE - MLA_ROPE)))
    wkv = mla_w_kvup.reshape(MLA_KV_LORA, MLA_HEADS, MLA_NOPE + MLA_V)
    wk = wkv[:, :, :MLA_NOPE]
    wv = wkv[:, :, MLA_NOPE:].reshape(MLA_KV_LORA, MLA_VALL)
    wk_slab = jnp.pad(wk, ((0, 0), (0, 0), (0, HEAD_SLAB - MLA_NOPE))).reshape(MLA_KV_LORA, MLA_SLABS)
    wk_g = jnp.pad(wk * mla_g_kn[None, None, :], ((0, 0), (0, 0), (0, HEAD_SLAB - MLA_NOPE)))
    eye = jnp.eye(MLA_HEADS, dtype=F32)
    w_kabs = jnp.einsum('nhd,hg->hdgn', wk_g, eye).reshape(MLA_SLABS, MLA_HEADS * MLA_KV_LORA)
    w_kT = wk.transpose(1, 2, 0).reshape(MLA_HEADS * MLA_NOPE, MLA_KV_LORA)

    lane = np.arange(2 * HEAD_SLAB)
    seg = np.where(lane % HEAD_SLAB < MLA_NOPE, 0, np.where(lane % HEAD_SLAB < MLA_NOPE + MLA_ROPE, 1, 2))
    same = (lane[:, None] // HEAD_SLAB == lane[None, :] // HEAD_SLAB) & (seg[:, None] == seg[None, :])
    segm = np.where(same & (seg[:, None] == 0), 1.0 / MLA_NOPE, np.where(same & (seg[:, None] == 1), 1.0 / MLA_ROPE, 0.0))

    gkr = jnp.concatenate([jnp.zeros((ROPE_LO,), F32), mla_g_kr, jnp.zeros((LANES - ROPE_LO - MLA_ROPE,), F32)])
    return dict(
        g_mix=g_mix[None, :], w_in=w_in_p.astype(BF16), w_a2=w_a2.astype(BF16), b_a=gla_b_a[None, :],
        g_out=gla_g_out[None, :], g_qa=mla_g_qa[None, :], w_qup=wq.astype(BF16), segm=jnp.asarray(segm, BF16),
        gq_slab=gq_slab, g_kva=mla_g_kva[None, :], w_kslab=wk_slab.astype(BF16),
        w_vT=wv.T.astype(BF16),
        gk_slab=slab_vec(mla_g_kn, jnp.zeros((MLA_ROPE,), F32)),
        g_kn=mla_g_kn, gkr_slab=gkr[None, :], w_kabs=w_kabs.astype(BF16), w_kT=w_kT.astype(BF16),
        w_v=wv.astype(BF16), w_o=w_o.astype(BF16), g_ffn=g_ffn[None, :], w_gate=ffn_w_gate.astype(BF16),
        w_up=ffn_w_up.astype(BF16), conv_w=ffn_conv_w, conv_b=ffn_conv_b[None, :],
        w_down=ffn_w_down.astype(BF16), g_ple=g_ple[None, :], w_pgate=ple_w_gate.astype(BF16),
        w_pproj=ple_w_proj.astype(BF16))


def _rope_tables(pos):
    inv = ROPE_THETA ** (-jnp.arange(ROPE_HALF, dtype=F32) * 2.0 / MLA_ROPE)
    ang = pos.astype(F32)[:, None] * inv[None, :]
    cos, sin = jnp.cos(ang), jnp.sin(ang)
    T = pos.shape[0]
    pad = jnp.zeros((T, HEAD_SLAB - MLA_NOPE - MLA_ROPE), F32)
    cos_t = jnp.concatenate([jnp.ones((T, MLA_NOPE), F32), cos, cos, pad], axis=1)
    sin_t = jnp.concatenate([jnp.zeros((T, MLA_NOPE), F32), -sin, sin, pad], axis=1)
    return cos_t, sin_t


def _pick_tile(n, cap):
    t = math.gcd(n, cap)
    return t


def _layer(w, x_p, x_s, p_p, p_s, pool_c, pool_r, layer, state_gla, state_conv, page_table):
    B, S, _ = x_p.shape
    Bd, Td, _ = x_s.shape
    tm = _pick_tile(S, 512)
    tps = S // tm
    cos_p, sin_p = _rope_tables(jnp.arange(S))
    xp2 = x_p.reshape(B * S, D_MODEL)
    gq, gk, gv, gg, la, qa, ckv_p, kr_p, ka, vaT = _in_proj(xp2, cos_p, sin_p, w, tm=tm, tiles_per_seq=tps,
                                                            sample=False)
    chunk_p = math.gcd(S, GLA_CHUNK)
    n_chunks = math.gcd(S // chunk_p, 8)
    og_p, sT_p = _gla(gq, gk, gv, gg, la, jnp.zeros((B, GLA_HEADS, GLA_DV, GLA_DK), F32), w['g_out'],
                      n_seq=B, seq_len=S, chunk=chunk_p, n_chunks=n_chunks)
    tq = _pick_tile(S, 512)
    om_p = _prompt_attn(qa, ka, vaT, n_seq=B, seq_len=S, tq=tq, tk=tq)
    y_p, tail_p = _post(xp2, og_p, om_p, p_p.reshape(B * S, PLE_DIM), w, tm=tm, tiles_per_seq=tps, seq_rows=None)
    conv_p = tail_p.reshape(B, tps, SUBLANES, D_FF)[:, -1, SUBLANES - (CONV_W - 1):, :]

    Ts = Bd * Td
    pos_s = PAST_LEN + jnp.arange(Td)
    cos_s, sin_s = _rope_tables(jnp.tile(pos_s, Bd))
    xs2 = x_s.reshape(Ts, D_MODEL)
    gq, gk, gv, gg, la, qa, ckv_s, kr_s, qabs = _in_proj(xs2, cos_s, sin_s, w, tm=Ts, tiles_per_seq=1, sample=True)
    chunk_s = math.gcd(Td, GLA_CHUNK)
    og_s, sT_s = _gla(gq, gk, gv, gg, la, state_gla.transpose(0, 1, 3, 2), w['g_out'],
                      n_seq=Bd, seq_len=Td, chunk=chunk_s, n_chunks=Td // chunk_s)
    rows = Td * MLA_HEADS
    qrope = qa.reshape(Ts, MLA_HEADS, HEAD_SLAB)[:, :, ROPE_LO:ROPE_LO + MLA_ROPE].reshape(Bd, rows, MLA_ROPE)
    krT_new = jnp.pad(jnp.swapaxes(kr_s.reshape(Bd, Td, MLA_ROPE), 1, 2), ((0, 0), (0, 0), (0, PAGE_SIZE - Td)))
    om_s = _sample_attn(page_table, qabs.reshape(Bd, rows, MLA_KV_LORA), qrope,
                        ckv_s.reshape(Bd, Td, MLA_KV_LORA), krT_new,
                        w['w_kT'], w['w_v'], pool_c, jnp.swapaxes(pool_r, 2, 3), layer,
                        pages_per_step=math.gcd(page_table.shape[1], SAMPLE_PAGES_PER_STEP))
    zpad = lambda a, lo: jnp.pad(a, ((0, 0), (lo, Td - lo - a.shape[1]), (0, 0))).reshape(Ts, D_FF)
    prev1 = zpad(state_conv[:, 1:2], 0)
    prev2 = zpad(state_conv, 0)
    y_s, a_s = _post(xs2, og_s, om_s, p_s.reshape(Ts, PLE_DIM), w, prev=(prev1, prev2), tm=Ts, tiles_per_seq=1,
                     seq_rows=Td)
    full = jnp.concatenate([state_conv, a_s.reshape(Bd, Td, D_FF)], axis=1)
    conv_s = full[:, full.shape[1] - (CONV_W - 1):]
    return (y_p.reshape(B, S, D_MODEL), y_s.reshape(Bd, Td, D_MODEL),
            ckv_p.reshape(B, S, MLA_KV_LORA), kr_p.reshape(B, S, MLA_ROPE),
            sT_p.transpose(0, 1, 3, 2), conv_p,
            ckv_s.reshape(Bd, Td, MLA_KV_LORA), kr_s.reshape(Bd, Td, MLA_ROPE),
            sT_s.transpose(0, 1, 3, 2), conv_s)


def kernel(x_prompt, x_sample, cache_ckv, cache_krope, state_gla, state_conv, page_table, p_prompt, p_sample, g_mix, w_in, gla_w_a2, gla_b_a, gla_g_out, mla_g_qa, mla_w_qup, mla_g_qn, mla_g_qr, mla_g_kva, mla_g_kr, mla_w_kvup, mla_g_kn, w_o, g_ffn, ffn_w_gate, ffn_w_up, ffn_conv_w, ffn_conv_b, ffn_w_down, g_ple, ple_w_gate, ple_w_proj):
    depth = w_in.shape[0]
    per_layer = (g_mix, w_in, gla_w_a2, gla_b_a, gla_g_out, mla_g_qa, mla_w_qup, mla_g_qn, mla_g_qr, mla_g_kva,
                 mla_g_kr, mla_w_kvup, mla_g_kn, w_o, g_ffn, ffn_w_gate, ffn_w_up, ffn_conv_w, ffn_conv_b,
                 ffn_w_down, g_ple, ple_w_gate, ple_w_proj)
    y_p, y_s = x_prompt, x_sample
    outs = [[] for _ in range(8)]
    for i in range(depth):
        w = _prep_weights(*(a[i] for a in per_layer))
        res = _layer(w, y_p, y_s, p_prompt[i], p_sample[i], cache_ckv, cache_krope, i, state_gla[i],
                     state_conv[i], page_table)
        y_p, y_s = res[0], res[1]
        for lst, r in zip(outs, res[2:]):
            lst.append(r)
    return (y_p, y_s) + tuple(jnp.stack(lst) for lst in outs)
```

```python
import functools
import math

import jax
import jax.numpy as jnp
import numpy as np
from jax import lax
from jax.experimental import pallas as pl
from jax.experimental.pallas import tpu as pltpu

D_MODEL = 1024
PAST_LEN = 16384
PAGE_SIZE = 128
GLA_HEADS = 4
GLA_DK = 64
GLA_DV = 128
GLA_GATE_RANK = 16
GLA_GATE_NORM = 16.0
GLA_CHUNK = 32
MLA_HEADS = 8
MLA_Q_LORA = 256
MLA_KV_LORA = 128
MLA_NOPE = 64
MLA_ROPE = 32
MLA_V = 64
MLA_SCALE = (MLA_NOPE + MLA_ROPE) ** -0.5
LOG2E = math.log2(math.e)
ROPE_THETA = 10000.0
D_FF = 2816
CONV_W = 3
PLE_DIM = 256
EPS = 1e-6
NEG = -1e30

LANES = 128
SUBLANES = 8
HEAD_SLAB = LANES
ROPE_LO = MLA_NOPE
ROPE_HALF = MLA_ROPE // 2
GLA_QK = GLA_HEADS * GLA_DK
GLA_V = GLA_HEADS * GLA_DV
MLA_SLABS = MLA_HEADS * HEAD_SLAB
MLA_VALL = MLA_HEADS * MLA_V
C_GQ, C_GK, C_GV, C_GG = 0, 256, 512, 1024
C_MQ, C_MKV, C_MISC = 1536, 1792, 1920
IN_COLS_P = 2048
FFN_CHUNK = 256
SAMPLE_PAGES_PER_STEP = 32
ATTN_HEADS_PER_STEP = 8
GLA_PROMPT_SEQS = 4
GLA_SAMPLE_SEQS = 8
PAGE_RING = 3
VMEM_LIMIT = 56 * 1024 * 1024

BF16 = jnp.bfloat16
F32 = jnp.float32


def _dot(a, b):
    return jnp.dot(a, b, preferred_element_type=F32)


def _dot_nt(a, b):
    return lax.dot_general(a, b, (((1,), (1,)), ((), ())), preferred_element_type=F32)


def _dot_tn(a, b):
    return lax.dot_general(a, b, (((0,), (0,)), ((), ())), preferred_element_type=F32)


def _rms(x, g):
    return x * lax.rsqrt(jnp.mean(x * x, axis=-1, keepdims=True) + EPS) * g


def _const_spec(shape):
    nd = len(shape)
    return pl.BlockSpec(shape, lambda *_: (0,) * nd, pipeline_mode=pl.Buffered(1))


def _rope_slab(y, cos_t, sin_t):
    lane = lax.broadcasted_iota(jnp.int32, y.shape, 1)
    swapped = jnp.where(lane < ROPE_LO + ROPE_HALF,
                        pltpu.roll(y, LANES - ROPE_HALF, axis=1),
                        pltpu.roll(y, ROPE_HALF, axis=1))
    return y * cos_t + swapped * sin_t


def _in_proj_kernel(x_ref, cos_ref, sin_ref, gmix_ref, win_ref, wa2_ref, ba_ref, gqa_ref, wqup_ref,
                    segm_ref, gq_ref, gkva_ref, gkr_ref, *rest, sample):
    if sample:
        (wkabs_ref, gq_o, gk_o, gv_o, gg_o, la_o, qa_o, ckv_o, kr_o, qabs_o) = rest
    else:
        (wk_ref, gk_ref, wvT_ref, gq_o, gk_o, gv_o, gg_o, la_o, qa_o, ckv_o, kr_o, ka_o, vaT_o) = rest
    x = x_ref[...]
    n = _rms(x, gmix_ref[...]).astype(BF16)
    u = _dot(n, win_ref[...])
    gq_o[...] = u[:, C_GQ:C_GQ + GLA_QK] * (GLA_DK ** -0.5)
    gk_o[...] = u[:, C_GK:C_GK + GLA_QK]
    gv_o[...] = u[:, C_GV:C_GV + GLA_V].astype(BF16)
    gg_o[...] = u[:, C_GG:C_GG + GLA_V]
    misc = u[:, C_MISC:C_MISC + LANES]
    z = _dot(misc.astype(BF16), wa2_ref[...]) + ba_ref[...]
    la_o[...] = (jnp.minimum(z, 0.0) - jnp.log1p(jnp.exp(-jnp.abs(z)))) * (1.0 / GLA_GATE_NORM)

    cos_t = cos_ref[...]
    sin_t = sin_ref[...]
    segm = segm_ref[...]

    lane = lax.broadcasted_iota(jnp.int32, misc.shape, 1)
    is_rope = (lane >= ROPE_LO) & (lane < ROPE_LO + MLA_ROPE)
    ms_r = jnp.sum(jnp.where(is_rope, misc * misc, 0.0), axis=-1, keepdims=True) * (1.0 / MLA_ROPE)
    kr_slab = _rope_slab(misc * lax.rsqrt(ms_r + EPS) * gkr_ref[...], cos_t, sin_t)
    kr_o[...] = kr_slab[:, ROPE_LO:ROPE_LO + MLA_ROPE]

    mkv = u[:, C_MKV:C_MKV + MLA_KV_LORA]
    ckv = _rms(mkv, gkva_ref[...])
    ckv_o[...] = ckv
    ckv_b = ckv.astype(BF16)
    if not sample:
        kn = _dot(ckv_b, wk_ref[...])
        vaT_o[...] = _dot_nt(wvT_ref[...], ckv_b).astype(BF16)

    cq = _rms(u[:, C_MQ:C_MQ + MLA_Q_LORA], gqa_ref[...]).astype(BF16)
    qf = _dot(cq, wqup_ref[...])
    q_gain = gq_ref[...]
    q_cos = cos_t * (q_gain[0:1] * (MLA_SCALE * LOG2E))
    q_sin = sin_t * (q_gain[1:2] * (MLA_SCALE * LOG2E))

    for pair in range(MLA_HEADS // 2):
        lo = pair * 2 * HEAD_SLAB
        q2 = qf[:, lo:lo + 2 * HEAD_SLAB]
        q_inv = lax.rsqrt(_dot((q2 * q2).astype(BF16), segm) + EPS)
        if not sample:
            k2 = kn[:, lo:lo + 2 * HEAD_SLAB]
            k2 = k2 * lax.rsqrt(_dot((k2 * k2).astype(BF16), segm) + EPS) * gk_ref[:, lo:lo + 2 * HEAD_SLAB]
        for j in range(2):
            sl = slice(lo + j * HEAD_SLAB, lo + (j + 1) * HEAD_SLAB)
            swapped = qf[:, MLA_SLABS + lo + j * HEAD_SLAB:MLA_SLABS + lo + (j + 1) * HEAD_SLAB]
            qh = q_inv[:, j * HEAD_SLAB:(j + 1) * HEAD_SLAB] * (qf[:, sl] * q_cos + swapped * q_sin)
            qa_o[:, sl] = qh.astype(BF16)
            if not sample:
                ka_o[:, sl] = (k2[:, j * HEAD_SLAB:(j + 1) * HEAD_SLAB] + kr_slab).astype(BF16)
    if sample:
        qabs_o[...] = _dot(qa_o[...], wkabs_ref[...]).astype(BF16)


def _in_proj(x2d, cos_t, sin_t, w, *, tm, tiles_per_seq, sample):
    T = x2d.shape[0]
    nt = T // tm
    row = lambda i: (i, 0)
    pos = lambda i: (i % tiles_per_seq, 0)
    consts = [w['g_mix'], w['w_in'], w['w_a2'], w['b_a'], w['g_qa'], w['w_qup'], w['segm'], w['gq_slab'],
              w['g_kva'], w['gkr_slab']]
    consts += [w['w_kabs']] if sample else [w['w_kslab'], w['gk_slab'], w['w_vT']]
    in_specs = [pl.BlockSpec((tm, D_MODEL), row), pl.BlockSpec((tm, LANES), pos), pl.BlockSpec((tm, LANES), pos)]
    in_specs += [_const_spec(c.shape) for c in consts]
    outs = [(GLA_QK, F32), (GLA_QK, F32), (GLA_V, BF16), (GLA_V, F32), (GLA_QK, F32),
            (MLA_SLABS, BF16), (MLA_KV_LORA, F32), (MLA_ROPE, F32), (MLA_SLABS, BF16)]
    out_specs = [pl.BlockSpec((tm, c), row) for c, _ in outs]
    out_shape = [jax.ShapeDtypeStruct((T, c), d) for c, d in outs]
    if not sample:
        out_specs.append(pl.BlockSpec((MLA_VALL, tm), lambda i: (0, i)))
        out_shape.append(jax.ShapeDtypeStruct((MLA_VALL, T), BF16))
    return pl.pallas_call(
        functools.partial(_in_proj_kernel, sample=sample),
        grid=(nt,),
        in_specs=in_specs,
        out_specs=out_specs,
        out_shape=out_shape,
        compiler_params=pltpu.CompilerParams(dimension_semantics=("parallel",), vmem_limit_bytes=VMEM_LIMIT),
        name="in_proj_sample" if sample else "in_proj_prompt",
    )(x2d, cos_t, sin_t, *consts)


def _gla_kernel(q_ref, k_ref, v_ref, gg_ref, la_ref, s0_ref, cum_ref, sel_ref, gout_ref,
                o_ref, s_out_ref, st_scr, *, chunk, n_chunks):
    step = pl.program_id(1)
    n_seq = q_ref.shape[0]
    tc = chunk * n_chunks
    heads = range(GLA_HEADS)
    dk = [slice(h * GLA_DK, (h + 1) * GLA_DK) for h in heads]
    dv = [slice(h * GLA_DV, (h + 1) * GLA_DV) for h in heads]
    rows = [slice(ci * chunk, (ci + 1) * chunk) for ci in range(n_chunks)]

    @pl.when(step == 0)
    def _():
        st_scr[...] = s0_ref[...]

    cum = cum_ref[...]
    sel = sel_ref[...]
    r = lax.broadcasted_iota(jnp.int32, (tc, tc), 0)
    c = lax.broadcasted_iota(jnp.int32, (tc, tc), 1)
    causal = (r >= c) & ((r // chunk) == (c // chunk))
    gout = gout_ref[...]

    def decays(i, t):
        la = la_ref[i]
        la_hi = la.astype(BF16)
        la_lo = (la - la_hi.astype(F32)).astype(BF16)
        t['bb'] = _dot(cum, la_hi) + _dot(cum, la_lo)
        t['dec'] = jnp.exp(_dot_tn(la_hi, sel) + _dot_tn(la_lo, sel))

    def scale(i, t):
        b, bl = t['bb'][:tc], t['bb'][tc:]
        q, k = q_ref[i], k_ref[i]
        t['q_dec'] = (q * jnp.exp(b)).astype(BF16)
        t['k_inv'] = (k * jnp.exp(-b)).astype(BF16)
        t['k_end'] = (k * jnp.exp(bl - b)).astype(BF16)

    def intra_scores(i, t):
        t['a'] = [_dot_nt(t['q_dec'][:, dk[h]], t['k_inv'][:, dk[h]]) for h in heads]

    def chunk_outer(i, t):
        v = v_ref[i]
        t['x'] = [[_dot_tn(t['k_end'][rw, dk[h]], v[rw, dv[h]]) for rw in rows] for h in heads]

    def intra_out(i, t):
        v = v_ref[i]
        t['o'] = [_dot(jnp.where(causal, t['a'][h], 0.0).astype(BF16), v[:, dv[h]]) for h in heads]

    def scan(i, t):
        t['s'] = []
        for h in heads:
            s_h = st_scr[i, h]
            starts = []
            for ci in range(n_chunks):
                starts.append(s_h.astype(BF16))
                s_h = s_h * t['dec'][dk[h], ci:ci + 1] + t['x'][h][ci]
            st_scr[i, h] = s_h
            t['s'].append(starts)

    def inter_out(i, t):
        for h in heads:
            parts = [_dot(t['q_dec'][rw, dk[h]], t['s'][h][ci]) for ci, rw in enumerate(rows)]
            t['o'][h] = t['o'][h] + (jnp.concatenate(parts, axis=0) if n_chunks > 1 else parts[0])

    def finish(i, t):
        gg = gg_ref[i]
        gate = gg / (1.0 + jnp.exp(-gg))
        for h in heads:
            o_ref[i, :, dv[h]] = (_rms(t['o'][h], gout) * gate[:, dv[h]]).astype(o_ref.dtype)

    temps = [{} for _ in range(n_seq)]
    for stage in (decays, scale, intra_scores, chunk_outer, intra_out, scan, inter_out, finish):
        for i in range(n_seq):
            stage(i, temps[i])

    @pl.when(step == pl.num_programs(1) - 1)
    def _():
        s_out_ref[...] = st_scr[...]


def _gla(q, k, v, gg, la, s0, g_out, *, n_seq, seq_len, chunk, n_chunks, seqs_per_step):
    tc = chunk * n_chunks
    steps = seq_len // tc
    nb = seqs_per_step
    idx = np.arange(tc)
    same = (idx[:, None] // chunk) == (idx[None, :] // chunk)
    cum = jnp.asarray(np.concatenate([same & (idx[:, None] >= idx[None, :]), same], axis=0), BF16)
    sel = jnp.asarray((idx[:, None] // chunk) == np.arange(LANES)[None, :], BF16)
    tok = lambda width: pl.BlockSpec((nb, tc, width), lambda b, s: (b, s, 0))
    per_seq = lambda a: a.reshape(n_seq, seq_len, a.shape[-1])
    state_spec = pl.BlockSpec((nb, GLA_HEADS, GLA_DK, GLA_DV), lambda b, s: (b, 0, 0, 0))
    o, state = pl.pallas_call(
        functools.partial(_gla_kernel, chunk=chunk, n_chunks=n_chunks),
        grid=(n_seq // nb, steps),
        in_specs=[tok(GLA_QK), tok(GLA_QK), tok(GLA_V), tok(GLA_V), tok(GLA_QK), state_spec,
                  _const_spec((2 * tc, tc)), _const_spec((tc, LANES)), _const_spec((1, GLA_DV))],
        out_specs=[tok(GLA_V), state_spec],
        out_shape=[jax.ShapeDtypeStruct((n_seq, seq_len, GLA_V), BF16),
                   jax.ShapeDtypeStruct((n_seq, GLA_HEADS, GLA_DK, GLA_DV), F32)],
        scratch_shapes=[pltpu.VMEM((nb, GLA_HEADS, GLA_DK, GLA_DV), F32)],
        compiler_params=pltpu.CompilerParams(dimension_semantics=("parallel", "arbitrary"),
                                             vmem_limit_bytes=VMEM_LIMIT),
        name=f"gla_c{chunk}",
    )(per_seq(q), per_seq(k), per_seq(v), per_seq(gg), per_seq(la), s0, cum, sel, g_out)
    return o.reshape(n_seq * seq_len, GLA_V), state


def _prompt_attn_kernel(q_ref, k_ref, vT_ref, o_ref, s_scr, *, tq, tk):
    assert tq == tk
    qi = pl.program_id(2)
    key = lax.broadcasted_iota(jnp.int32, (tk, tq), 0)
    qry = lax.broadcasted_iota(jnp.int32, (tk, tq), 1)

    def scores(j, kb, masked):
        ks = pl.multiple_of(kb * tk, tk)
        s = _dot_nt(k_ref[pl.ds(ks, tk), j * HEAD_SLAB:(j + 1) * HEAD_SLAB],
                    q_ref[:, j * HEAD_SLAB:(j + 1) * HEAD_SLAB])
        if masked:
            s = jnp.where(key <= qry, s, NEG)
        s_scr[j] = s
        return jnp.max(s, axis=0, keepdims=True)

    def update(j, kb, m_blk, state):
        m, l, acc = state
        ks = pl.multiple_of(kb * tk, tk)
        m_new = jnp.maximum(m, m_blk)
        alpha = jnp.exp2(m - m_new)
        p = jnp.exp2(s_scr[j] - m_new)
        l = alpha * l + jnp.sum(p, axis=0, keepdims=True)
        acc = alpha * acc + _dot(vT_ref[j * MLA_V:(j + 1) * MLA_V, pl.ds(ks, tk)], p.astype(BF16))
        return m_new, l, acc

    n_heads = q_ref.shape[1] // HEAD_SLAB

    def body(t, carry):
        m_blk, state = list(carry[0]), list(carry[1])
        kb = qi - 1 - t
        for j in range(n_heads):
            new_blk = scores(j, kb, False)
            nxt = (j + 1) % n_heads
            state[nxt] = update(nxt, kb if nxt == 0 else kb + 1, new_blk if n_heads == 1 else m_blk[nxt],
                                state[nxt])
            m_blk[j] = new_blk
        return tuple(m_blk), tuple(state)

    init = (jnp.full((1, tq), NEG, F32), jnp.zeros((1, tq), F32), jnp.zeros((MLA_V, tq), F32))
    m_blk = [scores(j, qi, True) for j in range(n_heads)]
    state = [update(0, qi, m_blk[0], init)] + [init] * (n_heads - 1)
    m_blk, state = lax.fori_loop(0, qi, body, (tuple(m_blk), tuple(state)))
    state = [state[0]] + [update(j, 0, m_blk[j], state[j]) for j in range(1, n_heads)]
    o_ref[...] = jnp.concatenate([acc / l for _, l, acc in state], axis=0).T.astype(o_ref.dtype)


def _prompt_attn(qa, ka, va, *, n_seq, seq_len, tq, tk):
    nq = seq_len // tq
    hs = ATTN_HEADS_PER_STEP
    return pl.pallas_call(
        functools.partial(_prompt_attn_kernel, tq=tq, tk=tk),
        grid=(n_seq, MLA_HEADS // hs, nq),
        in_specs=[pl.BlockSpec((tq, hs * HEAD_SLAB), lambda b, hp, i: (b * nq + i, hp)),
                  pl.BlockSpec((seq_len, hs * HEAD_SLAB), lambda b, hp, i: (b, hp)),
                  pl.BlockSpec((hs * MLA_V, seq_len), lambda b, hp, i: (hp, b))],
        out_specs=pl.BlockSpec((tq, hs * MLA_V), lambda b, hp, i: (b * nq + i, hp)),
        out_shape=jax.ShapeDtypeStruct((n_seq * seq_len, MLA_VALL), BF16),
        scratch_shapes=[pltpu.VMEM((hs, tk, tq), F32)],
        compiler_params=pltpu.CompilerParams(dimension_semantics=("parallel", "parallel", "arbitrary"),
                                             vmem_limit_bytes=VMEM_LIMIT),
        name="prompt_attn",
    )(qa, ka, va)


def _sample_attn_kernel(pt_ref, qabs_ref, qrope_ref, cnew_ref, krnew_ref, wkT_ref, wv_ref, poolc_hbm, poolr_hbm,
                        o_ref, lhs_scr, m_scr, l_scr, acc_scr, cb_scr, krT_scr, s_scr, cbuf, rbuf, sem,
                        *, n_tok, pages_per_step, layer):
    G = pages_per_step
    b = pl.program_id(0)
    g = pl.program_id(1)
    n_g = pl.num_programs(1)
    step = b * n_g + g
    n_steps = pl.num_programs(0) * n_g
    last = step == n_steps - 1
    slot = step % PAGE_RING

    def group_of(s):
        s = jnp.minimum(s, n_steps - 1)
        return s // n_g, s % n_g
    rows = n_tok * MLA_HEADS
    n_k = MLA_HEADS * MLA_NOPE

    def page_copies(sl, group=None):
        copies = []
        for j in range(G):
            page = 0 if group is None else pt_ref[group[0], group[1] * G + j]
            copies.append(pltpu.make_async_copy(poolc_hbm.at[layer, page], cbuf.at[sl, j], sem.at[0, sl]))
            copies.append(pltpu.make_async_copy(poolr_hbm.at[layer, page], rbuf.at[sl, j], sem.at[1, sl]))
        return copies

    @pl.when(step == 0)
    def _():
        for d in range(PAGE_RING - 1):
            for cp in page_copies(d, group_of(d)):
                cp.start()

    for cp in page_copies(slot):
        cp.wait()

    def scores(cb, krT):
        nk = cb.shape[0]
        big = _dot_nt(lhs_scr[...], cb)
        kvn = big[:n_k]
        s_nope = big[n_k:n_k + rows]
        ss = jnp.sum((kvn * kvn).reshape(MLA_HEADS, MLA_NOPE, nk), axis=1)
        inv = lax.rsqrt(ss * (1.0 / MLA_NOPE) + EPS)
        s_rope = _dot(qrope_ref[0], krT)
        return (s_nope.reshape(n_tok, MLA_HEADS, nk) * inv[None]).reshape(rows, nk) + s_rope

    def weights(s):
        m = m_scr[...]
        m_new = jnp.maximum(m, jnp.max(s, axis=-1, keepdims=True))
        alpha = jnp.exp2(m - m_new)
        p = jnp.exp2(s - m_new)
        l_scr[...] = alpha * l_scr[...] + jnp.sum(p, axis=-1, keepdims=True)
        m_scr[...] = m_new
        return alpha, p.astype(BF16)

    def accumulate(alpha, p, cb):
        acc_scr[...] = alpha * acc_scr[...] + _dot(p, cb)

    def absorb(s, cb):
        accumulate(*weights(s), cb)

    cur = g % 2
    prv = 1 - cur

    @pl.when(g == 0)
    def _():
        lhs_scr[:n_k, :] = wkT_ref[...]
        lhs_scr[n_k:n_k + rows, :] = qabs_ref[0]
        m_scr[...] = jnp.full(m_scr.shape, NEG, F32)
        l_scr[...] = jnp.zeros(l_scr.shape, F32)
        acc_scr[...] = jnp.zeros(acc_scr.shape, F32)
        s_scr[prv] = jnp.full(s_scr.shape[1:], NEG, F32)
        cb_scr[prv] = jnp.zeros(cb_scr.shape[1:], BF16)
        c_new = jnp.concatenate([cnew_ref[0], jnp.zeros((PAGE_SIZE - n_tok, MLA_KV_LORA), F32)],
                                axis=0).astype(BF16)
        key = lax.broadcasted_iota(jnp.int32, (rows, PAGE_SIZE), 1)
        tok = lax.broadcasted_iota(jnp.int32, (rows, PAGE_SIZE), 0) // MLA_HEADS
        absorb(jnp.where(key <= tok, scores(c_new, krnew_ref[0].astype(BF16)), NEG), c_new)

    for j in range(G):
        cb_scr[cur, j * PAGE_SIZE:(j + 1) * PAGE_SIZE, :] = cbuf[slot, j].astype(BF16)
        krT_scr[:, j * PAGE_SIZE:(j + 1) * PAGE_SIZE] = rbuf[slot, j].astype(BF16)

    ahead = step + (PAGE_RING - 1)
    for cp in page_copies(ahead % PAGE_RING, group_of(ahead)):
        cp.start()

    alpha, p = weights(s_scr[prv])
    s_scr[cur] = scores(cb_scr[cur], krT_scr[...])
    accumulate(alpha, p, cb_scr[prv])

    @pl.when(last)
    def _():
        for d in range(1, PAGE_RING):
            for cp in page_copies((step + d) % PAGE_RING):
                cp.wait()

    @pl.when(g == n_g - 1)
    def _():
        absorb(s_scr[cur], cb_scr[cur])
        o_lat =(acc_scr[...] / l_scr[...]).astype(BF16)
        full = _dot(o_lat, wv_ref[...])
        full = full.reshape(n_tok, MLA_HEADS, MLA_VALL)
        hr = lax.broadcasted_iota(jnp.int32, (MLA_HEADS, MLA_VALL), 0)
        hc = lax.broadcasted_iota(jnp.int32, (MLA_HEADS, MLA_VALL), 1) // MLA_V
        o_ref[...] = jnp.sum(jnp.where((hr == hc)[None], full, 0.0), axis=1).astype(o_ref.dtype)


def _sample_attn(page_table, qabs, qrope, c_new, kr_new, w_kT, w_v, pool_c, pool_r, layer, *, pages_per_step):
    n_b, n_pages = page_table.shape
    n_tok = c_new.shape[1]
    rows = n_tok * MLA_HEADS
    G = pages_per_step
    n_k = MLA_HEADS * MLA_NOPE

    per_b = lambda b, g, pt: (b, 0, 0)
    grid_spec = pltpu.PrefetchScalarGridSpec(
        num_scalar_prefetch=1,
        grid=(n_b, n_pages // G),
        in_specs=[pl.BlockSpec((1, rows, MLA_KV_LORA), per_b), pl.BlockSpec((1, rows, MLA_ROPE), per_b),
                  pl.BlockSpec((1, n_tok, MLA_KV_LORA), per_b), pl.BlockSpec((1, MLA_ROPE, PAGE_SIZE), per_b),
                  pl.BlockSpec((n_k, MLA_KV_LORA), lambda b, g, pt: (0, 0)),
                  pl.BlockSpec((MLA_KV_LORA, MLA_VALL), lambda b, g, pt: (0, 0)),
                  pl.BlockSpec(memory_space=pl.ANY), pl.BlockSpec(memory_space=pl.ANY)],
        out_specs=pl.BlockSpec((n_tok, MLA_VALL), lambda b, g, pt: (b, 0)),
        scratch_shapes=[pltpu.VMEM((n_k + rows, MLA_KV_LORA), BF16), pltpu.VMEM((rows, 1), F32),
                        pltpu.VMEM((rows, 1), F32), pltpu.VMEM((rows, MLA_KV_LORA), F32),
                        pltpu.VMEM((2, G * PAGE_SIZE, MLA_KV_LORA), BF16), pltpu.VMEM((MLA_ROPE, G * PAGE_SIZE), BF16),
                        pltpu.VMEM((2, rows, G * PAGE_SIZE), F32),
                        pltpu.VMEM((PAGE_RING, G, PAGE_SIZE, MLA_KV_LORA), F32),
                        pltpu.VMEM((PAGE_RING, G, MLA_ROPE, PAGE_SIZE), F32),
                        pltpu.SemaphoreType.DMA((2, PAGE_RING))])
    return pl.pallas_call(
        functools.partial(_sample_attn_kernel, n_tok=n_tok, pages_per_step=G, layer=layer),
        grid_spec=grid_spec,
        out_shape=jax.ShapeDtypeStruct((n_b * n_tok, MLA_VALL), BF16),
        compiler_params=pltpu.CompilerParams(dimension_semantics=("arbitrary", "arbitrary"),
                                             vmem_limit_bytes=VMEM_LIMIT),
        name="sample_attn",
    )(page_table, qabs, qrope, c_new, kr_new, w_kT, w_v, pool_c, pool_r)


def _post_kernel(x_ref, og_ref, om_ref, p_ref, wo_ref, gffn_ref, wg_ref, wu_ref, cw_ref, cb_ref, wd_ref,
                 gple_ref, wpg_ref, wpp_ref, *rest, tiles_per_seq, seq_rows):
    paged_prev = seq_rows is not None
    if paged_prev:
        p1_ref, p2_ref, y_ref, tail_ref, acc_scr = rest
    else:
        y_ref, tail_ref, acc_scr, carry_scr = rest
    tm = x_ref.shape[0]
    i = pl.program_id(0)
    h1 = x_ref[...] + _dot(og_ref[...], wo_ref[:GLA_V, :]) + _dot(om_ref[...], wo_ref[GLA_V:, :])
    n2 = _rms(h1, gffn_ref[...]).astype(BF16)
    row = lax.broadcasted_iota(jnp.int32, (tm, FFN_CHUNK), 0)
    if not paged_prev:
        @pl.when(i % tiles_per_seq == 0)
        def _():
            carry_scr[...] = jnp.zeros(carry_scr.shape, F32)

    def gate_up(f):
        return _dot(n2, wg_ref[:, f:f + FFN_CHUNK]), _dot(n2, wu_ref[:, f:f + FFN_CHUNK])

    acc_scr[...] = jnp.zeros(acc_scr.shape, F32)
    nxt = gate_up(0)
    for f in range(0, D_FF, FFN_CHUNK):
        cols = slice(f, f + FFN_CHUNK)
        a, up = nxt
        if f + FFN_CHUNK < D_FF:
            nxt = gate_up(f + FFN_CHUNK)
        r1 = pltpu.roll(a, 1, axis=0)
        r2 = pltpu.roll(a, 2, axis=0)
        if paged_prev:
            t = row % seq_rows
            a1 = jnp.where(t >= 1, r1, p1_ref[:, cols])
            a2 = jnp.where(t >= 2, r2, p2_ref[:, cols])
        else:
            prev = carry_scr[:, cols]
            pm1 = prev[SUBLANES - 1:SUBLANES, :]
            pm2 = prev[SUBLANES - 2:SUBLANES - 1, :]
            a1 = jnp.where(row >= 1, r1, pm1)
            a2 = jnp.where(row >= 2, r2, jnp.where(row == 0, pm2, pm1))
            carry_scr[:, cols] = a[tm - SUBLANES:, :]
        tail_ref[:, cols] = a[tm - SUBLANES:, :] if not paged_prev else a
        conv = cb_ref[:, cols] + cw_ref[0:1, cols] * a2 + cw_ref[1:2, cols] * a1 + cw_ref[2:3, cols] * a
        gact = (conv / (1.0 + jnp.exp(-conv)) * up).astype(BF16)
        acc_scr[...] += _dot(gact, wd_ref[cols, :])
    h2 = h1 + acc_scr[...]
    n3 = _rms(h2, gple_ref[...]).astype(BF16)
    gate = 1.0 / (1.0 + jnp.exp(-_dot(n3, wpg_ref[...])))
    y_ref[...] = h2 + _dot(p_ref[...].astype(BF16), wpp_ref[...]) * gate


def _post(x2d, og, om, p2d, w, prev=None, *, tm, tiles_per_seq, seq_rows):
    T = x2d.shape[0]
    nt = T // tm
    row = lambda i: (i, 0)
    consts = [w['w_o'], w['g_ffn'], w['w_gate'], w['w_up'], w['conv_w'], w['conv_b'], w['w_down'],
              w['g_ple'], w['w_pgate'], w['w_pproj']]
    in_specs = [pl.BlockSpec((tm, D_MODEL), row), pl.BlockSpec((tm, GLA_V), row),
                pl.BlockSpec((tm, MLA_VALL), row), pl.BlockSpec((tm, PLE_DIM), row)]
    in_specs += [_const_spec(c.shape) for c in consts]
    args = [x2d, og, om, p2d, *consts]
    scratch = [pltpu.VMEM((tm, D_MODEL), F32)]
    if prev is not None:
        in_specs += [pl.BlockSpec((tm, D_FF), row)] * 2
        args += list(prev)
        tail_rows = tm
    else:
        scratch.append(pltpu.VMEM((SUBLANES, D_FF), F32))
        tail_rows = SUBLANES
    return pl.pallas_call(
        functools.partial(_post_kernel, tiles_per_seq=tiles_per_seq, seq_rows=seq_rows),
        grid=(nt,),
        in_specs=in_specs,
        out_specs=[pl.BlockSpec((tm, D_MODEL), row), pl.BlockSpec((tail_rows, D_FF), row)],
        out_shape=[jax.ShapeDtypeStruct((T, D_MODEL), F32), jax.ShapeDtypeStruct((nt * tail_rows, D_FF), F32)],
        scratch_shapes=scratch,
        compiler_params=pltpu.CompilerParams(dimension_semantics=("arbitrary",), vmem_limit_bytes=VMEM_LIMIT),
        name="post_sample" if prev is not None else "post_prompt",
    )(*args)


def _prep_weights(g_mix, w_in, gla_w_a2, gla_b_a, gla_g_out, mla_g_qa, mla_w_qup, mla_g_qn, mla_g_qr,
                  mla_g_kva, mla_g_kr, mla_w_kvup, mla_g_kn, w_o, g_ffn, ffn_w_gate, ffn_w_up, ffn_conv_w,
                  ffn_conv_b, ffn_w_down, g_ple, ple_w_gate, ple_w_proj):
    sizes = (GLA_QK, GLA_QK, GLA_V, GLA_V, GLA_GATE_RANK, MLA_Q_LORA, MLA_KV_LORA, MLA_ROPE)
    offs = np.concatenate([[0], np.cumsum(sizes)])
    piece = lambda i: w_in[:, offs[i]:offs[i + 1]]
    zeros = lambda n: jnp.zeros((D_MODEL, n), w_in.dtype)
    misc = jnp.concatenate([piece(4), zeros(ROPE_LO - GLA_GATE_RANK), piece(7),
                            zeros(LANES - ROPE_LO - MLA_ROPE)], axis=1)
    w_in_p = jnp.concatenate([piece(0), piece(1), piece(2), piece(3), piece(5), piece(6), misc], axis=1)
    w_a2 = jnp.concatenate([gla_w_a2, jnp.zeros((LANES - GLA_GATE_RANK, GLA_QK), gla_w_a2.dtype)], axis=0)

    def slab_vec(nope, rope_):
        one = jnp.concatenate([nope, rope_, jnp.zeros((HEAD_SLAB - MLA_NOPE - MLA_ROPE,), F32)])
        return jnp.tile(one, MLA_HEADS)[None, :]

    wq = mla_w_qup.reshape(MLA_Q_LORA, MLA_HEADS, MLA_NOPE + MLA_ROPE)
    swap = np.arange(MLA_NOPE + MLA_ROPE)
    swap[MLA_NOPE:] = np.concatenate([swap[MLA_NOPE + ROPE_HALF:], swap[MLA_NOPE:MLA_NOPE + ROPE_HALF]])
    slab_pad = ((0, 0), (0, 0), (0, HEAD_SLAB - MLA_NOPE - MLA_ROPE))
    wq = jnp.concatenate([jnp.pad(wq, slab_pad).reshape(MLA_Q_LORA, MLA_SLABS),
                          jnp.pad(wq[:, :, swap], slab_pad).reshape(MLA_Q_LORA, MLA_SLABS)], axis=1)
    g_q = jnp.concatenate([mla_g_qn, mla_g_qr])
    gq_slab = jnp.pad(jnp.stack([g_q, g_q[swap]]), ((0, 0), (0, HEAD_SLAB - MLA_NOPE - MLA_ROPE)))
    wkv = mla_w_kvup.reshape(MLA_KV_LORA, MLA_HEADS, MLA_NOPE + MLA_V)
    wk = wkv[:, :, :MLA_NOPE]
    wv = wkv[:, :, MLA_NOPE:].reshape(MLA_KV_LORA, MLA_VALL)
    wk_slab = jnp.pad(wk, ((0, 0), (0, 0), (0, HEAD_SLAB - MLA_NOPE))).reshape(MLA_KV_LORA, MLA_SLABS)
    wk_g = jnp.pad(wk * mla_g_kn[None, None, :], ((0, 0), (0, 0), (0, HEAD_SLAB - MLA_NOPE)))
    eye = jnp.eye(MLA_HEADS, dtype=F32)
    w_kabs = jnp.einsum('nhd,hg->hdgn', wk_g, eye).reshape(MLA_SLABS, MLA_HEADS * MLA_KV_LORA)
    w_kT = wk.transpose(1, 2, 0).reshape(MLA_HEADS * MLA_NOPE, MLA_KV_LORA)

    lane = np.arange(2 * HEAD_SLAB)
    seg = np.where(lane % HEAD_SLAB < MLA_NOPE, 0, np.where(lane % HEAD_SLAB < MLA_NOPE + MLA_ROPE, 1, 2))
    same = (lane[:, None] // HEAD_SLAB == lane[None, :] // HEAD_SLAB) & (seg[:, None] == seg[None, :])
    segm = np.where(same & (seg[:, None] == 0), 1.0 / MLA_NOPE, np.where(same & (seg[:, None] == 1), 1.0 / MLA_ROPE, 0.0))

    gkr = jnp.concatenate([jnp.zeros((ROPE_LO,), F32), mla_g_kr, jnp.zeros((LANES - ROPE_LO - MLA_ROPE,), F32)])
    return dict(
        g_mix=g_mix[None, :], w_in=w_in_p.astype(BF16), w_a2=w_a2.astype(BF16), b_a=gla_b_a[None, :],
        g_out=gla_g_out[None, :], g_qa=mla_g_qa[None, :], w_qup=wq.astype(BF16), segm=jnp.asarray(segm, BF16),
        gq_slab=gq_slab, g_kva=mla_g_kva[None, :], w_kslab=wk_slab.astype(BF16),
        w_vT=wv.T.astype(BF16),
        gk_slab=slab_vec(mla_g_kn, jnp.zeros((MLA_ROPE,), F32)),
        g_kn=mla_g_kn, gkr_slab=gkr[None, :], w_kabs=w_kabs.astype(BF16), w_kT=w_kT.astype(BF16),
        w_v=wv.astype(BF16), w_o=w_o.astype(BF16), g_ffn=g_ffn[None, :], w_gate=ffn_w_gate.astype(BF16),
        w_up=ffn_w_up.astype(BF16), conv_w=ffn_conv_w, conv_b=ffn_conv_b[None, :],
        w_down=ffn_w_down.astype(BF16), g_ple=g_ple[None, :], w_pgate=ple_w_gate.astype(BF16),
        w_pproj=ple_w_proj.astype(BF16))


def _rope_tables(pos):
    inv = ROPE_THETA ** (-jnp.arange(ROPE_HALF, dtype=F32) * 2.0 / MLA_ROPE)
    ang = pos.astype(F32)[:, None] * inv[None, :]
    cos, sin = jnp.cos(ang), jnp.sin(ang)
    T = pos.shape[0]
    pad = jnp.zeros((T, HEAD_SLAB - MLA_NOPE - MLA_ROPE), F32)
    cos_t = jnp.concatenate([jnp.ones((T, MLA_NOPE), F32), cos, cos, pad], axis=1)
    sin_t = jnp.concatenate([jnp.zeros((T, MLA_NOPE), F32), -sin, sin, pad], axis=1)
    return cos_t, sin_t


def _pick_tile(n, cap):
    t = math.gcd(n, cap)
    return t


def _layer(w, x_p, x_s, p_p, p_s, pool_c, pool_r, layer, state_gla, state_conv, page_table):
    B, S, _ = x_p.shape
    Bd, Td, _ = x_s.shape
    tm = _pick_tile(S, 512)
    tps = S // tm
    cos_p, sin_p = _rope_tables(jnp.arange(S))
    xp2 = x_p.reshape(B * S, D_MODEL)
    gq, gk, gv, gg, la, qa, ckv_p, kr_p, ka, vaT = _in_proj(xp2, cos_p, sin_p, w, tm=tm, tiles_per_seq=tps,
                                                            sample=False)
    chunk_p = math.gcd(S, GLA_CHUNK)
    n_chunks = math.gcd(S // chunk_p, 8)
    og_p, gla_p = _gla(gq, gk, gv, gg, la, jnp.zeros((B, GLA_HEADS, GLA_DK, GLA_DV), F32), w['g_out'],
                      n_seq=B, seq_len=S, chunk=chunk_p, n_chunks=n_chunks,
                      seqs_per_step=math.gcd(B, GLA_PROMPT_SEQS))
    tq = _pick_tile(S, 512)
    om_p = _prompt_attn(qa, ka, vaT, n_seq=B, seq_len=S, tq=tq, tk=tq)
    y_p, tail_p = _post(xp2, og_p, om_p, p_p.reshape(B * S, PLE_DIM), w, tm=tm, tiles_per_seq=tps, seq_rows=None)
    conv_p = tail_p.reshape(B, tps, SUBLANES, D_FF)[:, -1, SUBLANES - (CONV_W - 1):, :]

    Ts = Bd * Td
    pos_s = PAST_LEN + jnp.arange(Td)
    cos_s, sin_s = _rope_tables(jnp.tile(pos_s, Bd))
    xs2 = x_s.reshape(Ts, D_MODEL)
    gq, gk, gv, gg, la, qa, ckv_s, kr_s, qabs = _in_proj(xs2, cos_s, sin_s, w, tm=Ts, tiles_per_seq=1, sample=True)
    chunk_s = math.gcd(Td, GLA_CHUNK)
    og_s, gla_s = _gla(gq, gk, gv, gg, la, state_gla, w['g_out'],
                      n_seq=Bd, seq_len=Td, chunk=chunk_s, n_chunks=Td // chunk_s,
                      seqs_per_step=math.gcd(Bd, GLA_SAMPLE_SEQS))
    rows = Td * MLA_HEADS
    qrope = qa.reshape(Ts, MLA_HEADS, HEAD_SLAB)[:, :, ROPE_LO:ROPE_LO + MLA_ROPE].reshape(Bd, rows, MLA_ROPE)
    krT_new = jnp.pad(jnp.swapaxes(kr_s.reshape(Bd, Td, MLA_ROPE), 1, 2), ((0, 0), (0, 0), (0, PAGE_SIZE - Td)))
    om_s = _sample_attn(page_table, qabs.reshape(Bd, rows, MLA_KV_LORA), qrope,
                        ckv_s.reshape(Bd, Td, MLA_KV_LORA), krT_new,
                        w['w_kT'], w['w_v'], pool_c, jnp.swapaxes(pool_r, 2, 3), layer,
                        pages_per_step=math.gcd(page_table.shape[1], SAMPLE_PAGES_PER_STEP))
    zpad = lambda a, lo: jnp.pad(a, ((0, 0), (lo, Td - lo - a.shape[1]), (0, 0))).reshape(Ts, D_FF)
    prev1 = zpad(state_conv[:, 1:2], 0)
    prev2 = zpad(state_conv, 0)
    y_s, a_s = _post(xs2, og_s, om_s, p_s.reshape(Ts, PLE_DIM), w, prev=(prev1, prev2), tm=Ts, tiles_per_seq=1,
                     seq_rows=Td)
    full = jnp.concatenate([state_conv, a_s.reshape(Bd, Td, D_FF)], axis=1)
    conv_s = full[:, full.shape[1] - (CONV_W - 1):]
    return (y_p.reshape(B, S, D_MODEL), y_s.reshape(Bd, Td, D_MODEL),
            ckv_p.reshape(B, S, MLA_KV_LORA), kr_p.reshape(B, S, MLA_ROPE),
            gla_p, conv_p,
            ckv_s.reshape(Bd, Td, MLA_KV_LORA), kr_s.reshape(Bd, Td, MLA_ROPE),
            gla_s, conv_s)


def kernel(x_prompt, x_sample, cache_ckv, cache_krope, state_gla, state_conv, page_table, p_prompt, p_sample, g_mix, w_in, gla_w_a2, gla_b_a, gla_g_out, mla_g_qa, mla_w_qup, mla_g_qn, mla_g_qr, mla_g_kva, mla_g_kr, mla_w_kvup, mla_g_kn, w_o, g_ffn, ffn_w_gate, ffn_w_up, ffn_conv_w, ffn_conv_b, ffn_w_down, g_ple, ple_w_gate, ple_w_proj):
    depth = w_in.shape[0]
    per_layer = (g_mix, w_in, gla_w_a2, gla_b_a, gla_g_out, mla_g_qa, mla_w_qup, mla_g_qn, mla_g_qr, mla_g_kva,
                 mla_g_kr, mla_w_kvup, mla_g_kn, w_o, g_ffn, ffn_w_gate, ffn_w_up, ffn_conv_w, ffn_conv_b,
                 ffn_w_down, g_ple, ple_w_gate, ple_w_proj)
    y_p, y_s = x_prompt, x_sample
    outs = [[] for _ in range(8)]
    for i in range(depth):
        w = _prep_weights(*(a[i] for a in per_layer))
        res = _layer(w, y_p, y_s, p_prompt[i], p_sample[i], cache_ckv, cache_krope, i, state_gla[i],
                     state_conv[i], page_table)
        y_p, y_s = res[0], res[1]
        for lst, r in zip(outs, res[2:]):
            lst.append(r)
    return (y_p, y_s) + tuple(jnp.stack(lst) for lst in outs)
```

```python
import functools
import math

import jax
import jax.numpy as jnp
import numpy as np
from jax import lax
from jax.experimental import pallas as pl
from jax.experimental.pallas import tpu as pltpu

D_MODEL = 1024
PAST_LEN = 16384
PAGE_SIZE = 128
GLA_HEADS = 4
GLA_DK = 64
GLA_DV = 128
GLA_GATE_RANK = 16
GLA_GATE_NORM = 16.0
GLA_CHUNK = 32
MLA_HEADS = 8
MLA_Q_LORA = 256
MLA_KV_LORA = 128
MLA_NOPE = 64
MLA_ROPE = 32
MLA_V = 64
MLA_SCALE = (MLA_NOPE + MLA_ROPE) ** -0.5
LOG2E = math.log2(math.e)
ROPE_THETA = 10000.0
D_FF = 2816
CONV_W = 3
PLE_DIM = 256
EPS = 1e-6
NEG = -1e30

LANES = 128
SUBLANES = 8
HEAD_SLAB = LANES
ROPE_LO = MLA_NOPE
ROPE_HALF = MLA_ROPE // 2
GLA_QK = GLA_HEADS * GLA_DK
GLA_V = GLA_HEADS * GLA_DV
MLA_SLABS = MLA_HEADS * HEAD_SLAB
MLA_VALL = MLA_HEADS * MLA_V
C_GQ, C_GK, C_GV, C_GG = 0, 256, 512, 1024
C_MQ, C_MKV, C_MISC = 1536, 1792, 1920
IN_COLS_P = 2048
FFN_CHUNKS = (1536, 1280)
SAMPLE_PAGES_PER_STEP = 32
ATTN_HEADS_PER_STEP = 8
GLA_PROMPT_SEQS = 4
GLA_SAMPLE_SEQS = 8
PAGE_RING = 3
VMEM_LIMIT = 56 * 1024 * 1024

BF16 = jnp.bfloat16
F32 = jnp.float32


def _dot(a, b):
    return jnp.dot(a, b, preferred_element_type=F32)


def _dot_nt(a, b):
    return lax.dot_general(a, b, (((1,), (1,)), ((), ())), preferred_element_type=F32)


def _dot_tn(a, b):
    return lax.dot_general(a, b, (((0,), (0,)), ((), ())), preferred_element_type=F32)


def _rms(x, g):
    return x * lax.rsqrt(jnp.mean(x * x, axis=-1, keepdims=True) + EPS) * g


def _const_spec(shape):
    nd = len(shape)
    return pl.BlockSpec(shape, lambda *_: (0,) * nd, pipeline_mode=pl.Buffered(1))


def _rope_slab(y, cos_t, sin_t):
    lane = lax.broadcasted_iota(jnp.int32, y.shape, 1)
    swapped = jnp.where(lane < ROPE_LO + ROPE_HALF,
                        pltpu.roll(y, LANES - ROPE_HALF, axis=1),
                        pltpu.roll(y, ROPE_HALF, axis=1))
    return y * cos_t + swapped * sin_t


def _in_proj_kernel(x_ref, cos_ref, sin_ref, gmix_ref, win_ref, wa2_ref, ba_ref, gqa_ref, wqup_ref,
                    segm_ref, gq_ref, gkva_ref, gkr_ref, *rest, sample):
    if sample:
        (wkabs_ref, gq_o, gk_o, gv_o, gg_o, la_o, qa_o, ckv_o, kr_o, qabs_o) = rest
    else:
        (wk_ref, gk_ref, wvT_ref, gq_o, gk_o, gv_o, gg_o, la_o, qa_o, ckv_o, kr_o, ka_o, vaT_o) = rest
    segm = segm_ref[...]
    q_gain = gq_ref[...]

    cos_t = cos_ref[...]
    sin_t = sin_ref[...]
    n = _rms(x_ref[...], gmix_ref[...]).astype(BF16)

    def proj(lo, width):
        return _dot(n, win_ref[:, lo:lo + width])

    um = proj(C_MQ, IN_COLS_P - C_MQ)
    ug = proj(C_GQ, 2 * GLA_QK)
    gq_o[...] = ug[:, :GLA_QK] * (GLA_DK ** -0.5)
    gk_o[...] = ug[:, GLA_QK:]
    misc = um[:, C_MISC - C_MQ:]
    z = _dot(misc.astype(BF16), wa2_ref[...]) + ba_ref[...]
    la_o[...] = (jnp.minimum(z, 0.0) - jnp.log1p(jnp.exp(-jnp.abs(z)))) * (1.0 / GLA_GATE_NORM)
    lane = lax.broadcasted_iota(jnp.int32, misc.shape, 1)
    is_rope = (lane >= ROPE_LO) & (lane < ROPE_LO + MLA_ROPE)
    ms_r = jnp.sum(jnp.where(is_rope, misc * misc, 0.0), axis=-1, keepdims=True) * (1.0 / MLA_ROPE)
    kr_slab = _rope_slab(misc * lax.rsqrt(ms_r + EPS) * gkr_ref[...], cos_t, sin_t)
    kr_o[...] = kr_slab[:, ROPE_LO:ROPE_LO + MLA_ROPE]
    ckv = _rms(um[:, C_MKV - C_MQ:C_MISC - C_MQ], gkva_ref[...])
    ckv_o[...] = ckv
    ckv_b = ckv.astype(BF16)
    cq = _rms(um[:, :MLA_Q_LORA], gqa_ref[...]).astype(BF16)

    if not sample:
        kn = _dot(ckv_b, wk_ref[...])
        vaT_o[...] = _dot_nt(wvT_ref[...], ckv_b).astype(BF16)
    qf = _dot(cq, wqup_ref[...])
    q_cos = cos_t * (q_gain[0:1] * (MLA_SCALE * LOG2E))
    q_sin = sin_t * (q_gain[1:2] * (MLA_SCALE * LOG2E))

    def head_pair(pair):
        lo = pair * 2 * HEAD_SLAB
        q2 = qf[:, lo:lo + 2 * HEAD_SLAB]
        q_inv = lax.rsqrt(_dot((q2 * q2).astype(BF16), segm) + EPS)
        if not sample:
            k2 = kn[:, lo:lo + 2 * HEAD_SLAB]
            k2 = k2 * lax.rsqrt(_dot((k2 * k2).astype(BF16), segm) + EPS) * gk_ref[:, lo:lo + 2 * HEAD_SLAB]
        for j in range(2):
            sl = slice(lo + j * HEAD_SLAB, lo + (j + 1) * HEAD_SLAB)
            swapped = qf[:, MLA_SLABS + lo + j * HEAD_SLAB:MLA_SLABS + lo + (j + 1) * HEAD_SLAB]
            qh = q_inv[:, j * HEAD_SLAB:(j + 1) * HEAD_SLAB] * (qf[:, sl] * q_cos + swapped * q_sin)
            qa_o[:, sl] = qh.astype(BF16)
            if not sample:
                ka_o[:, sl] = (k2[:, j * HEAD_SLAB:(j + 1) * HEAD_SLAB] + kr_slab).astype(BF16)

    half = GLA_V // 2
    gv_o[:, :half] = proj(C_GV, half).astype(BF16)
    head_pair(0)
    gv_o[:, half:] = proj(C_GV + half, half).astype(BF16)
    head_pair(1)
    gg_o[:, :half] = proj(C_GG, half)
    head_pair(2)
    gg_o[:, half:] = proj(C_GG + half, half)
    head_pair(3)
    if sample:
        qabs_o[...] = _dot(qa_o[...], wkabs_ref[...]).astype(BF16)


def _in_proj(x2d, cos_t, sin_t, w, *, tm, tiles_per_seq, sample):
    T = x2d.shape[0]
    nt = T // tm
    row = lambda i: (i, 0)
    pos = lambda i: (i % tiles_per_seq, 0)
    consts = [w['g_mix'], w['w_in'], w['w_a2'], w['b_a'], w['g_qa'], w['w_qup'], w['segm'], w['gq_slab'],
              w['g_kva'], w['gkr_slab']]
    consts += [w['w_kabs']] if sample else [w['w_kslab'], w['gk_slab'], w['w_vT']]
    in_specs = [pl.BlockSpec((tm, D_MODEL), row), pl.BlockSpec((tm, LANES), pos), pl.BlockSpec((tm, LANES), pos)]
    in_specs += [_const_spec(c.shape) for c in consts]
    outs = [(GLA_QK, F32), (GLA_QK, F32), (GLA_V, BF16), (GLA_V, F32), (GLA_QK, F32),
            (MLA_SLABS, BF16), (MLA_KV_LORA, F32), (MLA_ROPE, F32), (MLA_SLABS, BF16)]
    out_specs = [pl.BlockSpec((tm, c), row) for c, _ in outs]
    out_shape = [jax.ShapeDtypeStruct((T, c), d) for c, d in outs]
    if not sample:
        out_specs.append(pl.BlockSpec((MLA_VALL, tm), lambda i: (0, i)))
        out_shape.append(jax.ShapeDtypeStruct((MLA_VALL, T), BF16))
    return pl.pallas_call(
        functools.partial(_in_proj_kernel, sample=sample),
        grid=(nt,),
        in_specs=in_specs,
        out_specs=out_specs,
        out_shape=out_shape,
        compiler_params=pltpu.CompilerParams(dimension_semantics=("parallel",), vmem_limit_bytes=VMEM_LIMIT),
        name="in_proj_sample" if sample else "in_proj_prompt",
    )(x2d, cos_t, sin_t, *consts)


def _gla_kernel(q_ref, k_ref, v_ref, gg_ref, la_ref, s0_ref, cum_ref, sel_ref, gout_ref,
                o_ref, s_out_ref, st_scr, *, chunk, n_chunks):
    step = pl.program_id(1)
    n_seq = q_ref.shape[0]
    tc = chunk * n_chunks
    heads = range(GLA_HEADS)
    dk = [slice(h * GLA_DK, (h + 1) * GLA_DK) for h in heads]
    dv = [slice(h * GLA_DV, (h + 1) * GLA_DV) for h in heads]
    rows = [slice(ci * chunk, (ci + 1) * chunk) for ci in range(n_chunks)]

    @pl.when(step == 0)
    def _():
        st_scr[...] = s0_ref[...]

    cum = cum_ref[...]
    sel = sel_ref[...]
    r = lax.broadcasted_iota(jnp.int32, (tc, tc), 0)
    c = lax.broadcasted_iota(jnp.int32, (tc, tc), 1)
    causal = (r >= c) & ((r // chunk) == (c // chunk))
    gout = gout_ref[...]

    def decays(i, t):
        la = la_ref[i]
        la_hi = la.astype(BF16)
        la_lo = (la - la_hi.astype(F32)).astype(BF16)
        t['bb'] = _dot(cum, la_hi) + _dot(cum, la_lo)
        t['dec'] = jnp.exp(_dot_tn(la_hi, sel) + _dot_tn(la_lo, sel))

    def scale(i, t):
        b, bl = t['bb'][:tc], t['bb'][tc:]
        q, k = q_ref[i], k_ref[i]
        t['q_dec'] = (q * jnp.exp(b)).astype(BF16)
        t['k_inv'] = (k * jnp.exp(-b)).astype(BF16)
        t['k_end'] = (k * jnp.exp(bl - b)).astype(BF16)

    def intra_scores(i, t):
        t['a'] = [_dot_nt(t['q_dec'][:, dk[h]], t['k_inv'][:, dk[h]]) for h in heads]

    def chunk_outer(i, t):
        v = v_ref[i]
        t['x'] = [[_dot_tn(t['k_end'][rw, dk[h]], v[rw, dv[h]]) for rw in rows] for h in heads]

    def intra_out(i, t):
        v = v_ref[i]
        t['o'] = [_dot(jnp.where(causal, t['a'][h], 0.0).astype(BF16), v[:, dv[h]]) for h in heads]

    def scan(i, t):
        t['s'] = []
        for h in heads:
            s_h = st_scr[i, h]
            starts = []
            for ci in range(n_chunks):
                starts.append(s_h.astype(BF16))
                s_h = s_h * t['dec'][dk[h], ci:ci + 1] + t['x'][h][ci]
            st_scr[i, h] = s_h
            t['s'].append(starts)

    def inter_out(i, t):
        for h in heads:
            parts = [_dot(t['q_dec'][rw, dk[h]], t['s'][h][ci]) for ci, rw in enumerate(rows)]
            t['o'][h] = t['o'][h] + (jnp.concatenate(parts, axis=0) if n_chunks > 1 else parts[0])

    def finish(i, t):
        gg = gg_ref[i]
        gate = gg / (1.0 + jnp.exp(-gg))
        for h in heads:
            o_ref[i, :, dv[h]] = (_rms(t['o'][h], gout) * gate[:, dv[h]]).astype(o_ref.dtype)

    temps = [{} for _ in range(n_seq)]
    for stage in (decays, scale, intra_scores, chunk_outer, intra_out, scan, inter_out, finish):
        for i in range(n_seq):
            stage(i, temps[i])

    @pl.when(step == pl.num_programs(1) - 1)
    def _():
        s_out_ref[...] = st_scr[...]


def _gla(q, k, v, gg, la, s0, g_out, *, n_seq, seq_len, chunk, n_chunks, seqs_per_step):
    tc = chunk * n_chunks
    steps = seq_len // tc
    nb = seqs_per_step
    idx = np.arange(tc)
    same = (idx[:, None] // chunk) == (idx[None, :] // chunk)
    cum = jnp.asarray(np.concatenate([same & (idx[:, None] >= idx[None, :]), same], axis=0), BF16)
    sel = jnp.asarray((idx[:, None] // chunk) == np.arange(LANES)[None, :], BF16)
    tok = lambda width: pl.BlockSpec((nb, tc, width), lambda b, s: (b, s, 0))
    per_seq = lambda a: a.reshape(n_seq, seq_len, a.shape[-1])
    state_spec = pl.BlockSpec((nb, GLA_HEADS, GLA_DK, GLA_DV), lambda b, s: (b, 0, 0, 0))
    o, state = pl.pallas_call(
        functools.partial(_gla_kernel, chunk=chunk, n_chunks=n_chunks),
        grid=(n_seq // nb, steps),
        in_specs=[tok(GLA_QK), tok(GLA_QK), tok(GLA_V), tok(GLA_V), tok(GLA_QK), state_spec,
                  _const_spec((2 * tc, tc)), _const_spec((tc, LANES)), _const_spec((1, GLA_DV))],
        out_specs=[tok(GLA_V), state_spec],
        out_shape=[jax.ShapeDtypeStruct((n_seq, seq_len, GLA_V), BF16),
                   jax.ShapeDtypeStruct((n_seq, GLA_HEADS, GLA_DK, GLA_DV), F32)],
        scratch_shapes=[pltpu.VMEM((nb, GLA_HEADS, GLA_DK, GLA_DV), F32)],
        compiler_params=pltpu.CompilerParams(dimension_semantics=("parallel", "arbitrary"),
                                             vmem_limit_bytes=VMEM_LIMIT),
        name=f"gla_c{chunk}",
    )(per_seq(q), per_seq(k), per_seq(v), per_seq(gg), per_seq(la), s0, cum, sel, g_out)
    return o.reshape(n_seq * seq_len, GLA_V), state


def _prompt_attn_kernel(q_ref, k_ref, vT_ref, o_ref, s_scr, *, tq, tk):
    assert tq == tk
    qi = pl.program_id(2)
    key = lax.broadcasted_iota(jnp.int32, (tk, tq), 0)
    qry = lax.broadcasted_iota(jnp.int32, (tk, tq), 1)

    def scores(j, kb, masked):
        ks = pl.multiple_of(kb * tk, tk)
        s = _dot_nt(k_ref[pl.ds(ks, tk), j * HEAD_SLAB:(j + 1) * HEAD_SLAB],
                    q_ref[:, j * HEAD_SLAB:(j + 1) * HEAD_SLAB])
        if masked:
            s = jnp.where(key <= qry, s, NEG)
        s_scr[j] = s
        return jnp.max(s, axis=0, keepdims=True)

    def update(j, kb, m_blk, state):
        m, l, acc = state
        ks = pl.multiple_of(kb * tk, tk)
        m_new = jnp.maximum(m, m_blk)
        alpha = jnp.exp2(m - m_new)
        p = jnp.exp2(s_scr[j] - m_new)
        l = alpha * l + jnp.sum(p, axis=0, keepdims=True)
        acc = alpha * acc + _dot(vT_ref[j * MLA_V:(j + 1) * MLA_V, pl.ds(ks, tk)], p.astype(BF16))
        return m_new, l, acc

    n_heads = q_ref.shape[1] // HEAD_SLAB

    def body(t, carry):
        m_blk, state = list(carry[0]), list(carry[1])
        kb = qi - 1 - t
        for j in range(n_heads):
            new_blk = scores(j, kb, False)
            nxt = (j + 1) % n_heads
            state[nxt] = update(nxt, kb if nxt == 0 else kb + 1, new_blk if n_heads == 1 else m_blk[nxt],
                                state[nxt])
            m_blk[j] = new_blk
        return tuple(m_blk), tuple(state)

    init = (jnp.full((1, tq), NEG, F32), jnp.zeros((1, tq), F32), jnp.zeros((MLA_V, tq), F32))
    m_blk = [scores(j, qi, True) for j in range(n_heads)]
    state = [update(0, qi, m_blk[0], init)] + [init] * (n_heads - 1)
    m_blk, state = lax.fori_loop(0, qi, body, (tuple(m_blk), tuple(state)))
    state = [state[0]] + [update(j, 0, m_blk[j], state[j]) for j in range(1, n_heads)]
    o_ref[...] = jnp.concatenate([acc / l for _, l, acc in state], axis=0).T.astype(o_ref.dtype)


def _prompt_attn(qa, ka, va, *, n_seq, seq_len, tq, tk):
    nq = seq_len // tq
    hs = ATTN_HEADS_PER_STEP
    return pl.pallas_call(
        functools.partial(_prompt_attn_kernel, tq=tq, tk=tk),
        grid=(n_seq, MLA_HEADS // hs, nq),
        in_specs=[pl.BlockSpec((tq, hs * HEAD_SLAB), lambda b, hp, i: (b * nq + i, hp)),
                  pl.BlockSpec((seq_len, hs * HEAD_SLAB), lambda b, hp, i: (b, hp)),
                  pl.BlockSpec((hs * MLA_V, seq_len), lambda b, hp, i: (hp, b))],
        out_specs=pl.BlockSpec((tq, hs * MLA_V), lambda b, hp, i: (b * nq + i, hp)),
        out_shape=jax.ShapeDtypeStruct((n_seq * seq_len, MLA_VALL), BF16),
        scratch_shapes=[pltpu.VMEM((hs, tk, tq), F32)],
        compiler_params=pltpu.CompilerParams(dimension_semantics=("parallel", "parallel", "arbitrary"),
                                             vmem_limit_bytes=VMEM_LIMIT),
        name="prompt_attn",
    )(qa, ka, va)


def _sample_attn_kernel(pt_ref, qabs_ref, qrope_ref, cnew_ref, krnew_ref, wkT_ref, wv_ref, poolc_hbm, poolr_hbm,
                        o_ref, lhs_scr, m_scr, l_scr, acc_scr, cb_scr, krT_scr, s_scr, cbuf, rbuf, sem,
                        *, n_tok, pages_per_step, layer):
    G = pages_per_step
    b = pl.program_id(0)
    g = pl.program_id(1)
    n_g = pl.num_programs(1)
    step = b * n_g + g
    n_steps = pl.num_programs(0) * n_g
    last = step == n_steps - 1
    slot = step % PAGE_RING

    def group_of(s):
        s = jnp.minimum(s, n_steps - 1)
        return s // n_g, s % n_g
    rows = n_tok * MLA_HEADS
    n_k = MLA_HEADS * MLA_NOPE

    def page_copies(sl, group=None):
        copies = []
        for j in range(G):
            page = 0 if group is None else pt_ref[group[0], group[1] * G + j]
            copies.append(pltpu.make_async_copy(poolc_hbm.at[layer, page], cbuf.at[sl, j], sem.at[0, sl]))
            copies.append(pltpu.make_async_copy(poolr_hbm.at[layer, page], rbuf.at[sl, j], sem.at[1, sl]))
        return copies

    @pl.when(step == 0)
    def _():
        for d in range(PAGE_RING - 1):
            for cp in page_copies(d, group_of(d)):
                cp.start()

    for cp in page_copies(slot):
        cp.wait()

    def scores(cb, krT):
        nk = cb.shape[0]
        big = _dot_nt(lhs_scr[...], cb)
        kvn = big[:n_k]
        s_nope = big[n_k:n_k + rows]
        ss = jnp.sum((kvn * kvn).reshape(MLA_HEADS, MLA_NOPE, nk), axis=1)
        inv = lax.rsqrt(ss * (1.0 / MLA_NOPE) + EPS)
        s_rope = _dot(qrope_ref[0], krT)
        return (s_nope.reshape(n_tok, MLA_HEADS, nk) * inv[None]).reshape(rows, nk) + s_rope

    def weights(s):
        m = m_scr[...]
        m_new = jnp.maximum(m, jnp.max(s, axis=-1, keepdims=True))
        alpha = jnp.exp2(m - m_new)
        p = jnp.exp2(s - m_new)
        l_scr[...] = alpha * l_scr[...] + jnp.sum(p, axis=-1, keepdims=True)
        m_scr[...] = m_new
        return alpha, p.astype(BF16)

    def accumulate(alpha, p, cb):
        acc_scr[...] = alpha * acc_scr[...] + _dot(p, cb)

    def absorb(s, cb):
        accumulate(*weights(s), cb)

    cur = g % 2
    prv = 1 - cur

    @pl.when(g == 0)
    def _():
        lhs_scr[:n_k, :] = wkT_ref[...]
        lhs_scr[n_k:n_k + rows, :] = qabs_ref[0]
        m_scr[...] = jnp.full(m_scr.shape, NEG, F32)
        l_scr[...] = jnp.zeros(l_scr.shape, F32)
        acc_scr[...] = jnp.zeros(acc_scr.shape, F32)
        s_scr[prv] = jnp.full(s_scr.shape[1:], NEG, F32)
        cb_scr[prv] = jnp.zeros(cb_scr.shape[1:], BF16)
        c_new = jnp.concatenate([cnew_ref[0], jnp.zeros((PAGE_SIZE - n_tok, MLA_KV_LORA), F32)],
                                axis=0).astype(BF16)
        key = lax.broadcasted_iota(jnp.int32, (rows, PAGE_SIZE), 1)
        tok = lax.broadcasted_iota(jnp.int32, (rows, PAGE_SIZE), 0) // MLA_HEADS
        absorb(jnp.where(key <= tok, scores(c_new, krnew_ref[0].astype(BF16)), NEG), c_new)

    for j in range(G):
        cb_scr[cur, j * PAGE_SIZE:(j + 1) * PAGE_SIZE, :] = cbuf[slot, j].astype(BF16)
        krT_scr[:, j * PAGE_SIZE:(j + 1) * PAGE_SIZE] = rbuf[slot, j].astype(BF16)

    ahead = step + (PAGE_RING - 1)
    for cp in page_copies(ahead % PAGE_RING, group_of(ahead)):
        cp.start()

    alpha, p = weights(s_scr[prv])
    s_scr[cur] = scores(cb_scr[cur], krT_scr[...])
    accumulate(alpha, p, cb_scr[prv])

    @pl.when(last)
    def _():
        for d in range(1, PAGE_RING):
            for cp in page_copies((step + d) % PAGE_RING):
                cp.wait()

    @pl.when(g == n_g - 1)
    def _():
        absorb(s_scr[cur], cb_scr[cur])
        o_lat =(acc_scr[...] / l_scr[...]).astype(BF16)
        full = _dot(o_lat, wv_ref[...])
        full = full.reshape(n_tok, MLA_HEADS, MLA_VALL)
        hr = lax.broadcasted_iota(jnp.int32, (MLA_HEADS, MLA_VALL), 0)
        hc = lax.broadcasted_iota(jnp.int32, (MLA_HEADS, MLA_VALL), 1) // MLA_V
        o_ref[...] = jnp.sum(jnp.where((hr == hc)[None], full, 0.0), axis=1).astype(o_ref.dtype)


def _sample_attn(page_table, qabs, qrope, c_new, kr_new, w_kT, w_v, pool_c, pool_r, layer, *, pages_per_step):
    n_b, n_pages = page_table.shape
    n_tok = c_new.shape[1]
    rows = n_tok * MLA_HEADS
    G = pages_per_step
    n_k = MLA_HEADS * MLA_NOPE

    per_b = lambda b, g, pt: (b, 0, 0)
    grid_spec = pltpu.PrefetchScalarGridSpec(
        num_scalar_prefetch=1,
        grid=(n_b, n_pages // G),
        in_specs=[pl.BlockSpec((1, rows, MLA_KV_LORA), per_b), pl.BlockSpec((1, rows, MLA_ROPE), per_b),
                  pl.BlockSpec((1, n_tok, MLA_KV_LORA), per_b), pl.BlockSpec((1, MLA_ROPE, PAGE_SIZE), per_b),
                  pl.BlockSpec((n_k, MLA_KV_LORA), lambda b, g, pt: (0, 0)),
                  pl.BlockSpec((MLA_KV_LORA, MLA_VALL), lambda b, g, pt: (0, 0)),
                  pl.BlockSpec(memory_space=pl.ANY), pl.BlockSpec(memory_space=pl.ANY)],
        out_specs=pl.BlockSpec((n_tok, MLA_VALL), lambda b, g, pt: (b, 0)),
        scratch_shapes=[pltpu.VMEM((n_k + rows, MLA_KV_LORA), BF16), pltpu.VMEM((rows, 1), F32),
                        pltpu.VMEM((rows, 1), F32), pltpu.VMEM((rows, MLA_KV_LORA), F32),
                        pltpu.VMEM((2, G * PAGE_SIZE, MLA_KV_LORA), BF16), pltpu.VMEM((MLA_ROPE, G * PAGE_SIZE), BF16),
                        pltpu.VMEM((2, rows, G * PAGE_SIZE), F32),
                        pltpu.VMEM((PAGE_RING, G, PAGE_SIZE, MLA_KV_LORA), F32),
                        pltpu.VMEM((PAGE_RING, G, MLA_ROPE, PAGE_SIZE), F32),
                        pltpu.SemaphoreType.DMA((2, PAGE_RING))])
    return pl.pallas_call(
        functools.partial(_sample_attn_kernel, n_tok=n_tok, pages_per_step=G, layer=layer),
        grid_spec=grid_spec,
        out_shape=jax.ShapeDtypeStruct((n_b * n_tok, MLA_VALL), BF16),
        compiler_params=pltpu.CompilerParams(dimension_semantics=("arbitrary", "arbitrary"),
                                             vmem_limit_bytes=VMEM_LIMIT),
        name="sample_attn",
    )(page_table, qabs, qrope, c_new, kr_new, w_kT, w_v, pool_c, pool_r)


def _post_kernel(x_ref, og_ref, om_ref, p_ref, wo_ref, gffn_ref, wg_ref, wu_ref, cw_ref, cb_ref, wd_ref,
                 gple_ref, wpg_ref, wpp_ref, *rest, tiles_per_seq, seq_rows):
    paged_prev = seq_rows is not None
    if paged_prev:
        p1_ref, p2_ref, y_ref, tail_ref, acc_scr = rest
    else:
        y_ref, tail_ref, acc_scr, carry_scr = rest
    tm = x_ref.shape[0]
    i = pl.program_id(0)
    h1 = x_ref[...] + _dot(og_ref[...], wo_ref[:GLA_V, :]) + _dot(om_ref[...], wo_ref[GLA_V:, :])
    n2 = _rms(h1, gffn_ref[...]).astype(BF16)
    if not paged_prev:
        @pl.when(i % tiles_per_seq == 0)
        def _():
            carry_scr[...] = jnp.zeros(carry_scr.shape, F32)

    bounds = np.concatenate([[0], np.cumsum(FFN_CHUNKS)]).tolist()
    chunks = [slice(lo, hi) for lo, hi in zip(bounds[:-1], bounds[1:])]
    lo = bounds[-1]

    def gate_up(cols):
        return _dot(n2, wg_ref[:, cols]), _dot(n2, wu_ref[:, cols])

    nxt = gate_up(chunks[0])
    for ci, cols in enumerate(chunks):
        a, up = nxt
        if ci + 1 < len(chunks):
            nxt = gate_up(chunks[ci + 1])
        row =lax.broadcasted_iota(jnp.int32, a.shape, 0)
        slabs = [a[:, c0:c0 + 2 * LANES] for c0 in range(0, a.shape[1], 2 * LANES)]
        r1 = jnp.concatenate([pltpu.roll(sl, 1, axis=0) for sl in slabs], axis=1)
        r2 = jnp.concatenate([pltpu.roll(sl, 2, axis=0) for sl in slabs], axis=1)
        if paged_prev:
            t = row % seq_rows
            a1 = jnp.where(t >= 1, r1, p1_ref[:, cols])
            a2 = jnp.where(t >= 2, r2, p2_ref[:, cols])
            tail_ref[:, cols] = a
        else:
            prev = carry_scr[:, cols]
            pm1 = prev[SUBLANES - 1:SUBLANES, :]
            pm2 = prev[SUBLANES - 2:SUBLANES - 1, :]
            a1 = jnp.where(row >= 1, r1, pm1)
            a2 = jnp.where(row >= 2, r2, jnp.where(row == 0, pm2, pm1))
            carry_scr[:, cols] = a[tm - SUBLANES:, :]
            tail_ref[:, cols] = a[tm - SUBLANES:, :]
        conv = cb_ref[:, cols] + cw_ref[0:1, cols] * a2 + cw_ref[1:2, cols] * a1 + cw_ref[2:3, cols] * a
        gact = (conv / (1.0 + jnp.exp(-conv)) * up).astype(BF16)
        down = _dot(gact, wd_ref[cols, :])
        if ci == 0:
            acc_scr[...] = down
        else:
            acc_scr[...] += down
    assert lo == D_FF
    h2 = h1 + acc_scr[...]
    n3 =_rms(h2, gple_ref[...]).astype(BF16)
    gate = 1.0 / (1.0 + jnp.exp(-_dot(n3, wpg_ref[...])))
    y_ref[...] = h2 + _dot(p_ref[...].astype(BF16), wpp_ref[...]) * gate


def _post(x2d, og, om, p2d, w, prev=None, *, tm, tiles_per_seq, seq_rows):
    T = x2d.shape[0]
    nt = T // tm
    row = lambda i: (i, 0)
    consts = [w['w_o'], w['g_ffn'], w['w_gate'], w['w_up'], w['conv_w'], w['conv_b'], w['w_down'],
              w['g_ple'], w['w_pgate'], w['w_pproj']]
    in_specs = [pl.BlockSpec((tm, D_MODEL), row), pl.BlockSpec((tm, GLA_V), row),
                pl.BlockSpec((tm, MLA_VALL), row), pl.BlockSpec((tm, PLE_DIM), row)]
    in_specs += [_const_spec(c.shape) for c in consts]
    args = [x2d, og, om, p2d, *consts]
    scratch = [pltpu.VMEM((tm, D_MODEL), F32)]
    if prev is not None:
        in_specs += [pl.BlockSpec((tm, D_FF), row)] * 2
        args += list(prev)
        tail_rows = tm
    else:
        scratch.append(pltpu.VMEM((SUBLANES, D_FF), F32))
        tail_rows = SUBLANES
    return pl.pallas_call(
        functools.partial(_post_kernel, tiles_per_seq=tiles_per_seq, seq_rows=seq_rows),
        grid=(nt,),
        in_specs=in_specs,
        out_specs=[pl.BlockSpec((tm, D_MODEL), row), pl.BlockSpec((tail_rows, D_FF), row)],
        out_shape=[jax.ShapeDtypeStruct((T, D_MODEL), F32), jax.ShapeDtypeStruct((nt * tail_rows, D_FF), F32)],
        scratch_shapes=scratch,
        compiler_params=pltpu.CompilerParams(dimension_semantics=("arbitrary",), vmem_limit_bytes=VMEM_LIMIT),
        name="post_sample" if prev is not None else "post_prompt",
    )(*args)


def _prep_weights(g_mix, w_in, gla_w_a2, gla_b_a, gla_g_out, mla_g_qa, mla_w_qup, mla_g_qn, mla_g_qr,
                  mla_g_kva, mla_g_kr, mla_w_kvup, mla_g_kn, w_o, g_ffn, ffn_w_gate, ffn_w_up, ffn_conv_w,
                  ffn_conv_b, ffn_w_down, g_ple, ple_w_gate, ple_w_proj):
    sizes = (GLA_QK, GLA_QK, GLA_V, GLA_V, GLA_GATE_RANK, MLA_Q_LORA, MLA_KV_LORA, MLA_ROPE)
    offs = np.concatenate([[0], np.cumsum(sizes)])
    piece = lambda i: w_in[:, offs[i]:offs[i + 1]]
    zeros = lambda n: jnp.zeros((D_MODEL, n), w_in.dtype)
    misc = jnp.concatenate([piece(4), zeros(ROPE_LO - GLA_GATE_RANK), piece(7),
                            zeros(LANES - ROPE_LO - MLA_ROPE)], axis=1)
    w_in_p = jnp.concatenate([piece(0), piece(1), piece(2), piece(3), piece(5), piece(6), misc], axis=1)
    w_a2 = jnp.concatenate([gla_w_a2, jnp.zeros((LANES - GLA_GATE_RANK, GLA_QK), gla_w_a2.dtype)], axis=0)

    def slab_vec(nope, rope_):
        one = jnp.concatenate([nope, rope_, jnp.zeros((HEAD_SLAB - MLA_NOPE - MLA_ROPE,), F32)])
        return jnp.tile(one, MLA_HEADS)[None, :]

    wq = mla_w_qup.reshape(MLA_Q_LORA, MLA_HEADS, MLA_NOPE + MLA_ROPE)
    swap = np.arange(MLA_NOPE + MLA_ROPE)
    swap[MLA_NOPE:] = np.concatenate([swap[MLA_NOPE + ROPE_HALF:], swap[MLA_NOPE:MLA_NOPE + ROPE_HALF]])
    slab_pad = ((0, 0), (0, 0), (0, HEAD_SLAB - MLA_NOPE - MLA_ROPE))
    wq = jnp.concatenate([jnp.pad(wq, slab_pad).reshape(MLA_Q_LORA, MLA_SLABS),
                          jnp.pad(wq[:, :, swap], slab_pad).reshape(MLA_Q_LORA, MLA_SLABS)], axis=1)
    g_q = jnp.concatenate([mla_g_qn, mla_g_qr])
    gq_slab = jnp.pad(jnp.stack([g_q, g_q[swap]]), ((0, 0), (0, HEAD_SLAB - MLA_NOPE - MLA_ROPE)))
    wkv = mla_w_kvup.reshape(MLA_KV_LORA, MLA_HEADS, MLA_NOPE + MLA_V)
    wk = wkv[:, :, :MLA_NOPE]
    wv = wkv[:, :, MLA_NOPE:].reshape(MLA_KV_LORA, MLA_VALL)
    wk_slab = jnp.pad(wk, ((0, 0), (0, 0), (0, HEAD_SLAB - MLA_NOPE))).reshape(MLA_KV_LORA, MLA_SLABS)
    wk_g = jnp.pad(wk * mla_g_kn[None, None, :], ((0, 0), (0, 0), (0, HEAD_SLAB - MLA_NOPE)))
    eye = jnp.eye(MLA_HEADS, dtype=F32)
    w_kabs = jnp.einsum('nhd,hg->hdgn', wk_g, eye).reshape(MLA_SLABS, MLA_HEADS * MLA_KV_LORA)
    w_kT = wk.transpose(1, 2, 0).reshape(MLA_HEADS * MLA_NOPE, MLA_KV_LORA)

    lane = np.arange(2 * HEAD_SLAB)
    seg = np.where(lane % HEAD_SLAB < MLA_NOPE, 0, np.where(lane % HEAD_SLAB < MLA_NOPE + MLA_ROPE, 1, 2))
    same = (lane[:, None] // HEAD_SLAB == lane[None, :] // HEAD_SLAB) & (seg[:, None] == seg[None, :])
    segm = np.where(same & (seg[:, None] == 0), 1.0 / MLA_NOPE, np.where(same & (seg[:, None] == 1), 1.0 / MLA_ROPE, 0.0))

    gkr = jnp.concatenate([jnp.zeros((ROPE_LO,), F32), mla_g_kr, jnp.zeros((LANES - ROPE_LO - MLA_ROPE,), F32)])
    return dict(
        g_mix=g_mix[None, :], w_in=w_in_p.astype(BF16), w_a2=w_a2.astype(BF16), b_a=gla_b_a[None, :],
        g_out=gla_g_out[None, :], g_qa=mla_g_qa[None, :], w_qup=wq.astype(BF16), segm=jnp.asarray(segm, BF16),
        gq_slab=gq_slab, g_kva=mla_g_kva[None, :], w_kslab=wk_slab.astype(BF16),
        w_vT=wv.T.astype(BF16),
        gk_slab=slab_vec(mla_g_kn, jnp.zeros((MLA_ROPE,), F32)),
        g_kn=mla_g_kn, gkr_slab=gkr[None, :], w_kabs=w_kabs.astype(BF16), w_kT=w_kT.astype(BF16),
        w_v=wv.astype(BF16), w_o=w_o.astype(BF16), g_ffn=g_ffn[None, :], w_gate=ffn_w_gate.astype(BF16),
        w_up=ffn_w_up.astype(BF16), conv_w=ffn_conv_w, conv_b=ffn_conv_b[None, :],
        w_down=ffn_w_down.astype(BF16), g_ple=g_ple[None, :], w_pgate=ple_w_gate.astype(BF16),
        w_pproj=ple_w_proj.astype(BF16))


def _rope_tables(pos):
    inv = ROPE_THETA ** (-jnp.arange(ROPE_HALF, dtype=F32) * 2.0 / MLA_ROPE)
    ang = pos.astype(F32)[:, None] * inv[None, :]
    cos, sin = jnp.cos(ang), jnp.sin(ang)
    T = pos.shape[0]
    pad = jnp.zeros((T, HEAD_SLAB - MLA_NOPE - MLA_ROPE), F32)
    cos_t = jnp.concatenate([jnp.ones((T, MLA_NOPE), F32), cos, cos, pad], axis=1)
    sin_t = jnp.concatenate([jnp.zeros((T, MLA_NOPE), F32), -sin, sin, pad], axis=1)
    return cos_t, sin_t


def _pick_tile(n, cap):
    t = math.gcd(n, cap)
    return t


def _layer(w, x_p, x_s, p_p, p_s, pool_c, pool_r, layer, state_gla, state_conv, page_table):
    B, S, _ = x_p.shape
    Bd, Td, _ = x_s.shape
    tm = _pick_tile(S, 512)
    tps = S // tm
    cos_p, sin_p = _rope_tables(jnp.arange(S))
    xp2 = x_p.reshape(B * S, D_MODEL)
    gq, gk, gv, gg, la, qa, ckv_p, kr_p, ka, vaT = _in_proj(xp2, cos_p, sin_p, w, tm=tm, tiles_per_seq=tps,
                                                            sample=False)
    chunk_p = math.gcd(S, GLA_CHUNK)
    n_chunks = math.gcd(S // chunk_p, 8)
    og_p, gla_p = _gla(gq, gk, gv, gg, la, jnp.zeros((B, GLA_HEADS, GLA_DK, GLA_DV), F32), w['g_out'],
                      n_seq=B, seq_len=S, chunk=chunk_p, n_chunks=n_chunks,
                      seqs_per_step=math.gcd(B, GLA_PROMPT_SEQS))
    tq = _pick_tile(S, 512)
    om_p = _prompt_attn(qa, ka, vaT, n_seq=B, seq_len=S, tq=tq, tk=tq)
    y_p, tail_p = _post(xp2, og_p, om_p, p_p.reshape(B * S, PLE_DIM), w, tm=tm, tiles_per_seq=tps, seq_rows=None)
    conv_p = tail_p.reshape(B, tps, SUBLANES, D_FF)[:, -1, SUBLANES - (CONV_W - 1):, :]

    Ts = Bd * Td
    pos_s = PAST_LEN + jnp.arange(Td)
    cos_s, sin_s = _rope_tables(jnp.tile(pos_s, Bd))
    xs2 = x_s.reshape(Ts, D_MODEL)
    gq, gk, gv, gg, la, qa, ckv_s, kr_s, qabs = _in_proj(xs2, cos_s, sin_s, w, tm=Ts, tiles_per_seq=1, sample=True)
    chunk_s = math.gcd(Td, GLA_CHUNK)
    og_s, gla_s = _gla(gq, gk, gv, gg, la, state_gla, w['g_out'],
                      n_seq=Bd, seq_len=Td, chunk=chunk_s, n_chunks=Td // chunk_s,
                      seqs_per_step=math.gcd(Bd, GLA_SAMPLE_SEQS))
    rows = Td * MLA_HEADS
    qrope = qa.reshape(Ts, MLA_HEADS, HEAD_SLAB)[:, :, ROPE_LO:ROPE_LO + MLA_ROPE].reshape(Bd, rows, MLA_ROPE)
    krT_new = jnp.pad(jnp.swapaxes(kr_s.reshape(Bd, Td, MLA_ROPE), 1, 2), ((0, 0), (0, 0), (0, PAGE_SIZE - Td)))
    om_s = _sample_attn(page_table, qabs.reshape(Bd, rows, MLA_KV_LORA), qrope,
                        ckv_s.reshape(Bd, Td, MLA_KV_LORA), krT_new,
                        w['w_kT'], w['w_v'], pool_c, jnp.swapaxes(pool_r, 2, 3), layer,
                        pages_per_step=math.gcd(page_table.shape[1], SAMPLE_PAGES_PER_STEP))
    zpad = lambda a, lo: jnp.pad(a, ((0, 0), (lo, Td - lo - a.shape[1]), (0, 0))).reshape(Ts, D_FF)
    prev1 = zpad(state_conv[:, 1:2], 0)
    prev2 = zpad(state_conv, 0)
    y_s, a_s = _post(xs2, og_s, om_s, p_s.reshape(Ts, PLE_DIM), w, prev=(prev1, prev2), tm=Ts, tiles_per_seq=1,
                     seq_rows=Td)
    full = jnp.concatenate([state_conv, a_s.reshape(Bd, Td, D_FF)], axis=1)
    conv_s = full[:, full.shape[1] - (CONV_W - 1):]
    return (y_p.reshape(B, S, D_MODEL), y_s.reshape(Bd, Td, D_MODEL),
            ckv_p.reshape(B, S, MLA_KV_LORA), kr_p.reshape(B, S, MLA_ROPE),
            gla_p, conv_p,
            ckv_s.reshape(Bd, Td, MLA_KV_LORA), kr_s.reshape(Bd, Td, MLA_ROPE),
            gla_s, conv_s)


def kernel(x_prompt, x_sample, cache_ckv, cache_krope, state_gla, state_conv, page_table, p_prompt, p_sample, g_mix, w_in, gla_w_a2, gla_b_a, gla_g_out, mla_g_qa, mla_w_qup, mla_g_qn, mla_g_qr, mla_g_kva, mla_g_kr, mla_w_kvup, mla_g_kn, w_o, g_ffn, ffn_w_gate, ffn_w_up, ffn_conv_w, ffn_conv_b, ffn_w_down, g_ple, ple_w_gate, ple_w_proj):
    depth = w_in.shape[0]
    per_layer = (g_mix, w_in, gla_w_a2, gla_b_a, gla_g_out, mla_g_qa, mla_w_qup, mla_g_qn, mla_g_qr, mla_g_kva,
                 mla_g_kr, mla_w_kvup, mla_g_kn, w_o, g_ffn, ffn_w_gate, ffn_w_up, ffn_conv_w, ffn_conv_b,
                 ffn_w_down, g_ple, ple_w_gate, ple_w_proj)
    y_p, y_s = x_prompt, x_sample
    outs = [[] for _ in range(8)]
    for i in range(depth):
        w = _prep_weights(*(a[i] for a in per_layer))
        res = _layer(w, y_p, y_s, p_prompt[i], p_sample[i], cache_ckv, cache_krope, i, state_gla[i],
                     state_conv[i], page_table)
        y_p, y_s = res[0], res[1]
        for lst, r in zip(outs, res[2:]):
            lst.append(r)
    return (y_p, y_s) + tuple(jnp.stack(lst) for lst in outs)
```

```python
import functools
import math

import jax
import jax.numpy as jnp
import numpy as np
from jax import lax
from jax.experimental import pallas as pl
from jax.experimental.pallas import tpu as pltpu

D_MODEL = 1024
PAST_LEN = 16384
PAGE_SIZE = 128
GLA_HEADS = 4
GLA_DK = 64
GLA_DV = 128
GLA_GATE_RANK = 16
GLA_GATE_NORM = 16.0
GLA_CHUNK = 32
MLA_HEADS = 8
MLA_Q_LORA = 256
MLA_KV_LORA = 128
MLA_NOPE = 64
MLA_ROPE = 32
MLA_V = 64
MLA_SCALE = (MLA_NOPE + MLA_ROPE) ** -0.5
LOG2E = math.log2(math.e)
ROPE_THETA = 10000.0
D_FF = 2816
CONV_W = 3
PLE_DIM = 256
EPS = 1e-6
NEG = -1e30

LANES = 128
SUBLANES = 8
HEAD_SLAB = LANES
ROPE_LO = MLA_NOPE
ROPE_HALF = MLA_ROPE // 2
GLA_QK = GLA_HEADS * GLA_DK
GLA_V = GLA_HEADS * GLA_DV
MLA_SLABS = MLA_HEADS * HEAD_SLAB
MLA_VALL = MLA_HEADS * MLA_V
C_GQ, C_GK, C_GV, C_GG = 0, 256, 512, 1024
C_MQ, C_MKV, C_MISC = 1536, 1792, 1920
IN_COLS_P = 2048
FFN_CHUNKS = (1536, 1280)
SAMPLE_PAGES_PER_STEP = 32
ATTN_HEADS_PER_STEP = 8
GLA_PROMPT_SEQS = 4
GLA_SAMPLE_SEQS = 8
PAGE_RING = 3
VMEM_LIMIT = 56 * 1024 * 1024

BF16 = jnp.bfloat16
F32 = jnp.float32


def _dot(a, b):
    return jnp.dot(a, b, preferred_element_type=F32)


def _dot_nt(a, b):
    return lax.dot_general(a, b, (((1,), (1,)), ((), ())), preferred_element_type=F32)


def _dot_tn(a, b):
    return lax.dot_general(a, b, (((0,), (0,)), ((), ())), preferred_element_type=F32)


def _rms(x, g):
    return x * lax.rsqrt(jnp.mean(x * x, axis=-1, keepdims=True) + EPS) * g


def _const_spec(shape):
    nd = len(shape)
    return pl.BlockSpec(shape, lambda *_: (0,) * nd, pipeline_mode=pl.Buffered(1))


def _rope_slab(y, cos_t, sin_t):
    lane = lax.broadcasted_iota(jnp.int32, y.shape, 1)
    swapped = jnp.where(lane < ROPE_LO + ROPE_HALF,
                        pltpu.roll(y, LANES - ROPE_HALF, axis=1),
                        pltpu.roll(y, ROPE_HALF, axis=1))
    return y * cos_t + swapped * sin_t


def _in_proj_kernel(x_ref, cos_ref, sin_ref, gmix_ref, win_ref, wa2_ref, ba_ref, gqa_ref, wqup_ref,
                    segm_ref, gq_ref, gkva_ref, gkr_ref, *rest, sample):
    if sample:
        (wkabs_ref, gq_o, gk_o, gv_o, gg_o, la_o, qa_o, ckv_o, kr_o, qabs_o) = rest
    else:
        (wk_ref, gk_ref, wvT_ref, gq_o, gk_o, gv_o, gg_o, la_o, qa_o, ckv_o, kr_o, ka_o, vaT_o) = rest
    segm = segm_ref[...]
    q_gain = gq_ref[...]

    cos_t = cos_ref[...]
    sin_t = sin_ref[...]
    n = _rms(x_ref[...], gmix_ref[...]).astype(BF16)

    def proj(lo, width):
        return _dot(n, win_ref[:, lo:lo + width])

    um = proj(C_MQ, IN_COLS_P - C_MQ)
    ug = proj(C_GQ, 2 * GLA_QK)
    gq_o[...] = ug[:, :GLA_QK] * (GLA_DK ** -0.5)
    gk_o[...] = ug[:, GLA_QK:]
    misc = um[:, C_MISC - C_MQ:]
    z = _dot(misc.astype(BF16), wa2_ref[...]) + ba_ref[...]
    la_o[...] = (jnp.minimum(z, 0.0) - jnp.log1p(jnp.exp(-jnp.abs(z)))) * (1.0 / GLA_GATE_NORM)
    lane = lax.broadcasted_iota(jnp.int32, misc.shape, 1)
    is_rope = (lane >= ROPE_LO) & (lane < ROPE_LO + MLA_ROPE)
    ms_r = jnp.sum(jnp.where(is_rope, misc * misc, 0.0), axis=-1, keepdims=True) * (1.0 / MLA_ROPE)
    kr_slab = _rope_slab(misc * lax.rsqrt(ms_r + EPS) * gkr_ref[...], cos_t, sin_t)
    kr_o[...] = kr_slab[:, ROPE_LO:ROPE_LO + MLA_ROPE]
    ckv = _rms(um[:, C_MKV - C_MQ:C_MISC - C_MQ], gkva_ref[...])
    ckv_o[...] = ckv
    ckv_b = ckv.astype(BF16)
    cq = _rms(um[:, :MLA_Q_LORA], gqa_ref[...]).astype(BF16)

    if not sample:
        kn = _dot(ckv_b, wk_ref[...])
        vaT_o[...] = _dot_nt(wvT_ref[...], ckv_b).astype(BF16)
    qf = _dot(cq, wqup_ref[...])
    q_cos = cos_t * (q_gain[0:1] * (MLA_SCALE * LOG2E))
    q_sin = sin_t * (q_gain[1:2] * (MLA_SCALE * LOG2E))

    def head_pair(pair):
        lo = pair * 2 * HEAD_SLAB
        q2 = qf[:, lo:lo + 2 * HEAD_SLAB]
        q_inv = lax.rsqrt(_dot((q2 * q2).astype(BF16), segm) + EPS)
        if not sample:
            k2 = kn[:, lo:lo + 2 * HEAD_SLAB]
            k2 = k2 * lax.rsqrt(_dot((k2 * k2).astype(BF16), segm) + EPS) * gk_ref[:, lo:lo + 2 * HEAD_SLAB]
        for j in range(2):
            sl = slice(lo + j * HEAD_SLAB, lo + (j + 1) * HEAD_SLAB)
            swapped = qf[:, MLA_SLABS + lo + j * HEAD_SLAB:MLA_SLABS + lo + (j + 1) * HEAD_SLAB]
            qh = q_inv[:, j * HEAD_SLAB:(j + 1) * HEAD_SLAB] * (qf[:, sl] * q_cos + swapped * q_sin)
            qa_o[:, sl] = qh.astype(BF16)
            if not sample:
                ka_o[:, sl] = (k2[:, j * HEAD_SLAB:(j + 1) * HEAD_SLAB] + kr_slab).astype(BF16)

    half = GLA_V // 2
    gv_o[:, :half] = proj(C_GV, half).astype(BF16)
    head_pair(0)
    gv_o[:, half:] = proj(C_GV + half, half).astype(BF16)
    head_pair(1)
    gg_o[:, :half] = proj(C_GG, half)
    head_pair(2)
    gg_o[:, half:] = proj(C_GG + half, half)
    head_pair(3)
    if sample:
        qabs_o[...] = _dot(qa_o[...], wkabs_ref[...]).astype(BF16)


def _in_proj(x2d, cos_t, sin_t, w, *, tm, tiles_per_seq, sample):
    T = x2d.shape[0]
    nt = T // tm
    row = lambda i: (i, 0)
    pos = lambda i: (i % tiles_per_seq, 0)
    consts = [w['g_mix'], w['w_in'], w['w_a2'], w['b_a'], w['g_qa'], w['w_qup'], w['segm'], w['gq_slab'],
              w['g_kva'], w['gkr_slab']]
    consts += [w['w_kabs']] if sample else [w['w_kslab'], w['gk_slab'], w['w_vT']]
    in_specs = [pl.BlockSpec((tm, D_MODEL), row), pl.BlockSpec((tm, LANES), pos), pl.BlockSpec((tm, LANES), pos)]
    in_specs += [_const_spec(c.shape) for c in consts]
    outs = [(GLA_QK, F32), (GLA_QK, F32), (GLA_V, BF16), (GLA_V, F32), (GLA_QK, F32),
            (MLA_SLABS, BF16), (MLA_KV_LORA, F32), (MLA_ROPE, F32), (MLA_SLABS, BF16)]
    out_specs = [pl.BlockSpec((tm, c), row) for c, _ in outs]
    out_shape = [jax.ShapeDtypeStruct((T, c), d) for c, d in outs]
    if not sample:
        out_specs.append(pl.BlockSpec((MLA_VALL, tm), lambda i: (0, i)))
        out_shape.append(jax.ShapeDtypeStruct((MLA_VALL, T), BF16))
    return pl.pallas_call(
        functools.partial(_in_proj_kernel, sample=sample),
        grid=(nt,),
        in_specs=in_specs,
        out_specs=out_specs,
        out_shape=out_shape,
        compiler_params=pltpu.CompilerParams(dimension_semantics=("parallel",), vmem_limit_bytes=VMEM_LIMIT),
        name="in_proj_sample" if sample else "in_proj_prompt",
    )(x2d, cos_t, sin_t, *consts)


def _gla_kernel(q_ref, k_ref, v_ref, gg_ref, la_ref, s0_ref, cum_ref, sel_ref, gout_ref,
                o_ref, s_out_ref, st_scr, *, chunk, n_chunks):
    step = pl.program_id(1)
    n_seq = q_ref.shape[0]
    tc = chunk * n_chunks
    heads = range(GLA_HEADS)
    dk = [slice(h * GLA_DK, (h + 1) * GLA_DK) for h in heads]
    dv = [slice(h * GLA_DV, (h + 1) * GLA_DV) for h in heads]
    rows = [slice(ci * chunk, (ci + 1) * chunk) for ci in range(n_chunks)]

    @pl.when(step == 0)
    def _():
        st_scr[...] = s0_ref[...]

    cum = cum_ref[...]
    sel = sel_ref[...]
    r = lax.broadcasted_iota(jnp.int32, (tc, tc), 0)
    c = lax.broadcasted_iota(jnp.int32, (tc, tc), 1)
    causal = (r >= c) & ((r // chunk) == (c // chunk))
    gout = gout_ref[...]

    def decays(i, t):
        la = la_ref[i]
        la_hi = la.astype(BF16)
        la_lo = (la - la_hi.astype(F32)).astype(BF16)
        t['bb'] = _dot(cum, la_hi) + _dot(cum, la_lo)
        t['dec'] = jnp.exp(_dot_tn(la_hi, sel) + _dot_tn(la_lo, sel))

    def scale(i, t):
        b, bl = t['bb'][:tc], t['bb'][tc:]
        q, k = q_ref[i], k_ref[i]
        t['q_dec'] = (q * jnp.exp(b)).astype(BF16)
        t['k_inv'] = (k * jnp.exp(-b)).astype(BF16)
        t['k_end'] = (k * jnp.exp(bl - b)).astype(BF16)

    def intra_scores(i, t):
        t['a'] = [_dot_nt(t['q_dec'][:, dk[h]], t['k_inv'][:, dk[h]]) for h in heads]

    def chunk_outer(i, t):
        v = v_ref[i]
        t['x'] = [[_dot_tn(t['k_end'][rw, dk[h]], v[rw, dv[h]]) for rw in rows] for h in heads]

    def intra_out(i, t):
        v = v_ref[i]
        t['o'] = [_dot(jnp.where(causal, t['a'][h], 0.0).astype(BF16), v[:, dv[h]]) for h in heads]

    def scan(i, t):
        t['s'] = []
        for h in heads:
            s_h = st_scr[i, h]
            starts = []
            for ci in range(n_chunks):
                starts.append(s_h.astype(BF16))
                s_h = s_h * t['dec'][dk[h], ci:ci + 1] + t['x'][h][ci]
            st_scr[i, h] = s_h
            t['s'].append(starts)

    def inter_out(i, t):
        for h in heads:
            parts = [_dot(t['q_dec'][rw, dk[h]], t['s'][h][ci]) for ci, rw in enumerate(rows)]
            t['o'][h] = t['o'][h] + (jnp.concatenate(parts, axis=0) if n_chunks > 1 else parts[0])

    def finish(i, t):
        gg = gg_ref[i]
        gate = gg / (1.0 + jnp.exp(-gg))
        for h in heads:
            o_ref[i, :, dv[h]] = (_rms(t['o'][h], gout) * gate[:, dv[h]]).astype(o_ref.dtype)

    temps = [{} for _ in range(n_seq)]
    for stage in (decays, scale, intra_scores, chunk_outer, intra_out, scan, inter_out, finish):
        for i in range(n_seq):
            stage(i, temps[i])

    @pl.when(step == pl.num_programs(1) - 1)
    def _():
        s_out_ref[...] = st_scr[...]


def _gla(q, k, v, gg, la, s0, g_out, *, n_seq, seq_len, chunk, n_chunks, seqs_per_step):
    tc = chunk * n_chunks
    steps = seq_len // tc
    nb = seqs_per_step
    idx = np.arange(tc)
    same = (idx[:, None] // chunk) == (idx[None, :] // chunk)
    cum = jnp.asarray(np.concatenate([same & (idx[:, None] >= idx[None, :]), same], axis=0), BF16)
    sel = jnp.asarray((idx[:, None] // chunk) == np.arange(LANES)[None, :], BF16)
    tok = lambda width: pl.BlockSpec((nb, tc, width), lambda b, s: (b, s, 0))
    per_seq = lambda a: a.reshape(n_seq, seq_len, a.shape[-1])
    state_spec = pl.BlockSpec((nb, GLA_HEADS, GLA_DK, GLA_DV), lambda b, s: (b, 0, 0, 0))
    o, state = pl.pallas_call(
        functools.partial(_gla_kernel, chunk=chunk, n_chunks=n_chunks),
        grid=(n_seq // nb, steps),
        in_specs=[tok(GLA_QK), tok(GLA_QK), tok(GLA_V), tok(GLA_V), tok(GLA_QK), state_spec,
                  _const_spec((2 * tc, tc)), _const_spec((tc, LANES)), _const_spec((1, GLA_DV))],
        out_specs=[tok(GLA_V), state_spec],
        out_shape=[jax.ShapeDtypeStruct((n_seq, seq_len, GLA_V), BF16),
                   jax.ShapeDtypeStruct((n_seq, GLA_HEADS, GLA_DK, GLA_DV), F32)],
        scratch_shapes=[pltpu.VMEM((nb, GLA_HEADS, GLA_DK, GLA_DV), F32)],
        compiler_params=pltpu.CompilerParams(dimension_semantics=("parallel", "arbitrary"),
                                             vmem_limit_bytes=VMEM_LIMIT),
        name=f"gla_c{chunk}",
    )(per_seq(q), per_seq(k), per_seq(v), per_seq(gg), per_seq(la), s0, cum, sel, g_out)
    return o.reshape(n_seq * seq_len, GLA_V), state


def _prompt_attn_kernel(q_ref, k_ref, vT_ref, o_ref, s_scr, *, tq, tk):
    assert tq == tk
    qi = pl.program_id(2)
    key = lax.broadcasted_iota(jnp.int32, (tk, tq), 0)
    qry = lax.broadcasted_iota(jnp.int32, (tk, tq), 1)

    def scores(j, kb, masked):
        ks = pl.multiple_of(kb * tk, tk)
        s = _dot_nt(k_ref[pl.ds(ks, tk), j * HEAD_SLAB:(j + 1) * HEAD_SLAB],
                    q_ref[:, j * HEAD_SLAB:(j + 1) * HEAD_SLAB])
        if masked:
            s = jnp.where(key <= qry, s, NEG)
        s_scr[j] = s
        return jnp.max(s, axis=0, keepdims=True)

    def update(j, kb, m_blk, state):
        m, l, acc = state
        ks = pl.multiple_of(kb * tk, tk)
        m_new = jnp.maximum(m, m_blk)
        alpha = jnp.exp2(m - m_new)
        p = jnp.exp2(s_scr[j] - m_new)
        l = alpha * l + jnp.sum(p, axis=0, keepdims=True)
        acc = alpha * acc + _dot(vT_ref[j * MLA_V:(j + 1) * MLA_V, pl.ds(ks, tk)], p.astype(BF16))
        return m_new, l, acc

    n_heads = q_ref.shape[1] // HEAD_SLAB

    def body(t, carry):
        m_blk, state = list(carry[0]), list(carry[1])
        kb = qi - 1 - t
        for j in range(n_heads):
            new_blk = scores(j, kb, False)
            nxt = (j + 1) % n_heads
            state[nxt] = update(nxt, kb if nxt == 0 else kb + 1, new_blk if n_heads == 1 else m_blk[nxt],
                                state[nxt])
            m_blk[j] = new_blk
        return tuple(m_blk), tuple(state)

    init = (jnp.full((1, tq), NEG, F32), jnp.zeros((1, tq), F32), jnp.zeros((MLA_V, tq), F32))
    m_blk = [scores(j, qi, True) for j in range(n_heads)]
    state = [update(0, qi, m_blk[0], init)] + [init] * (n_heads - 1)
    m_blk, state = lax.fori_loop(0, qi, body, (tuple(m_blk), tuple(state)))
    state = [state[0]] + [update(j, 0, m_blk[j], state[j]) for j in range(1, n_heads)]
    o_ref[...] = jnp.concatenate([acc / l for _, l, acc in state], axis=0).T.astype(o_ref.dtype)


def _prompt_attn(qa, ka, va, *, n_seq, seq_len, tq, tk):
    nq = seq_len // tq
    hs = ATTN_HEADS_PER_STEP
    return pl.pallas_call(
        functools.partial(_prompt_attn_kernel, tq=tq, tk=tk),
        grid=(n_seq, MLA_HEADS // hs, nq),
        in_specs=[pl.BlockSpec((tq, hs * HEAD_SLAB), lambda b, hp, i: (b * nq + i, hp)),
                  pl.BlockSpec((seq_len, hs * HEAD_SLAB), lambda b, hp, i: (b, hp)),
                  pl.BlockSpec((hs * MLA_V, seq_len), lambda b, hp, i: (hp, b))],
        out_specs=pl.BlockSpec((tq, hs * MLA_V), lambda b, hp, i: (b * nq + i, hp)),
        out_shape=jax.ShapeDtypeStruct((n_seq * seq_len, MLA_VALL), BF16),
        scratch_shapes=[pltpu.VMEM((hs, tk, tq), F32)],
        compiler_params=pltpu.CompilerParams(dimension_semantics=("parallel", "parallel", "arbitrary"),
                                             vmem_limit_bytes=VMEM_LIMIT),
        name="prompt_attn",
    )(qa, ka, va)


def _sample_attn_kernel(pt_ref, qabs_ref, qrope_ref, cnew_ref, krnew_ref, wkT_ref, wv_ref, poolc_hbm, poolr_hbm,
                        o_ref, lhs_scr, m_scr, l_scr, acc_scr, cb_scr, krT_scr, s_scr, cbuf, rbuf, sem,
                        *, n_tok, pages_per_step, layer):
    G = pages_per_step
    b = pl.program_id(0)
    g = pl.program_id(1)
    n_g = pl.num_programs(1)
    step = b * n_g + g
    n_steps = pl.num_programs(0) * n_g
    last = step == n_steps - 1
    slot = step % PAGE_RING

    def group_of(s):
        s = jnp.minimum(s, n_steps - 1)
        return s // n_g, s % n_g
    rows = n_tok * MLA_HEADS
    n_k = MLA_HEADS * MLA_NOPE

    def page_copies(sl, group):
        copies = []
        for j in range(G):
            page = pt_ref[group[0], group[1] * G + j]
            copies.append(pltpu.make_async_copy(poolc_hbm.at[layer, page], cbuf.at[sl, j], sem.at[0, sl]))
            copies.append(pltpu.make_async_copy(poolr_hbm.at[layer, page], rbuf.at[sl, j], sem.at[1, sl]))
        return copies

    def wait_slot(sl):
        pltpu.make_async_copy(cbuf.at[sl], cbuf.at[sl], sem.at[0, sl]).wait()
        pltpu.make_async_copy(rbuf.at[sl], rbuf.at[sl], sem.at[1, sl]).wait()

    @pl.when(step == 0)
    def _():
        for d in range(PAGE_RING - 1):
            for cp in page_copies(d, group_of(d)):
                cp.start()

    wait_slot(slot)

    def scores(cb, krT):
        nk = cb.shape[0]
        big = _dot_nt(lhs_scr[...], cb)
        kvn = big[:n_k]
        s_nope = big[n_k:n_k + rows]
        ss = jnp.sum((kvn * kvn).reshape(MLA_HEADS, MLA_NOPE, nk), axis=1)
        inv = lax.rsqrt(ss * (1.0 / MLA_NOPE) + EPS)
        s_rope = _dot(qrope_ref[0], krT)
        return (s_nope.reshape(n_tok, MLA_HEADS, nk) * inv[None]).reshape(rows, nk) + s_rope

    def weights(s):
        m = m_scr[...]
        m_new = jnp.maximum(m, jnp.max(s, axis=-1, keepdims=True))
        alpha = jnp.exp2(m - m_new)
        p = jnp.exp2(s - m_new)
        l_scr[...] = alpha * l_scr[...] + jnp.sum(p, axis=-1, keepdims=True)
        m_scr[...] = m_new
        return alpha, p.astype(BF16)

    def accumulate(alpha, p, cb):
        acc_scr[...] = alpha * acc_scr[...] + _dot(p, cb)

    def absorb(s, cb):
        accumulate(*weights(s), cb)

    cur = g % 2
    prv = 1 - cur

    @pl.when(g == 0)
    def _():
        lhs_scr[:n_k, :] = wkT_ref[...]
        lhs_scr[n_k:n_k + rows, :] = qabs_ref[0]
        m_scr[...] = jnp.full(m_scr.shape, NEG, F32)
        l_scr[...] = jnp.zeros(l_scr.shape, F32)
        acc_scr[...] = jnp.zeros(acc_scr.shape, F32)
        s_scr[prv] = jnp.full(s_scr.shape[1:], NEG, F32)
        cb_scr[prv] = jnp.zeros(cb_scr.shape[1:], BF16)
        c_new = jnp.concatenate([cnew_ref[0], jnp.zeros((PAGE_SIZE - n_tok, MLA_KV_LORA), F32)],
                                axis=0).astype(BF16)
        key = lax.broadcasted_iota(jnp.int32, (rows, PAGE_SIZE), 1)
        tok = lax.broadcasted_iota(jnp.int32, (rows, PAGE_SIZE), 0) // MLA_HEADS
        absorb(jnp.where(key <= tok, scores(c_new, krnew_ref[0].astype(BF16)), NEG), c_new)

    for j in range(G):
        cb_scr[cur, j * PAGE_SIZE:(j + 1) * PAGE_SIZE, :] = cbuf[slot, j].astype(BF16)
        krT_scr[:, j * PAGE_SIZE:(j + 1) * PAGE_SIZE] = rbuf[slot, j].astype(BF16)

    ahead = step + (PAGE_RING - 1)
    for cp in page_copies(ahead % PAGE_RING, group_of(ahead)):
        cp.start()

    alpha, p = weights(s_scr[prv])
    s_scr[cur] = scores(cb_scr[cur], krT_scr[...])
    accumulate(alpha, p, cb_scr[prv])

    @pl.when(last)
    def _():
        for d in range(1, PAGE_RING):
            wait_slot((step + d) % PAGE_RING)

    @pl.when(g == n_g - 1)
    def _():
        absorb(s_scr[cur], cb_scr[cur])
        o_lat =(acc_scr[...] / l_scr[...]).astype(BF16)
        full = _dot(o_lat, wv_ref[...])
        full = full.reshape(n_tok, MLA_HEADS, MLA_VALL)
        hr = lax.broadcasted_iota(jnp.int32, (MLA_HEADS, MLA_VALL), 0)
        hc = lax.broadcasted_iota(jnp.int32, (MLA_HEADS, MLA_VALL), 1) // MLA_V
        o_ref[...] = jnp.sum(jnp.where((hr == hc)[None], full, 0.0), axis=1).astype(o_ref.dtype)


def _sample_attn(page_table, qabs, qrope, c_new, kr_new, w_kT, w_v, pool_c, pool_r, layer, *, pages_per_step):
    n_b, n_pages = page_table.shape
    n_tok = c_new.shape[1]
    rows = n_tok * MLA_HEADS
    G = pages_per_step
    n_k = MLA_HEADS * MLA_NOPE

    per_b = lambda b, g, pt: (b, 0, 0)
    grid_spec = pltpu.PrefetchScalarGridSpec(
        num_scalar_prefetch=1,
        grid=(n_b, n_pages // G),
        in_specs=[pl.BlockSpec((1, rows, MLA_KV_LORA), per_b), pl.BlockSpec((1, rows, MLA_ROPE), per_b),
                  pl.BlockSpec((1, n_tok, MLA_KV_LORA), per_b), pl.BlockSpec((1, MLA_ROPE, PAGE_SIZE), per_b),
                  pl.BlockSpec((n_k, MLA_KV_LORA), lambda b, g, pt: (0, 0)),
                  pl.BlockSpec((MLA_KV_LORA, MLA_VALL), lambda b, g, pt: (0, 0)),
                  pl.BlockSpec(memory_space=pl.ANY), pl.BlockSpec(memory_space=pl.ANY)],
        out_specs=pl.BlockSpec((n_tok, MLA_VALL), lambda b, g, pt: (b, 0)),
        scratch_shapes=[pltpu.VMEM((n_k + rows, MLA_KV_LORA), BF16), pltpu.VMEM((rows, 1), F32),
                        pltpu.VMEM((rows, 1), F32), pltpu.VMEM((rows, MLA_KV_LORA), F32),
                        pltpu.VMEM((2, G * PAGE_SIZE, MLA_KV_LORA), BF16), pltpu.VMEM((MLA_ROPE, G * PAGE_SIZE), BF16),
                        pltpu.VMEM((2, rows, G * PAGE_SIZE), F32),
                        pltpu.VMEM((PAGE_RING, G, PAGE_SIZE, MLA_KV_LORA), F32),
                        pltpu.VMEM((PAGE_RING, G, MLA_ROPE, PAGE_SIZE), F32),
                        pltpu.SemaphoreType.DMA((2, PAGE_RING))])
    return pl.pallas_call(
        functools.partial(_sample_attn_kernel, n_tok=n_tok, pages_per_step=G, layer=layer),
        grid_spec=grid_spec,
        out_shape=jax.ShapeDtypeStruct((n_b * n_tok, MLA_VALL), BF16),
        compiler_params=pltpu.CompilerParams(dimension_semantics=("arbitrary", "arbitrary"),
                                             vmem_limit_bytes=VMEM_LIMIT),
        name="sample_attn",
    )(page_table, qabs, qrope, c_new, kr_new, w_kT, w_v, pool_c, pool_r)


def _post_kernel(x_ref, og_ref, om_ref, p_ref, wo_ref, gffn_ref, wg_ref, wu_ref, cw_ref, cb_ref, wd_ref,
                 gple_ref, wpg_ref, wpp_ref, *rest, tiles_per_seq, seq_rows):
    paged_prev = seq_rows is not None
    if paged_prev:
        p1_ref, p2_ref, y_ref, tail_ref, acc_scr = rest
    else:
        y_ref, tail_ref, acc_scr, carry_scr = rest
    tm = x_ref.shape[0]
    i = pl.program_id(0)
    h1 = x_ref[...] + _dot(og_ref[...], wo_ref[:GLA_V, :]) + _dot(om_ref[...], wo_ref[GLA_V:, :])
    n2 = _rms(h1, gffn_ref[...]).astype(BF16)
    if not paged_prev:
        @pl.when(i % tiles_per_seq == 0)
        def _():
            carry_scr[...] = jnp.zeros(carry_scr.shape, F32)

    bounds = np.concatenate([[0], np.cumsum(FFN_CHUNKS)]).tolist()
    chunks = [slice(lo, hi) for lo, hi in zip(bounds[:-1], bounds[1:])]
    lo = bounds[-1]

    def gate_up(cols):
        return _dot(n2, wg_ref[:, cols]), _dot(n2, wu_ref[:, cols])

    nxt = gate_up(chunks[0])
    for ci, cols in enumerate(chunks):
        a, up = nxt
        if ci + 1 < len(chunks):
            nxt = gate_up(chunks[ci + 1])
        row =lax.broadcasted_iota(jnp.int32, a.shape, 0)
        slabs = [a[:, c0:c0 + 2 * LANES] for c0 in range(0, a.shape[1], 2 * LANES)]
        r1 = jnp.concatenate([pltpu.roll(sl, 1, axis=0) for sl in slabs], axis=1)
        r2 = jnp.concatenate([pltpu.roll(sl, 2, axis=0) for sl in slabs], axis=1)
        if paged_prev:
            t = row % seq_rows
            a1 = jnp.where(t >= 1, r1, p1_ref[:, cols])
            a2 = jnp.where(t >= 2, r2, p2_ref[:, cols])
            tail_ref[:, cols] = a
        else:
            prev = carry_scr[:, cols]
            pm1 = prev[SUBLANES - 1:SUBLANES, :]
            pm2 = prev[SUBLANES - 2:SUBLANES - 1, :]
            a1 = jnp.where(row >= 1, r1, pm1)
            a2 = jnp.where(row >= 2, r2, jnp.where(row == 0, pm2, pm1))
            carry_scr[:, cols] = a[tm - SUBLANES:, :]
            tail_ref[:, cols] = a[tm - SUBLANES:, :]
        conv = cb_ref[:, cols] + cw_ref[0:1, cols] * a2 + cw_ref[1:2, cols] * a1 + cw_ref[2:3, cols] * a
        gact = (conv / (1.0 + jnp.exp(-conv)) * up).astype(BF16)
        down = _dot(gact, wd_ref[cols, :])
        if ci == 0:
            acc_scr[...] = down
        else:
            acc_scr[...] += down
    assert lo == D_FF
    h2 = h1 + acc_scr[...]
    n3 =_rms(h2, gple_ref[...]).astype(BF16)
    gate = 1.0 / (1.0 + jnp.exp(-_dot(n3, wpg_ref[...])))
    y_ref[...] = h2 + _dot(p_ref[...].astype(BF16), wpp_ref[...]) * gate


def _post(x2d, og, om, p2d, w, prev=None, *, tm, tiles_per_seq, seq_rows):
    T = x2d.shape[0]
    nt = T // tm
    row = lambda i: (i, 0)
    consts = [w['w_o'], w['g_ffn'], w['w_gate'], w['w_up'], w['conv_w'], w['conv_b'], w['w_down'],
              w['g_ple'], w['w_pgate'], w['w_pproj']]
    in_specs = [pl.BlockSpec((tm, D_MODEL), row), pl.BlockSpec((tm, GLA_V), row),
                pl.BlockSpec((tm, MLA_VALL), row), pl.BlockSpec((tm, PLE_DIM), row)]
    in_specs += [_const_spec(c.shape) for c in consts]
    args = [x2d, og, om, p2d, *consts]
    scratch = [pltpu.VMEM((tm, D_MODEL), F32)]
    if prev is not None:
        in_specs += [pl.BlockSpec((tm, D_FF), row)] * 2
        args += list(prev)
        tail_rows = tm
    else:
        scratch.append(pltpu.VMEM((SUBLANES, D_FF), F32))
        tail_rows = SUBLANES
    return pl.pallas_call(
        functools.partial(_post_kernel, tiles_per_seq=tiles_per_seq, seq_rows=seq_rows),
        grid=(nt,),
        in_specs=in_specs,
        out_specs=[pl.BlockSpec((tm, D_MODEL), row), pl.BlockSpec((tail_rows, D_FF), row)],
        out_shape=[jax.ShapeDtypeStruct((T, D_MODEL), F32), jax.ShapeDtypeStruct((nt * tail_rows, D_FF), F32)],
        scratch_shapes=scratch,
        compiler_params=pltpu.CompilerParams(dimension_semantics=("arbitrary",), vmem_limit_bytes=VMEM_LIMIT),
        name="post_sample" if prev is not None else "post_prompt",
    )(*args)


def _prep_weights(g_mix, w_in, gla_w_a2, gla_b_a, gla_g_out, mla_g_qa, mla_w_qup, mla_g_qn, mla_g_qr,
                  mla_g_kva, mla_g_kr, mla_w_kvup, mla_g_kn, w_o, g_ffn, ffn_w_gate, ffn_w_up, ffn_conv_w,
                  ffn_conv_b, ffn_w_down, g_ple, ple_w_gate, ple_w_proj):
    sizes = (GLA_QK, GLA_QK, GLA_V, GLA_V, GLA_GATE_RANK, MLA_Q_LORA, MLA_KV_LORA, MLA_ROPE)
    offs = np.concatenate([[0], np.cumsum(sizes)])
    piece = lambda i: w_in[:, offs[i]:offs[i + 1]]
    zeros = lambda n: jnp.zeros((D_MODEL, n), w_in.dtype)
    misc = jnp.concatenate([piece(4), zeros(ROPE_LO - GLA_GATE_RANK), piece(7),
                            zeros(LANES - ROPE_LO - MLA_ROPE)], axis=1)
    w_in_p = jnp.concatenate([piece(0), piece(1), piece(2), piece(3), piece(5), piece(6), misc], axis=1)
    w_a2 = jnp.concatenate([gla_w_a2, jnp.zeros((LANES - GLA_GATE_RANK, GLA_QK), gla_w_a2.dtype)], axis=0)

    def slab_vec(nope, rope_):
        one = jnp.concatenate([nope, rope_, jnp.zeros((HEAD_SLAB - MLA_NOPE - MLA_ROPE,), F32)])
        return jnp.tile(one, MLA_HEADS)[None, :]

    wq = mla_w_qup.reshape(MLA_Q_LORA, MLA_HEADS, MLA_NOPE + MLA_ROPE)
    swap = np.arange(MLA_NOPE + MLA_ROPE)
    swap[MLA_NOPE:] = np.concatenate([swap[MLA_NOPE + ROPE_HALF:], swap[MLA_NOPE:MLA_NOPE + ROPE_HALF]])
    slab_pad = ((0, 0), (0, 0), (0, HEAD_SLAB - MLA_NOPE - MLA_ROPE))
    wq = jnp.concatenate([jnp.pad(wq, slab_pad).reshape(MLA_Q_LORA, MLA_SLABS),
                          jnp.pad(wq[:, :, swap], slab_pad).reshape(MLA_Q_LORA, MLA_SLABS)], axis=1)
    g_q = jnp.concatenate([mla_g_qn, mla_g_qr])
    gq_slab = jnp.pad(jnp.stack([g_q, g_q[swap]]), ((0, 0), (0, HEAD_SLAB - MLA_NOPE - MLA_ROPE)))
    wkv = mla_w_kvup.reshape(MLA_KV_LORA, MLA_HEADS, MLA_NOPE + MLA_V)
    wk = wkv[:, :, :MLA_NOPE]
    wv = wkv[:, :, MLA_NOPE:].reshape(MLA_KV_LORA, MLA_VALL)
    wk_slab = jnp.pad(wk, ((0, 0), (0, 0), (0, HEAD_SLAB - MLA_NOPE))).reshape(MLA_KV_LORA, MLA_SLABS)
    wk_g = jnp.pad(wk * mla_g_kn[None, None, :], ((0, 0), (0, 0), (0, HEAD_SLAB - MLA_NOPE)))
    eye = jnp.eye(MLA_HEADS, dtype=F32)
    w_kabs = jnp.einsum('nhd,hg->hdgn', wk_g, eye).reshape(MLA_SLABS, MLA_HEADS * MLA_KV_LORA)
    w_kT = wk.transpose(1, 2, 0).reshape(MLA_HEADS * MLA_NOPE, MLA_KV_LORA)

    lane = np.arange(2 * HEAD_SLAB)
    seg = np.where(lane % HEAD_SLAB < MLA_NOPE, 0, np.where(lane % HEAD_SLAB < MLA_NOPE + MLA_ROPE, 1, 2))
    same = (lane[:, None] // HEAD_SLAB == lane[None, :] // HEAD_SLAB) & (seg[:, None] == seg[None, :])
    segm = np.where(same & (seg[:, None] == 0), 1.0 / MLA_NOPE, np.where(same & (seg[:, None] == 1), 1.0 / MLA_ROPE, 0.0))

    gkr = jnp.concatenate([jnp.zeros((ROPE_LO,), F32), mla_g_kr, jnp.zeros((LANES - ROPE_LO - MLA_ROPE,), F32)])
    return dict(
        g_mix=g_mix[None, :], w_in=w_in_p.astype(BF16), w_a2=w_a2.astype(BF16), b_a=gla_b_a[None, :],
        g_out=gla_g_out[None, :], g_qa=mla_g_qa[None, :], w_qup=wq.astype(BF16), segm=jnp.asarray(segm, BF16),
        gq_slab=gq_slab, g_kva=mla_g_kva[None, :], w_kslab=wk_slab.astype(BF16),
        w_vT=wv.T.astype(BF16),
        gk_slab=slab_vec(mla_g_kn, jnp.zeros((MLA_ROPE,), F32)),
        g_kn=mla_g_kn, gkr_slab=gkr[None, :], w_kabs=w_kabs.astype(BF16), w_kT=w_kT.astype(BF16),
        w_v=wv.astype(BF16), w_o=w_o.astype(BF16), g_ffn=g_ffn[None, :], w_gate=ffn_w_gate.astype(BF16),
        w_up=ffn_w_up.astype(BF16), conv_w=ffn_conv_w, conv_b=ffn_conv_b[None, :],
        w_down=ffn_w_down.astype(BF16), g_ple=g_ple[None, :], w_pgate=ple_w_gate.astype(BF16),
        w_pproj=ple_w_proj.astype(BF16))


def _rope_tables(pos):
    inv = ROPE_THETA ** (-jnp.arange(ROPE_HALF, dtype=F32) * 2.0 / MLA_ROPE)
    ang = pos.astype(F32)[:, None] * inv[None, :]
    cos, sin = jnp.cos(ang), jnp.sin(ang)
    T = pos.shape[0]
    pad = jnp.zeros((T, HEAD_SLAB - MLA_NOPE - MLA_ROPE), F32)
    cos_t = jnp.concatenate([jnp.ones((T, MLA_NOPE), F32), cos, cos, pad], axis=1)
    sin_t = jnp.concatenate([jnp.zeros((T, MLA_NOPE), F32), -sin, sin, pad], axis=1)
    return cos_t, sin_t


def _pick_tile(n, cap):
    t = math.gcd(n, cap)
    return t


def _layer(w, x_p, x_s, p_p, p_s, pool_c, pool_r, layer, state_gla, state_conv, page_table):
    B, S, _ = x_p.shape
    Bd, Td, _ = x_s.shape
    tm = _pick_tile(S, 512)
    tps = S // tm
    cos_p, sin_p = _rope_tables(jnp.arange(S))
    xp2 = x_p.reshape(B * S, D_MODEL)
    gq, gk, gv, gg, la, qa, ckv_p, kr_p, ka, vaT = _in_proj(xp2, cos_p, sin_p, w, tm=tm, tiles_per_seq=tps,
                                                            sample=False)
    chunk_p = math.gcd(S, GLA_CHUNK)
    n_chunks = math.gcd(S // chunk_p, 8)
    og_p, gla_p = _gla(gq, gk, gv, gg, la, jnp.zeros((B, GLA_HEADS, GLA_DK, GLA_DV), F32), w['g_out'],
                      n_seq=B, seq_len=S, chunk=chunk_p, n_chunks=n_chunks,
                      seqs_per_step=math.gcd(B, GLA_PROMPT_SEQS))
    tq = _pick_tile(S, 512)
    om_p = _prompt_attn(qa, ka, vaT, n_seq=B, seq_len=S, tq=tq, tk=tq)
    y_p, tail_p = _post(xp2, og_p, om_p, p_p.reshape(B * S, PLE_DIM), w, tm=tm, tiles_per_seq=tps, seq_rows=None)
    conv_p = tail_p.reshape(B, tps, SUBLANES, D_FF)[:, -1, SUBLANES - (CONV_W - 1):, :]

    Ts = Bd * Td
    pos_s = PAST_LEN + jnp.arange(Td)
    cos_s, sin_s = _rope_tables(jnp.tile(pos_s, Bd))
    xs2 = x_s.reshape(Ts, D_MODEL)
    gq, gk, gv, gg, la, qa, ckv_s, kr_s, qabs = _in_proj(xs2, cos_s, sin_s, w, tm=Ts, tiles_per_seq=1, sample=True)
    chunk_s = math.gcd(Td, GLA_CHUNK)
    og_s, gla_s = _gla(gq, gk, gv, gg, la, state_gla, w['g_out'],
                      n_seq=Bd, seq_len=Td, chunk=chunk_s, n_chunks=Td // chunk_s,
                      seqs_per_step=math.gcd(Bd, GLA_SAMPLE_SEQS))
    rows = Td * MLA_HEADS
    qrope = qa.reshape(Ts, MLA_HEADS, HEAD_SLAB)[:, :, ROPE_LO:ROPE_LO + MLA_ROPE].reshape(Bd, rows, MLA_ROPE)
    krT_new = jnp.pad(jnp.swapaxes(kr_s.reshape(Bd, Td, MLA_ROPE), 1, 2), ((0, 0), (0, 0), (0, PAGE_SIZE - Td)))
    om_s = _sample_attn(page_table, qabs.reshape(Bd, rows, MLA_KV_LORA), qrope,
                        ckv_s.reshape(Bd, Td, MLA_KV_LORA), krT_new,
                        w['w_kT'], w['w_v'], pool_c, jnp.swapaxes(pool_r, 2, 3), layer,
                        pages_per_step=math.gcd(page_table.shape[1], SAMPLE_PAGES_PER_STEP))
    zpad = lambda a, lo: jnp.pad(a, ((0, 0), (lo, Td - lo - a.shape[1]), (0, 0))).reshape(Ts, D_FF)
    prev1 = zpad(state_conv[:, 1:2], 0)
    prev2 = zpad(state_conv, 0)
    y_s, a_s = _post(xs2, og_s, om_s, p_s.reshape(Ts, PLE_DIM), w, prev=(prev1, prev2), tm=Ts, tiles_per_seq=1,
                     seq_rows=Td)
    full = jnp.concatenate([state_conv, a_s.reshape(Bd, Td, D_FF)], axis=1)
    conv_s = full[:, full.shape[1] - (CONV_W - 1):]
    return (y_p.reshape(B, S, D_MODEL), y_s.reshape(Bd, Td, D_MODEL),
            ckv_p.reshape(B, S, MLA_KV_LORA), kr_p.reshape(B, S, MLA_ROPE),
            gla_p, conv_p,
            ckv_s.reshape(Bd, Td, MLA_KV_LORA), kr_s.reshape(Bd, Td, MLA_ROPE),
            gla_s, conv_s)


def kernel(x_prompt, x_sample, cache_ckv, cache_krope, state_gla, state_conv, page_table, p_prompt, p_sample, g_mix, w_in, gla_w_a2, gla_b_a, gla_g_out, mla_g_qa, mla_w_qup, mla_g_qn, mla_g_qr, mla_g_kva, mla_g_kr, mla_w_kvup, mla_g_kn, w_o, g_ffn, ffn_w_gate, ffn_w_up, ffn_conv_w, ffn_conv_b, ffn_w_down, g_ple, ple_w_gate, ple_w_proj):
    depth = w_in.shape[0]
    per_layer = (g_mix, w_in, gla_w_a2, gla_b_a, gla_g_out, mla_g_qa, mla_w_qup, mla_g_qn, mla_g_qr, mla_g_kva,
                 mla_g_kr, mla_w_kvup, mla_g_kn, w_o, g_ffn, ffn_w_gate, ffn_w_up, ffn_conv_w, ffn_conv_b,
                 ffn_w_down, g_ple, ple_w_gate, ple_w_proj)
    y_p, y_s = x_prompt, x_sample
    outs = [[] for _ in range(8)]
    for i in range(depth):
        w = _prep_weights(*(a[i] for a in per_layer))
        res = _layer(w, y_p, y_s, p_prompt[i], p_sample[i], cache_ckv, cache_krope, i, state_gla[i],
                     state_conv[i], page_table)
        y_p, y_s = res[0], res[1]
        for lst, r in zip(outs, res[2:]):
            lst.append(r)
    return (y_p, y_s) + tuple(jnp.stack(lst) for lst in outs)
```

```python
import functools
import math

import jax
import jax.numpy as jnp
import numpy as np
from jax import lax
from jax.experimental import pallas as pl
from jax.experimental.pallas import tpu as pltpu

D_MODEL = 1024
PAST_LEN = 16384
PAGE_SIZE = 128
GLA_HEADS = 4
GLA_DK = 64
GLA_DV = 128
GLA_GATE_RANK = 16
GLA_GATE_NORM = 16.0
GLA_CHUNK = 32
MLA_HEADS = 8
MLA_Q_LORA = 256
MLA_KV_LORA = 128
MLA_NOPE = 64
MLA_ROPE = 32
MLA_V = 64
MLA_SCALE = (MLA_NOPE + MLA_ROPE) ** -0.5
LOG2E = math.log2(math.e)
ROPE_THETA = 10000.0
D_FF = 2816
CONV_W = 3
PLE_DIM = 256
EPS = 1e-6
NEG = -1e30

LANES = 128
SUBLANES = 8
HEAD_SLAB = LANES
ROPE_LO = MLA_NOPE
ROPE_HALF = MLA_ROPE // 2
GLA_QK = GLA_HEADS * GLA_DK
GLA_V = GLA_HEADS * GLA_DV
MLA_SLABS = MLA_HEADS * HEAD_SLAB
MLA_VALL = MLA_HEADS * MLA_V
C_GQ, C_GK, C_GV, C_GG = 0, 256, 512, 1024
C_MQ, C_MKV, C_MISC = 1536, 1792, 1920
IN_COLS_P = 2048
FFN_CHUNKS = (1536, 1280)
SAMPLE_PAGES_PER_STEP = 32
ATTN_HEADS_PER_STEP = 8
GLA_PROMPT_SEQS = 4
GLA_SAMPLE_SEQS = 8
PAGE_RING = 3
VMEM_LIMIT = 56 * 1024 * 1024

BF16 = jnp.bfloat16
F32 = jnp.float32


def _dot(a, b):
    return jnp.dot(a, b, preferred_element_type=F32)


def _dot_nt(a, b):
    return lax.dot_general(a, b, (((1,), (1,)), ((), ())), preferred_element_type=F32)


def _dot_tn(a, b):
    return lax.dot_general(a, b, (((0,), (0,)), ((), ())), preferred_element_type=F32)


def _rms(x, g):
    return x * lax.rsqrt(jnp.mean(x * x, axis=-1, keepdims=True) + EPS) * g


def _const_spec(shape):
    nd = len(shape)
    return pl.BlockSpec(shape, lambda *_: (0,) * nd, pipeline_mode=pl.Buffered(1))


def _rope_slab(y, cos_t, sin_t):
    lane = lax.broadcasted_iota(jnp.int32, y.shape, 1)
    swapped = jnp.where(lane < ROPE_LO + ROPE_HALF,
                        pltpu.roll(y, LANES - ROPE_HALF, axis=1),
                        pltpu.roll(y, ROPE_HALF, axis=1))
    return y * cos_t + swapped * sin_t


def _in_proj_kernel(x_ref, cos_ref, sin_ref, gmix_ref, win_ref, wa2_ref, ba_ref, gqa_ref, wqup_ref,
                    segm_ref, gq_ref, gkva_ref, gkr_ref, *rest, sample):
    if sample:
        (wkabs_ref, gq_o, gk_o, gv_o, gg_o, la_o, qa_o, ckv_o, kr_o, qabs_o) = rest
    else:
        (wk_ref, gk_ref, wvT_ref, gq_o, gk_o, gv_o, gg_o, la_o, qa_o, ckv_o, kr_o, ka_o, vaT_o) = rest
    segm = segm_ref[...]
    q_gain = gq_ref[...]

    cos_t = cos_ref[...]
    sin_t = sin_ref[...]
    n = _rms(x_ref[...], gmix_ref[...]).astype(BF16)

    def proj(lo, width):
        return _dot(n, win_ref[:, lo:lo + width])

    um = proj(C_MQ, IN_COLS_P - C_MQ)
    ug = proj(C_GQ, 2 * GLA_QK)
    gq_o[...] = ug[:, :GLA_QK] * (GLA_DK ** -0.5)
    gk_o[...] = ug[:, GLA_QK:]
    misc = um[:, C_MISC - C_MQ:]
    z = _dot(misc.astype(BF16), wa2_ref[...]) + ba_ref[...]
    la_o[...] = (jnp.minimum(z, 0.0) - jnp.log1p(jnp.exp(-jnp.abs(z)))) * (1.0 / GLA_GATE_NORM)
    lane = lax.broadcasted_iota(jnp.int32, misc.shape, 1)
    is_rope = (lane >= ROPE_LO) & (lane < ROPE_LO + MLA_ROPE)
    ms_r = jnp.sum(jnp.where(is_rope, misc * misc, 0.0), axis=-1, keepdims=True) * (1.0 / MLA_ROPE)
    kr_slab = _rope_slab(misc * lax.rsqrt(ms_r + EPS) * gkr_ref[...], cos_t, sin_t)
    kr_o[...] = kr_slab[:, ROPE_LO:ROPE_LO + MLA_ROPE]
    ckv = _rms(um[:, C_MKV - C_MQ:C_MISC - C_MQ], gkva_ref[...])
    ckv_o[...] = ckv
    ckv_b = ckv.astype(BF16)
    cq = _rms(um[:, :MLA_Q_LORA], gqa_ref[...]).astype(BF16)

    if not sample:
        kn = _dot(ckv_b, wk_ref[...])
        vaT_o[...] = _dot_nt(wvT_ref[...], ckv_b).astype(BF16)
    qf = _dot(cq, wqup_ref[...])
    q_cos = cos_t * (q_gain[0:1] * (MLA_SCALE * LOG2E))
    q_sin = sin_t * (q_gain[1:2] * (MLA_SCALE * LOG2E))

    def head_pair(pair):
        lo = pair * 2 * HEAD_SLAB
        q2 = qf[:, lo:lo + 2 * HEAD_SLAB]
        q_inv = lax.rsqrt(_dot((q2 * q2).astype(BF16), segm) + EPS)
        if not sample:
            k2 = kn[:, lo:lo + 2 * HEAD_SLAB]
            k2 = k2 * lax.rsqrt(_dot((k2 * k2).astype(BF16), segm) + EPS) * gk_ref[:, lo:lo + 2 * HEAD_SLAB]
        for j in range(2):
            sl = slice(lo + j * HEAD_SLAB, lo + (j + 1) * HEAD_SLAB)
            swapped = qf[:, MLA_SLABS + lo + j * HEAD_SLAB:MLA_SLABS + lo + (j + 1) * HEAD_SLAB]
            qh = q_inv[:, j * HEAD_SLAB:(j + 1) * HEAD_SLAB] * (qf[:, sl] * q_cos + swapped * q_sin)
            qa_o[:, sl] = qh.astype(BF16)
            if not sample:
                ka_o[:, sl] = (k2[:, j * HEAD_SLAB:(j + 1) * HEAD_SLAB] + kr_slab).astype(BF16)

    half = GLA_V // 2
    gv_o[:, :half] = proj(C_GV, half).astype(BF16)
    head_pair(0)
    gv_o[:, half:] = proj(C_GV + half, half).astype(BF16)
    head_pair(1)
    gg_o[:, :half] = proj(C_GG, half)
    head_pair(2)
    gg_o[:, half:] = proj(C_GG + half, half)
    head_pair(3)
    if sample:
        qabs_o[...] = _dot(qa_o[...], wkabs_ref[...]).astype(BF16)


def _in_proj(x2d, cos_t, sin_t, w, *, tm, tiles_per_seq, sample):
    T = x2d.shape[0]
    nt = T // tm
    row = lambda i: (i, 0)
    pos = lambda i: (i % tiles_per_seq, 0)
    consts = [w['g_mix'], w['w_in'], w['w_a2'], w['b_a'], w['g_qa'], w['w_qup'], w['segm'], w['gq_slab'],
              w['g_kva'], w['gkr_slab']]
    consts += [w['w_kabs']] if sample else [w['w_kslab'], w['gk_slab'], w['w_vT']]
    in_specs = [pl.BlockSpec((tm, D_MODEL), row), pl.BlockSpec((tm, LANES), pos), pl.BlockSpec((tm, LANES), pos)]
    in_specs += [_const_spec(c.shape) for c in consts]
    outs = [(GLA_QK, F32), (GLA_QK, F32), (GLA_V, BF16), (GLA_V, F32), (GLA_QK, F32),
            (MLA_SLABS, BF16), (MLA_KV_LORA, F32), (MLA_ROPE, F32), (MLA_SLABS, BF16)]
    out_specs = [pl.BlockSpec((tm, c), row) for c, _ in outs]
    out_shape = [jax.ShapeDtypeStruct((T, c), d) for c, d in outs]
    if not sample:
        out_specs.append(pl.BlockSpec((MLA_VALL, tm), lambda i: (0, i)))
        out_shape.append(jax.ShapeDtypeStruct((MLA_VALL, T), BF16))
    return pl.pallas_call(
        functools.partial(_in_proj_kernel, sample=sample),
        grid=(nt,),
        in_specs=in_specs,
        out_specs=out_specs,
        out_shape=out_shape,
        compiler_params=pltpu.CompilerParams(dimension_semantics=("parallel",), vmem_limit_bytes=VMEM_LIMIT),
        name="in_proj_sample" if sample else "in_proj_prompt",
    )(x2d, cos_t, sin_t, *consts)


def _gla_kernel(q_ref, k_ref, v_ref, gg_ref, la_ref, s0_ref, cum_ref, sel_ref, gout_ref,
                o_ref, s_out_ref, st_scr, *, chunk, n_chunks):
    step = pl.program_id(1)
    n_seq = q_ref.shape[0]
    tc = chunk * n_chunks
    heads = range(GLA_HEADS)
    dk = [slice(h * GLA_DK, (h + 1) * GLA_DK) for h in heads]
    dv = [slice(h * GLA_DV, (h + 1) * GLA_DV) for h in heads]
    rows = [slice(ci * chunk, (ci + 1) * chunk) for ci in range(n_chunks)]

    @pl.when(step == 0)
    def _():
        st_scr[...] = s0_ref[...]

    cum = cum_ref[...]
    sel = sel_ref[...]
    r = lax.broadcasted_iota(jnp.int32, (tc, tc), 0)
    c = lax.broadcasted_iota(jnp.int32, (tc, tc), 1)
    causal = (r >= c) & ((r // chunk) == (c // chunk))
    gout = gout_ref[...]

    def decays(i, t):
        la = la_ref[i]
        la_hi = la.astype(BF16)
        la_lo = (la - la_hi.astype(F32)).astype(BF16)
        t['bb'] = _dot(cum, la_hi) + _dot(cum, la_lo)
        t['dec'] = jnp.exp(_dot_tn(la_hi, sel) + _dot_tn(la_lo, sel))

    def scale(i, t):
        b, bl = t['bb'][:tc], t['bb'][tc:]
        q, k = q_ref[i], k_ref[i]
        t['q_dec'] = (q * jnp.exp(b)).astype(BF16)
        t['k_inv'] = (k * jnp.exp(-b)).astype(BF16)
        t['k_end'] = (k * jnp.exp(bl - b)).astype(BF16)

    def intra_scores(i, t):
        t['a'] = [_dot_nt(t['q_dec'][:, dk[h]], t['k_inv'][:, dk[h]]) for h in heads]

    def chunk_outer(i, t):
        v = v_ref[i]
        t['x'] = [[_dot_tn(t['k_end'][rw, dk[h]], v[rw, dv[h]]) for rw in rows] for h in heads]

    def intra_out(i, t):
        v = v_ref[i]
        t['o'] = [_dot(jnp.where(causal, t['a'][h], 0.0).astype(BF16), v[:, dv[h]]) for h in heads]

    def scan(i, t):
        t['s'] = []
        for h in heads:
            s_h = st_scr[i, h]
            starts = []
            for ci in range(n_chunks):
                starts.append(s_h.astype(BF16))
                s_h = s_h * t['dec'][dk[h], ci:ci + 1] + t['x'][h][ci]
            st_scr[i, h] = s_h
            t['s'].append(starts)

    def inter_out(i, t):
        for h in heads:
            parts = [_dot(t['q_dec'][rw, dk[h]], t['s'][h][ci]) for ci, rw in enumerate(rows)]
            t['o'][h] = t['o'][h] + (jnp.concatenate(parts, axis=0) if n_chunks > 1 else parts[0])

    def finish(i, t):
        gg = gg_ref[i]
        gate = gg / (1.0 + jnp.exp(-gg))
        for h in heads:
            o_ref[i, :, dv[h]] = (_rms(t['o'][h], gout) * gate[:, dv[h]]).astype(o_ref.dtype)

    temps = [{} for _ in range(n_seq)]
    for stage in (decays, scale, intra_scores, chunk_outer, intra_out, scan, inter_out, finish):
        for i in range(n_seq):
            stage(i, temps[i])

    @pl.when(step == pl.num_programs(1) - 1)
    def _():
        s_out_ref[...] = st_scr[...]


def _gla(q, k, v, gg, la, s0, g_out, *, n_seq, seq_len, chunk, n_chunks, seqs_per_step):
    tc = chunk * n_chunks
    steps = seq_len // tc
    nb = seqs_per_step
    idx = np.arange(tc)
    same = (idx[:, None] // chunk) == (idx[None, :] // chunk)
    cum = jnp.asarray(np.concatenate([same & (idx[:, None] >= idx[None, :]), same], axis=0), BF16)
    sel = jnp.asarray((idx[:, None] // chunk) == np.arange(LANES)[None, :], BF16)
    tok = lambda width: pl.BlockSpec((nb, tc, width), lambda b, s: (b, s, 0))
    per_seq = lambda a: a.reshape(n_seq, seq_len, a.shape[-1])
    state_spec = pl.BlockSpec((nb, GLA_HEADS, GLA_DK, GLA_DV), lambda b, s: (b, 0, 0, 0))
    o, state = pl.pallas_call(
        functools.partial(_gla_kernel, chunk=chunk, n_chunks=n_chunks),
        grid=(n_seq // nb, steps),
        in_specs=[tok(GLA_QK), tok(GLA_QK), tok(GLA_V), tok(GLA_V), tok(GLA_QK), state_spec,
                  _const_spec((2 * tc, tc)), _const_spec((tc, LANES)), _const_spec((1, GLA_DV))],
        out_specs=[tok(GLA_V), state_spec],
        out_shape=[jax.ShapeDtypeStruct((n_seq, seq_len, GLA_V), BF16),
                   jax.ShapeDtypeStruct((n_seq, GLA_HEADS, GLA_DK, GLA_DV), F32)],
        scratch_shapes=[pltpu.VMEM((nb, GLA_HEADS, GLA_DK, GLA_DV), F32)],
        compiler_params=pltpu.CompilerParams(dimension_semantics=("parallel", "arbitrary"),
                                             vmem_limit_bytes=VMEM_LIMIT),
        name=f"gla_c{chunk}",
    )(per_seq(q), per_seq(k), per_seq(v), per_seq(gg), per_seq(la), s0, cum, sel, g_out)
    return o.reshape(n_seq * seq_len, GLA_V), state


def _prompt_attn_kernel(q_ref, k_ref, vT_ref, o_ref, s_scr, *, tq, tk):
    assert tq == tk
    qi = pl.program_id(2)
    key = lax.broadcasted_iota(jnp.int32, (tk, tq), 0)
    qry = lax.broadcasted_iota(jnp.int32, (tk, tq), 1)

    def scores(j, kb, masked):
        ks = pl.multiple_of(kb * tk, tk)
        s = _dot_nt(k_ref[pl.ds(ks, tk), j * HEAD_SLAB:(j + 1) * HEAD_SLAB],
                    q_ref[:, j * HEAD_SLAB:(j + 1) * HEAD_SLAB])
        if masked:
            s = jnp.where(key <= qry, s, NEG)
        s_scr[j] = s
        return jnp.max(s, axis=0, keepdims=True)

    def update(j, kb, m_blk, state):
        m, l, acc = state
        ks = pl.multiple_of(kb * tk, tk)
        m_new = jnp.maximum(m, m_blk)
        alpha = jnp.exp2(m - m_new)
        p = jnp.exp2(s_scr[j] - m_new)
        l = alpha * l + jnp.sum(p, axis=0, keepdims=True)
        acc = alpha * acc + _dot(vT_ref[j * MLA_V:(j + 1) * MLA_V, pl.ds(ks, tk)], p.astype(BF16))
        return m_new, l, acc

    n_heads = q_ref.shape[1] // HEAD_SLAB

    def body(t, carry):
        m_blk, state = list(carry[0]), list(carry[1])
        kb = qi - 1 - t
        for j in range(n_heads):
            new_blk = scores(j, kb, False)
            nxt = (j + 1) % n_heads
            state[nxt] = update(nxt, kb if nxt == 0 else kb + 1, new_blk if n_heads == 1 else m_blk[nxt],
                                state[nxt])
            m_blk[j] = new_blk
        return tuple(m_blk), tuple(state)

    init = (jnp.full((1, tq), NEG, F32), jnp.zeros((1, tq), F32), jnp.zeros((MLA_V, tq), F32))
    m_blk = [scores(j, qi, True) for j in range(n_heads)]
    state = [update(0, qi, m_blk[0], init)] + [init] * (n_heads - 1)
    m_blk, state = lax.fori_loop(0, qi, body, (tuple(m_blk), tuple(state)))
    state = [state[0]] + [update(j, 0, m_blk[j], state[j]) for j in range(1, n_heads)]
    o_ref[...] = jnp.concatenate([acc / l for _, l, acc in state], axis=0).T.astype(o_ref.dtype)


def _prompt_attn(qa, ka, va, *, n_seq, seq_len, tq, tk):
    nq = seq_len // tq
    hs = ATTN_HEADS_PER_STEP
    return pl.pallas_call(
        functools.partial(_prompt_attn_kernel, tq=tq, tk=tk),
        grid=(n_seq, MLA_HEADS // hs, nq),
        in_specs=[pl.BlockSpec((tq, hs * HEAD_SLAB), lambda b, hp, i: (b * nq + i, hp)),
                  pl.BlockSpec((seq_len, hs * HEAD_SLAB), lambda b, hp, i: (b, hp)),
                  pl.BlockSpec((hs * MLA_V, seq_len), lambda b, hp, i: (hp, b))],
        out_specs=pl.BlockSpec((tq, hs * MLA_V), lambda b, hp, i: (b * nq + i, hp)),
        out_shape=jax.ShapeDtypeStruct((n_seq * seq_len, MLA_VALL), BF16),
        scratch_shapes=[pltpu.VMEM((hs, tk, tq), F32)],
        compiler_params=pltpu.CompilerParams(dimension_semantics=("parallel", "parallel", "arbitrary"),
                                             vmem_limit_bytes=VMEM_LIMIT),
        name="prompt_attn",
    )(qa, ka, va)


def _sample_attn_kernel(pt_ref, qabs_ref, qrope_ref, cnew_ref, krnew_ref, wkT_ref, wv_ref, poolc_hbm, poolr_hbm,
                        o_ref, lhs_scr, m_scr, l_scr, acc_scr, cb_scr, krT_scr, s_scr, cbuf, rbuf, sem,
                        *, n_tok, pages_per_step, layer):
    G = pages_per_step
    b = pl.program_id(0)
    g = pl.program_id(1)
    n_g = pl.num_programs(1)
    step = b * n_g + g
    n_steps = pl.num_programs(0) * n_g
    last = step == n_steps - 1
    slot = step % PAGE_RING

    def group_of(s):
        s = jnp.minimum(s, n_steps - 1)
        return s // n_g, s % n_g
    rows = n_tok * MLA_HEADS
    n_k = MLA_HEADS * MLA_NOPE

    def page_copies(sl, group):
        copies = []
        for j in range(G):
            page = pt_ref[group[0], group[1] * G + j]
            copies.append(pltpu.make_async_copy(poolc_hbm.at[layer, page], cbuf.at[sl, j], sem.at[0, sl]))
            copies.append(pltpu.make_async_copy(poolr_hbm.at[layer, page], rbuf.at[sl, j], sem.at[1, sl]))
        return copies

    def start_pages(sl, group):
        for idx, cp in enumerate(page_copies(sl, group)):
            cp.start(priority=(idx // 2) % 2)

    def wait_slot(sl):
        pltpu.make_async_copy(cbuf.at[sl], cbuf.at[sl], sem.at[0, sl]).wait()
        pltpu.make_async_copy(rbuf.at[sl], rbuf.at[sl], sem.at[1, sl]).wait()

    @pl.when(step == 0)
    def _():
        for d in range(PAGE_RING - 1):
            start_pages(d, group_of(d))

    wait_slot(slot)

    def scores(cb, krT):
        nk = cb.shape[0]
        big = _dot_nt(lhs_scr[...], cb)
        kvn = big[:n_k]
        s_nope = big[n_k:n_k + rows]
        ss = jnp.sum((kvn * kvn).reshape(MLA_HEADS, MLA_NOPE, nk), axis=1)
        inv = lax.rsqrt(ss * (1.0 / MLA_NOPE) + EPS)
        s_rope = _dot(qrope_ref[0], krT)
        return (s_nope.reshape(n_tok, MLA_HEADS, nk) * inv[None]).reshape(rows, nk) + s_rope

    def weights(s):
        m = m_scr[...]
        m_new = jnp.maximum(m, jnp.max(s, axis=-1, keepdims=True))
        alpha = jnp.exp2(m - m_new)
        p = jnp.exp2(s - m_new)
        l_scr[...] = alpha * l_scr[...] + jnp.sum(p, axis=-1, keepdims=True)
        m_scr[...] = m_new
        return alpha, p.astype(BF16)

    def accumulate(alpha, p, cb):
        acc_scr[...] = alpha * acc_scr[...] + _dot(p, cb)

    def absorb(s, cb):
        accumulate(*weights(s), cb)

    cur = g % 2
    prv = 1 - cur

    @pl.when(g == 0)
    def _():
        lhs_scr[:n_k, :] = wkT_ref[...]
        lhs_scr[n_k:n_k + rows, :] = qabs_ref[0]
        m_scr[...] = jnp.full(m_scr.shape, NEG, F32)
        l_scr[...] = jnp.zeros(l_scr.shape, F32)
        acc_scr[...] = jnp.zeros(acc_scr.shape, F32)
        s_scr[prv] = jnp.full(s_scr.shape[1:], NEG, F32)
        cb_scr[prv] = jnp.zeros(cb_scr.shape[1:], BF16)
        c_new = jnp.concatenate([cnew_ref[0], jnp.zeros((PAGE_SIZE - n_tok, MLA_KV_LORA), F32)],
                                axis=0).astype(BF16)
        key = lax.broadcasted_iota(jnp.int32, (rows, PAGE_SIZE), 1)
        tok = lax.broadcasted_iota(jnp.int32, (rows, PAGE_SIZE), 0) // MLA_HEADS
        absorb(jnp.where(key <= tok, scores(c_new, krnew_ref[0].astype(BF16)), NEG), c_new)

    for j in range(G):
        cb_scr[cur, j * PAGE_SIZE:(j + 1) * PAGE_SIZE, :] = cbuf[slot, j].astype(BF16)
        krT_scr[:, j * PAGE_SIZE:(j + 1) * PAGE_SIZE] = rbuf[slot, j].astype(BF16)

    ahead = step + (PAGE_RING - 1)
    start_pages(ahead % PAGE_RING, group_of(ahead))

    alpha, p = weights(s_scr[prv])
    s_scr[cur] = scores(cb_scr[cur], krT_scr[...])
    accumulate(alpha, p, cb_scr[prv])

    @pl.when(last)
    def _():
        for d in range(1, PAGE_RING):
            wait_slot((step + d) % PAGE_RING)

    @pl.when(g == n_g - 1)
    def _():
        absorb(s_scr[cur], cb_scr[cur])
        o_lat =(acc_scr[...] / l_scr[...]).astype(BF16)
        full = _dot(o_lat, wv_ref[...])
        full = full.reshape(n_tok, MLA_HEADS, MLA_VALL)
        hr = lax.broadcasted_iota(jnp.int32, (MLA_HEADS, MLA_VALL), 0)
        hc = lax.broadcasted_iota(jnp.int32, (MLA_HEADS, MLA_VALL), 1) // MLA_V
        o_ref[...] = jnp.sum(jnp.where((hr == hc)[None], full, 0.0), axis=1).astype(o_ref.dtype)


def _sample_attn(page_table, qabs, qrope, c_new, kr_new, w_kT, w_v, pool_c, pool_r, layer, *, pages_per_step):
    n_b, n_pages = page_table.shape
    n_tok = c_new.shape[1]
    rows = n_tok * MLA_HEADS
    G = pages_per_step
    n_k = MLA_HEADS * MLA_NOPE

    per_b = lambda b, g, pt: (b, 0, 0)
    grid_spec = pltpu.PrefetchScalarGridSpec(
        num_scalar_prefetch=1,
        grid=(n_b, n_pages // G),
        in_specs=[pl.BlockSpec((1, rows, MLA_KV_LORA), per_b), pl.BlockSpec((1, rows, MLA_ROPE), per_b),
                  pl.BlockSpec((1, n_tok, MLA_KV_LORA), per_b), pl.BlockSpec((1, MLA_ROPE, PAGE_SIZE), per_b),
                  pl.BlockSpec((n_k, MLA_KV_LORA), lambda b, g, pt: (0, 0)),
                  pl.BlockSpec((MLA_KV_LORA, MLA_VALL), lambda b, g, pt: (0, 0)),
                  pl.BlockSpec(memory_space=pl.ANY), pl.BlockSpec(memory_space=pl.ANY)],
        out_specs=pl.BlockSpec((n_tok, MLA_VALL), lambda b, g, pt: (b, 0)),
        scratch_shapes=[pltpu.VMEM((n_k + rows, MLA_KV_LORA), BF16), pltpu.VMEM((rows, 1), F32),
                        pltpu.VMEM((rows, 1), F32), pltpu.VMEM((rows, MLA_KV_LORA), F32),
                        pltpu.VMEM((2, G * PAGE_SIZE, MLA_KV_LORA), BF16), pltpu.VMEM((MLA_ROPE, G * PAGE_SIZE), BF16),
                        pltpu.VMEM((2, rows, G * PAGE_SIZE), F32),
                        pltpu.VMEM((PAGE_RING, G, PAGE_SIZE, MLA_KV_LORA), F32),
                        pltpu.VMEM((PAGE_RING, G, MLA_ROPE, PAGE_SIZE), F32),
                        pltpu.SemaphoreType.DMA((2, PAGE_RING))])
    return pl.pallas_call(
        functools.partial(_sample_attn_kernel, n_tok=n_tok, pages_per_step=G, layer=layer),
        grid_spec=grid_spec,
        out_shape=jax.ShapeDtypeStruct((n_b * n_tok, MLA_VALL), BF16),
        compiler_params=pltpu.CompilerParams(dimension_semantics=("arbitrary", "arbitrary"),
                                             vmem_limit_bytes=VMEM_LIMIT),
        name="sample_attn",
    )(page_table, qabs, qrope, c_new, kr_new, w_kT, w_v, pool_c, pool_r)


def _post_kernel(x_ref, og_ref, om_ref, p_ref, wo_ref, gffn_ref, wg_ref, wu_ref, cw_ref, cb_ref, wd_ref,
                 gple_ref, wpg_ref, wpp_ref, *rest, tiles_per_seq, seq_rows):
    paged_prev = seq_rows is not None
    if paged_prev:
        p1_ref, p2_ref, y_ref, tail_ref, acc_scr = rest
    else:
        y_ref, tail_ref, acc_scr, carry_scr = rest
    tm = x_ref.shape[0]
    i = pl.program_id(0)
    h1 = x_ref[...] + _dot(og_ref[...], wo_ref[:GLA_V, :]) + _dot(om_ref[...], wo_ref[GLA_V:, :])
    n2 = _rms(h1, gffn_ref[...]).astype(BF16)
    if not paged_prev:
        @pl.when(i % tiles_per_seq == 0)
        def _():
            carry_scr[...] = jnp.zeros(carry_scr.shape, F32)

    bounds = np.concatenate([[0], np.cumsum(FFN_CHUNKS)]).tolist()
    chunks = [slice(lo, hi) for lo, hi in zip(bounds[:-1], bounds[1:])]
    lo = bounds[-1]

    def gate_up(cols):
        return _dot(n2, wg_ref[:, cols]), _dot(n2, wu_ref[:, cols])

    nxt = gate_up(chunks[0])
    for ci, cols in enumerate(chunks):
        a, up = nxt
        if ci + 1 < len(chunks):
            nxt = gate_up(chunks[ci + 1])
        row =lax.broadcasted_iota(jnp.int32, a.shape, 0)
        slabs = [a[:, c0:c0 + 2 * LANES] for c0 in range(0, a.shape[1], 2 * LANES)]
        r1 = jnp.concatenate([pltpu.roll(sl, 1, axis=0) for sl in slabs], axis=1)
        r2 = jnp.concatenate([pltpu.roll(sl, 2, axis=0) for sl in slabs], axis=1)
        if paged_prev:
            t = row % seq_rows
            a1 = jnp.where(t >= 1, r1, p1_ref[:, cols])
            a2 = jnp.where(t >= 2, r2, p2_ref[:, cols])
            tail_ref[:, cols] = a
        else:
            prev = carry_scr[:, cols]
            pm1 = prev[SUBLANES - 1:SUBLANES, :]
            pm2 = prev[SUBLANES - 2:SUBLANES - 1, :]
            a1 = jnp.where(row >= 1, r1, pm1)
            a2 = jnp.where(row >= 2, r2, jnp.where(row == 0, pm2, pm1))
            carry_scr[:, cols] = a[tm - SUBLANES:, :]
            tail_ref[:, cols] = a[tm - SUBLANES:, :]
        conv = cb_ref[:, cols] + cw_ref[0:1, cols] * a2 + cw_ref[1:2, cols] * a1 + cw_ref[2:3, cols] * a
        gact = (conv / (1.0 + jnp.exp(-conv)) * up).astype(BF16)
        down = _dot(gact, wd_ref[cols, :])
        if ci == 0:
            acc_scr[...] = down
        else:
            acc_scr[...] += down
    assert lo == D_FF
    h2 = h1 + acc_scr[...]
    n3 =_rms(h2, gple_ref[...]).astype(BF16)
    gate = 1.0 / (1.0 + jnp.exp(-_dot(n3, wpg_ref[...])))
    y_ref[...] = h2 + _dot(p_ref[...].astype(BF16), wpp_ref[...]) * gate


def _post(x2d, og, om, p2d, w, prev=None, *, tm, tiles_per_seq, seq_rows):
    T = x2d.shape[0]
    nt = T // tm
    row = lambda i: (i, 0)
    consts = [w['w_o'], w['g_ffn'], w['w_gate'], w['w_up'], w['conv_w'], w['conv_b'], w['w_down'],
              w['g_ple'], w['w_pgate'], w['w_pproj']]
    in_specs = [pl.BlockSpec((tm, D_MODEL), row), pl.BlockSpec((tm, GLA_V), row),
                pl.BlockSpec((tm, MLA_VALL), row), pl.BlockSpec((tm, PLE_DIM), row)]
    in_specs += [_const_spec(c.shape) for c in consts]
    args = [x2d, og, om, p2d, *consts]
    scratch = [pltpu.VMEM((tm, D_MODEL), F32)]
    if prev is not None:
        in_specs += [pl.BlockSpec((tm, D_FF), row)] * 2
        args += list(prev)
        tail_rows = tm
    else:
        scratch.append(pltpu.VMEM((SUBLANES, D_FF), F32))
        tail_rows = SUBLANES
    return pl.pallas_call(
        functools.partial(_post_kernel, tiles_per_seq=tiles_per_seq, seq_rows=seq_rows),
        grid=(nt,),
        in_specs=in_specs,
        out_specs=[pl.BlockSpec((tm, D_MODEL), row), pl.BlockSpec((tail_rows, D_FF), row)],
        out_shape=[jax.ShapeDtypeStruct((T, D_MODEL), F32), jax.ShapeDtypeStruct((nt * tail_rows, D_FF), F32)],
        scratch_shapes=scratch,
        compiler_params=pltpu.CompilerParams(dimension_semantics=("arbitrary",), vmem_limit_bytes=VMEM_LIMIT),
        name="post_sample" if prev is not None else "post_prompt",
    )(*args)


def _prep_weights(g_mix, w_in, gla_w_a2, gla_b_a, gla_g_out, mla_g_qa, mla_w_qup, mla_g_qn, mla_g_qr,
                  mla_g_kva, mla_g_kr, mla_w_kvup, mla_g_kn, w_o, g_ffn, ffn_w_gate, ffn_w_up, ffn_conv_w,
                  ffn_conv_b, ffn_w_down, g_ple, ple_w_gate, ple_w_proj):
    sizes = (GLA_QK, GLA_QK, GLA_V, GLA_V, GLA_GATE_RANK, MLA_Q_LORA, MLA_KV_LORA, MLA_ROPE)
    offs = np.concatenate([[0], np.cumsum(sizes)])
    piece = lambda i: w_in[:, offs[i]:offs[i + 1]]
    zeros = lambda n: jnp.zeros((D_MODEL, n), w_in.dtype)
    misc = jnp.concatenate([piece(4), zeros(ROPE_LO - GLA_GATE_RANK), piece(7),
                            zeros(LANES - ROPE_LO - MLA_ROPE)], axis=1)
    w_in_p = jnp.concatenate([piece(0), piece(1), piece(2), piece(3), piece(5), piece(6), misc], axis=1)
    w_a2 = jnp.concatenate([gla_w_a2, jnp.zeros((LANES - GLA_GATE_RANK, GLA_QK), gla_w_a2.dtype)], axis=0)

    def slab_vec(nope, rope_):
        one = jnp.concatenate([nope, rope_, jnp.zeros((HEAD_SLAB - MLA_NOPE - MLA_ROPE,), F32)])
        return jnp.tile(one, MLA_HEADS)[None, :]

    wq = mla_w_qup.reshape(MLA_Q_LORA, MLA_HEADS, MLA_NOPE + MLA_ROPE)
    swap = np.arange(MLA_NOPE + MLA_ROPE)
    swap[MLA_NOPE:] = np.concatenate([swap[MLA_NOPE + ROPE_HALF:], swap[MLA_NOPE:MLA_NOPE + ROPE_HALF]])
    slab_pad = ((0, 0), (0, 0), (0, HEAD_SLAB - MLA_NOPE - MLA_ROPE))
    wq = jnp.concatenate([jnp.pad(wq, slab_pad).reshape(MLA_Q_LORA, MLA_SLABS),
                          jnp.pad(wq[:, :, swap], slab_pad).reshape(MLA_Q_LORA, MLA_SLABS)], axis=1)
    g_q = jnp.concatenate([mla_g_qn, mla_g_qr])
    gq_slab = jnp.pad(jnp.stack([g_q, g_q[swap]]), ((0, 0), (0, HEAD_SLAB - MLA_NOPE - MLA_ROPE)))
    wkv = mla_w_kvup.reshape(MLA_KV_LORA, MLA_HEADS, MLA_NOPE + MLA_V)
    wk = wkv[:, :, :MLA_NOPE]
    wv = wkv[:, :, MLA_NOPE:].reshape(MLA_KV_LORA, MLA_VALL)
    wk_slab = jnp.pad(wk, ((0, 0), (0, 0), (0, HEAD_SLAB - MLA_NOPE))).reshape(MLA_KV_LORA, MLA_SLABS)
    wk_g = jnp.pad(wk * mla_g_kn[None, None, :], ((0, 0), (0, 0), (0, HEAD_SLAB - MLA_NOPE)))
    eye = jnp.eye(MLA_HEADS, dtype=F32)
    w_kabs = jnp.einsum('nhd,hg->hdgn', wk_g, eye).reshape(MLA_SLABS, MLA_HEADS * MLA_KV_LORA)
    w_kT = wk.transpose(1, 2, 0).reshape(MLA_HEADS * MLA_NOPE, MLA_KV_LORA)

    lane = np.arange(2 * HEAD_SLAB)
    seg = np.where(lane % HEAD_SLAB < MLA_NOPE, 0, np.where(lane % HEAD_SLAB < MLA_NOPE + MLA_ROPE, 1, 2))
    same = (lane[:, None] // HEAD_SLAB == lane[None, :] // HEAD_SLAB) & (seg[:, None] == seg[None, :])
    segm = np.where(same & (seg[:, None] == 0), 1.0 / MLA_NOPE, np.where(same & (seg[:, None] == 1), 1.0 / MLA_ROPE, 0.0))

    gkr = jnp.concatenate([jnp.zeros((ROPE_LO,), F32), mla_g_kr, jnp.zeros((LANES - ROPE_LO - MLA_ROPE,), F32)])
    return dict(
        g_mix=g_mix[None, :], w_in=w_in_p.astype(BF16), w_a2=w_a2.astype(BF16), b_a=gla_b_a[None, :],
        g_out=gla_g_out[None, :], g_qa=mla_g_qa[None, :], w_qup=wq.astype(BF16), segm=jnp.asarray(segm, BF16),
        gq_slab=gq_slab, g_kva=mla_g_kva[None, :], w_kslab=wk_slab.astype(BF16),
        w_vT=wv.T.astype(BF16),
        gk_slab=slab_vec(mla_g_kn, jnp.zeros((MLA_ROPE,), F32)),
        g_kn=mla_g_kn, gkr_slab=gkr[None, :], w_kabs=w_kabs.astype(BF16), w_kT=w_kT.astype(BF16),
        w_v=wv.astype(BF16), w_o=w_o.astype(BF16), g_ffn=g_ffn[None, :], w_gate=ffn_w_gate.astype(BF16),
        w_up=ffn_w_up.astype(BF16), conv_w=ffn_conv_w, conv_b=ffn_conv_b[None, :],
        w_down=ffn_w_down.astype(BF16), g_ple=g_ple[None, :], w_pgate=ple_w_gate.astype(BF16),
        w_pproj=ple_w_proj.astype(BF16))


def _rope_tables(pos):
    inv = ROPE_THETA ** (-jnp.arange(ROPE_HALF, dtype=F32) * 2.0 / MLA_ROPE)
    ang = pos.astype(F32)[:, None] * inv[None, :]
    cos, sin = jnp.cos(ang), jnp.sin(ang)
    T = pos.shape[0]
    pad = jnp.zeros((T, HEAD_SLAB - MLA_NOPE - MLA_ROPE), F32)
    cos_t = jnp.concatenate([jnp.ones((T, MLA_NOPE), F32), cos, cos, pad], axis=1)
    sin_t = jnp.concatenate([jnp.zeros((T, MLA_NOPE), F32), -sin, sin, pad], axis=1)
    return cos_t, sin_t


def _pick_tile(n, cap):
    t = math.gcd(n, cap)
    return t


def _layer(w, x_p, x_s, p_p, p_s, pool_c, pool_r, layer, state_gla, state_conv, page_table):
    B, S, _ = x_p.shape
    Bd, Td, _ = x_s.shape
    tm = _pick_tile(S, 512)
    tps = S // tm
    cos_p, sin_p = _rope_tables(jnp.arange(S))
    xp2 = x_p.reshape(B * S, D_MODEL)
    gq, gk, gv, gg, la, qa, ckv_p, kr_p, ka, vaT = _in_proj(xp2, cos_p, sin_p, w, tm=tm, tiles_per_seq=tps,
                                                            sample=False)
    chunk_p = math.gcd(S, GLA_CHUNK)
    n_chunks = math.gcd(S // chunk_p, 8)
    og_p, gla_p = _gla(gq, gk, gv, gg, la, jnp.zeros((B, GLA_HEADS, GLA_DK, GLA_DV), F32), w['g_out'],
                      n_seq=B, seq_len=S, chunk=chunk_p, n_chunks=n_chunks,
                      seqs_per_step=math.gcd(B, GLA_PROMPT_SEQS))
    tq = _pick_tile(S, 512)
    om_p = _prompt_attn(qa, ka, vaT, n_seq=B, seq_len=S, tq=tq, tk=tq)
    y_p, tail_p = _post(xp2, og_p, om_p, p_p.reshape(B * S, PLE_DIM), w, tm=tm, tiles_per_seq=tps, seq_rows=None)
    conv_p = tail_p.reshape(B, tps, SUBLANES, D_FF)[:, -1, SUBLANES - (CONV_W - 1):, :]

    Ts = Bd * Td
    pos_s = PAST_LEN + jnp.arange(Td)
    cos_s, sin_s = _rope_tables(jnp.tile(pos_s, Bd))
    xs2 = x_s.reshape(Ts, D_MODEL)
    gq, gk, gv, gg, la, qa, ckv_s, kr_s, qabs = _in_proj(xs2, cos_s, sin_s, w, tm=Ts, tiles_per_seq=1, sample=True)
    chunk_s = math.gcd(Td, GLA_CHUNK)
    og_s, gla_s = _gla(gq, gk, gv, gg, la, state_gla, w['g_out'],
                      n_seq=Bd, seq_len=Td, chunk=chunk_s, n_chunks=Td // chunk_s,
                      seqs_per_step=math.gcd(Bd, GLA_SAMPLE_SEQS))
    rows = Td * MLA_HEADS
    qrope = qa.reshape(Ts, MLA_HEADS, HEAD_SLAB)[:, :, ROPE_LO:ROPE_LO + MLA_ROPE].reshape(Bd, rows, MLA_ROPE)
    krT_new = jnp.pad(jnp.swapaxes(kr_s.reshape(Bd, Td, MLA_ROPE), 1, 2), ((0, 0), (0, 0), (0, PAGE_SIZE - Td)))
    om_s = _sample_attn(page_table, qabs.reshape(Bd, rows, MLA_KV_LORA), qrope,
                        ckv_s.reshape(Bd, Td, MLA_KV_LORA), krT_new,
                        w['w_kT'], w['w_v'], pool_c, jnp.swapaxes(pool_r, 2, 3), layer,
                        pages_per_step=math.gcd(page_table.shape[1], SAMPLE_PAGES_PER_STEP))
    zpad = lambda a, lo: jnp.pad(a, ((0, 0), (lo, Td - lo - a.shape[1]), (0, 0))).reshape(Ts, D_FF)
    prev1 = zpad(state_conv[:, 1:2], 0)
    prev2 = zpad(state_conv, 0)
    y_s, a_s = _post(xs2, og_s, om_s, p_s.reshape(Ts, PLE_DIM), w, prev=(prev1, prev2), tm=Ts, tiles_per_seq=1,
                     seq_rows=Td)
    full = jnp.concatenate([state_conv, a_s.reshape(Bd, Td, D_FF)], axis=1)
    conv_s = full[:, full.shape[1] - (CONV_W - 1):]
    return (y_p.reshape(B, S, D_MODEL), y_s.reshape(Bd, Td, D_MODEL),
            ckv_p.reshape(B, S, MLA_KV_LORA), kr_p.reshape(B, S, MLA_ROPE),
            gla_p, conv_p,
            ckv_s.reshape(Bd, Td, MLA_KV_LORA), kr_s.reshape(Bd, Td, MLA_ROPE),
            gla_s, conv_s)


def kernel(x_prompt, x_sample, cache_ckv, cache_krope, state_gla, state_conv, page_table, p_prompt, p_sample, g_mix, w_in, gla_w_a2, gla_b_a, gla_g_out, mla_g_qa, mla_w_qup, mla_g_qn, mla_g_qr, mla_g_kva, mla_g_kr, mla_w_kvup, mla_g_kn, w_o, g_ffn, ffn_w_gate, ffn_w_up, ffn_conv_w, ffn_conv_b, ffn_w_down, g_ple, ple_w_gate, ple_w_proj):
    depth = w_in.shape[0]
    per_layer = (g_mix, w_in, gla_w_a2, gla_b_a, gla_g_out, mla_g_qa, mla_w_qup, mla_g_qn, mla_g_qr, mla_g_kva,
                 mla_g_kr, mla_w_kvup, mla_g_kn, w_o, g_ffn, ffn_w_gate, ffn_w_up, ffn_conv_w, ffn_conv_b,
                 ffn_w_down, g_ple, ple_w_gate, ple_w_proj)
    y_p, y_s = x_prompt, x_sample
    outs = [[] for _ in range(8)]
    for i in range(depth):
        w = _prep_weights(*(a[i] for a in per_layer))
        res = _layer(w, y_p, y_s, p_prompt[i], p_sample[i], cache_ckv, cache_krope, i, state_gla[i],
                     state_conv[i], page_table)
        y_p, y_s = res[0], res[1]
        for lst, r in zip(outs, res[2:]):
            lst.append(r)
    return (y_p, y_s) + tuple(jnp.stack(lst) for lst in outs)
```

```python
import functools
import math

import jax
import jax.numpy as jnp
import numpy as np
from jax import lax
from jax.experimental import pallas as pl
from jax.experimental.pallas import tpu as pltpu

D_MODEL = 1024
PAST_LEN = 16384
PAGE_SIZE = 128
GLA_HEADS = 4
GLA_DK = 64
GLA_DV = 128
GLA_GATE_RANK = 16
GLA_GATE_NORM = 16.0
GLA_CHUNK = 32
MLA_HEADS = 8
MLA_Q_LORA = 256
MLA_KV_LORA = 128
MLA_NOPE = 64
MLA_ROPE = 32
MLA_V = 64
MLA_SCALE = (MLA_NOPE + MLA_ROPE) ** -0.5
LOG2E = math.log2(math.e)
ROPE_THETA = 10000.0
D_FF = 2816
CONV_W = 3
PLE_DIM = 256
EPS = 1e-6
NEG = -1e30

LANES = 128
SUBLANES = 8
HEAD_SLAB = LANES
ROPE_LO = MLA_NOPE
ROPE_HALF = MLA_ROPE // 2
GLA_QK = GLA_HEADS * GLA_DK
GLA_V = GLA_HEADS * GLA_DV
MLA_SLABS = MLA_HEADS * HEAD_SLAB
MLA_VALL = MLA_HEADS * MLA_V
C_GQ, C_GK, C_GV, C_GG = 0, 256, 512, 1024
C_MQ, C_MKV, C_MISC = 1536, 1792, 1920
IN_COLS_P = 2048

TOKEN_TILE = 512
FFN_CHUNKS = (1536, 1280)
ATTN_TILE = 512
ATTN_HEADS_PER_STEP = 8
ATTN_PIPELINE_DEPTH = 1
GLA_CHUNKS_PER_STEP = 8
GLA_PROMPT_SEQS = 4
GLA_SAMPLE_SEQS = 8
SAMPLE_PAGES_PER_STEP = 32
PAGE_RING = 3
VMEM_LIMIT = 56 * 1024 * 1024

BF16 = jnp.bfloat16
F32 = jnp.float32


def _dot(a, b):
    return jnp.dot(a, b, preferred_element_type=F32)


def _dot_nt(a, b):
    return lax.dot_general(a, b, (((1,), (1,)), ((), ())), preferred_element_type=F32)


def _dot_tn(a, b):
    return lax.dot_general(a, b, (((0,), (0,)), ((), ())), preferred_element_type=F32)


def _rms(x, g):
    return x * lax.rsqrt(jnp.mean(x * x, axis=-1, keepdims=True) + EPS) * g


def _const_spec(shape):
    nd = len(shape)
    return pl.BlockSpec(shape, lambda *_: (0,) * nd, pipeline_mode=pl.Buffered(1))


def _rope_slab(y, cos_t, sin_t):
    lane = lax.broadcasted_iota(jnp.int32, y.shape, 1)
    swapped = jnp.where(lane < ROPE_LO + ROPE_HALF,
                        pltpu.roll(y, LANES - ROPE_HALF, axis=1),
                        pltpu.roll(y, ROPE_HALF, axis=1))
    return y * cos_t + swapped * sin_t


def _in_proj_kernel(x_ref, cos_ref, sin_ref, gmix_ref, win_ref, wa2_ref, ba_ref, gqa_ref, wqup_ref,
                    segm_ref, gq_ref, gkva_ref, gkr_ref, *rest, sample):
    if sample:
        (wkabs_ref, gq_o, gk_o, gv_o, gg_o, la_o, qa_o, ckv_o, kr_o, qabs_o) = rest
    else:
        (wk_ref, gk_ref, wvT_ref, gq_o, gk_o, gv_o, gg_o, la_o, qa_o, ckv_o, kr_o, ka_o, vaT_o) = rest
    segm = segm_ref[...]
    q_gain = gq_ref[...]

    cos_t = cos_ref[...]
    sin_t = sin_ref[...]
    n = _rms(x_ref[...], gmix_ref[...]).astype(BF16)

    def proj(lo, width):
        return _dot(n, win_ref[:, lo:lo + width])

    um = proj(C_MQ, IN_COLS_P - C_MQ)
    ug = proj(C_GQ, 2 * GLA_QK)
    gq_o[...] = ug[:, :GLA_QK] * (GLA_DK ** -0.5)
    gk_o[...] = ug[:, GLA_QK:]
    misc = um[:, C_MISC - C_MQ:]
    z = _dot(misc.astype(BF16), wa2_ref[...]) + ba_ref[...]
    la_o[...] = (jnp.minimum(z, 0.0) - jnp.log1p(jnp.exp(-jnp.abs(z)))) * (1.0 / GLA_GATE_NORM)
    lane = lax.broadcasted_iota(jnp.int32, misc.shape, 1)
    is_rope = (lane >= ROPE_LO) & (lane < ROPE_LO + MLA_ROPE)
    ms_r = jnp.sum(jnp.where(is_rope, misc * misc, 0.0), axis=-1, keepdims=True) * (1.0 / MLA_ROPE)
    kr_slab = _rope_slab(misc * lax.rsqrt(ms_r + EPS) * gkr_ref[...], cos_t, sin_t)
    kr_o[...] = kr_slab[:, ROPE_LO:ROPE_LO + MLA_ROPE]
    ckv = _rms(um[:, C_MKV - C_MQ:C_MISC - C_MQ], gkva_ref[...])
    ckv_o[...] = ckv
    ckv_b = ckv.astype(BF16)
    cq = _rms(um[:, :MLA_Q_LORA], gqa_ref[...]).astype(BF16)

    if not sample:
        kn = _dot(ckv_b, wk_ref[...])
        vaT_o[...] = _dot_nt(wvT_ref[...], ckv_b).astype(BF16)
    qf = _dot(cq, wqup_ref[...])
    q_cos = cos_t * (q_gain[0:1] * (MLA_SCALE * LOG2E))
    q_sin = sin_t * (q_gain[1:2] * (MLA_SCALE * LOG2E))

    def head_pair(pair):
        lo = pair * 2 * HEAD_SLAB
        q2 = qf[:, lo:lo + 2 * HEAD_SLAB]
        q_inv = lax.rsqrt(_dot((q2 * q2).astype(BF16), segm) + EPS)
        if not sample:
            k2 = kn[:, lo:lo + 2 * HEAD_SLAB]
            k2 = k2 * lax.rsqrt(_dot((k2 * k2).astype(BF16), segm) + EPS) * gk_ref[:, lo:lo + 2 * HEAD_SLAB]
        for j in range(2):
            sl = slice(lo + j * HEAD_SLAB, lo + (j + 1) * HEAD_SLAB)
            swapped = qf[:, MLA_SLABS + lo + j * HEAD_SLAB:MLA_SLABS + lo + (j + 1) * HEAD_SLAB]
            qh = q_inv[:, j * HEAD_SLAB:(j + 1) * HEAD_SLAB] * (qf[:, sl] * q_cos + swapped * q_sin)
            qa_o[:, sl] = qh.astype(BF16)
            if not sample:
                ka_o[:, sl] = (k2[:, j * HEAD_SLAB:(j + 1) * HEAD_SLAB] + kr_slab).astype(BF16)

    half = GLA_V // 2
    gv_o[:, :half] = proj(C_GV, half).astype(BF16)
    head_pair(0)
    gv_o[:, half:] = proj(C_GV + half, half).astype(BF16)
    head_pair(1)
    gg_o[:, :half] = proj(C_GG, half)
    head_pair(2)
    gg_o[:, half:] = proj(C_GG + half, half)
    head_pair(3)
    if sample:
        qabs_o[...] = _dot(qa_o[...], wkabs_ref[...]).astype(BF16)


def _in_proj(x2d, cos_t, sin_t, w, *, tm, tiles_per_seq, sample):
    T = x2d.shape[0]
    nt = T // tm
    row = lambda i: (i, 0)
    pos = lambda i: (i % tiles_per_seq, 0)
    consts = [w['g_mix'], w['w_in'], w['w_a2'], w['b_a'], w['g_qa'], w['w_qup'], w['segm'], w['gq_slab'],
              w['g_kva'], w['gkr_slab']]
    consts += [w['w_kabs']] if sample else [w['w_kslab'], w['gk_slab'], w['w_vT']]
    in_specs = [pl.BlockSpec((tm, D_MODEL), row), pl.BlockSpec((tm, LANES), pos), pl.BlockSpec((tm, LANES), pos)]
    in_specs += [_const_spec(c.shape) for c in consts]
    outs = [(GLA_QK, F32), (GLA_QK, F32), (GLA_V, BF16), (GLA_V, F32), (GLA_QK, F32),
            (MLA_SLABS, BF16), (MLA_KV_LORA, F32), (MLA_ROPE, F32), (MLA_SLABS, BF16)]
    out_specs = [pl.BlockSpec((tm, c), row) for c, _ in outs]
    out_shape = [jax.ShapeDtypeStruct((T, c), d) for c, d in outs]
    if not sample:
        out_specs.append(pl.BlockSpec((MLA_VALL, tm), lambda i: (0, i)))
        out_shape.append(jax.ShapeDtypeStruct((MLA_VALL, T), BF16))
    return pl.pallas_call(
        functools.partial(_in_proj_kernel, sample=sample),
        grid=(nt,),
        in_specs=in_specs,
        out_specs=out_specs,
        out_shape=out_shape,
        compiler_params=pltpu.CompilerParams(dimension_semantics=("parallel",), vmem_limit_bytes=VMEM_LIMIT),
        name="in_proj_sample" if sample else "in_proj_prompt",
    )(x2d, cos_t, sin_t, *consts)


def _gla_kernel(q_ref, k_ref, v_ref, gg_ref, la_ref, s0_ref, cum_ref, sel_ref, gout_ref,
                o_ref, s_out_ref, st_scr, *, chunk, n_chunks):
    step = pl.program_id(1)
    n_seq = q_ref.shape[0]
    tc = chunk * n_chunks
    heads = range(GLA_HEADS)
    dk = [slice(h * GLA_DK, (h + 1) * GLA_DK) for h in heads]
    dv = [slice(h * GLA_DV, (h + 1) * GLA_DV) for h in heads]
    rows = [slice(ci * chunk, (ci + 1) * chunk) for ci in range(n_chunks)]

    @pl.when(step == 0)
    def _():
        st_scr[...] = s0_ref[...]

    cum = cum_ref[...]
    sel = sel_ref[...]
    r = lax.broadcasted_iota(jnp.int32, (tc, tc), 0)
    c = lax.broadcasted_iota(jnp.int32, (tc, tc), 1)
    causal = (r >= c) & ((r // chunk) == (c // chunk))
    gout = gout_ref[...]

    def decays(i, t):
        la = la_ref[i]
        la_hi = la.astype(BF16)
        la_lo = (la - la_hi.astype(F32)).astype(BF16)
        t['bb'] = _dot(cum, la_hi) + _dot(cum, la_lo)
        t['dec'] = jnp.exp(_dot_tn(la_hi, sel) + _dot_tn(la_lo, sel))

    def scale(i, t):
        b, bl = t['bb'][:tc], t['bb'][tc:]
        q, k = q_ref[i], k_ref[i]
        t['q_dec'] = (q * jnp.exp(b)).astype(BF16)
        t['k_inv'] = (k * jnp.exp(-b)).astype(BF16)
        t['k_end'] = (k * jnp.exp(bl - b)).astype(BF16)

    def intra_scores(i, t):
        t['a'] = [_dot_nt(t['q_dec'][:, dk[h]], t['k_inv'][:, dk[h]]) for h in heads]

    def chunk_outer(i, t):
        v = v_ref[i]
        t['x'] = [[_dot_tn(t['k_end'][rw, dk[h]], v[rw, dv[h]]) for rw in rows] for h in heads]

    def intra_out(i, t):
        v = v_ref[i]
        t['o'] = [_dot(jnp.where(causal, t['a'][h], 0.0).astype(BF16), v[:, dv[h]]) for h in heads]

    def scan(i, t):
        t['s'] = []
        for h in heads:
            s_h = st_scr[i, h]
            starts = []
            for ci in range(n_chunks):
                starts.append(s_h.astype(BF16))
                s_h = s_h * t['dec'][dk[h], ci:ci + 1] + t['x'][h][ci]
            st_scr[i, h] = s_h
            t['s'].append(starts)

    def inter_out(i, t):
        for h in heads:
            parts = [_dot(t['q_dec'][rw, dk[h]], t['s'][h][ci]) for ci, rw in enumerate(rows)]
            t['o'][h] = t['o'][h] + (jnp.concatenate(parts, axis=0) if n_chunks > 1 else parts[0])

    def finish(i, t):
        gg = gg_ref[i]
        gate = gg / (1.0 + jnp.exp(-gg))
        for h in heads:
            o_ref[i, :, dv[h]] = (_rms(t['o'][h], gout) * gate[:, dv[h]]).astype(o_ref.dtype)

    temps = [{} for _ in range(n_seq)]
    for stage in (decays, scale, intra_scores, chunk_outer, intra_out, scan, inter_out, finish):
        for i in range(n_seq):
            stage(i, temps[i])

    @pl.when(step == pl.num_programs(1) - 1)
    def _():
        s_out_ref[...] = st_scr[...]


def _gla(q, k, v, gg, la, s0, g_out, *, n_seq, seq_len, chunk, n_chunks, seqs_per_step):
    tc = chunk * n_chunks
    steps = seq_len // tc
    nb = seqs_per_step
    idx = np.arange(tc)
    same = (idx[:, None] // chunk) == (idx[None, :] // chunk)
    cum = jnp.asarray(np.concatenate([same & (idx[:, None] >= idx[None, :]), same], axis=0), BF16)
    sel = jnp.asarray((idx[:, None] // chunk) == np.arange(LANES)[None, :], BF16)
    tok = lambda width: pl.BlockSpec((nb, tc, width), lambda b, s: (b, s, 0))
    per_seq = lambda a: a.reshape(n_seq, seq_len, a.shape[-1])
    state_spec = pl.BlockSpec((nb, GLA_HEADS, GLA_DK, GLA_DV), lambda b, s: (b, 0, 0, 0))
    o, state = pl.pallas_call(
        functools.partial(_gla_kernel, chunk=chunk, n_chunks=n_chunks),
        grid=(n_seq // nb, steps),
        in_specs=[tok(GLA_QK), tok(GLA_QK), tok(GLA_V), tok(GLA_V), tok(GLA_QK), state_spec,
                  _const_spec((2 * tc, tc)), _const_spec((tc, LANES)), _const_spec((1, GLA_DV))],
        out_specs=[tok(GLA_V), state_spec],
        out_shape=[jax.ShapeDtypeStruct((n_seq, seq_len, GLA_V), BF16),
                   jax.ShapeDtypeStruct((n_seq, GLA_HEADS, GLA_DK, GLA_DV), F32)],
        scratch_shapes=[pltpu.VMEM((nb, GLA_HEADS, GLA_DK, GLA_DV), F32)],
        compiler_params=pltpu.CompilerParams(dimension_semantics=("parallel", "arbitrary"),
                                             vmem_limit_bytes=VMEM_LIMIT),
        name=f"gla_c{chunk}",
    )(per_seq(q), per_seq(k), per_seq(v), per_seq(gg), per_seq(la), s0, cum, sel, g_out)
    return o.reshape(n_seq * seq_len, GLA_V), state


def _prompt_attn_kernel(q_ref, k_ref, vT_ref, o_ref, s_scr, *, tq, tk):
    assert tq == tk
    qi = pl.program_id(2)
    key = lax.broadcasted_iota(jnp.int32, (tk, tq), 0)
    qry = lax.broadcasted_iota(jnp.int32, (tk, tq), 1)
    ones_rows = jnp.ones((2 * SUBLANES, tk), BF16)

    def scores(j, kb, masked):
        ks = pl.multiple_of(kb * tk, tk)
        s = _dot_nt(k_ref[pl.ds(ks, tk), j * HEAD_SLAB:(j + 1) * HEAD_SLAB],
                    q_ref[:, j * HEAD_SLAB:(j + 1) * HEAD_SLAB])
        if masked:
            s = jnp.where(key <= qry, s, NEG)
        s_scr[j] = s
        return jnp.max(s, axis=0, keepdims=True)

    def update(j, kb, m_blk, state):
        m, acc = state
        ks = pl.multiple_of(kb * tk, tk)
        m_new = jnp.maximum(m, m_blk)
        alpha = jnp.exp2(m - m_new)
        p = jnp.exp2(s_scr[j] - m_new)
        v_aug = jnp.concatenate([vT_ref[j * MLA_V:(j + 1) * MLA_V, pl.ds(ks, tk)], ones_rows], axis=0)
        acc = alpha * acc + _dot(v_aug, p.astype(BF16))
        return m_new, acc

    n_heads = q_ref.shape[1] // HEAD_SLAB
    depth = min(ATTN_PIPELINE_DEPTH, n_heads - 1)

    def block(kb, masked, pending, state):
        state = list(state)
        queue = [(n_heads - len(pending) + i, kb + 1, m) for i, m in enumerate(pending)]
        for j in range(n_heads):
            queue.append((j, kb, scores(j, kb, masked)))
            if len(queue) > depth:
                pj, pkb, pm = queue.pop(0)
                state[pj] = update(pj, pkb, pm, state[pj])
        return tuple(m for _, _, m in queue), tuple(state)

    init = (jnp.full((1, tq), NEG, F32), jnp.zeros((MLA_V + 2 * SUBLANES, tq), F32))
    pending, state = block(qi, True, (), (init,) * n_heads)

    def body(t, carry):
        return block(qi - 1 - t, False, *carry)

    pending, state = lax.fori_loop(0, qi, body, (pending, state))
    state = list(state)
    for i, m_blk in enumerate(pending):
        head = n_heads - depth + i
        state[head] = update(head, 0, m_blk, state[head])
    o_ref[...] = jnp.concatenate([acc[:MLA_V] / acc[MLA_V:MLA_V + 1] for _, acc in state],
                                 axis=0).T.astype(o_ref.dtype)


def _prompt_attn(qa, ka, va, *, n_seq, seq_len, tq, tk):
    nq = seq_len // tq
    hs = ATTN_HEADS_PER_STEP
    return pl.pallas_call(
        functools.partial(_prompt_attn_kernel, tq=tq, tk=tk),
        grid=(n_seq, MLA_HEADS // hs, nq),
        in_specs=[pl.BlockSpec((tq, hs * HEAD_SLAB), lambda b, hp, i: (b * nq + i, hp)),
                  pl.BlockSpec((seq_len, hs * HEAD_SLAB), lambda b, hp, i: (b, hp)),
                  pl.BlockSpec((hs * MLA_V, seq_len), lambda b, hp, i: (hp, b))],
        out_specs=pl.BlockSpec((tq, hs * MLA_V), lambda b, hp, i: (b * nq + i, hp)),
        out_shape=jax.ShapeDtypeStruct((n_seq * seq_len, MLA_VALL), BF16),
        scratch_shapes=[pltpu.VMEM((hs, tk, tq), F32)],
        compiler_params=pltpu.CompilerParams(dimension_semantics=("parallel", "parallel", "arbitrary"),
                                             vmem_limit_bytes=VMEM_LIMIT),
        name="prompt_attn",
    )(qa, ka, va)


def _sample_attn_kernel(pt_ref, qabs_ref, qrope_ref, cnew_ref, krnew_ref, wkT_ref, wv_ref, poolc_hbm, poolr_hbm,
                        o_ref, lhs_scr, m_scr, l_scr, acc_scr, cb_scr, krT_scr, s_scr, cbuf, rbuf, sem,
                        *, n_tok, pages_per_step, layer):
    G = pages_per_step
    b = pl.program_id(0)
    g = pl.program_id(1)
    n_g = pl.num_programs(1)
    step = b * n_g + g
    n_steps = pl.num_programs(0) * n_g
    last = step == n_steps - 1
    slot = step % PAGE_RING

    def group_of(s):
        s = jnp.minimum(s, n_steps - 1)
        return s // n_g, s % n_g
    rows = n_tok * MLA_HEADS
    n_k = MLA_HEADS * MLA_NOPE

    def page_copies(sl, group):
        copies = []
        for j in range(G):
            page = pt_ref[group[0], group[1] * G + j]
            copies.append(pltpu.make_async_copy(poolc_hbm.at[layer, page], cbuf.at[sl, j], sem.at[0, sl]))
            copies.append(pltpu.make_async_copy(poolr_hbm.at[layer, page], rbuf.at[sl, j], sem.at[1, sl]))
        return copies

    def wait_slot(sl):
        pltpu.make_async_copy(cbuf.at[sl], cbuf.at[sl], sem.at[0, sl]).wait()
        pltpu.make_async_copy(rbuf.at[sl], rbuf.at[sl], sem.at[1, sl]).wait()

    @pl.when(step == 0)
    def _():
        for d in range(PAGE_RING - 1):
            for cp in page_copies(d, group_of(d)):
                cp.start()

    wait_slot(slot)

    def scores(cb, krT):
        nk = cb.shape[0]
        big = _dot_nt(lhs_scr[...], cb)
        kvn = big[:n_k]
        s_nope = big[n_k:n_k + rows]
        ss = jnp.sum((kvn * kvn).reshape(MLA_HEADS, MLA_NOPE, nk), axis=1)
        inv = lax.rsqrt(ss * (1.0 / MLA_NOPE) + EPS)
        s_rope = _dot(qrope_ref[0], krT)
        return (s_nope.reshape(n_tok, MLA_HEADS, nk) * inv[None]).reshape(rows, nk) + s_rope

    def weights(s):
        m = m_scr[...]
        m_new = jnp.maximum(m, jnp.max(s, axis=-1, keepdims=True))
        alpha = jnp.exp2(m - m_new)
        p = jnp.exp2(s - m_new)
        l_scr[...] = alpha * l_scr[...] + jnp.sum(p, axis=-1, keepdims=True)
        m_scr[...] = m_new
        return alpha, p.astype(BF16)

    def accumulate(alpha, p, cb):
        acc_scr[...] = alpha * acc_scr[...] + _dot(p, cb)

    def absorb(s, cb):
        accumulate(*weights(s), cb)

    cur = g % 2
    prv = 1 - cur

    @pl.when(g == 0)
    def _():
        lhs_scr[:n_k, :] = wkT_ref[...]
        lhs_scr[n_k:n_k + rows, :] = qabs_ref[0]
        m_scr[...] = jnp.full(m_scr.shape, NEG, F32)
        l_scr[...] = jnp.zeros(l_scr.shape, F32)
        acc_scr[...] = jnp.zeros(acc_scr.shape, F32)
        s_scr[prv] = jnp.full(s_scr.shape[1:], NEG, F32)
        cb_scr[prv] = jnp.zeros(cb_scr.shape[1:], BF16)
        c_new = jnp.concatenate([cnew_ref[0], jnp.zeros((PAGE_SIZE - n_tok, MLA_KV_LORA), F32)],
                                axis=0).astype(BF16)
        key = lax.broadcasted_iota(jnp.int32, (rows, PAGE_SIZE), 1)
        tok = lax.broadcasted_iota(jnp.int32, (rows, PAGE_SIZE), 0) // MLA_HEADS
        absorb(jnp.where(key <= tok, scores(c_new, krnew_ref[0].astype(BF16)), NEG), c_new)

    for j in range(G):
        cb_scr[cur, j * PAGE_SIZE:(j + 1) * PAGE_SIZE, :] = cbuf[slot, j].astype(BF16)
        krT_scr[:, j * PAGE_SIZE:(j + 1) * PAGE_SIZE] = rbuf[slot, j].astype(BF16)

    ahead = step + (PAGE_RING - 1)
    for cp in page_copies(ahead % PAGE_RING, group_of(ahead)):
        cp.start()

    alpha, p = weights(s_scr[prv])
    s_scr[cur] = scores(cb_scr[cur], krT_scr[...])
    accumulate(alpha, p, cb_scr[prv])

    @pl.when(last)
    def _():
        for d in range(1, PAGE_RING):
            wait_slot((step + d) % PAGE_RING)

    @pl.when(g == n_g - 1)
    def _():
        absorb(s_scr[cur], cb_scr[cur])
        o_lat =(acc_scr[...] / l_scr[...]).astype(BF16)
        full = _dot(o_lat, wv_ref[...])
        full = full.reshape(n_tok, MLA_HEADS, MLA_VALL)
        hr = lax.broadcasted_iota(jnp.int32, (MLA_HEADS, MLA_VALL), 0)
        hc = lax.broadcasted_iota(jnp.int32, (MLA_HEADS, MLA_VALL), 1) // MLA_V
        o_ref[...] = jnp.sum(jnp.where((hr == hc)[None], full, 0.0), axis=1).astype(o_ref.dtype)


def _sample_attn(page_table, qabs, qrope, c_new, kr_new, w_kT, w_v, pool_c, pool_r, layer, *, pages_per_step):
    n_b, n_pages = page_table.shape
    n_tok = c_new.shape[1]
    rows = n_tok * MLA_HEADS
    G = pages_per_step
    n_k = MLA_HEADS * MLA_NOPE

    per_b = lambda b, g, pt: (b, 0, 0)
    grid_spec = pltpu.PrefetchScalarGridSpec(
        num_scalar_prefetch=1,
        grid=(n_b, n_pages // G),
        in_specs=[pl.BlockSpec((1, rows, MLA_KV_LORA), per_b), pl.BlockSpec((1, rows, MLA_ROPE), per_b),
                  pl.BlockSpec((1, n_tok, MLA_KV_LORA), per_b), pl.BlockSpec((1, MLA_ROPE, PAGE_SIZE), per_b),
                  pl.BlockSpec((n_k, MLA_KV_LORA), lambda b, g, pt: (0, 0)),
                  pl.BlockSpec((MLA_KV_LORA, MLA_VALL), lambda b, g, pt: (0, 0)),
                  pl.BlockSpec(memory_space=pl.ANY), pl.BlockSpec(memory_space=pl.ANY)],
        out_specs=pl.BlockSpec((n_tok, MLA_VALL), lambda b, g, pt: (b, 0)),
        scratch_shapes=[pltpu.VMEM((n_k + rows, MLA_KV_LORA), BF16), pltpu.VMEM((rows, 1), F32),
                        pltpu.VMEM((rows, 1), F32), pltpu.VMEM((rows, MLA_KV_LORA), F32),
                        pltpu.VMEM((2, G * PAGE_SIZE, MLA_KV_LORA), BF16), pltpu.VMEM((MLA_ROPE, G * PAGE_SIZE), BF16),
                        pltpu.VMEM((2, rows, G * PAGE_SIZE), F32),
                        pltpu.VMEM((PAGE_RING, G, PAGE_SIZE, MLA_KV_LORA), F32),
                        pltpu.VMEM((PAGE_RING, G, MLA_ROPE, PAGE_SIZE), F32),
                        pltpu.SemaphoreType.DMA((2, PAGE_RING))])
    return pl.pallas_call(
        functools.partial(_sample_attn_kernel, n_tok=n_tok, pages_per_step=G, layer=layer),
        grid_spec=grid_spec,
        out_shape=jax.ShapeDtypeStruct((n_b * n_tok, MLA_VALL), BF16),
        compiler_params=pltpu.CompilerParams(dimension_semantics=("arbitrary", "arbitrary"),
                                             vmem_limit_bytes=VMEM_LIMIT),
        name="sample_attn",
    )(page_table, qabs, qrope, c_new, kr_new, w_kT, w_v, pool_c, pool_r)


def _post_kernel(x_ref, og_ref, om_ref, p_ref, wo_ref, gffn_ref, wg_ref, wu_ref, cw_ref, cb_ref, wd_ref,
                 gple_ref, wpg_ref, wpp_ref, *rest, tiles_per_seq, seq_rows):
    paged_prev = seq_rows is not None
    if paged_prev:
        p1_ref, p2_ref, y_ref, tail_ref, acc_scr = rest
    else:
        y_ref, tail_ref, acc_scr, carry_scr = rest
    tm = x_ref.shape[0]
    i = pl.program_id(0)
    h1 = x_ref[...] + _dot(og_ref[...], wo_ref[:GLA_V, :]) + _dot(om_ref[...], wo_ref[GLA_V:, :])
    n2 = _rms(h1, gffn_ref[...]).astype(BF16)
    if not paged_prev:
        @pl.when(i % tiles_per_seq == 0)
        def _():
            carry_scr[...] = jnp.zeros(carry_scr.shape, F32)

    bounds = np.concatenate([[0], np.cumsum(FFN_CHUNKS)]).tolist()
    chunks = [slice(lo, hi) for lo, hi in zip(bounds[:-1], bounds[1:])]
    lo = bounds[-1]

    def gate_up(cols):
        return _dot(n2, wg_ref[:, cols]), _dot(n2, wu_ref[:, cols])

    nxt = gate_up(chunks[0])
    for ci, cols in enumerate(chunks):
        a, up = nxt
        if ci + 1 < len(chunks):
            nxt = gate_up(chunks[ci + 1])
        row =lax.broadcasted_iota(jnp.int32, a.shape, 0)
        slabs = [a[:, c0:c0 + 2 * LANES] for c0 in range(0, a.shape[1], 2 * LANES)]
        r1 = jnp.concatenate([pltpu.roll(sl, 1, axis=0) for sl in slabs], axis=1)
        r2 = jnp.concatenate([pltpu.roll(sl, 2, axis=0) for sl in slabs], axis=1)
        if paged_prev:
            t = row % seq_rows
            a1 = jnp.where(t >= 1, r1, p1_ref[:, cols])
            a2 = jnp.where(t >= 2, r2, p2_ref[:, cols])
            tail_ref[:, cols] = a
        else:
            prev = carry_scr[:, cols]
            pm1 = prev[SUBLANES - 1:SUBLANES, :]
            pm2 = prev[SUBLANES - 2:SUBLANES - 1, :]
            a1 = jnp.where(row >= 1, r1, pm1)
            a2 = jnp.where(row >= 2, r2, jnp.where(row == 0, pm2, pm1))
            carry_scr[:, cols] = a[tm - SUBLANES:, :]
            tail_ref[:, cols] = a[tm - SUBLANES:, :]
        conv = cb_ref[:, cols] + cw_ref[0:1, cols] * a2 + cw_ref[1:2, cols] * a1 + cw_ref[2:3, cols] * a
        gact = (conv / (1.0 + jnp.exp(-conv)) * up).astype(BF16)
        down = _dot(gact, wd_ref[cols, :])
        if ci == 0:
            acc_scr[...] = down
        else:
            acc_scr[...] += down
    assert lo == D_FF
    h2 = h1 + acc_scr[...]
    n3 =_rms(h2, gple_ref[...]).astype(BF16)
    gate = 1.0 / (1.0 + jnp.exp(-_dot(n3, wpg_ref[...])))
    y_ref[...] = h2 + _dot(p_ref[...].astype(BF16), wpp_ref[...]) * gate


def _post(x2d, og, om, p2d, w, prev=None, *, tm, tiles_per_seq, seq_rows):
    T = x2d.shape[0]
    nt = T // tm
    row = lambda i: (i, 0)
    consts = [w['w_o'], w['g_ffn'], w['w_gate'], w['w_up'], w['conv_w'], w['conv_b'], w['w_down'],
              w['g_ple'], w['w_pgate'], w['w_pproj']]
    in_specs = [pl.BlockSpec((tm, D_MODEL), row), pl.BlockSpec((tm, GLA_V), row),
                pl.BlockSpec((tm, MLA_VALL), row), pl.BlockSpec((tm, PLE_DIM), row)]
    in_specs += [_const_spec(c.shape) for c in consts]
    args = [x2d, og, om, p2d, *consts]
    scratch = [pltpu.VMEM((tm, D_MODEL), F32)]
    if prev is not None:
        in_specs += [pl.BlockSpec((tm, D_FF), row)] * 2
        args += list(prev)
        tail_rows = tm
    else:
        scratch.append(pltpu.VMEM((SUBLANES, D_FF), F32))
        tail_rows = SUBLANES
    return pl.pallas_call(
        functools.partial(_post_kernel, tiles_per_seq=tiles_per_seq, seq_rows=seq_rows),
        grid=(nt,),
        in_specs=in_specs,
        out_specs=[pl.BlockSpec((tm, D_MODEL), row), pl.BlockSpec((tail_rows, D_FF), row)],
        out_shape=[jax.ShapeDtypeStruct((T, D_MODEL), F32), jax.ShapeDtypeStruct((nt * tail_rows, D_FF), F32)],
        scratch_shapes=scratch,
        compiler_params=pltpu.CompilerParams(dimension_semantics=("arbitrary",), vmem_limit_bytes=VMEM_LIMIT),
        name="post_sample" if prev is not None else "post_prompt",
    )(*args)


def _prep_weights(g_mix, w_in, gla_w_a2, gla_b_a, gla_g_out, mla_g_qa, mla_w_qup, mla_g_qn, mla_g_qr,
                  mla_g_kva, mla_g_kr, mla_w_kvup, mla_g_kn, w_o, g_ffn, ffn_w_gate, ffn_w_up, ffn_conv_w,
                  ffn_conv_b, ffn_w_down, g_ple, ple_w_gate, ple_w_proj):
    sizes = (GLA_QK, GLA_QK, GLA_V, GLA_V, GLA_GATE_RANK, MLA_Q_LORA, MLA_KV_LORA, MLA_ROPE)
    offs = np.concatenate([[0], np.cumsum(sizes)])
    piece = lambda i: w_in[:, offs[i]:offs[i + 1]]
    zeros = lambda n: jnp.zeros((D_MODEL, n), w_in.dtype)
    misc = jnp.concatenate([piece(4), zeros(ROPE_LO - GLA_GATE_RANK), piece(7),
                            zeros(LANES - ROPE_LO - MLA_ROPE)], axis=1)
    w_in_p = jnp.concatenate([piece(0), piece(1), piece(2), piece(3), piece(5), piece(6), misc], axis=1)
    w_a2 = jnp.concatenate([gla_w_a2, jnp.zeros((LANES - GLA_GATE_RANK, GLA_QK), gla_w_a2.dtype)], axis=0)

    def slab_vec(nope, rope_):
        one = jnp.concatenate([nope, rope_, jnp.zeros((HEAD_SLAB - MLA_NOPE - MLA_ROPE,), F32)])
        return jnp.tile(one, MLA_HEADS)[None, :]

    wq = mla_w_qup.reshape(MLA_Q_LORA, MLA_HEADS, MLA_NOPE + MLA_ROPE)
    swap = np.arange(MLA_NOPE + MLA_ROPE)
    swap[MLA_NOPE:] = np.concatenate([swap[MLA_NOPE + ROPE_HALF:], swap[MLA_NOPE:MLA_NOPE + ROPE_HALF]])
    slab_pad = ((0, 0), (0, 0), (0, HEAD_SLAB - MLA_NOPE - MLA_ROPE))
    wq = jnp.concatenate([jnp.pad(wq, slab_pad).reshape(MLA_Q_LORA, MLA_SLABS),
                          jnp.pad(wq[:, :, swap], slab_pad).reshape(MLA_Q_LORA, MLA_SLABS)], axis=1)
    g_q = jnp.concatenate([mla_g_qn, mla_g_qr])
    gq_slab = jnp.pad(jnp.stack([g_q, g_q[swap]]), ((0, 0), (0, HEAD_SLAB - MLA_NOPE - MLA_ROPE)))
    wkv = mla_w_kvup.reshape(MLA_KV_LORA, MLA_HEADS, MLA_NOPE + MLA_V)
    wk = wkv[:, :, :MLA_NOPE]
    wv = wkv[:, :, MLA_NOPE:].reshape(MLA_KV_LORA, MLA_VALL)
    wk_slab = jnp.pad(wk, ((0, 0), (0, 0), (0, HEAD_SLAB - MLA_NOPE))).reshape(MLA_KV_LORA, MLA_SLABS)
    wk_g = jnp.pad(wk * mla_g_kn[None, None, :], ((0, 0), (0, 0), (0, HEAD_SLAB - MLA_NOPE)))
    eye = jnp.eye(MLA_HEADS, dtype=F32)
    w_kabs = jnp.einsum('nhd,hg->hdgn', wk_g, eye).reshape(MLA_SLABS, MLA_HEADS * MLA_KV_LORA)
    w_kT = wk.transpose(1, 2, 0).reshape(MLA_HEADS * MLA_NOPE, MLA_KV_LORA)

    lane = np.arange(2 * HEAD_SLAB)
    seg = np.where(lane % HEAD_SLAB < MLA_NOPE, 0, np.where(lane % HEAD_SLAB < MLA_NOPE + MLA_ROPE, 1, 2))
    same = (lane[:, None] // HEAD_SLAB == lane[None, :] // HEAD_SLAB) & (seg[:, None] == seg[None, :])
    segm = np.where(same & (seg[:, None] == 0), 1.0 / MLA_NOPE, np.where(same & (seg[:, None] == 1), 1.0 / MLA_ROPE, 0.0))

    gkr = jnp.concatenate([jnp.zeros((ROPE_LO,), F32), mla_g_kr, jnp.zeros((LANES - ROPE_LO - MLA_ROPE,), F32)])
    return dict(
        g_mix=g_mix[None, :], w_in=w_in_p.astype(BF16), w_a2=w_a2.astype(BF16), b_a=gla_b_a[None, :],
        g_out=gla_g_out[None, :], g_qa=mla_g_qa[None, :], w_qup=wq.astype(BF16), segm=jnp.asarray(segm, BF16),
        gq_slab=gq_slab, g_kva=mla_g_kva[None, :], w_kslab=wk_slab.astype(BF16),
        w_vT=wv.T.astype(BF16),
        gk_slab=slab_vec(mla_g_kn, jnp.zeros((MLA_ROPE,), F32)),
        gkr_slab=gkr[None, :], w_kabs=w_kabs.astype(BF16), w_kT=w_kT.astype(BF16),
        w_v=wv.astype(BF16), w_o=w_o.astype(BF16), g_ffn=g_ffn[None, :], w_gate=ffn_w_gate.astype(BF16),
        w_up=ffn_w_up.astype(BF16), conv_w=ffn_conv_w, conv_b=ffn_conv_b[None, :],
        w_down=ffn_w_down.astype(BF16), g_ple=g_ple[None, :], w_pgate=ple_w_gate.astype(BF16),
        w_pproj=ple_w_proj.astype(BF16))


def _rope_tables(pos):
    inv = ROPE_THETA ** (-jnp.arange(ROPE_HALF, dtype=F32) * 2.0 / MLA_ROPE)
    ang = pos.astype(F32)[:, None] * inv[None, :]
    cos, sin = jnp.cos(ang), jnp.sin(ang)
    T = pos.shape[0]
    pad = jnp.zeros((T, HEAD_SLAB - MLA_NOPE - MLA_ROPE), F32)
    cos_t = jnp.concatenate([jnp.ones((T, MLA_NOPE), F32), cos, cos, pad], axis=1)
    sin_t = jnp.concatenate([jnp.zeros((T, MLA_NOPE), F32), -sin, sin, pad], axis=1)
    return cos_t, sin_t


def _layer(w, x_p, x_s, p_p, p_s, pool_c, pool_r, layer, state_gla, state_conv, page_table):
    B, S, _ = x_p.shape
    Bd, Td, _ = x_s.shape
    tm = math.gcd(S, TOKEN_TILE)
    tps = S // tm
    cos_p, sin_p = _rope_tables(jnp.arange(S))
    xp2 = x_p.reshape(B * S, D_MODEL)
    gq, gk, gv, gg, la, qa, ckv_p, kr_p, ka, vaT = _in_proj(xp2, cos_p, sin_p, w, tm=tm, tiles_per_seq=tps,
                                                            sample=False)
    chunk_p = math.gcd(S, GLA_CHUNK)
    n_chunks = math.gcd(S // chunk_p, GLA_CHUNKS_PER_STEP)
    og_p, gla_p = _gla(gq, gk, gv, gg, la, jnp.zeros((B, GLA_HEADS, GLA_DK, GLA_DV), F32), w['g_out'],
                      n_seq=B, seq_len=S, chunk=chunk_p, n_chunks=n_chunks,
                      seqs_per_step=math.gcd(B, GLA_PROMPT_SEQS))
    tq = math.gcd(S, ATTN_TILE)
    om_p = _prompt_attn(qa, ka, vaT, n_seq=B, seq_len=S, tq=tq, tk=tq)
    y_p, tail_p = _post(xp2, og_p, om_p, p_p.reshape(B * S, PLE_DIM), w, tm=tm, tiles_per_seq=tps, seq_rows=None)
    conv_p = tail_p.reshape(B, tps, SUBLANES, D_FF)[:, -1, SUBLANES - (CONV_W - 1):, :]

    Ts = Bd * Td
    pos_s = PAST_LEN + jnp.arange(Td)
    cos_s, sin_s = _rope_tables(jnp.tile(pos_s, Bd))
    xs2 = x_s.reshape(Ts, D_MODEL)
    gq, gk, gv, gg, la, qa, ckv_s, kr_s, qabs = _in_proj(xs2, cos_s, sin_s, w, tm=Ts, tiles_per_seq=1, sample=True)
    chunk_s = math.gcd(Td, GLA_CHUNK)
    og_s, gla_s = _gla(gq, gk, gv, gg, la, state_gla, w['g_out'],
                      n_seq=Bd, seq_len=Td, chunk=chunk_s, n_chunks=Td // chunk_s,
                      seqs_per_step=math.gcd(Bd, GLA_SAMPLE_SEQS))
    rows = Td * MLA_HEADS
    qrope = qa.reshape(Ts, MLA_HEADS, HEAD_SLAB)[:, :, ROPE_LO:ROPE_LO + MLA_ROPE].reshape(Bd, rows, MLA_ROPE)
    krT_new = jnp.pad(jnp.swapaxes(kr_s.reshape(Bd, Td, MLA_ROPE), 1, 2), ((0, 0), (0, 0), (0, PAGE_SIZE - Td)))
    om_s = _sample_attn(page_table, qabs.reshape(Bd, rows, MLA_KV_LORA), qrope,
                        ckv_s.reshape(Bd, Td, MLA_KV_LORA), krT_new,
                        w['w_kT'], w['w_v'], pool_c, jnp.swapaxes(pool_r, 2, 3), layer,
                        pages_per_step=math.gcd(page_table.shape[1], SAMPLE_PAGES_PER_STEP))
    zpad = lambda a, lo: jnp.pad(a, ((0, 0), (lo, Td - lo - a.shape[1]), (0, 0))).reshape(Ts, D_FF)
    prev1 = zpad(state_conv[:, 1:2], 0)
    prev2 = zpad(state_conv, 0)
    y_s, a_s = _post(xs2, og_s, om_s, p_s.reshape(Ts, PLE_DIM), w, prev=(prev1, prev2), tm=Ts, tiles_per_seq=1,
                     seq_rows=Td)
    full = jnp.concatenate([state_conv, a_s.reshape(Bd, Td, D_FF)], axis=1)
    conv_s = full[:, full.shape[1] - (CONV_W - 1):]
    return (y_p.reshape(B, S, D_MODEL), y_s.reshape(Bd, Td, D_MODEL),
            ckv_p.reshape(B, S, MLA_KV_LORA), kr_p.reshape(B, S, MLA_ROPE),
            gla_p, conv_p,
            ckv_s.reshape(Bd, Td, MLA_KV_LORA), kr_s.reshape(Bd, Td, MLA_ROPE),
            gla_s, conv_s)


def kernel(x_prompt, x_sample, cache_ckv, cache_krope, state_gla, state_conv, page_table, p_prompt, p_sample, g_mix, w_in, gla_w_a2, gla_b_a, gla_g_out, mla_g_qa, mla_w_qup, mla_g_qn, mla_g_qr, mla_g_kva, mla_g_kr, mla_w_kvup, mla_g_kn, w_o, g_ffn, ffn_w_gate, ffn_w_up, ffn_conv_w, ffn_conv_b, ffn_w_down, g_ple, ple_w_gate, ple_w_proj):
    depth = w_in.shape[0]
    per_layer = (g_mix, w_in, gla_w_a2, gla_b_a, gla_g_out, mla_g_qa, mla_w_qup, mla_g_qn, mla_g_qr, mla_g_kva,
                 mla_g_kr, mla_w_kvup, mla_g_kn, w_o, g_ffn, ffn_w_gate, ffn_w_up, ffn_conv_w, ffn_conv_b,
                 ffn_w_down, g_ple, ple_w_gate, ple_w_proj)
    y_p, y_s = x_prompt, x_sample
    outs = [[] for _ in range(8)]
    for i in range(depth):
        w = _prep_weights(*(a[i] for a in per_layer))
        res = _layer(w, y_p, y_s, p_prompt[i], p_sample[i], cache_ckv, cache_krope, i, state_gla[i],
                     state_conv[i], page_table)
        y_p, y_s = res[0], res[1]
        for lst, r in zip(outs, res[2:]):
            lst.append(r)
    return (y_p, y_s) + tuple(jnp.stack(lst) for lst in outs)
```

```python
import functools
import math

import jax
import jax.numpy as jnp
import numpy as np
from jax import lax
from jax.experimental import pallas as pl
from jax.experimental.pallas import tpu as pltpu

D_MODEL = 1024
PAST_LEN = 16384
PAGE_SIZE = 128
GLA_HEADS = 4
GLA_DK = 64
GLA_DV = 128
GLA_GATE_RANK = 16
GLA_GATE_NORM = 16.0
GLA_CHUNK = 32
MLA_HEADS = 8
MLA_Q_LORA = 256
MLA_KV_LORA = 128
MLA_NOPE = 64
MLA_ROPE = 32
MLA_V = 64
MLA_SCALE = (MLA_NOPE + MLA_ROPE) ** -0.5
LOG2E = math.log2(math.e)
ROPE_THETA = 10000.0
D_FF = 2816
CONV_W = 3
PLE_DIM = 256
EPS = 1e-6
NEG = -1e30

LANES = 128
SUBLANES = 8
HEAD_SLAB = LANES
ROPE_LO = MLA_NOPE
ROPE_HALF = MLA_ROPE // 2
GLA_QK = GLA_HEADS * GLA_DK
GLA_V = GLA_HEADS * GLA_DV
MLA_SLABS = MLA_HEADS * HEAD_SLAB
MLA_VALL = MLA_HEADS * MLA_V
C_GQ, C_GK, C_GV, C_GG = 0, 256, 512, 1024
C_MQ, C_MKV, C_MISC = 1536, 1792, 1920
IN_COLS_P = 2048

TOKEN_TILE = 512
FFN_CHUNKS = (1536, 1280)
ATTN_TILE = 512
ATTN_HEADS_PER_STEP = 8
ATTN_PIPELINE_DEPTH = 1
GLA_CHUNKS_PER_STEP = 8
GLA_PROMPT_SEQS = 4
GLA_SAMPLE_SEQS = 8
SAMPLE_PAGES_PER_STEP = 64
PAGE_RING = 3
VMEM_LIMIT = 56 * 1024 * 1024

BF16 = jnp.bfloat16
F32 = jnp.float32


def _dot(a, b):
    return jnp.dot(a, b, preferred_element_type=F32)


def _dot_nt(a, b):
    return lax.dot_general(a, b, (((1,), (1,)), ((), ())), preferred_element_type=F32)


def _dot_tn(a, b):
    return lax.dot_general(a, b, (((0,), (0,)), ((), ())), preferred_element_type=F32)


def _rms(x, g):
    return x * lax.rsqrt(jnp.mean(x * x, axis=-1, keepdims=True) + EPS) * g


def _const_spec(shape):
    nd = len(shape)
    return pl.BlockSpec(shape, lambda *_: (0,) * nd, pipeline_mode=pl.Buffered(1))


def _rope_slab(y, cos_t, sin_t):
    lane = lax.broadcasted_iota(jnp.int32, y.shape, 1)
    swapped = jnp.where(lane < ROPE_LO + ROPE_HALF,
                        pltpu.roll(y, LANES - ROPE_HALF, axis=1),
                        pltpu.roll(y, ROPE_HALF, axis=1))
    return y * cos_t + swapped * sin_t


def _in_proj_kernel(x_ref, cos_ref, sin_ref, gmix_ref, win_ref, wa2_ref, ba_ref, gqa_ref, wqup_ref,
                    segm_ref, gq_ref, gkva_ref, gkr_ref, *rest, sample):
    if sample:
        (wkabs_ref, gq_o, gk_o, gv_o, gg_o, la_o, qa_o, ckv_o, kr_o, qabs_o) = rest
    else:
        (wk_ref, gk_ref, wvT_ref, gq_o, gk_o, gv_o, gg_o, la_o, qa_o, ckv_o, kr_o, ka_o, vaT_o) = rest
    segm = segm_ref[...]
    q_gain = gq_ref[...]

    cos_t = cos_ref[...]
    sin_t = sin_ref[...]
    n = _rms(x_ref[...], gmix_ref[...]).astype(BF16)

    def proj(lo, width):
        return _dot(n, win_ref[:, lo:lo + width])

    um = proj(C_MQ, IN_COLS_P - C_MQ)
    ug = proj(C_GQ, 2 * GLA_QK)
    gq_o[...] = ug[:, :GLA_QK] * (GLA_DK ** -0.5)
    gk_o[...] = ug[:, GLA_QK:]
    misc = um[:, C_MISC - C_MQ:]
    z = _dot(misc.astype(BF16), wa2_ref[...]) + ba_ref[...]
    la_o[...] = (jnp.minimum(z, 0.0) - jnp.log1p(jnp.exp(-jnp.abs(z)))) * (1.0 / GLA_GATE_NORM)
    lane = lax.broadcasted_iota(jnp.int32, misc.shape, 1)
    is_rope = (lane >= ROPE_LO) & (lane < ROPE_LO + MLA_ROPE)
    ms_r = jnp.sum(jnp.where(is_rope, misc * misc, 0.0), axis=-1, keepdims=True) * (1.0 / MLA_ROPE)
    kr_slab = _rope_slab(misc * lax.rsqrt(ms_r + EPS) * gkr_ref[...], cos_t, sin_t)
    kr_o[...] = kr_slab[:, ROPE_LO:ROPE_LO + MLA_ROPE]
    ckv = _rms(um[:, C_MKV - C_MQ:C_MISC - C_MQ], gkva_ref[...])
    ckv_o[...] = ckv
    ckv_b = ckv.astype(BF16)
    cq = _rms(um[:, :MLA_Q_LORA], gqa_ref[...]).astype(BF16)

    if not sample:
        kn = _dot(ckv_b, wk_ref[...])
        vaT_o[...] = _dot_nt(wvT_ref[...], ckv_b).astype(BF16)
    qf = _dot(cq, wqup_ref[...])
    q_cos = cos_t * (q_gain[0:1] * (MLA_SCALE * LOG2E))
    q_sin = sin_t * (q_gain[1:2] * (MLA_SCALE * LOG2E))

    def head_pair(pair):
        lo = pair * 2 * HEAD_SLAB
        q2 = qf[:, lo:lo + 2 * HEAD_SLAB]
        q_inv = lax.rsqrt(_dot((q2 * q2).astype(BF16), segm) + EPS)
        if not sample:
            k2 = kn[:, lo:lo + 2 * HEAD_SLAB]
            k2 = k2 * lax.rsqrt(_dot((k2 * k2).astype(BF16), segm) + EPS) * gk_ref[:, lo:lo + 2 * HEAD_SLAB]
        for j in range(2):
            sl = slice(lo + j * HEAD_SLAB, lo + (j + 1) * HEAD_SLAB)
            swapped = qf[:, MLA_SLABS + lo + j * HEAD_SLAB:MLA_SLABS + lo + (j + 1) * HEAD_SLAB]
            qh = q_inv[:, j * HEAD_SLAB:(j + 1) * HEAD_SLAB] * (qf[:, sl] * q_cos + swapped * q_sin)
            qa_o[:, sl] = qh.astype(BF16)
            if not sample:
                ka_o[:, sl] = (k2[:, j * HEAD_SLAB:(j + 1) * HEAD_SLAB] + kr_slab).astype(BF16)

    half = GLA_V // 2
    gv_o[:, :half] = proj(C_GV, half).astype(BF16)
    head_pair(0)
    gv_o[:, half:] = proj(C_GV + half, half).astype(BF16)
    head_pair(1)
    gg_o[:, :half] = proj(C_GG, half)
    head_pair(2)
    gg_o[:, half:] = proj(C_GG + half, half)
    head_pair(3)
    if sample:
        qabs_o[...] = _dot(qa_o[...], wkabs_ref[...]).astype(BF16)


def _in_proj(x2d, cos_t, sin_t, w, *, tm, tiles_per_seq, sample):
    T = x2d.shape[0]
    nt = T // tm
    row = lambda i: (i, 0)
    pos = lambda i: (i % tiles_per_seq, 0)
    consts = [w['g_mix'], w['w_in'], w['w_a2'], w['b_a'], w['g_qa'], w['w_qup'], w['segm'], w['gq_slab'],
              w['g_kva'], w['gkr_slab']]
    consts += [w['w_kabs']] if sample else [w['w_kslab'], w['gk_slab'], w['w_vT']]
    in_specs = [pl.BlockSpec((tm, D_MODEL), row), pl.BlockSpec((tm, LANES), pos), pl.BlockSpec((tm, LANES), pos)]
    in_specs += [_const_spec(c.shape) for c in consts]
    outs = [(GLA_QK, F32), (GLA_QK, F32), (GLA_V, BF16), (GLA_V, F32), (GLA_QK, F32),
            (MLA_SLABS, BF16), (MLA_KV_LORA, F32), (MLA_ROPE, F32), (MLA_SLABS, BF16)]
    out_specs = [pl.BlockSpec((tm, c), row) for c, _ in outs]
    out_shape = [jax.ShapeDtypeStruct((T, c), d) for c, d in outs]
    if not sample:
        out_specs.append(pl.BlockSpec((MLA_VALL, tm), lambda i: (0, i)))
        out_shape.append(jax.ShapeDtypeStruct((MLA_VALL, T), BF16))
    return pl.pallas_call(
        functools.partial(_in_proj_kernel, sample=sample),
        grid=(nt,),
        in_specs=in_specs,
        out_specs=out_specs,
        out_shape=out_shape,
        compiler_params=pltpu.CompilerParams(dimension_semantics=("parallel",), vmem_limit_bytes=VMEM_LIMIT),
        name="in_proj_sample" if sample else "in_proj_prompt",
    )(x2d, cos_t, sin_t, *consts)


def _gla_kernel(q_ref, k_ref, v_ref, gg_ref, la_ref, s0_ref, cum_ref, sel_ref, gout_ref,
                o_ref, s_out_ref, st_scr, *, chunk, n_chunks):
    step = pl.program_id(1)
    n_seq = q_ref.shape[0]
    tc = chunk * n_chunks
    heads = range(GLA_HEADS)
    dk = [slice(h * GLA_DK, (h + 1) * GLA_DK) for h in heads]
    dv = [slice(h * GLA_DV, (h + 1) * GLA_DV) for h in heads]
    rows = [slice(ci * chunk, (ci + 1) * chunk) for ci in range(n_chunks)]

    @pl.when(step == 0)
    def _():
        st_scr[...] = s0_ref[...]

    cum = cum_ref[...]
    sel = sel_ref[...]
    r = lax.broadcasted_iota(jnp.int32, (tc, tc), 0)
    c = lax.broadcasted_iota(jnp.int32, (tc, tc), 1)
    causal = (r >= c) & ((r // chunk) == (c // chunk))
    gout = gout_ref[...]

    def decays(i, t):
        la = la_ref[i]
        la_hi = la.astype(BF16)
        la_lo = (la - la_hi.astype(F32)).astype(BF16)
        t['bb'] = _dot(cum, la_hi) + _dot(cum, la_lo)
        t['dec'] = jnp.exp(_dot_tn(la_hi, sel) + _dot_tn(la_lo, sel))

    def scale(i, t):
        b, bl = t['bb'][:tc], t['bb'][tc:]
        q, k = q_ref[i], k_ref[i]
        t['q_dec'] = (q * jnp.exp(b)).astype(BF16)
        t['k_inv'] = (k * jnp.exp(-b)).astype(BF16)
        t['k_end'] = (k * jnp.exp(bl - b)).astype(BF16)

    def intra_scores(i, t):
        t['a'] = [_dot_nt(t['q_dec'][:, dk[h]], t['k_inv'][:, dk[h]]) for h in heads]

    def chunk_outer(i, t):
        v = v_ref[i]
        t['x'] = [[_dot_tn(t['k_end'][rw, dk[h]], v[rw, dv[h]]) for rw in rows] for h in heads]

    def intra_out(i, t):
        v = v_ref[i]
        t['o'] = [_dot(jnp.where(causal, t['a'][h], 0.0).astype(BF16), v[:, dv[h]]) for h in heads]

    def scan(i, t):
        t['s'] = []
        for h in heads:
            s_h = st_scr[i, h]
            starts = []
            for ci in range(n_chunks):
                starts.append(s_h.astype(BF16))
                s_h = s_h * t['dec'][dk[h], ci:ci + 1] + t['x'][h][ci]
            st_scr[i, h] = s_h
            t['s'].append(starts)

    def inter_out(i, t):
        for h in heads:
            parts = [_dot(t['q_dec'][rw, dk[h]], t['s'][h][ci]) for ci, rw in enumerate(rows)]
            t['o'][h] = t['o'][h] + (jnp.concatenate(parts, axis=0) if n_chunks > 1 else parts[0])

    def finish(i, t):
        gg = gg_ref[i]
        gate = gg / (1.0 + jnp.exp(-gg))
        for h in heads:
            o_ref[i, :, dv[h]] = (_rms(t['o'][h], gout) * gate[:, dv[h]]).astype(o_ref.dtype)

    temps = [{} for _ in range(n_seq)]
    for stage in (decays, scale, intra_scores, chunk_outer, intra_out, scan, inter_out, finish):
        for i in range(n_seq):
            stage(i, temps[i])

    @pl.when(step == pl.num_programs(1) - 1)
    def _():
        s_out_ref[...] = st_scr[...]


def _gla(q, k, v, gg, la, s0, g_out, *, n_seq, seq_len, chunk, n_chunks, seqs_per_step):
    tc = chunk * n_chunks
    steps = seq_len // tc
    nb = seqs_per_step
    idx = np.arange(tc)
    same = (idx[:, None] // chunk) == (idx[None, :] // chunk)
    cum = jnp.asarray(np.concatenate([same & (idx[:, None] >= idx[None, :]), same], axis=0), BF16)
    sel = jnp.asarray((idx[:, None] // chunk) == np.arange(LANES)[None, :], BF16)
    tok = lambda width: pl.BlockSpec((nb, tc, width), lambda b, s: (b, s, 0))
    per_seq = lambda a: a.reshape(n_seq, seq_len, a.shape[-1])
    state_spec = pl.BlockSpec((nb, GLA_HEADS, GLA_DK, GLA_DV), lambda b, s: (b, 0, 0, 0))
    o, state = pl.pallas_call(
        functools.partial(_gla_kernel, chunk=chunk, n_chunks=n_chunks),
        grid=(n_seq // nb, steps),
        in_specs=[tok(GLA_QK), tok(GLA_QK), tok(GLA_V), tok(GLA_V), tok(GLA_QK), state_spec,
                  _const_spec((2 * tc, tc)), _const_spec((tc, LANES)), _const_spec((1, GLA_DV))],
        out_specs=[tok(GLA_V), state_spec],
        out_shape=[jax.ShapeDtypeStruct((n_seq, seq_len, GLA_V), BF16),
                   jax.ShapeDtypeStruct((n_seq, GLA_HEADS, GLA_DK, GLA_DV), F32)],
        scratch_shapes=[pltpu.VMEM((nb, GLA_HEADS, GLA_DK, GLA_DV), F32)],
        compiler_params=pltpu.CompilerParams(dimension_semantics=("parallel", "arbitrary"),
                                             vmem_limit_bytes=VMEM_LIMIT),
        name=f"gla_c{chunk}",
    )(per_seq(q), per_seq(k), per_seq(v), per_seq(gg), per_seq(la), s0, cum, sel, g_out)
    return o.reshape(n_seq * seq_len, GLA_V), state


def _prompt_attn_kernel(q_ref, k_ref, vT_ref, o_ref, s_scr, *, tq, tk):
    assert tq == tk
    qi = pl.program_id(2)
    key = lax.broadcasted_iota(jnp.int32, (tk, tq), 0)
    qry = lax.broadcasted_iota(jnp.int32, (tk, tq), 1)
    ones_rows = jnp.ones((2 * SUBLANES, tk), BF16)

    def scores(j, kb, masked):
        ks = pl.multiple_of(kb * tk, tk)
        s = _dot_nt(k_ref[pl.ds(ks, tk), j * HEAD_SLAB:(j + 1) * HEAD_SLAB],
                    q_ref[:, j * HEAD_SLAB:(j + 1) * HEAD_SLAB])
        if masked:
            s = jnp.where(key <= qry, s, NEG)
        s_scr[j] = s
        return jnp.max(s, axis=0, keepdims=True)

    def update(j, kb, m_blk, state):
        m, acc = state
        ks = pl.multiple_of(kb * tk, tk)
        m_new = jnp.maximum(m, m_blk)
        alpha = jnp.exp2(m - m_new)
        p = jnp.exp2(s_scr[j] - m_new)
        v_aug = jnp.concatenate([vT_ref[j * MLA_V:(j + 1) * MLA_V, pl.ds(ks, tk)], ones_rows], axis=0)
        acc = alpha * acc + _dot(v_aug, p.astype(BF16))
        return m_new, acc

    n_heads = q_ref.shape[1] // HEAD_SLAB
    depth = min(ATTN_PIPELINE_DEPTH, n_heads - 1)

    def block(kb, masked, pending, state):
        state = list(state)
        queue = [(n_heads - len(pending) + i, kb + 1, m) for i, m in enumerate(pending)]
        for j in range(n_heads):
            queue.append((j, kb, scores(j, kb, masked)))
            if len(queue) > depth:
                pj, pkb, pm = queue.pop(0)
                state[pj] = update(pj, pkb, pm, state[pj])
        return tuple(m for _, _, m in queue), tuple(state)

    init = (jnp.full((1, tq), NEG, F32), jnp.zeros((MLA_V + 2 * SUBLANES, tq), F32))
    pending, state = block(qi, True, (), (init,) * n_heads)

    def body(t, carry):
        return block(qi - 1 - t, False, *carry)

    pending, state = lax.fori_loop(0, qi, body, (pending, state))
    state = list(state)
    for i, m_blk in enumerate(pending):
        head = n_heads - depth + i
        state[head] = update(head, 0, m_blk, state[head])
    o_ref[...] = jnp.concatenate([acc[:MLA_V] / acc[MLA_V:MLA_V + 1] for _, acc in state],
                                 axis=0).T.astype(o_ref.dtype)


def _prompt_attn(qa, ka, va, *, n_seq, seq_len, tq, tk):
    nq = seq_len // tq
    hs = ATTN_HEADS_PER_STEP
    return pl.pallas_call(
        functools.partial(_prompt_attn_kernel, tq=tq, tk=tk),
        grid=(n_seq, MLA_HEADS // hs, nq),
        in_specs=[pl.BlockSpec((tq, hs * HEAD_SLAB), lambda b, hp, i: (b * nq + i, hp)),
                  pl.BlockSpec((seq_len, hs * HEAD_SLAB), lambda b, hp, i: (b, hp)),
                  pl.BlockSpec((hs * MLA_V, seq_len), lambda b, hp, i: (hp, b))],
        out_specs=pl.BlockSpec((tq, hs * MLA_V), lambda b, hp, i: (b * nq + i, hp)),
        out_shape=jax.ShapeDtypeStruct((n_seq * seq_len, MLA_VALL), BF16),
        scratch_shapes=[pltpu.VMEM((hs, tk, tq), F32)],
        compiler_params=pltpu.CompilerParams(dimension_semantics=("parallel", "parallel", "arbitrary"),
                                             vmem_limit_bytes=VMEM_LIMIT),
        name="prompt_attn",
    )(qa, ka, va)


def _sample_attn_kernel(pt_ref, qabs_ref, qrope_ref, cnew_ref, krnew_ref, wkT_ref, wv_ref, poolc_hbm, poolr_hbm,
                        o_ref, lhs_scr, m_scr, l_scr, acc_scr, cb_scr, krT_scr, s_scr, cbuf, rbuf, sem,
                        *, n_tok, pages_per_step, layer):
    G = pages_per_step
    b = pl.program_id(0)
    g = pl.program_id(1)
    n_g = pl.num_programs(1)
    step = b * n_g + g
    n_steps = pl.num_programs(0) * n_g
    last = step == n_steps - 1
    slot = step % PAGE_RING

    def group_of(s):
        s = jnp.minimum(s, n_steps - 1)
        return s // n_g, s % n_g
    rows = n_tok * MLA_HEADS
    n_k = MLA_HEADS * MLA_NOPE

    def page_copies(sl, group):
        copies = []
        for j in range(G):
            page = pt_ref[group[0], group[1] * G + j]
            copies.append(pltpu.make_async_copy(poolc_hbm.at[layer, page], cbuf.at[sl, j], sem.at[0, sl]))
            copies.append(pltpu.make_async_copy(poolr_hbm.at[layer, page], rbuf.at[sl, j], sem.at[1, sl]))
        return copies

    def wait_slot(sl):
        pltpu.make_async_copy(cbuf.at[sl], cbuf.at[sl], sem.at[0, sl]).wait()
        pltpu.make_async_copy(rbuf.at[sl], rbuf.at[sl], sem.at[1, sl]).wait()

    @pl.when(step == 0)
    def _():
        for d in range(PAGE_RING - 1):
            for cp in page_copies(d, group_of(d)):
                cp.start()

    wait_slot(slot)

    def scores(cb, krT):
        nk = cb.shape[0]
        big = _dot_nt(lhs_scr[...], cb)
        kvn = big[:n_k]
        s_nope = big[n_k:n_k + rows]
        ss = jnp.sum((kvn * kvn).reshape(MLA_HEADS, MLA_NOPE, nk), axis=1)
        inv = lax.rsqrt(ss * (1.0 / MLA_NOPE) + EPS)
        s_rope = _dot(qrope_ref[0], krT)
        return (s_nope.reshape(n_tok, MLA_HEADS, nk) * inv[None]).reshape(rows, nk) + s_rope

    def weights(s):
        m = m_scr[...]
        m_new = jnp.maximum(m, jnp.max(s, axis=-1, keepdims=True))
        alpha = jnp.exp2(m - m_new)
        p = jnp.exp2(s - m_new)
        l_scr[...] = alpha * l_scr[...] + jnp.sum(p, axis=-1, keepdims=True)
        m_scr[...] = m_new
        return alpha, p.astype(BF16)

    def accumulate(alpha, p, cb):
        acc_scr[...] = alpha * acc_scr[...] + _dot(p, cb)

    def absorb(s, cb):
        accumulate(*weights(s), cb)

    cur = g % 2
    prv = 1 - cur

    @pl.when(g == 0)
    def _():
        lhs_scr[:n_k, :] = wkT_ref[...]
        lhs_scr[n_k:n_k + rows, :] = qabs_ref[0]
        m_scr[...] = jnp.full(m_scr.shape, NEG, F32)
        l_scr[...] = jnp.zeros(l_scr.shape, F32)
        acc_scr[...] = jnp.zeros(acc_scr.shape, F32)
        s_scr[prv] = jnp.full(s_scr.shape[1:], NEG, F32)
        cb_scr[prv] = jnp.zeros(cb_scr.shape[1:], BF16)
        c_new = jnp.concatenate([cnew_ref[0], jnp.zeros((PAGE_SIZE - n_tok, MLA_KV_LORA), F32)],
                                axis=0).astype(BF16)
        key = lax.broadcasted_iota(jnp.int32, (rows, PAGE_SIZE), 1)
        tok = lax.broadcasted_iota(jnp.int32, (rows, PAGE_SIZE), 0) // MLA_HEADS
        absorb(jnp.where(key <= tok, scores(c_new, krnew_ref[0].astype(BF16)), NEG), c_new)

    for j in range(G):
        cb_scr[cur, j * PAGE_SIZE:(j + 1) * PAGE_SIZE, :] = cbuf[slot, j].astype(BF16)
        krT_scr[:, j * PAGE_SIZE:(j + 1) * PAGE_SIZE] = rbuf[slot, j].astype(BF16)

    ahead = step + (PAGE_RING - 1)
    for cp in page_copies(ahead % PAGE_RING, group_of(ahead)):
        cp.start()

    alpha, p = weights(s_scr[prv])
    s_scr[cur] = scores(cb_scr[cur], krT_scr[...])
    accumulate(alpha, p, cb_scr[prv])

    @pl.when(last)
    def _():
        for d in range(1, PAGE_RING):
            wait_slot((step + d) % PAGE_RING)

    @pl.when(g == n_g - 1)
    def _():
        absorb(s_scr[cur], cb_scr[cur])
        o_lat =(acc_scr[...] / l_scr[...]).astype(BF16)
        full = _dot(o_lat, wv_ref[...])
        full = full.reshape(n_tok, MLA_HEADS, MLA_VALL)
        hr = lax.broadcasted_iota(jnp.int32, (MLA_HEADS, MLA_VALL), 0)
        hc = lax.broadcasted_iota(jnp.int32, (MLA_HEADS, MLA_VALL), 1) // MLA_V
        o_ref[...] = jnp.sum(jnp.where((hr == hc)[None], full, 0.0), axis=1).astype(o_ref.dtype)


def _sample_attn(page_table, qabs, qrope, c_new, kr_new, w_kT, w_v, pool_c, pool_r, layer, *, pages_per_step):
    n_b, n_pages = page_table.shape
    n_tok = c_new.shape[1]
    rows = n_tok * MLA_HEADS
    G = pages_per_step
    n_k = MLA_HEADS * MLA_NOPE

    per_b = lambda b, g, pt: (b, 0, 0)
    grid_spec = pltpu.PrefetchScalarGridSpec(
        num_scalar_prefetch=1,
        grid=(n_b, n_pages // G),
        in_specs=[pl.BlockSpec((1, rows, MLA_KV_LORA), per_b), pl.BlockSpec((1, rows, MLA_ROPE), per_b),
                  pl.BlockSpec((1, n_tok, MLA_KV_LORA), per_b), pl.BlockSpec((1, MLA_ROPE, PAGE_SIZE), per_b),
                  pl.BlockSpec((n_k, MLA_KV_LORA), lambda b, g, pt: (0, 0)),
                  pl.BlockSpec((MLA_KV_LORA, MLA_VALL), lambda b, g, pt: (0, 0)),
                  pl.BlockSpec(memory_space=pl.ANY), pl.BlockSpec(memory_space=pl.ANY)],
        out_specs=pl.BlockSpec((n_tok, MLA_VALL), lambda b, g, pt: (b, 0)),
        scratch_shapes=[pltpu.VMEM((n_k + rows, MLA_KV_LORA), BF16), pltpu.VMEM((rows, 1), F32),
                        pltpu.VMEM((rows, 1), F32), pltpu.VMEM((rows, MLA_KV_LORA), F32),
                        pltpu.VMEM((2, G * PAGE_SIZE, MLA_KV_LORA), BF16), pltpu.VMEM((MLA_ROPE, G * PAGE_SIZE), BF16),
                        pltpu.VMEM((2, rows, G * PAGE_SIZE), F32),
                        pltpu.VMEM((PAGE_RING, G, PAGE_SIZE, MLA_KV_LORA), F32),
                        pltpu.VMEM((PAGE_RING, G, MLA_ROPE, PAGE_SIZE), F32),
                        pltpu.SemaphoreType.DMA((2, PAGE_RING))])
    return pl.pallas_call(
        functools.partial(_sample_attn_kernel, n_tok=n_tok, pages_per_step=G, layer=layer),
        grid_spec=grid_spec,
        out_shape=jax.ShapeDtypeStruct((n_b * n_tok, MLA_VALL), BF16),
        compiler_params=pltpu.CompilerParams(dimension_semantics=("arbitrary", "arbitrary"),
                                             vmem_limit_bytes=VMEM_LIMIT),
        name="sample_attn",
    )(page_table, qabs, qrope, c_new, kr_new, w_kT, w_v, pool_c, pool_r)


def _post_kernel(x_ref, og_ref, om_ref, p_ref, wo_ref, gffn_ref, wg_ref, wu_ref, cw_ref, cb_ref, wd_ref,
                 gple_ref, wpg_ref, wpp_ref, *rest, tiles_per_seq, seq_rows):
    paged_prev = seq_rows is not None
    if paged_prev:
        p1_ref, p2_ref, y_ref, tail_ref, acc_scr = rest
    else:
        y_ref, tail_ref, acc_scr, carry_scr = rest
    tm = x_ref.shape[0]
    i = pl.program_id(0)
    h1 = x_ref[...] + _dot(og_ref[...], wo_ref[:GLA_V, :]) + _dot(om_ref[...], wo_ref[GLA_V:, :])
    n2 = _rms(h1, gffn_ref[...]).astype(BF16)
    if not paged_prev:
        @pl.when(i % tiles_per_seq == 0)
        def _():
            carry_scr[...] = jnp.zeros(carry_scr.shape, F32)

    bounds = np.concatenate([[0], np.cumsum(FFN_CHUNKS)]).tolist()
    chunks = [slice(lo, hi) for lo, hi in zip(bounds[:-1], bounds[1:])]
    lo = bounds[-1]

    def gate_up(cols):
        return _dot(n2, wg_ref[:, cols]), _dot(n2, wu_ref[:, cols])

    nxt = gate_up(chunks[0])
    for ci, cols in enumerate(chunks):
        a, up = nxt
        if ci + 1 < len(chunks):
            nxt = gate_up(chunks[ci + 1])
        row =lax.broadcasted_iota(jnp.int32, a.shape, 0)
        slabs = [a[:, c0:c0 + 2 * LANES] for c0 in range(0, a.shape[1], 2 * LANES)]
        r1 = jnp.concatenate([pltpu.roll(sl, 1, axis=0) for sl in slabs], axis=1)
        r2 = jnp.concatenate([pltpu.roll(sl, 2, axis=0) for sl in slabs], axis=1)
        if paged_prev:
            t = row % seq_rows
            a1 = jnp.where(t >= 1, r1, p1_ref[:, cols])
            a2 = jnp.where(t >= 2, r2, p2_ref[:, cols])
            tail_ref[:, cols] = a
        else:
            prev = carry_scr[:, cols]
            pm1 = prev[SUBLANES - 1:SUBLANES, :]
            pm2 = prev[SUBLANES - 2:SUBLANES - 1, :]
            a1 = jnp.where(row >= 1, r1, pm1)
            a2 = jnp.where(row >= 2, r2, jnp.where(row == 0, pm2, pm1))
            carry_scr[:, cols] = a[tm - SUBLANES:, :]
            tail_ref[:, cols] = a[tm - SUBLANES:, :]
        conv = cb_ref[:, cols] + cw_ref[0:1, cols] * a2 + cw_ref[1:2, cols] * a1 + cw_ref[2:3, cols] * a
        gact = (conv / (1.0 + jnp.exp(-conv)) * up).astype(BF16)
        down = _dot(gact, wd_ref[cols, :])
        if ci == 0:
            acc_scr[...] = down
        else:
            acc_scr[...] += down
    assert lo == D_FF
    h2 = h1 + acc_scr[...]
    n3 =_rms(h2, gple_ref[...]).astype(BF16)
    gate = 1.0 / (1.0 + jnp.exp(-_dot(n3, wpg_ref[...])))
    y_ref[...] = h2 + _dot(p_ref[...].astype(BF16), wpp_ref[...]) * gate


def _post(x2d, og, om, p2d, w, prev=None, *, tm, tiles_per_seq, seq_rows):
    T = x2d.shape[0]
    nt = T // tm
    row = lambda i: (i, 0)
    consts = [w['w_o'], w['g_ffn'], w['w_gate'], w['w_up'], w['conv_w'], w['conv_b'], w['w_down'],
              w['g_ple'], w['w_pgate'], w['w_pproj']]
    in_specs = [pl.BlockSpec((tm, D_MODEL), row), pl.BlockSpec((tm, GLA_V), row),
                pl.BlockSpec((tm, MLA_VALL), row), pl.BlockSpec((tm, PLE_DIM), row)]
    in_specs += [_const_spec(c.shape) for c in consts]
    args = [x2d, og, om, p2d, *consts]
    scratch = [pltpu.VMEM((tm, D_MODEL), F32)]
    if prev is not None:
        in_specs += [pl.BlockSpec((tm, D_FF), row)] * 2
        args += list(prev)
        tail_rows = tm
    else:
        scratch.append(pltpu.VMEM((SUBLANES, D_FF), F32))
        tail_rows = SUBLANES
    return pl.pallas_call(
        functools.partial(_post_kernel, tiles_per_seq=tiles_per_seq, seq_rows=seq_rows),
        grid=(nt,),
        in_specs=in_specs,
        out_specs=[pl.BlockSpec((tm, D_MODEL), row), pl.BlockSpec((tail_rows, D_FF), row)],
        out_shape=[jax.ShapeDtypeStruct((T, D_MODEL), F32), jax.ShapeDtypeStruct((nt * tail_rows, D_FF), F32)],
        scratch_shapes=scratch,
        compiler_params=pltpu.CompilerParams(dimension_semantics=("arbitrary",), vmem_limit_bytes=VMEM_LIMIT),
        name="post_sample" if prev is not None else "post_prompt",
    )(*args)


def _prep_weights(g_mix, w_in, gla_w_a2, gla_b_a, gla_g_out, mla_g_qa, mla_w_qup, mla_g_qn, mla_g_qr,
                  mla_g_kva, mla_g_kr, mla_w_kvup, mla_g_kn, w_o, g_ffn, ffn_w_gate, ffn_w_up, ffn_conv_w,
                  ffn_conv_b, ffn_w_down, g_ple, ple_w_gate, ple_w_proj):
    sizes = (GLA_QK, GLA_QK, GLA_V, GLA_V, GLA_GATE_RANK, MLA_Q_LORA, MLA_KV_LORA, MLA_ROPE)
    offs = np.concatenate([[0], np.cumsum(sizes)])
    piece = lambda i: w_in[:, offs[i]:offs[i + 1]]
    zeros = lambda n: jnp.zeros((D_MODEL, n), w_in.dtype)
    misc = jnp.concatenate([piece(4), zeros(ROPE_LO - GLA_GATE_RANK), piece(7),
                            zeros(LANES - ROPE_LO - MLA_ROPE)], axis=1)
    w_in_p = jnp.concatenate([piece(0), piece(1), piece(2), piece(3), piece(5), piece(6), misc], axis=1)
    w_a2 = jnp.concatenate([gla_w_a2, jnp.zeros((LANES - GLA_GATE_RANK, GLA_QK), gla_w_a2.dtype)], axis=0)

    def slab_vec(nope, rope_):
        one = jnp.concatenate([nope, rope_, jnp.zeros((HEAD_SLAB - MLA_NOPE - MLA_ROPE,), F32)])
        return jnp.tile(one, MLA_HEADS)[None, :]

    wq = mla_w_qup.reshape(MLA_Q_LORA, MLA_HEADS, MLA_NOPE + MLA_ROPE)
    swap = np.arange(MLA_NOPE + MLA_ROPE)
    swap[MLA_NOPE:] = np.concatenate([swap[MLA_NOPE + ROPE_HALF:], swap[MLA_NOPE:MLA_NOPE + ROPE_HALF]])
    slab_pad = ((0, 0), (0, 0), (0, HEAD_SLAB - MLA_NOPE - MLA_ROPE))
    wq = jnp.concatenate([jnp.pad(wq, slab_pad).reshape(MLA_Q_LORA, MLA_SLABS),
                          jnp.pad(wq[:, :, swap], slab_pad).reshape(MLA_Q_LORA, MLA_SLABS)], axis=1)
    g_q = jnp.concatenate([mla_g_qn, mla_g_qr])
    gq_slab = jnp.pad(jnp.stack([g_q, g_q[swap]]), ((0, 0), (0, HEAD_SLAB - MLA_NOPE - MLA_ROPE)))
    wkv = mla_w_kvup.reshape(MLA_KV_LORA, MLA_HEADS, MLA_NOPE + MLA_V)
    wk = wkv[:, :, :MLA_NOPE]
    wv = wkv[:, :, MLA_NOPE:].reshape(MLA_KV_LORA, MLA_VALL)
    wk_slab = jnp.pad(wk, ((0, 0), (0, 0), (0, HEAD_SLAB - MLA_NOPE))).reshape(MLA_KV_LORA, MLA_SLABS)
    wk_g = jnp.pad(wk * mla_g_kn[None, None, :], ((0, 0), (0, 0), (0, HEAD_SLAB - MLA_NOPE)))
    eye = jnp.eye(MLA_HEADS, dtype=F32)
    w_kabs = jnp.einsum('nhd,hg->hdgn', wk_g, eye).reshape(MLA_SLABS, MLA_HEADS * MLA_KV_LORA)
    w_kT = wk.transpose(1, 2, 0).reshape(MLA_HEADS * MLA_NOPE, MLA_KV_LORA)

    lane = np.arange(2 * HEAD_SLAB)
    seg = np.where(lane % HEAD_SLAB < MLA_NOPE, 0, np.where(lane % HEAD_SLAB < MLA_NOPE + MLA_ROPE, 1, 2))
    same = (lane[:, None] // HEAD_SLAB == lane[None, :] // HEAD_SLAB) & (seg[:, None] == seg[None, :])
    segm = np.where(same & (seg[:, None] == 0), 1.0 / MLA_NOPE, np.where(same & (seg[:, None] == 1), 1.0 / MLA_ROPE, 0.0))

    gkr = jnp.concatenate([jnp.zeros((ROPE_LO,), F32), mla_g_kr, jnp.zeros((LANES - ROPE_LO - MLA_ROPE,), F32)])
    return dict(
        g_mix=g_mix[None, :], w_in=w_in_p.astype(BF16), w_a2=w_a2.astype(BF16), b_a=gla_b_a[None, :],
        g_out=gla_g_out[None, :], g_qa=mla_g_qa[None, :], w_qup=wq.astype(BF16), segm=jnp.asarray(segm, BF16),
        gq_slab=gq_slab, g_kva=mla_g_kva[None, :], w_kslab=wk_slab.astype(BF16),
        w_vT=wv.T.astype(BF16),
        gk_slab=slab_vec(mla_g_kn, jnp.zeros((MLA_ROPE,), F32)),
        gkr_slab=gkr[None, :], w_kabs=w_kabs.astype(BF16), w_kT=w_kT.astype(BF16),
        w_v=wv.astype(BF16), w_o=w_o.astype(BF16), g_ffn=g_ffn[None, :], w_gate=ffn_w_gate.astype(BF16),
        w_up=ffn_w_up.astype(BF16), conv_w=ffn_conv_w, conv_b=ffn_conv_b[None, :],
        w_down=ffn_w_down.astype(BF16), g_ple=g_ple[None, :], w_pgate=ple_w_gate.astype(BF16),
        w_pproj=ple_w_proj.astype(BF16))


def _rope_tables(pos):
    inv = ROPE_THETA ** (-jnp.arange(ROPE_HALF, dtype=F32) * 2.0 / MLA_ROPE)
    ang = pos.astype(F32)[:, None] * inv[None, :]
    cos, sin = jnp.cos(ang), jnp.sin(ang)
    T = pos.shape[0]
    pad = jnp.zeros((T, HEAD_SLAB - MLA_NOPE - MLA_ROPE), F32)
    cos_t = jnp.concatenate([jnp.ones((T, MLA_NOPE), F32), cos, cos, pad], axis=1)
    sin_t = jnp.concatenate([jnp.zeros((T, MLA_NOPE), F32), -sin, sin, pad], axis=1)
    return cos_t, sin_t


def _layer(w, x_p, x_s, p_p, p_s, pool_c, pool_r, layer, state_gla, state_conv, page_table):
    B, S, _ = x_p.shape
    Bd, Td, _ = x_s.shape
    tm = math.gcd(S, TOKEN_TILE)
    tps = S // tm
    cos_p, sin_p = _rope_tables(jnp.arange(S))
    xp2 = x_p.reshape(B * S, D_MODEL)
    gq, gk, gv, gg, la, qa, ckv_p, kr_p, ka, vaT = _in_proj(xp2, cos_p, sin_p, w, tm=tm, tiles_per_seq=tps,
                                                            sample=False)
    chunk_p = math.gcd(S, GLA_CHUNK)
    n_chunks = math.gcd(S // chunk_p, GLA_CHUNKS_PER_STEP)
    og_p, gla_p = _gla(gq, gk, gv, gg, la, jnp.zeros((B, GLA_HEADS, GLA_DK, GLA_DV), F32), w['g_out'],
                      n_seq=B, seq_len=S, chunk=chunk_p, n_chunks=n_chunks,
                      seqs_per_step=math.gcd(B, GLA_PROMPT_SEQS))
    tq = math.gcd(S, ATTN_TILE)
    om_p = _prompt_attn(qa, ka, vaT, n_seq=B, seq_len=S, tq=tq, tk=tq)
    y_p, tail_p = _post(xp2, og_p, om_p, p_p.reshape(B * S, PLE_DIM), w, tm=tm, tiles_per_seq=tps, seq_rows=None)
    conv_p = tail_p.reshape(B, tps, SUBLANES, D_FF)[:, -1, SUBLANES - (CONV_W - 1):, :]

    Ts = Bd * Td
    pos_s = PAST_LEN + jnp.arange(Td)
    cos_s, sin_s = _rope_tables(jnp.tile(pos_s, Bd))
    xs2 = x_s.reshape(Ts, D_MODEL)
    gq, gk, gv, gg, la, qa, ckv_s, kr_s, qabs = _in_proj(xs2, cos_s, sin_s, w, tm=Ts, tiles_per_seq=1, sample=True)
    chunk_s = math.gcd(Td, GLA_CHUNK)
    og_s, gla_s = _gla(gq, gk, gv, gg, la, state_gla, w['g_out'],
                      n_seq=Bd, seq_len=Td, chunk=chunk_s, n_chunks=Td // chunk_s,
                      seqs_per_step=math.gcd(Bd, GLA_SAMPLE_SEQS))
    rows = Td * MLA_HEADS
    qrope = qa.reshape(Ts, MLA_HEADS, HEAD_SLAB)[:, :, ROPE_LO:ROPE_LO + MLA_ROPE].reshape(Bd, rows, MLA_ROPE)
    krT_new = jnp.pad(jnp.swapaxes(kr_s.reshape(Bd, Td, MLA_ROPE), 1, 2), ((0, 0), (0, 0), (0, PAGE_SIZE - Td)))
    om_s = _sample_attn(page_table, qabs.reshape(Bd, rows, MLA_KV_LORA), qrope,
                        ckv_s.reshape(Bd, Td, MLA_KV_LORA), krT_new,
                        w['w_kT'], w['w_v'], pool_c, jnp.swapaxes(pool_r, 2, 3), layer,
                        pages_per_step=math.gcd(page_table.shape[1], SAMPLE_PAGES_PER_STEP))
    zpad = lambda a, lo: jnp.pad(a, ((0, 0), (lo, Td - lo - a.shape[1]), (0, 0))).reshape(Ts, D_FF)
    prev1 = zpad(state_conv[:, 1:2], 0)
    prev2 = zpad(state_conv, 0)
    y_s, a_s = _post(xs2, og_s, om_s, p_s.reshape(Ts, PLE_DIM), w, prev=(prev1, prev2), tm=Ts, tiles_per_seq=1,
                     seq_rows=Td)
    full = jnp.concatenate([state_conv, a_s.reshape(Bd, Td, D_FF)], axis=1)
    conv_s = full[:, full.shape[1] - (CONV_W - 1):]
    return (y_p.reshape(B, S, D_MODEL), y_s.reshape(Bd, Td, D_MODEL),
            ckv_p.reshape(B, S, MLA_KV_LORA), kr_p.reshape(B, S, MLA_ROPE),
            gla_p, conv_p,
            ckv_s.reshape(Bd, Td, MLA_KV_LORA), kr_s.reshape(Bd, Td, MLA_ROPE),
            gla_s, conv_s)


def kernel(x_prompt, x_sample, cache_ckv, cache_krope, state_gla, state_conv, page_table, p_prompt, p_sample, g_mix, w_in, gla_w_a2, gla_b_a, gla_g_out, mla_g_qa, mla_w_qup, mla_g_qn, mla_g_qr, mla_g_kva, mla_g_kr, mla_w_kvup, mla_g_kn, w_o, g_ffn, ffn_w_gate, ffn_w_up, ffn_conv_w, ffn_conv_b, ffn_w_down, g_ple, ple_w_gate, ple_w_proj):
    depth = w_in.shape[0]
    per_layer = (g_mix, w_in, gla_w_a2, gla_b_a, gla_g_out, mla_g_qa, mla_w_qup, mla_g_qn, mla_g_qr, mla_g_kva,
                 mla_g_kr, mla_w_kvup, mla_g_kn, w_o, g_ffn, ffn_w_gate, ffn_w_up, ffn_conv_w, ffn_conv_b,
                 ffn_w_down, g_ple, ple_w_gate, ple_w_proj)
    y_p, y_s = x_prompt, x_sample
    outs = [[] for _ in range(8)]
    for i in range(depth):
        w = _prep_weights(*(a[i] for a in per_layer))
        res = _layer(w, y_p, y_s, p_prompt[i], p_sample[i], cache_ckv, cache_krope, i, state_gla[i],
                     state_conv[i], page_table)
        y_p, y_s = res[0], res[1]
        for lst, r in zip(outs, res[2:]):
            lst.append(r)
    return (y_p, y_s) + tuple(jnp.stack(lst) for lst in outs)
```

```python
import functools
import math

import jax
import jax.numpy as jnp
import numpy as np
from jax import lax
from jax.experimental import pallas as pl
from jax.experimental.pallas import tpu as pltpu

D_MODEL = 1024
PAST_LEN = 16384
PAGE_SIZE = 128
GLA_HEADS = 4
GLA_DK = 64
GLA_DV = 128
GLA_GATE_RANK = 16
GLA_GATE_NORM = 16.0
GLA_CHUNK = 32
MLA_HEADS = 8
MLA_Q_LORA = 256
MLA_KV_LORA = 128
MLA_NOPE = 64
MLA_ROPE = 32
MLA_V = 64
MLA_SCALE = (MLA_NOPE + MLA_ROPE) ** -0.5
LOG2E = math.log2(math.e)
ROPE_THETA = 10000.0
D_FF = 2816
CONV_W = 3
PLE_DIM = 256
EPS = 1e-6
NEG = -1e30

LANES = 128
SUBLANES = 8
HEAD_SLAB = LANES
ROPE_LO = MLA_NOPE
ROPE_HALF = MLA_ROPE // 2
GLA_QK = GLA_HEADS * GLA_DK
GLA_V = GLA_HEADS * GLA_DV
MLA_SLABS = MLA_HEADS * HEAD_SLAB
MLA_VALL = MLA_HEADS * MLA_V
C_GQ, C_GK, C_GV, C_GG = 0, 256, 512, 1024
C_MQ, C_MKV, C_MISC = 1536, 1792, 1920
IN_COLS_P = 2048

TOKEN_TILE = 512
FFN_CHUNKS = (1536, 1280)
ATTN_TILE = 512
ATTN_HEADS_PER_STEP = 8
ATTN_PIPELINE_DEPTH = 1
GLA_CHUNKS_PER_STEP = 8
GLA_PROMPT_SEQS = 4
GLA_SAMPLE_SEQS = 8
SAMPLE_PAGES_PER_STEP = 64
PAGE_RING = 3
VMEM_LIMIT = 56 * 1024 * 1024

BF16 = jnp.bfloat16
F32 = jnp.float32


def _dot(a, b):
    return jnp.dot(a, b, preferred_element_type=F32)


def _dot_nt(a, b):
    return lax.dot_general(a, b, (((1,), (1,)), ((), ())), preferred_element_type=F32)


def _dot_tn(a, b):
    return lax.dot_general(a, b, (((0,), (0,)), ((), ())), preferred_element_type=F32)


def _rms(x, g):
    return x * lax.rsqrt(jnp.mean(x * x, axis=-1, keepdims=True) + EPS) * g


def _const_spec(shape):
    nd = len(shape)
    return pl.BlockSpec(shape, lambda *_: (0,) * nd, pipeline_mode=pl.Buffered(1))


def _rope_slab(y, cos_t, sin_t):
    lane = lax.broadcasted_iota(jnp.int32, y.shape, 1)
    swapped = jnp.where(lane < ROPE_LO + ROPE_HALF,
                        pltpu.roll(y, LANES - ROPE_HALF, axis=1),
                        pltpu.roll(y, ROPE_HALF, axis=1))
    return y * cos_t + swapped * sin_t


def _in_proj_kernel(x_ref, cos_ref, sin_ref, gmix_ref, win_ref, wa2_ref, ba_ref, gqa_ref, wqup_ref,
                    segm_ref, gq_ref, gkva_ref, gkr_ref, *rest, sample):
    if sample:
        (wkabs_ref, gq_o, gk_o, gv_o, gg_o, la_o, qa_o, ckv_o, kr_o, qabs_o) = rest
    else:
        (wk_ref, gk_ref, wvT_ref, gq_o, gk_o, gv_o, gg_o, la_o, qa_o, ckv_o, kr_o, ka_o, vaT_o) = rest
    segm = segm_ref[...]
    q_gain = gq_ref[...]

    cos_t = cos_ref[...]
    sin_t = sin_ref[...]
    n = _rms(x_ref[...], gmix_ref[...]).astype(BF16)

    def proj(lo, width):
        return _dot(n, win_ref[:, lo:lo + width])

    um = proj(C_MQ, IN_COLS_P - C_MQ)
    ug = proj(C_GQ, 2 * GLA_QK)
    gq_o[...] = ug[:, :GLA_QK] * (GLA_DK ** -0.5)
    gk_o[...] = ug[:, GLA_QK:]
    misc = um[:, C_MISC - C_MQ:]
    z = _dot(misc.astype(BF16), wa2_ref[...]) + ba_ref[...]
    la_o[...] = (jnp.minimum(z, 0.0) - jnp.log1p(jnp.exp(-jnp.abs(z)))) * (1.0 / GLA_GATE_NORM)
    lane = lax.broadcasted_iota(jnp.int32, misc.shape, 1)
    is_rope = (lane >= ROPE_LO) & (lane < ROPE_LO + MLA_ROPE)
    ms_r = jnp.sum(jnp.where(is_rope, misc * misc, 0.0), axis=-1, keepdims=True) * (1.0 / MLA_ROPE)
    kr_slab = _rope_slab(misc * lax.rsqrt(ms_r + EPS) * gkr_ref[...], cos_t, sin_t)
    kr_o[...] = kr_slab[:, ROPE_LO:ROPE_LO + MLA_ROPE]
    ckv = _rms(um[:, C_MKV - C_MQ:C_MISC - C_MQ], gkva_ref[...])
    ckv_o[...] = ckv
    ckv_b = ckv.astype(BF16)
    cq = _rms(um[:, :MLA_Q_LORA], gqa_ref[...]).astype(BF16)

    if not sample:
        kn = _dot(ckv_b, wk_ref[...])
        vaT_o[...] = _dot_nt(wvT_ref[...], ckv_b).astype(BF16)
    qf = _dot(cq, wqup_ref[...])
    q_cos = cos_t * (q_gain[0:1] * (MLA_SCALE * LOG2E))
    q_sin = sin_t * (q_gain[1:2] * (MLA_SCALE * LOG2E))

    def head_pair(pair):
        lo = pair * 2 * HEAD_SLAB
        q2 = qf[:, lo:lo + 2 * HEAD_SLAB]
        q_inv = lax.rsqrt(_dot((q2 * q2).astype(BF16), segm) + EPS)
        if not sample:
            k2 = kn[:, lo:lo + 2 * HEAD_SLAB]
            k2 = k2 * lax.rsqrt(_dot((k2 * k2).astype(BF16), segm) + EPS) * gk_ref[:, lo:lo + 2 * HEAD_SLAB]
        for j in range(2):
            sl = slice(lo + j * HEAD_SLAB, lo + (j + 1) * HEAD_SLAB)
            swapped = qf[:, MLA_SLABS + lo + j * HEAD_SLAB:MLA_SLABS + lo + (j + 1) * HEAD_SLAB]
            qh = q_inv[:, j * HEAD_SLAB:(j + 1) * HEAD_SLAB] * (qf[:, sl] * q_cos + swapped * q_sin)
            qa_o[:, sl] = qh.astype(BF16)
            if not sample:
                ka_o[:, sl] = (k2[:, j * HEAD_SLAB:(j + 1) * HEAD_SLAB] + kr_slab).astype(BF16)

    half = GLA_V // 2
    gv_o[:, :half] = proj(C_GV, half).astype(BF16)
    head_pair(0)
    gv_o[:, half:] = proj(C_GV + half, half).astype(BF16)
    head_pair(1)
    gg_o[:, :half] = proj(C_GG, half)
    head_pair(2)
    gg_o[:, half:] = proj(C_GG + half, half)
    head_pair(3)
    if sample:
        qabs_o[...] = _dot(qa_o[...], wkabs_ref[...]).astype(BF16)


def _in_proj(x2d, cos_t, sin_t, w, *, tm, tiles_per_seq, sample):
    T = x2d.shape[0]
    nt = T // tm
    row = lambda i: (i, 0)
    pos = lambda i: (i % tiles_per_seq, 0)
    consts = [w['g_mix'], w['w_in'], w['w_a2'], w['b_a'], w['g_qa'], w['w_qup'], w['segm'], w['gq_slab'],
              w['g_kva'], w['gkr_slab']]
    consts += [w['w_kabs']] if sample else [w['w_kslab'], w['gk_slab'], w['w_vT']]
    in_specs = [pl.BlockSpec((tm, D_MODEL), row), pl.BlockSpec((tm, LANES), pos), pl.BlockSpec((tm, LANES), pos)]
    in_specs += [_const_spec(c.shape) for c in consts]
    outs = [(GLA_QK, F32), (GLA_QK, F32), (GLA_V, BF16), (GLA_V, F32), (GLA_QK, F32),
            (MLA_SLABS, BF16), (MLA_KV_LORA, F32), (MLA_ROPE, F32), (MLA_SLABS, BF16)]
    out_specs = [pl.BlockSpec((tm, c), row) for c, _ in outs]
    out_shape = [jax.ShapeDtypeStruct((T, c), d) for c, d in outs]
    if not sample:
        out_specs.append(pl.BlockSpec((MLA_VALL, tm), lambda i: (0, i)))
        out_shape.append(jax.ShapeDtypeStruct((MLA_VALL, T), BF16))
    return pl.pallas_call(
        functools.partial(_in_proj_kernel, sample=sample),
        grid=(nt,),
        in_specs=in_specs,
        out_specs=out_specs,
        out_shape=out_shape,
        compiler_params=pltpu.CompilerParams(dimension_semantics=("parallel",), vmem_limit_bytes=VMEM_LIMIT),
        name="in_proj_sample" if sample else "in_proj_prompt",
    )(x2d, cos_t, sin_t, *consts)


def _gla_kernel(q_ref, k_ref, v_ref, gg_ref, la_ref, s0_ref, cum_ref, sel_ref, gout_ref,
                o_ref, s_out_ref, st_scr, *, chunk, n_chunks):
    step = pl.program_id(1)
    n_seq = q_ref.shape[0]
    tc = chunk * n_chunks
    heads = range(GLA_HEADS)
    dk = [slice(h * GLA_DK, (h + 1) * GLA_DK) for h in heads]
    dv = [slice(h * GLA_DV, (h + 1) * GLA_DV) for h in heads]
    rows = [slice(ci * chunk, (ci + 1) * chunk) for ci in range(n_chunks)]

    @pl.when(step == 0)
    def _():
        st_scr[...] = s0_ref[...]

    cum = cum_ref[...]
    sel = sel_ref[...]
    r = lax.broadcasted_iota(jnp.int32, (tc, tc), 0)
    c = lax.broadcasted_iota(jnp.int32, (tc, tc), 1)
    causal = (r >= c) & ((r // chunk) == (c // chunk))
    gout = gout_ref[...]

    def decays(i, t):
        la = la_ref[i]
        la_hi = la.astype(BF16)
        la_lo = (la - la_hi.astype(F32)).astype(BF16)
        t['bb'] = _dot(cum, la_hi) + _dot(cum, la_lo)
        t['dec'] = jnp.exp(_dot_tn(la_hi, sel) + _dot_tn(la_lo, sel))

    def scale(i, t):
        b, bl = t['bb'][:tc], t['bb'][tc:]
        q, k = q_ref[i], k_ref[i]
        t['q_dec'] = (q * jnp.exp(b)).astype(BF16)
        t['k_inv'] = (k * jnp.exp(-b)).astype(BF16)
        t['k_end'] = (k * jnp.exp(bl - b)).astype(BF16)

    def intra_scores(i, t):
        t['a'] = [_dot_nt(t['q_dec'][:, dk[h]], t['k_inv'][:, dk[h]]) for h in heads]

    def chunk_outer(i, t):
        v = v_ref[i]
        t['x'] = [[_dot_tn(t['k_end'][rw, dk[h]], v[rw, dv[h]]) for rw in rows] for h in heads]

    def intra_out(i, t):
        v = v_ref[i]
        t['o'] = [_dot(jnp.where(causal, t['a'][h], 0.0).astype(BF16), v[:, dv[h]]) for h in heads]

    def scan(i, t):
        t['s'] = []
        for h in heads:
            s_h = st_scr[i, h]
            starts = []
            for ci in range(n_chunks):
                starts.append(s_h.astype(BF16))
                s_h = s_h * t['dec'][dk[h], ci:ci + 1] + t['x'][h][ci]
            st_scr[i, h] = s_h
            t['s'].append(starts)

    def inter_out(i, t):
        for h in heads:
            parts = [_dot(t['q_dec'][rw, dk[h]], t['s'][h][ci]) for ci, rw in enumerate(rows)]
            t['o'][h] = t['o'][h] + (jnp.concatenate(parts, axis=0) if n_chunks > 1 else parts[0])

    def finish(i, t):
        gg = gg_ref[i]
        gate = gg / (1.0 + jnp.exp(-gg))
        for h in heads:
            o_ref[i, :, dv[h]] = (_rms(t['o'][h], gout) * gate[:, dv[h]]).astype(o_ref.dtype)

    temps = [{} for _ in range(n_seq)]
    for stage in (decays, scale, intra_scores, chunk_outer, intra_out, scan, inter_out, finish):
        for i in range(n_seq):
            stage(i, temps[i])

    @pl.when(step == pl.num_programs(1) - 1)
    def _():
        s_out_ref[...] = st_scr[...]


def _gla(q, k, v, gg, la, s0, g_out, *, n_seq, seq_len, chunk, n_chunks, seqs_per_step):
    tc = chunk * n_chunks
    steps = seq_len // tc
    nb = seqs_per_step
    idx = np.arange(tc)
    same = (idx[:, None] // chunk) == (idx[None, :] // chunk)
    cum = jnp.asarray(np.concatenate([same & (idx[:, None] >= idx[None, :]), same], axis=0), BF16)
    sel = jnp.asarray((idx[:, None] // chunk) == np.arange(LANES)[None, :], BF16)
    tok = lambda width: pl.BlockSpec((nb, tc, width), lambda b, s: (b, s, 0))
    per_seq = lambda a: a.reshape(n_seq, seq_len, a.shape[-1])
    state_spec = pl.BlockSpec((nb, GLA_HEADS, GLA_DK, GLA_DV), lambda b, s: (b, 0, 0, 0))
    o, state = pl.pallas_call(
        functools.partial(_gla_kernel, chunk=chunk, n_chunks=n_chunks),
        grid=(n_seq // nb, steps),
        in_specs=[tok(GLA_QK), tok(GLA_QK), tok(GLA_V), tok(GLA_V), tok(GLA_QK), state_spec,
                  _const_spec((2 * tc, tc)), _const_spec((tc, LANES)), _const_spec((1, GLA_DV))],
        out_specs=[tok(GLA_V), state_spec],
        out_shape=[jax.ShapeDtypeStruct((n_seq, seq_len, GLA_V), BF16),
                   jax.ShapeDtypeStruct((n_seq, GLA_HEADS, GLA_DK, GLA_DV), F32)],
        scratch_shapes=[pltpu.VMEM((nb, GLA_HEADS, GLA_DK, GLA_DV), F32)],
        compiler_params=pltpu.CompilerParams(dimension_semantics=("parallel", "arbitrary"),
                                             vmem_limit_bytes=VMEM_LIMIT),
        name=f"gla_c{chunk}",
    )(per_seq(q), per_seq(k), per_seq(v), per_seq(gg), per_seq(la), s0, cum, sel, g_out)
    return o.reshape(n_seq * seq_len, GLA_V), state


def _prompt_attn_kernel(q_ref, k_ref, vT_ref, o_ref, s_scr, *, tq, tk):
    for qi in range(q_ref.shape[0] // tq):
        _attn_tile(q_ref, k_ref, vT_ref, o_ref, s_scr, qi, tq=tq, tk=tk)


def _attn_tile(q_ref, k_ref, vT_ref, o_ref, s_scr, qi, *, tq, tk):
    assert tq == tk
    key = lax.broadcasted_iota(jnp.int32, (tk, tq), 0)
    qry = lax.broadcasted_iota(jnp.int32, (tk, tq), 1)
    ones_rows = jnp.ones((2 * SUBLANES, tk), BF16)

    def scores(j, kb, masked):
        ks = kb * tk if isinstance(kb, int) else pl.multiple_of(kb * tk, tk)
        s = _dot_nt(k_ref[pl.ds(ks, tk), j * HEAD_SLAB:(j + 1) * HEAD_SLAB],
                    q_ref[qi * tq:(qi + 1) * tq, j * HEAD_SLAB:(j + 1) * HEAD_SLAB])
        if masked:
            s = jnp.where(key <= qry, s, NEG)
        s_scr[j] = s
        return jnp.max(s, axis=0, keepdims=True)

    def update(j, kb, m_blk, state):
        m, acc = state
        ks = kb * tk if isinstance(kb, int) else pl.multiple_of(kb * tk, tk)
        m_new = jnp.maximum(m, m_blk)
        alpha = jnp.exp2(m - m_new)
        p = jnp.exp2(s_scr[j] - m_new)
        v_aug = jnp.concatenate([vT_ref[j * MLA_V:(j + 1) * MLA_V, pl.ds(ks, tk)], ones_rows], axis=0)
        acc = alpha * acc + _dot(v_aug, p.astype(BF16))
        return m_new, acc

    n_heads = q_ref.shape[1] // HEAD_SLAB
    depth = min(ATTN_PIPELINE_DEPTH, n_heads - 1)

    def block(kb, masked, pending, state):
        state = list(state)
        queue = [(n_heads - len(pending) + i, kb + 1, m) for i, m in enumerate(pending)]
        for j in range(n_heads):
            queue.append((j, kb, scores(j, kb, masked)))
            if len(queue) > depth:
                pj, pkb, pm = queue.pop(0)
                state[pj] = update(pj, pkb, pm, state[pj])
        return tuple(m for _, _, m in queue), tuple(state)

    init = (jnp.full((1, tq), NEG, F32), jnp.zeros((MLA_V + 2 * SUBLANES, tq), F32))
    pending, state = block(qi, True, (), (init,) * n_heads)

    def body(t, carry):
        return block(qi - 1 - t, False, *carry)

    pending, state = lax.fori_loop(0, qi, body, (pending, state))
    state = list(state)
    for i, m_blk in enumerate(pending):
        head = n_heads - depth + i
        state[head] = update(head, 0, m_blk, state[head])
    o_ref[qi * tq:(qi + 1) * tq, :] = jnp.concatenate([acc[:MLA_V] / acc[MLA_V:MLA_V + 1] for _, acc in state],
                                 axis=0).T.astype(o_ref.dtype)


def _prompt_attn(qa, ka, va, *, n_seq, seq_len, tq, tk):
    hs = ATTN_HEADS_PER_STEP
    return pl.pallas_call(
        functools.partial(_prompt_attn_kernel, tq=tq, tk=tk),
        grid=(n_seq, MLA_HEADS // hs),
        in_specs=[pl.BlockSpec((seq_len, hs * HEAD_SLAB), lambda b, hp: (b, hp)),
                  pl.BlockSpec((seq_len, hs * HEAD_SLAB), lambda b, hp: (b, hp)),
                  pl.BlockSpec((hs * MLA_V, seq_len), lambda b, hp: (hp, b))],
        out_specs=pl.BlockSpec((seq_len, hs * MLA_V), lambda b, hp: (b, hp)),
        out_shape=jax.ShapeDtypeStruct((n_seq * seq_len, MLA_VALL), BF16),
        scratch_shapes=[pltpu.VMEM((hs, tk, tq), F32)],
        compiler_params=pltpu.CompilerParams(dimension_semantics=("parallel", "parallel"),
                                             vmem_limit_bytes=VMEM_LIMIT),
        name="prompt_attn",
    )(qa, ka, va)


def _sample_attn_kernel(pt_ref, qabs_ref, qrope_ref, cnew_ref, krnew_ref, wkT_ref, wv_ref, poolc_hbm, poolr_hbm,
                        o_ref, lhs_scr, m_scr, l_scr, acc_scr, cb_scr, krT_scr, s_scr, cbuf, rbuf, sem,
                        *, n_tok, pages_per_step, layer):
    G = pages_per_step
    b = pl.program_id(0)
    g = pl.program_id(1)
    n_g = pl.num_programs(1)
    step = b * n_g + g
    n_steps = pl.num_programs(0) * n_g
    last = step == n_steps - 1
    slot = step % PAGE_RING

    def group_of(s):
        s = jnp.minimum(s, n_steps - 1)
        return s // n_g, s % n_g
    rows = n_tok * MLA_HEADS
    n_k = MLA_HEADS * MLA_NOPE

    def page_copies(sl, group):
        copies = []
        for j in range(G):
            page = pt_ref[group[0], group[1] * G + j]
            copies.append(pltpu.make_async_copy(poolc_hbm.at[layer, page], cbuf.at[sl, j], sem.at[0, sl]))
            copies.append(pltpu.make_async_copy(poolr_hbm.at[layer, page], rbuf.at[sl, j], sem.at[1, sl]))
        return copies

    def wait_slot(sl):
        pltpu.make_async_copy(cbuf.at[sl], cbuf.at[sl], sem.at[0, sl]).wait()
        pltpu.make_async_copy(rbuf.at[sl], rbuf.at[sl], sem.at[1, sl]).wait()

    @pl.when(step == 0)
    def _():
        for d in range(PAGE_RING - 1):
            for cp in page_copies(d, group_of(d)):
                cp.start()

    wait_slot(slot)

    def scores(cb, krT):
        nk = cb.shape[0]
        big = _dot_nt(lhs_scr[...], cb)
        kvn = big[:n_k]
        s_nope = big[n_k:n_k + rows]
        ss = jnp.sum((kvn * kvn).reshape(MLA_HEADS, MLA_NOPE, nk), axis=1)
        inv = lax.rsqrt(ss * (1.0 / MLA_NOPE) + EPS)
        s_rope = _dot(qrope_ref[0], krT)
        return (s_nope.reshape(n_tok, MLA_HEADS, nk) * inv[None]).reshape(rows, nk) + s_rope

    def weights(s):
        m = m_scr[...]
        m_new = jnp.maximum(m, jnp.max(s, axis=-1, keepdims=True))
        alpha = jnp.exp2(m - m_new)
        p = jnp.exp2(s - m_new)
        l_scr[...] = alpha * l_scr[...] + jnp.sum(p, axis=-1, keepdims=True)
        m_scr[...] = m_new
        return alpha, p.astype(BF16)

    def accumulate(alpha, p, cb):
        acc_scr[...] = alpha * acc_scr[...] + _dot(p, cb)

    def absorb(s, cb):
        accumulate(*weights(s), cb)

    cur = g % 2
    prv = 1 - cur

    @pl.when(g == 0)
    def _():
        lhs_scr[:n_k, :] = wkT_ref[...]
        lhs_scr[n_k:n_k + rows, :] = qabs_ref[0]
        m_scr[...] = jnp.full(m_scr.shape, NEG, F32)
        l_scr[...] = jnp.zeros(l_scr.shape, F32)
        acc_scr[...] = jnp.zeros(acc_scr.shape, F32)
        s_scr[prv] = jnp.full(s_scr.shape[1:], NEG, F32)
        cb_scr[prv] = jnp.zeros(cb_scr.shape[1:], BF16)
        c_new = jnp.concatenate([cnew_ref[0], jnp.zeros((PAGE_SIZE - n_tok, MLA_KV_LORA), F32)],
                                axis=0).astype(BF16)
        key = lax.broadcasted_iota(jnp.int32, (rows, PAGE_SIZE), 1)
        tok = lax.broadcasted_iota(jnp.int32, (rows, PAGE_SIZE), 0) // MLA_HEADS
        absorb(jnp.where(key <= tok, scores(c_new, krnew_ref[0].astype(BF16)), NEG), c_new)

    for j in range(G):
        cb_scr[cur, j * PAGE_SIZE:(j + 1) * PAGE_SIZE, :] = cbuf[slot, j].astype(BF16)
        krT_scr[:, j * PAGE_SIZE:(j + 1) * PAGE_SIZE] = rbuf[slot, j].astype(BF16)

    ahead = step + (PAGE_RING - 1)
    for cp in page_copies(ahead % PAGE_RING, group_of(ahead)):
        cp.start()

    alpha, p = weights(s_scr[prv])
    s_scr[cur] = scores(cb_scr[cur], krT_scr[...])
    accumulate(alpha, p, cb_scr[prv])

    @pl.when(last)
    def _():
        for d in range(1, PAGE_RING):
            wait_slot((step + d) % PAGE_RING)

    @pl.when(g == n_g - 1)
    def _():
        absorb(s_scr[cur], cb_scr[cur])
        o_lat =(acc_scr[...] / l_scr[...]).astype(BF16)
        full = _dot(o_lat, wv_ref[...])
        full = full.reshape(n_tok, MLA_HEADS, MLA_VALL)
        hr = lax.broadcasted_iota(jnp.int32, (MLA_HEADS, MLA_VALL), 0)
        hc = lax.broadcasted_iota(jnp.int32, (MLA_HEADS, MLA_VALL), 1) // MLA_V
        o_ref[...] = jnp.sum(jnp.where((hr == hc)[None], full, 0.0), axis=1).astype(o_ref.dtype)


def _sample_attn(page_table, qabs, qrope, c_new, kr_new, w_kT, w_v, pool_c, pool_r, layer, *, pages_per_step):
    n_b, n_pages = page_table.shape
    n_tok = c_new.shape[1]
    rows = n_tok * MLA_HEADS
    G = pages_per_step
    n_k = MLA_HEADS * MLA_NOPE

    per_b = lambda b, g, pt: (b, 0, 0)
    grid_spec = pltpu.PrefetchScalarGridSpec(
        num_scalar_prefetch=1,
        grid=(n_b, n_pages // G),
        in_specs=[pl.BlockSpec((1, rows, MLA_KV_LORA), per_b), pl.BlockSpec((1, rows, MLA_ROPE), per_b),
                  pl.BlockSpec((1, n_tok, MLA_KV_LORA), per_b), pl.BlockSpec((1, MLA_ROPE, PAGE_SIZE), per_b),
                  pl.BlockSpec((n_k, MLA_KV_LORA), lambda b, g, pt: (0, 0)),
                  pl.BlockSpec((MLA_KV_LORA, MLA_VALL), lambda b, g, pt: (0, 0)),
                  pl.BlockSpec(memory_space=pl.ANY), pl.BlockSpec(memory_space=pl.ANY)],
        out_specs=pl.BlockSpec((n_tok, MLA_VALL), lambda b, g, pt: (b, 0)),
        scratch_shapes=[pltpu.VMEM((n_k + rows, MLA_KV_LORA), BF16), pltpu.VMEM((rows, 1), F32),
                        pltpu.VMEM((rows, 1), F32), pltpu.VMEM((rows, MLA_KV_LORA), F32),
                        pltpu.VMEM((2, G * PAGE_SIZE, MLA_KV_LORA), BF16), pltpu.VMEM((MLA_ROPE, G * PAGE_SIZE), BF16),
                        pltpu.VMEM((2, rows, G * PAGE_SIZE), F32),
                        pltpu.VMEM((PAGE_RING, G, PAGE_SIZE, MLA_KV_LORA), F32),
                        pltpu.VMEM((PAGE_RING, G, MLA_ROPE, PAGE_SIZE), F32),
                        pltpu.SemaphoreType.DMA((2, PAGE_RING))])
    return pl.pallas_call(
        functools.partial(_sample_attn_kernel, n_tok=n_tok, pages_per_step=G, layer=layer),
        grid_spec=grid_spec,
        out_shape=jax.ShapeDtypeStruct((n_b * n_tok, MLA_VALL), BF16),
        compiler_params=pltpu.CompilerParams(dimension_semantics=("arbitrary", "arbitrary"),
                                             vmem_limit_bytes=VMEM_LIMIT),
        name="sample_attn",
    )(page_table, qabs, qrope, c_new, kr_new, w_kT, w_v, pool_c, pool_r)


def _post_kernel(x_ref, og_ref, om_ref, p_ref, wo_ref, gffn_ref, wg_ref, wu_ref, cw_ref, cb_ref, wd_ref,
                 gple_ref, wpg_ref, wpp_ref, *rest, tiles_per_seq, seq_rows):
    paged_prev = seq_rows is not None
    if paged_prev:
        p1_ref, p2_ref, y_ref, tail_ref, acc_scr = rest
    else:
        y_ref, tail_ref, acc_scr, carry_scr = rest
    tm = x_ref.shape[0]
    i = pl.program_id(0)
    h1 = x_ref[...] + _dot(og_ref[...], wo_ref[:GLA_V, :]) + _dot(om_ref[...], wo_ref[GLA_V:, :])
    n2 = _rms(h1, gffn_ref[...]).astype(BF16)
    if not paged_prev:
        @pl.when(i % tiles_per_seq == 0)
        def _():
            carry_scr[...] = jnp.zeros(carry_scr.shape, F32)

    bounds = np.concatenate([[0], np.cumsum(FFN_CHUNKS)]).tolist()
    chunks = [slice(lo, hi) for lo, hi in zip(bounds[:-1], bounds[1:])]
    lo = bounds[-1]

    def gate_up(cols):
        return _dot(n2, wg_ref[:, cols]), _dot(n2, wu_ref[:, cols])

    nxt = gate_up(chunks[0])
    for ci, cols in enumerate(chunks):
        a, up = nxt
        if ci + 1 < len(chunks):
            nxt = gate_up(chunks[ci + 1])
        row =lax.broadcasted_iota(jnp.int32, a.shape, 0)
        slabs = [a[:, c0:c0 + 2 * LANES] for c0 in range(0, a.shape[1], 2 * LANES)]
        r1 = jnp.concatenate([pltpu.roll(sl, 1, axis=0) for sl in slabs], axis=1)
        r2 = jnp.concatenate([pltpu.roll(sl, 2, axis=0) for sl in slabs], axis=1)
        if paged_prev:
            t = row % seq_rows
            a1 = jnp.where(t >= 1, r1, p1_ref[:, cols])
            a2 = jnp.where(t >= 2, r2, p2_ref[:, cols])
            tail_ref[:, cols] = a
        else:
            prev = carry_scr[:, cols]
            pm1 = prev[SUBLANES - 1:SUBLANES, :]
            pm2 = prev[SUBLANES - 2:SUBLANES - 1, :]
            a1 = jnp.where(row >= 1, r1, pm1)
            a2 = jnp.where(row >= 2, r2, jnp.where(row == 0, pm2, pm1))
            carry_scr[:, cols] = a[tm - SUBLANES:, :]
            tail_ref[:, cols] = a[tm - SUBLANES:, :]
        conv = cb_ref[:, cols] + cw_ref[0:1, cols] * a2 + cw_ref[1:2, cols] * a1 + cw_ref[2:3, cols] * a
        gact = (conv / (1.0 + jnp.exp(-conv)) * up).astype(BF16)
        down = _dot(gact, wd_ref[cols, :])
        if ci == 0:
            acc_scr[...] = down
        else:
            acc_scr[...] += down
    assert lo == D_FF
    h2 = h1 + acc_scr[...]
    n3 =_rms(h2, gple_ref[...]).astype(BF16)
    gate = 1.0 / (1.0 + jnp.exp(-_dot(n3, wpg_ref[...])))
    y_ref[...] = h2 + _dot(p_ref[...].astype(BF16), wpp_ref[...]) * gate


def _post(x2d, og, om, p2d, w, prev=None, *, tm, tiles_per_seq, seq_rows):
    T = x2d.shape[0]
    nt = T // tm
    row = lambda i: (i, 0)
    consts = [w['w_o'], w['g_ffn'], w['w_gate'], w['w_up'], w['conv_w'], w['conv_b'], w['w_down'],
              w['g_ple'], w['w_pgate'], w['w_pproj']]
    in_specs = [pl.BlockSpec((tm, D_MODEL), row), pl.BlockSpec((tm, GLA_V), row),
                pl.BlockSpec((tm, MLA_VALL), row), pl.BlockSpec((tm, PLE_DIM), row)]
    in_specs += [_const_spec(c.shape) for c in consts]
    args = [x2d, og, om, p2d, *consts]
    scratch = [pltpu.VMEM((tm, D_MODEL), F32)]
    if prev is not None:
        in_specs += [pl.BlockSpec((tm, D_FF), row)] * 2
        args += list(prev)
        tail_rows = tm
    else:
        scratch.append(pltpu.VMEM((SUBLANES, D_FF), F32))
        tail_rows = SUBLANES
    return pl.pallas_call(
        functools.partial(_post_kernel, tiles_per_seq=tiles_per_seq, seq_rows=seq_rows),
        grid=(nt,),
        in_specs=in_specs,
        out_specs=[pl.BlockSpec((tm, D_MODEL), row), pl.BlockSpec((tail_rows, D_FF), row)],
        out_shape=[jax.ShapeDtypeStruct((T, D_MODEL), F32), jax.ShapeDtypeStruct((nt * tail_rows, D_FF), F32)],
        scratch_shapes=scratch,
        compiler_params=pltpu.CompilerParams(dimension_semantics=("arbitrary",), vmem_limit_bytes=VMEM_LIMIT),
        name="post_sample" if prev is not None else "post_prompt",
    )(*args)


def _prep_weights(g_mix, w_in, gla_w_a2, gla_b_a, gla_g_out, mla_g_qa, mla_w_qup, mla_g_qn, mla_g_qr,
                  mla_g_kva, mla_g_kr, mla_w_kvup, mla_g_kn, w_o, g_ffn, ffn_w_gate, ffn_w_up, ffn_conv_w,
                  ffn_conv_b, ffn_w_down, g_ple, ple_w_gate, ple_w_proj):
    sizes = (GLA_QK, GLA_QK, GLA_V, GLA_V, GLA_GATE_RANK, MLA_Q_LORA, MLA_KV_LORA, MLA_ROPE)
    offs = np.concatenate([[0], np.cumsum(sizes)])
    piece = lambda i: w_in[:, offs[i]:offs[i + 1]]
    zeros = lambda n: jnp.zeros((D_MODEL, n), w_in.dtype)
    misc = jnp.concatenate([piece(4), zeros(ROPE_LO - GLA_GATE_RANK), piece(7),
                            zeros(LANES - ROPE_LO - MLA_ROPE)], axis=1)
    w_in_p = jnp.concatenate([piece(0), piece(1), piece(2), piece(3), piece(5), piece(6), misc], axis=1)
    w_a2 = jnp.concatenate([gla_w_a2, jnp.zeros((LANES - GLA_GATE_RANK, GLA_QK), gla_w_a2.dtype)], axis=0)

    def slab_vec(nope, rope_):
        one = jnp.concatenate([nope, rope_, jnp.zeros((HEAD_SLAB - MLA_NOPE - MLA_ROPE,), F32)])
        return jnp.tile(one, MLA_HEADS)[None, :]

    wq = mla_w_qup.reshape(MLA_Q_LORA, MLA_HEADS, MLA_NOPE + MLA_ROPE)
    swap = np.arange(MLA_NOPE + MLA_ROPE)
    swap[MLA_NOPE:] = np.concatenate([swap[MLA_NOPE + ROPE_HALF:], swap[MLA_NOPE:MLA_NOPE + ROPE_HALF]])
    slab_pad = ((0, 0), (0, 0), (0, HEAD_SLAB - MLA_NOPE - MLA_ROPE))
    wq = jnp.concatenate([jnp.pad(wq, slab_pad).reshape(MLA_Q_LORA, MLA_SLABS),
                          jnp.pad(wq[:, :, swap], slab_pad).reshape(MLA_Q_LORA, MLA_SLABS)], axis=1)
    g_q = jnp.concatenate([mla_g_qn, mla_g_qr])
    gq_slab = jnp.pad(jnp.stack([g_q, g_q[swap]]), ((0, 0), (0, HEAD_SLAB - MLA_NOPE - MLA_ROPE)))
    wkv = mla_w_kvup.reshape(MLA_KV_LORA, MLA_HEADS, MLA_NOPE + MLA_V)
    wk = wkv[:, :, :MLA_NOPE]
    wv = wkv[:, :, MLA_NOPE:].reshape(MLA_KV_LORA, MLA_VALL)
    wk_slab = jnp.pad(wk, ((0, 0), (0, 0), (0, HEAD_SLAB - MLA_NOPE))).reshape(MLA_KV_LORA, MLA_SLABS)
    wk_g = jnp.pad(wk * mla_g_kn[None, None, :], ((0, 0), (0, 0), (0, HEAD_SLAB - MLA_NOPE)))
    eye = jnp.eye(MLA_HEADS, dtype=F32)
    w_kabs = jnp.einsum('nhd,hg->hdgn', wk_g, eye).reshape(MLA_SLABS, MLA_HEADS * MLA_KV_LORA)
    w_kT = wk.transpose(1, 2, 0).reshape(MLA_HEADS * MLA_NOPE, MLA_KV_LORA)

    lane = np.arange(2 * HEAD_SLAB)
    seg = np.where(lane % HEAD_SLAB < MLA_NOPE, 0, np.where(lane % HEAD_SLAB < MLA_NOPE + MLA_ROPE, 1, 2))
    same = (lane[:, None] // HEAD_SLAB == lane[None, :] // HEAD_SLAB) & (seg[:, None] == seg[None, :])
    segm = np.where(same & (seg[:, None] == 0), 1.0 / MLA_NOPE, np.where(same & (seg[:, None] == 1), 1.0 / MLA_ROPE, 0.0))

    gkr = jnp.concatenate([jnp.zeros((ROPE_LO,), F32), mla_g_kr, jnp.zeros((LANES - ROPE_LO - MLA_ROPE,), F32)])
    return dict(
        g_mix=g_mix[None, :], w_in=w_in_p.astype(BF16), w_a2=w_a2.astype(BF16), b_a=gla_b_a[None, :],
        g_out=gla_g_out[None, :], g_qa=mla_g_qa[None, :], w_qup=wq.astype(BF16), segm=jnp.asarray(segm, BF16),
        gq_slab=gq_slab, g_kva=mla_g_kva[None, :], w_kslab=wk_slab.astype(BF16),
        w_vT=wv.T.astype(BF16),
        gk_slab=slab_vec(mla_g_kn, jnp.zeros((MLA_ROPE,), F32)),
        gkr_slab=gkr[None, :], w_kabs=w_kabs.astype(BF16), w_kT=w_kT.astype(BF16),
        w_v=wv.astype(BF16), w_o=w_o.astype(BF16), g_ffn=g_ffn[None, :], w_gate=ffn_w_gate.astype(BF16),
        w_up=ffn_w_up.astype(BF16), conv_w=ffn_conv_w, conv_b=ffn_conv_b[None, :],
        w_down=ffn_w_down.astype(BF16), g_ple=g_ple[None, :], w_pgate=ple_w_gate.astype(BF16),
        w_pproj=ple_w_proj.astype(BF16))


def _rope_tables(pos):
    inv = ROPE_THETA ** (-jnp.arange(ROPE_HALF, dtype=F32) * 2.0 / MLA_ROPE)
    ang = pos.astype(F32)[:, None] * inv[None, :]
    cos, sin = jnp.cos(ang), jnp.sin(ang)
    T = pos.shape[0]
    pad = jnp.zeros((T, HEAD_SLAB - MLA_NOPE - MLA_ROPE), F32)
    cos_t = jnp.concatenate([jnp.ones((T, MLA_NOPE), F32), cos, cos, pad], axis=1)
    sin_t = jnp.concatenate([jnp.zeros((T, MLA_NOPE), F32), -sin, sin, pad], axis=1)
    return cos_t, sin_t


def _layer(w, x_p, x_s, p_p, p_s, pool_c, pool_r, layer, state_gla, state_conv, page_table):
    B, S, _ = x_p.shape
    Bd, Td, _ = x_s.shape
    tm = math.gcd(S, TOKEN_TILE)
    tps = S // tm
    cos_p, sin_p = _rope_tables(jnp.arange(S))
    xp2 = x_p.reshape(B * S, D_MODEL)
    gq, gk, gv, gg, la, qa, ckv_p, kr_p, ka, vaT = _in_proj(xp2, cos_p, sin_p, w, tm=tm, tiles_per_seq=tps,
                                                            sample=False)
    chunk_p = math.gcd(S, GLA_CHUNK)
    n_chunks = math.gcd(S // chunk_p, GLA_CHUNKS_PER_STEP)
    og_p, gla_p = _gla(gq, gk, gv, gg, la, jnp.zeros((B, GLA_HEADS, GLA_DK, GLA_DV), F32), w['g_out'],
                      n_seq=B, seq_len=S, chunk=chunk_p, n_chunks=n_chunks,
                      seqs_per_step=math.gcd(B, GLA_PROMPT_SEQS))
    tq = math.gcd(S, ATTN_TILE)
    om_p = _prompt_attn(qa, ka, vaT, n_seq=B, seq_len=S, tq=tq, tk=tq)
    y_p, tail_p = _post(xp2, og_p, om_p, p_p.reshape(B * S, PLE_DIM), w, tm=tm, tiles_per_seq=tps, seq_rows=None)
    conv_p = tail_p.reshape(B, tps, SUBLANES, D_FF)[:, -1, SUBLANES - (CONV_W - 1):, :]

    Ts = Bd * Td
    pos_s = PAST_LEN + jnp.arange(Td)
    cos_s, sin_s = _rope_tables(jnp.tile(pos_s, Bd))
    xs2 = x_s.reshape(Ts, D_MODEL)
    gq, gk, gv, gg, la, qa, ckv_s, kr_s, qabs = _in_proj(xs2, cos_s, sin_s, w, tm=Ts, tiles_per_seq=1, sample=True)
    chunk_s = math.gcd(Td, GLA_CHUNK)
    og_s, gla_s = _gla(gq, gk, gv, gg, la, state_gla, w['g_out'],
                      n_seq=Bd, seq_len=Td, chunk=chunk_s, n_chunks=Td // chunk_s,
                      seqs_per_step=math.gcd(Bd, GLA_SAMPLE_SEQS))
    rows = Td * MLA_HEADS
    qrope = qa.reshape(Ts, MLA_HEADS, HEAD_SLAB)[:, :, ROPE_LO:ROPE_LO + MLA_ROPE].reshape(Bd, rows, MLA_ROPE)
    krT_new = jnp.pad(jnp.swapaxes(kr_s.reshape(Bd, Td, MLA_ROPE), 1, 2), ((0, 0), (0, 0), (0, PAGE_SIZE - Td)))
    om_s = _sample_attn(page_table, qabs.reshape(Bd, rows, MLA_KV_LORA), qrope,
                        ckv_s.reshape(Bd, Td, MLA_KV_LORA), krT_new,
                        w['w_kT'], w['w_v'], pool_c, jnp.swapaxes(pool_r, 2, 3), layer,
                        pages_per_step=math.gcd(page_table.shape[1], SAMPLE_PAGES_PER_STEP))
    zpad = lambda a, lo: jnp.pad(a, ((0, 0), (lo, Td - lo - a.shape[1]), (0, 0))).reshape(Ts, D_FF)
    prev1 = zpad(state_conv[:, 1:2], 0)
    prev2 = zpad(state_conv, 0)
    y_s, a_s = _post(xs2, og_s, om_s, p_s.reshape(Ts, PLE_DIM), w, prev=(prev1, prev2), tm=Ts, tiles_per_seq=1,
                     seq_rows=Td)
    full = jnp.concatenate([state_conv, a_s.reshape(Bd, Td, D_FF)], axis=1)
    conv_s = full[:, full.shape[1] - (CONV_W - 1):]
    return (y_p.reshape(B, S, D_MODEL), y_s.reshape(Bd, Td, D_MODEL),
            ckv_p.reshape(B, S, MLA_KV_LORA), kr_p.reshape(B, S, MLA_ROPE),
            gla_p, conv_p,
            ckv_s.reshape(Bd, Td, MLA_KV_LORA), kr_s.reshape(Bd, Td, MLA_ROPE),
            gla_s, conv_s)


def kernel(x_prompt, x_sample, cache_ckv, cache_krope, state_gla, state_conv, page_table, p_prompt, p_sample, g_mix, w_in, gla_w_a2, gla_b_a, gla_g_out, mla_g_qa, mla_w_qup, mla_g_qn, mla_g_qr, mla_g_kva, mla_g_kr, mla_w_kvup, mla_g_kn, w_o, g_ffn, ffn_w_gate, ffn_w_up, ffn_conv_w, ffn_conv_b, ffn_w_down, g_ple, ple_w_gate, ple_w_proj):
    depth = w_in.shape[0]
    per_layer = (g_mix, w_in, gla_w_a2, gla_b_a, gla_g_out, mla_g_qa, mla_w_qup, mla_g_qn, mla_g_qr, mla_g_kva,
                 mla_g_kr, mla_w_kvup, mla_g_kn, w_o, g_ffn, ffn_w_gate, ffn_w_up, ffn_conv_w, ffn_conv_b,
                 ffn_w_down, g_ple, ple_w_gate, ple_w_proj)
    y_p, y_s = x_prompt, x_sample
    outs = [[] for _ in range(8)]
    for i in range(depth):
        w = _prep_weights(*(a[i] for a in per_layer))
        res = _layer(w, y_p, y_s, p_prompt[i], p_sample[i], cache_ckv, cache_krope, i, state_gla[i],
                     state_conv[i], page_table)
        y_p, y_s = res[0], res[1]
        for lst, r in zip(outs, res[2:]):
            lst.append(r)
    return (y_p, y_s) + tuple(jnp.stack(lst) for lst in outs)
```

```python
import functools
import math

import jax
import jax.numpy as jnp
import numpy as np
from jax import lax
from jax.experimental import pallas as pl
from jax.experimental.pallas import tpu as pltpu

D_MODEL = 1024
PAST_LEN = 16384
PAGE_SIZE = 128
GLA_HEADS = 4
GLA_DK = 64
GLA_DV = 128
GLA_GATE_RANK = 16
GLA_GATE_NORM = 16.0
GLA_CHUNK = 32
MLA_HEADS = 8
MLA_Q_LORA = 256
MLA_KV_LORA = 128
MLA_NOPE = 64
MLA_ROPE = 32
MLA_V = 64
MLA_SCALE = (MLA_NOPE + MLA_ROPE) ** -0.5
LOG2E = math.log2(math.e)
ROPE_THETA = 10000.0
D_FF = 2816
CONV_W = 3
PLE_DIM = 256
EPS = 1e-6
NEG = -1e30

LANES = 128
SUBLANES = 8
HEAD_SLAB = LANES
ROPE_LO = MLA_NOPE
ROPE_HALF = MLA_ROPE // 2
GLA_QK = GLA_HEADS * GLA_DK
GLA_V = GLA_HEADS * GLA_DV
MLA_SLABS = MLA_HEADS * HEAD_SLAB
MLA_VALL = MLA_HEADS * MLA_V
C_GQ, C_GK, C_GV, C_GG = 0, 256, 512, 1024
C_MQ, C_MKV, C_MISC = 1536, 1792, 1920
IN_COLS_P = 2048

TOKEN_TILE = 512
FFN_CHUNKS = (1536, 1280)
ATTN_TILE = 512
ATTN_HEADS_PER_STEP = 8
ATTN_PIPELINE_DEPTH = 1
GLA_CHUNKS_PER_STEP = 8
GLA_PROMPT_SEQS = 4
GLA_SAMPLE_SEQS = 8
SAMPLE_PAGES_PER_STEP = 64
PAGE_RING = 3
VMEM_LIMIT = 56 * 1024 * 1024

BF16 = jnp.bfloat16
F32 = jnp.float32


def _dot(a, b):
    return jnp.dot(a, b, preferred_element_type=F32)


def _dot_nt(a, b):
    return lax.dot_general(a, b, (((1,), (1,)), ((), ())), preferred_element_type=F32)


def _dot_tn(a, b):
    return lax.dot_general(a, b, (((0,), (0,)), ((), ())), preferred_element_type=F32)


def _rms(x, g):
    return x * lax.rsqrt(jnp.mean(x * x, axis=-1, keepdims=True) + EPS) * g


def _const_spec(shape):
    nd = len(shape)
    return pl.BlockSpec(shape, lambda *_: (0,) * nd, pipeline_mode=pl.Buffered(1))


def _rope_slab(y, cos_t, sin_t):
    lane = lax.broadcasted_iota(jnp.int32, y.shape, 1)
    swapped = jnp.where(lane < ROPE_LO + ROPE_HALF,
                        pltpu.roll(y, LANES - ROPE_HALF, axis=1),
                        pltpu.roll(y, ROPE_HALF, axis=1))
    return y * cos_t + swapped * sin_t


def _in_proj_kernel(x_ref, cos_ref, sin_ref, gmix_ref, win_ref, wa2_ref, ba_ref, gqa_ref, wqup_ref,
                    segm_ref, gq_ref, gkva_ref, gkr_ref, *rest, sample):
    if sample:
        (wkabs_ref, gq_o, gk_o, gv_o, gg_o, la_o, qa_o, ckv_o, kr_o, qabs_o) = rest
    else:
        (wk_ref, gk_ref, wvT_ref, gq_o, gk_o, gv_o, gg_o, la_o, qa_o, ckv_o, kr_o, ka_o, vaT_o) = rest
    segm = segm_ref[...]
    q_gain = gq_ref[...]

    cos_t = cos_ref[...]
    sin_t = sin_ref[...]
    n = _rms(x_ref[...], gmix_ref[...]).astype(BF16)

    def proj(lo, width):
        return _dot(n, win_ref[:, lo:lo + width])

    um = proj(C_MQ, IN_COLS_P - C_MQ)
    ug = proj(C_GQ, 2 * GLA_QK)
    gq_o[...] = ug[:, :GLA_QK] * (GLA_DK ** -0.5)
    gk_o[...] = ug[:, GLA_QK:]
    misc = um[:, C_MISC - C_MQ:]
    z = _dot(misc.astype(BF16), wa2_ref[...]) + ba_ref[...]
    la_o[...] = (jnp.minimum(z, 0.0) - jnp.log1p(jnp.exp(-jnp.abs(z)))) * (1.0 / GLA_GATE_NORM)
    lane = lax.broadcasted_iota(jnp.int32, misc.shape, 1)
    is_rope = (lane >= ROPE_LO) & (lane < ROPE_LO + MLA_ROPE)
    ms_r = jnp.sum(jnp.where(is_rope, misc * misc, 0.0), axis=-1, keepdims=True) * (1.0 / MLA_ROPE)
    kr_slab = _rope_slab(misc * lax.rsqrt(ms_r + EPS) * gkr_ref[...], cos_t, sin_t)
    kr_o[...] = kr_slab[:, ROPE_LO:ROPE_LO + MLA_ROPE]
    ckv = _rms(um[:, C_MKV - C_MQ:C_MISC - C_MQ], gkva_ref[...])
    ckv_o[...] = ckv
    ckv_b = ckv.astype(BF16)
    cq = _rms(um[:, :MLA_Q_LORA], gqa_ref[...]).astype(BF16)

    if not sample:
        kn = _dot(ckv_b, wk_ref[...])
        vaT_o[...] = _dot_nt(wvT_ref[...], ckv_b).astype(BF16)
    qf = _dot(cq, wqup_ref[...])
    q_cos = cos_t * (q_gain[0:1] * (MLA_SCALE * LOG2E))
    q_sin = sin_t * (q_gain[1:2] * (MLA_SCALE * LOG2E))

    def head_pair(pair):
        lo = pair * 2 * HEAD_SLAB
        q2 = qf[:, lo:lo + 2 * HEAD_SLAB]
        q_inv = lax.rsqrt(_dot((q2 * q2).astype(BF16), segm) + EPS)
        if not sample:
            k2 = kn[:, lo:lo + 2 * HEAD_SLAB]
            k2 = k2 * lax.rsqrt(_dot((k2 * k2).astype(BF16), segm) + EPS) * gk_ref[:, lo:lo + 2 * HEAD_SLAB]
        for j in range(2):
            sl = slice(lo + j * HEAD_SLAB, lo + (j + 1) * HEAD_SLAB)
            swapped = qf[:, MLA_SLABS + lo + j * HEAD_SLAB:MLA_SLABS + lo + (j + 1) * HEAD_SLAB]
            qh = q_inv[:, j * HEAD_SLAB:(j + 1) * HEAD_SLAB] * (qf[:, sl] * q_cos + swapped * q_sin)
            qa_o[:, sl] = qh.astype(BF16)
            if not sample:
                ka_o[:, sl] = (k2[:, j * HEAD_SLAB:(j + 1) * HEAD_SLAB] + kr_slab).astype(BF16)

    half = GLA_V // 2
    gv_o[:, :half] = proj(C_GV, half).astype(BF16)
    head_pair(0)
    gv_o[:, half:] = proj(C_GV + half, half).astype(BF16)
    head_pair(1)
    gg_o[:, :half] = proj(C_GG, half)
    head_pair(2)
    gg_o[:, half:] = proj(C_GG + half, half)
    head_pair(3)
    if sample:
        qabs_o[...] = _dot(qa_o[...], wkabs_ref[...]).astype(BF16)


def _in_proj(x2d, cos_t, sin_t, w, *, tm, tiles_per_seq, sample):
    T = x2d.shape[0]
    nt = T // tm
    row = lambda i: (i, 0)
    pos = lambda i: (i % tiles_per_seq, 0)
    consts = [w['g_mix'], w['w_in'], w['w_a2'], w['b_a'], w['g_qa'], w['w_qup'], w['segm'], w['gq_slab'],
              w['g_kva'], w['gkr_slab']]
    consts += [w['w_kabs']] if sample else [w['w_kslab'], w['gk_slab'], w['w_vT']]
    in_specs = [pl.BlockSpec((tm, D_MODEL), row), pl.BlockSpec((tm, LANES), pos), pl.BlockSpec((tm, LANES), pos)]
    in_specs += [_const_spec(c.shape) for c in consts]
    outs = [(GLA_QK, F32), (GLA_QK, F32), (GLA_V, BF16), (GLA_V, F32), (GLA_QK, F32),
            (MLA_SLABS, BF16), (MLA_KV_LORA, F32), (MLA_ROPE, F32), (MLA_SLABS, BF16)]
    out_specs = [pl.BlockSpec((tm, c), row) for c, _ in outs]
    out_shape = [jax.ShapeDtypeStruct((T, c), d) for c, d in outs]
    if not sample:
        out_specs.append(pl.BlockSpec((MLA_VALL, tm), lambda i: (0, i)))
        out_shape.append(jax.ShapeDtypeStruct((MLA_VALL, T), BF16))
    return pl.pallas_call(
        functools.partial(_in_proj_kernel, sample=sample),
        grid=(nt,),
        in_specs=in_specs,
        out_specs=out_specs,
        out_shape=out_shape,
        compiler_params=pltpu.CompilerParams(dimension_semantics=("parallel",), vmem_limit_bytes=VMEM_LIMIT),
        name="in_proj_sample" if sample else "in_proj_prompt",
    )(x2d, cos_t, sin_t, *consts)


def _gla_kernel(q_ref, k_ref, v_ref, gg_ref, la_ref, s0_ref, cum_ref, sel_ref, gout_ref,
                o_ref, s_out_ref, st_scr, *, chunk, n_chunks):
    step = pl.program_id(1)
    n_seq = q_ref.shape[0]
    tc = chunk * n_chunks
    heads = range(GLA_HEADS)
    dk = [slice(h * GLA_DK, (h + 1) * GLA_DK) for h in heads]
    dv = [slice(h * GLA_DV, (h + 1) * GLA_DV) for h in heads]
    rows = [slice(ci * chunk, (ci + 1) * chunk) for ci in range(n_chunks)]

    @pl.when(step == 0)
    def _():
        st_scr[...] = s0_ref[...]

    cum = cum_ref[...]
    sel = sel_ref[...]
    r = lax.broadcasted_iota(jnp.int32, (tc, tc), 0)
    c = lax.broadcasted_iota(jnp.int32, (tc, tc), 1)
    causal = (r >= c) & ((r // chunk) == (c // chunk))
    gout = gout_ref[...]

    def decays(i, t):
        la = la_ref[i]
        la_hi = la.astype(BF16)
        la_lo = (la - la_hi.astype(F32)).astype(BF16)
        t['bb'] = _dot(cum, la_hi) + _dot(cum, la_lo)
        t['dec'] = jnp.exp(_dot_tn(la_hi, sel) + _dot_tn(la_lo, sel))

    def scale(i, t):
        b, bl = t['bb'][:tc], t['bb'][tc:]
        q, k = q_ref[i], k_ref[i]
        t['q_dec'] = (q * jnp.exp(b)).astype(BF16)
        t['k_inv'] = (k * jnp.exp(-b)).astype(BF16)
        t['k_end'] = (k * jnp.exp(bl - b)).astype(BF16)

    def intra_scores(i, t):
        t['a'] = [_dot_nt(t['q_dec'][:, dk[h]], t['k_inv'][:, dk[h]]) for h in heads]

    def chunk_outer(i, t):
        v = v_ref[i]
        t['x'] = [[_dot_tn(t['k_end'][rw, dk[h]], v[rw, dv[h]]) for rw in rows] for h in heads]

    def intra_out(i, t):
        v = v_ref[i]
        t['o'] = [_dot(jnp.where(causal, t['a'][h], 0.0).astype(BF16), v[:, dv[h]]) for h in heads]

    def scan(i, t):
        t['s'] = []
        for h in heads:
            s_h = st_scr[i, h]
            starts = []
            for ci in range(n_chunks):
                starts.append(s_h.astype(BF16))
                s_h = s_h * t['dec'][dk[h], ci:ci + 1] + t['x'][h][ci]
            st_scr[i, h] = s_h
            t['s'].append(starts)

    def inter_out(i, t):
        for h in heads:
            parts = [_dot(t['q_dec'][rw, dk[h]], t['s'][h][ci]) for ci, rw in enumerate(rows)]
            t['o'][h] = t['o'][h] + (jnp.concatenate(parts, axis=0) if n_chunks > 1 else parts[0])

    def finish(i, t):
        gg = gg_ref[i]
        gate = gg / (1.0 + jnp.exp(-gg))
        for h in heads:
            o_ref[i, :, dv[h]] = (_rms(t['o'][h], gout) * gate[:, dv[h]]).astype(o_ref.dtype)

    temps = [{} for _ in range(n_seq)]
    for stage in (decays, scale, intra_scores, chunk_outer, intra_out, scan, inter_out, finish):
        for i in range(n_seq):
            stage(i, temps[i])

    @pl.when(step == pl.num_programs(1) - 1)
    def _():
        s_out_ref[...] = st_scr[...]


def _gla(q, k, v, gg, la, s0, g_out, *, n_seq, seq_len, chunk, n_chunks, seqs_per_step):
    tc = chunk * n_chunks
    steps = seq_len // tc
    nb = seqs_per_step
    idx = np.arange(tc)
    same = (idx[:, None] // chunk) == (idx[None, :] // chunk)
    cum = jnp.asarray(np.concatenate([same & (idx[:, None] >= idx[None, :]), same], axis=0), BF16)
    sel = jnp.asarray((idx[:, None] // chunk) == np.arange(LANES)[None, :], BF16)
    tok = lambda width: pl.BlockSpec((nb, tc, width), lambda b, s: (b, s, 0))
    per_seq = lambda a: a.reshape(n_seq, seq_len, a.shape[-1])
    state_spec = pl.BlockSpec((nb, GLA_HEADS, GLA_DK, GLA_DV), lambda b, s: (b, 0, 0, 0))
    o, state = pl.pallas_call(
        functools.partial(_gla_kernel, chunk=chunk, n_chunks=n_chunks),
        grid=(n_seq // nb, steps),
        in_specs=[tok(GLA_QK), tok(GLA_QK), tok(GLA_V), tok(GLA_V), tok(GLA_QK), state_spec,
                  _const_spec((2 * tc, tc)), _const_spec((tc, LANES)), _const_spec((1, GLA_DV))],
        out_specs=[tok(GLA_V), state_spec],
        out_shape=[jax.ShapeDtypeStruct((n_seq, seq_len, GLA_V), BF16),
                   jax.ShapeDtypeStruct((n_seq, GLA_HEADS, GLA_DK, GLA_DV), F32)],
        scratch_shapes=[pltpu.VMEM((nb, GLA_HEADS, GLA_DK, GLA_DV), F32)],
        compiler_params=pltpu.CompilerParams(dimension_semantics=("parallel", "arbitrary"),
                                             vmem_limit_bytes=VMEM_LIMIT),
        name=f"gla_c{chunk}",
    )(per_seq(q), per_seq(k), per_seq(v), per_seq(gg), per_seq(la), s0, cum, sel, g_out)
    return o.reshape(n_seq * seq_len, GLA_V), state


def _prompt_attn_kernel(q_ref, k_ref, vT_ref, o_ref, s_scr, *, tq, tk):
    for qi in range(q_ref.shape[0] // tq):
        _attn_tile(q_ref, k_ref, vT_ref, o_ref, s_scr, qi, tq=tq, tk=tk)


def _attn_tile(q_ref, k_ref, vT_ref, o_ref, s_scr, qi, *, tq, tk):
    assert tq == tk
    key = lax.broadcasted_iota(jnp.int32, (tk, tq), 0)
    qry = lax.broadcasted_iota(jnp.int32, (tk, tq), 1)
    ones_rows = jnp.ones((2 * SUBLANES, tk), BF16)

    def scores(j, kb, masked):
        ks = kb * tk if isinstance(kb, int) else pl.multiple_of(kb * tk, tk)
        s = _dot_nt(k_ref[pl.ds(ks, tk), j * HEAD_SLAB:(j + 1) * HEAD_SLAB],
                    q_ref[qi * tq:(qi + 1) * tq, j * HEAD_SLAB:(j + 1) * HEAD_SLAB])
        if masked:
            s = jnp.where(key <= qry, s, NEG)
        s_scr[j] = s
        return jnp.max(s, axis=0, keepdims=True)

    def update(j, kb, m_blk, state):
        m, acc = state
        ks = kb * tk if isinstance(kb, int) else pl.multiple_of(kb * tk, tk)
        m_new = jnp.maximum(m, m_blk)
        alpha = jnp.exp2(m - m_new)
        p = jnp.exp2(s_scr[j] - m_new)
        v_aug = jnp.concatenate([vT_ref[j * MLA_V:(j + 1) * MLA_V, pl.ds(ks, tk)], ones_rows], axis=0)
        acc = alpha * acc + _dot(v_aug, p.astype(BF16))
        return m_new, acc

    n_heads = q_ref.shape[1] // HEAD_SLAB
    depth = min(ATTN_PIPELINE_DEPTH, n_heads - 1)

    def block(kb, masked, pending, state):
        state = list(state)
        queue = [(n_heads - len(pending) + i, kb + 1, m) for i, m in enumerate(pending)]
        for j in range(n_heads):
            queue.append((j, kb, scores(j, kb, masked)))
            if len(queue) > depth:
                pj, pkb, pm = queue.pop(0)
                state[pj] = update(pj, pkb, pm, state[pj])
        return tuple(m for _, _, m in queue), tuple(state)

    init = (jnp.full((1, tq), NEG, F32), jnp.zeros((MLA_V + 2 * SUBLANES, tq), F32))
    pending, state = block(qi, True, (), (init,) * n_heads)

    def body(t, carry):
        return block(qi - 1 - t, False, *carry)

    pending, state = lax.fori_loop(0, qi, body, (pending, state))
    state = list(state)
    for i, m_blk in enumerate(pending):
        head = n_heads - depth + i
        state[head] = update(head, 0, m_blk, state[head])
    o_ref[qi * tq:(qi + 1) * tq, :] = jnp.concatenate([acc[:MLA_V] / acc[MLA_V:MLA_V + 1] for _, acc in state],
                                 axis=0).T.astype(o_ref.dtype)


def _prompt_attn(qa, ka, va, *, n_seq, seq_len, tq, tk):
    hs = ATTN_HEADS_PER_STEP
    return pl.pallas_call(
        functools.partial(_prompt_attn_kernel, tq=tq, tk=tk),
        grid=(n_seq, MLA_HEADS // hs),
        in_specs=[pl.BlockSpec((seq_len, hs * HEAD_SLAB), lambda b, hp: (b, hp)),
                  pl.BlockSpec((seq_len, hs * HEAD_SLAB), lambda b, hp: (b, hp)),
                  pl.BlockSpec((hs * MLA_V, seq_len), lambda b, hp: (hp, b))],
        out_specs=pl.BlockSpec((seq_len, hs * MLA_V), lambda b, hp: (b, hp)),
        out_shape=jax.ShapeDtypeStruct((n_seq * seq_len, MLA_VALL), BF16),
        scratch_shapes=[pltpu.VMEM((hs, tk, tq), F32)],
        compiler_params=pltpu.CompilerParams(dimension_semantics=("parallel", "parallel"),
                                             vmem_limit_bytes=VMEM_LIMIT),
        name="prompt_attn",
    )(qa, ka, va)


def _sample_attn_kernel(pt_ref, qabs_ref, qrope_ref, cnew_ref, krnew_ref, wkT_ref, wv_ref, poolc_hbm, poolr_hbm,
                        o_ref, lhs_scr, m_scr, l_scr, acc_scr, cb_scr, krT_scr, s_scr, cbuf, rbuf, sem,
                        *, n_tok, pages_per_step, n_groups, layer):
    G = pages_per_step
    b = pl.program_id(0)
    n_all = pl.num_programs(0) * n_groups

    def group_of(s):
        s = jnp.minimum(s, n_all - 1)
        return s // n_groups, s % n_groups
    rows = n_tok * MLA_HEADS
    n_k = MLA_HEADS * MLA_NOPE

    def page_copies(sl, group):
        copies = []
        for j in range(G):
            page = pt_ref[group[0], group[1] * G + j]
            copies.append(pltpu.make_async_copy(poolc_hbm.at[layer, page], cbuf.at[sl, j], sem.at[0, sl]))
            copies.append(pltpu.make_async_copy(poolr_hbm.at[layer, page], rbuf.at[sl, j], sem.at[1, sl]))
        return copies

    def wait_slot(sl):
        pltpu.make_async_copy(cbuf.at[sl], cbuf.at[sl], sem.at[0, sl]).wait()
        pltpu.make_async_copy(rbuf.at[sl], rbuf.at[sl], sem.at[1, sl]).wait()

    @pl.when(b == 0)
    def _():
        for d in range(PAGE_RING - 1):
            for cp in page_copies(d, group_of(d)):
                cp.start()

    def scores(cb, krT):
        nk = cb.shape[0]
        big = _dot_nt(lhs_scr[...], cb)
        kvn = big[:n_k]
        s_nope = big[n_k:n_k + rows]
        ss = jnp.sum((kvn * kvn).reshape(MLA_HEADS, MLA_NOPE, nk), axis=1)
        inv = lax.rsqrt(ss * (1.0 / MLA_NOPE) + EPS)
        s_rope = _dot(qrope_ref[0], krT)
        return (s_nope.reshape(n_tok, MLA_HEADS, nk) * inv[None]).reshape(rows, nk) + s_rope

    def weights(s):
        m = m_scr[...]
        m_new = jnp.maximum(m, jnp.max(s, axis=-1, keepdims=True))
        alpha = jnp.exp2(m - m_new)
        p = jnp.exp2(s - m_new)
        l_scr[...] = alpha * l_scr[...] + jnp.sum(p, axis=-1, keepdims=True)
        m_scr[...] = m_new
        return alpha, p.astype(BF16)

    def accumulate(alpha, p, cb):
        acc_scr[...] = alpha * acc_scr[...] + _dot(p, cb)

    def absorb(s, cb):
        accumulate(*weights(s), cb)

    lhs_scr[:n_k, :] = wkT_ref[...]
    lhs_scr[n_k:n_k + rows, :] = qabs_ref[0]
    m_scr[...] = jnp.full(m_scr.shape, NEG, F32)
    l_scr[...] = jnp.zeros(l_scr.shape, F32)
    acc_scr[...] = jnp.zeros(acc_scr.shape, F32)
    c_new = jnp.concatenate([cnew_ref[0], jnp.zeros((PAGE_SIZE - n_tok, MLA_KV_LORA), F32)], axis=0).astype(BF16)
    key = lax.broadcasted_iota(jnp.int32, (rows, PAGE_SIZE), 1)
    tok = lax.broadcasted_iota(jnp.int32, (rows, PAGE_SIZE), 0) // MLA_HEADS
    absorb(jnp.where(key <= tok, scores(c_new, krnew_ref[0].astype(BF16)), NEG), c_new)

    for gi in range(n_groups):
        group = b * n_groups + gi
        slot = group % PAGE_RING
        cur, prv = gi % 2, 1 - gi % 2
        wait_slot(slot)
        for j in range(G):
            cb_scr[cur, j * PAGE_SIZE:(j + 1) * PAGE_SIZE, :] = cbuf[slot, j].astype(BF16)
            krT_scr[:, j * PAGE_SIZE:(j + 1) * PAGE_SIZE] = rbuf[slot, j].astype(BF16)
        ahead = group + (PAGE_RING - 1)
        for cp in page_copies(ahead % PAGE_RING, group_of(ahead)):
            cp.start()
        if gi == 0:
            s_scr[cur] = scores(cb_scr[cur], krT_scr[...])
        else:
            alpha, p = weights(s_scr[prv])
            s_scr[cur] = scores(cb_scr[cur], krT_scr[...])
            accumulate(alpha, p, cb_scr[prv])
    absorb(s_scr[cur], cb_scr[cur])

    o_lat = (acc_scr[...] / l_scr[...]).astype(BF16)
    full = _dot(o_lat, wv_ref[...])
    full = full.reshape(n_tok, MLA_HEADS, MLA_VALL)
    hr = lax.broadcasted_iota(jnp.int32, (MLA_HEADS, MLA_VALL), 0)
    hc = lax.broadcasted_iota(jnp.int32, (MLA_HEADS, MLA_VALL), 1) // MLA_V
    o_ref[...] = jnp.sum(jnp.where((hr == hc)[None], full, 0.0), axis=1).astype(o_ref.dtype)

    @pl.when(b == pl.num_programs(0) - 1)
    def _():
        for d in range(1, PAGE_RING):
            wait_slot((n_all - 1 + d) % PAGE_RING)


def _sample_attn(page_table, qabs, qrope, c_new, kr_new, w_kT, w_v, pool_c, pool_r, layer, *, pages_per_step):
    n_b, n_pages = page_table.shape
    n_tok = c_new.shape[1]
    rows = n_tok * MLA_HEADS
    G = pages_per_step
    n_k = MLA_HEADS * MLA_NOPE

    per_b = lambda b, pt: (b, 0, 0)
    grid_spec = pltpu.PrefetchScalarGridSpec(
        num_scalar_prefetch=1,
        grid=(n_b,),
        in_specs=[pl.BlockSpec((1, rows, MLA_KV_LORA), per_b), pl.BlockSpec((1, rows, MLA_ROPE), per_b),
                  pl.BlockSpec((1, n_tok, MLA_KV_LORA), per_b), pl.BlockSpec((1, MLA_ROPE, PAGE_SIZE), per_b),
                  pl.BlockSpec((n_k, MLA_KV_LORA), lambda b, pt: (0, 0)),
                  pl.BlockSpec((MLA_KV_LORA, MLA_VALL), lambda b, pt: (0, 0)),
                  pl.BlockSpec(memory_space=pl.ANY), pl.BlockSpec(memory_space=pl.ANY)],
        out_specs=pl.BlockSpec((n_tok, MLA_VALL), lambda b, pt: (b, 0)),
        scratch_shapes=[pltpu.VMEM((n_k + rows, MLA_KV_LORA), BF16), pltpu.VMEM((rows, 1), F32),
                        pltpu.VMEM((rows, 1), F32), pltpu.VMEM((rows, MLA_KV_LORA), F32),
                        pltpu.VMEM((2, G * PAGE_SIZE, MLA_KV_LORA), BF16), pltpu.VMEM((MLA_ROPE, G * PAGE_SIZE), BF16),
                        pltpu.VMEM((2, rows, G * PAGE_SIZE), F32),
                        pltpu.VMEM((PAGE_RING, G, PAGE_SIZE, MLA_KV_LORA), F32),
                        pltpu.VMEM((PAGE_RING, G, MLA_ROPE, PAGE_SIZE), F32),
                        pltpu.SemaphoreType.DMA((2, PAGE_RING))])
    return pl.pallas_call(
        functools.partial(_sample_attn_kernel, n_tok=n_tok, pages_per_step=G, n_groups=n_pages // G, layer=layer),
        grid_spec=grid_spec,
        out_shape=jax.ShapeDtypeStruct((n_b * n_tok, MLA_VALL), BF16),
        compiler_params=pltpu.CompilerParams(dimension_semantics=("arbitrary",),
                                             vmem_limit_bytes=VMEM_LIMIT),
        name="sample_attn",
    )(page_table, qabs, qrope, c_new, kr_new, w_kT, w_v, pool_c, pool_r)


def _post_kernel(x_ref, og_ref, om_ref, p_ref, wo_ref, gffn_ref, wg_ref, wu_ref, cw_ref, cb_ref, wd_ref,
                 gple_ref, wpg_ref, wpp_ref, *rest, tiles_per_seq, seq_rows):
    paged_prev = seq_rows is not None
    if paged_prev:
        p1_ref, p2_ref, y_ref, tail_ref, acc_scr = rest
    else:
        y_ref, tail_ref, acc_scr, carry_scr = rest
    tm = x_ref.shape[0]
    i = pl.program_id(0)
    h1 = x_ref[...] + _dot(og_ref[...], wo_ref[:GLA_V, :]) + _dot(om_ref[...], wo_ref[GLA_V:, :])
    n2 = _rms(h1, gffn_ref[...]).astype(BF16)
    if not paged_prev:
        @pl.when(i % tiles_per_seq == 0)
        def _():
            carry_scr[...] = jnp.zeros(carry_scr.shape, F32)

    bounds = np.concatenate([[0], np.cumsum(FFN_CHUNKS)]).tolist()
    chunks = [slice(lo, hi) for lo, hi in zip(bounds[:-1], bounds[1:])]
    lo = bounds[-1]

    def gate_up(cols):
        return _dot(n2, wg_ref[:, cols]), _dot(n2, wu_ref[:, cols])

    nxt = gate_up(chunks[0])
    for ci, cols in enumerate(chunks):
        a, up = nxt
        if ci + 1 < len(chunks):
            nxt = gate_up(chunks[ci + 1])
        row =lax.broadcasted_iota(jnp.int32, a.shape, 0)
        slabs = [a[:, c0:c0 + 2 * LANES] for c0 in range(0, a.shape[1], 2 * LANES)]
        r1 = jnp.concatenate([pltpu.roll(sl, 1, axis=0) for sl in slabs], axis=1)
        r2 = jnp.concatenate([pltpu.roll(sl, 2, axis=0) for sl in slabs], axis=1)
        if paged_prev:
            t = row % seq_rows
            a1 = jnp.where(t >= 1, r1, p1_ref[:, cols])
            a2 = jnp.where(t >= 2, r2, p2_ref[:, cols])
            tail_ref[:, cols] = a
        else:
            prev = carry_scr[:, cols]
            pm1 = prev[SUBLANES - 1:SUBLANES, :]
            pm2 = prev[SUBLANES - 2:SUBLANES - 1, :]
            a1 = jnp.where(row >= 1, r1, pm1)
            a2 = jnp.where(row >= 2, r2, jnp.where(row == 0, pm2, pm1))
            carry_scr[:, cols] = a[tm - SUBLANES:, :]
            tail_ref[:, cols] = a[tm - SUBLANES:, :]
        conv = cb_ref[:, cols] + cw_ref[0:1, cols] * a2 + cw_ref[1:2, cols] * a1 + cw_ref[2:3, cols] * a
        gact = (conv / (1.0 + jnp.exp(-conv)) * up).astype(BF16)
        down = _dot(gact, wd_ref[cols, :])
        if ci == 0:
            acc_scr[...] = down
        else:
            acc_scr[...] += down
    assert lo == D_FF
    h2 = h1 + acc_scr[...]
    n3 =_rms(h2, gple_ref[...]).astype(BF16)
    gate = 1.0 / (1.0 + jnp.exp(-_dot(n3, wpg_ref[...])))
    y_ref[...] = h2 + _dot(p_ref[...].astype(BF16), wpp_ref[...]) * gate


def _post(x2d, og, om, p2d, w, prev=None, *, tm, tiles_per_seq, seq_rows):
    T = x2d.shape[0]
    nt = T // tm
    row = lambda i: (i, 0)
    consts = [w['w_o'], w['g_ffn'], w['w_gate'], w['w_up'], w['conv_w'], w['conv_b'], w['w_down'],
              w['g_ple'], w['w_pgate'], w['w_pproj']]
    in_specs = [pl.BlockSpec((tm, D_MODEL), row), pl.BlockSpec((tm, GLA_V), row),
                pl.BlockSpec((tm, MLA_VALL), row), pl.BlockSpec((tm, PLE_DIM), row)]
    in_specs += [_const_spec(c.shape) for c in consts]
    args = [x2d, og, om, p2d, *consts]
    scratch = [pltpu.VMEM((tm, D_MODEL), F32)]
    if prev is not None:
        in_specs += [pl.BlockSpec((tm, D_FF), row)] * 2
        args += list(prev)
        tail_rows = tm
    else:
        scratch.append(pltpu.VMEM((SUBLANES, D_FF), F32))
        tail_rows = SUBLANES
    return pl.pallas_call(
        functools.partial(_post_kernel, tiles_per_seq=tiles_per_seq, seq_rows=seq_rows),
        grid=(nt,),
        in_specs=in_specs,
        out_specs=[pl.BlockSpec((tm, D_MODEL), row), pl.BlockSpec((tail_rows, D_FF), row)],
        out_shape=[jax.ShapeDtypeStruct((T, D_MODEL), F32), jax.ShapeDtypeStruct((nt * tail_rows, D_FF), F32)],
        scratch_shapes=scratch,
        compiler_params=pltpu.CompilerParams(dimension_semantics=("arbitrary",), vmem_limit_bytes=VMEM_LIMIT),
        name="post_sample" if prev is not None else "post_prompt",
    )(*args)


def _prep_weights(g_mix, w_in, gla_w_a2, gla_b_a, gla_g_out, mla_g_qa, mla_w_qup, mla_g_qn, mla_g_qr,
                  mla_g_kva, mla_g_kr, mla_w_kvup, mla_g_kn, w_o, g_ffn, ffn_w_gate, ffn_w_up, ffn_conv_w,
                  ffn_conv_b, ffn_w_down, g_ple, ple_w_gate, ple_w_proj):
    sizes = (GLA_QK, GLA_QK, GLA_V, GLA_V, GLA_GATE_RANK, MLA_Q_LORA, MLA_KV_LORA, MLA_ROPE)
    offs = np.concatenate([[0], np.cumsum(sizes)])
    piece = lambda i: w_in[:, offs[i]:offs[i + 1]]
    zeros = lambda n: jnp.zeros((D_MODEL, n), w_in.dtype)
    misc = jnp.concatenate([piece(4), zeros(ROPE_LO - GLA_GATE_RANK), piece(7),
                            zeros(LANES - ROPE_LO - MLA_ROPE)], axis=1)
    w_in_p = jnp.concatenate([piece(0), piece(1), piece(2), piece(3), piece(5), piece(6), misc], axis=1)
    w_a2 = jnp.concatenate([gla_w_a2, jnp.zeros((LANES - GLA_GATE_RANK, GLA_QK), gla_w_a2.dtype)], axis=0)

    def slab_vec(nope, rope_):
        one = jnp.concatenate([nope, rope_, jnp.zeros((HEAD_SLAB - MLA_NOPE - MLA_ROPE,), F32)])
        return jnp.tile(one, MLA_HEADS)[None, :]

    wq = mla_w_qup.reshape(MLA_Q_LORA, MLA_HEADS, MLA_NOPE + MLA_ROPE)
    swap = np.arange(MLA_NOPE + MLA_ROPE)
    swap[MLA_NOPE:] = np.concatenate([swap[MLA_NOPE + ROPE_HALF:], swap[MLA_NOPE:MLA_NOPE + ROPE_HALF]])
    slab_pad = ((0, 0), (0, 0), (0, HEAD_SLAB - MLA_NOPE - MLA_ROPE))
    wq = jnp.concatenate([jnp.pad(wq, slab_pad).reshape(MLA_Q_LORA, MLA_SLABS),
                          jnp.pad(wq[:, :, swap], slab_pad).reshape(MLA_Q_LORA, MLA_SLABS)], axis=1)
    g_q = jnp.concatenate([mla_g_qn, mla_g_qr])
    gq_slab = jnp.pad(jnp.stack([g_q, g_q[swap]]), ((0, 0), (0, HEAD_SLAB - MLA_NOPE - MLA_ROPE)))
    wkv = mla_w_kvup.reshape(MLA_KV_LORA, MLA_HEADS, MLA_NOPE + MLA_V)
    wk = wkv[:, :, :MLA_NOPE]
    wv = wkv[:, :, MLA_NOPE:].reshape(MLA_KV_LORA, MLA_VALL)
    wk_slab = jnp.pad(wk, ((0, 0), (0, 0), (0, HEAD_SLAB - MLA_NOPE))).reshape(MLA_KV_LORA, MLA_SLABS)
    wk_g = jnp.pad(wk * mla_g_kn[None, None, :], ((0, 0), (0, 0), (0, HEAD_SLAB - MLA_NOPE)))
    eye = jnp.eye(MLA_HEADS, dtype=F32)
    w_kabs = jnp.einsum('nhd,hg->hdgn', wk_g, eye).reshape(MLA_SLABS, MLA_HEADS * MLA_KV_LORA)
    w_kT = wk.transpose(1, 2, 0).reshape(MLA_HEADS * MLA_NOPE, MLA_KV_LORA)

    lane = np.arange(2 * HEAD_SLAB)
    seg = np.where(lane % HEAD_SLAB < MLA_NOPE, 0, np.where(lane % HEAD_SLAB < MLA_NOPE + MLA_ROPE, 1, 2))
    same = (lane[:, None] // HEAD_SLAB == lane[None, :] // HEAD_SLAB) & (seg[:, None] == seg[None, :])
    segm = np.where(same & (seg[:, None] == 0), 1.0 / MLA_NOPE, np.where(same & (seg[:, None] == 1), 1.0 / MLA_ROPE, 0.0))

    gkr = jnp.concatenate([jnp.zeros((ROPE_LO,), F32), mla_g_kr, jnp.zeros((LANES - ROPE_LO - MLA_ROPE,), F32)])
    return dict(
        g_mix=g_mix[None, :], w_in=w_in_p.astype(BF16), w_a2=w_a2.astype(BF16), b_a=gla_b_a[None, :],
        g_out=gla_g_out[None, :], g_qa=mla_g_qa[None, :], w_qup=wq.astype(BF16), segm=jnp.asarray(segm, BF16),
        gq_slab=gq_slab, g_kva=mla_g_kva[None, :], w_kslab=wk_slab.astype(BF16),
        w_vT=wv.T.astype(BF16),
        gk_slab=slab_vec(mla_g_kn, jnp.zeros((MLA_ROPE,), F32)),
        gkr_slab=gkr[None, :], w_kabs=w_kabs.astype(BF16), w_kT=w_kT.astype(BF16),
        w_v=wv.astype(BF16), w_o=w_o.astype(BF16), g_ffn=g_ffn[None, :], w_gate=ffn_w_gate.astype(BF16),
        w_up=ffn_w_up.astype(BF16), conv_w=ffn_conv_w, conv_b=ffn_conv_b[None, :],
        w_down=ffn_w_down.astype(BF16), g_ple=g_ple[None, :], w_pgate=ple_w_gate.astype(BF16),
        w_pproj=ple_w_proj.astype(BF16))


def _rope_tables(pos):
    inv = ROPE_THETA ** (-jnp.arange(ROPE_HALF, dtype=F32) * 2.0 / MLA_ROPE)
    ang = pos.astype(F32)[:, None] * inv[None, :]
    cos, sin = jnp.cos(ang), jnp.sin(ang)
    T = pos.shape[0]
    pad = jnp.zeros((T, HEAD_SLAB - MLA_NOPE - MLA_ROPE), F32)
    cos_t = jnp.concatenate([jnp.ones((T, MLA_NOPE), F32), cos, cos, pad], axis=1)
    sin_t = jnp.concatenate([jnp.zeros((T, MLA_NOPE), F32), -sin, sin, pad], axis=1)
    return cos_t, sin_t


def _layer(w, x_p, x_s, p_p, p_s, pool_c, pool_r, layer, state_gla, state_conv, page_table):
    B, S, _ = x_p.shape
    Bd, Td, _ = x_s.shape
    tm = math.gcd(S, TOKEN_TILE)
    tps = S // tm
    cos_p, sin_p = _rope_tables(jnp.arange(S))
    xp2 = x_p.reshape(B * S, D_MODEL)
    gq, gk, gv, gg, la, qa, ckv_p, kr_p, ka, vaT = _in_proj(xp2, cos_p, sin_p, w, tm=tm, tiles_per_seq=tps,
                                                            sample=False)
    chunk_p = math.gcd(S, GLA_CHUNK)
    n_chunks = math.gcd(S // chunk_p, GLA_CHUNKS_PER_STEP)
    og_p, gla_p = _gla(gq, gk, gv, gg, la, jnp.zeros((B, GLA_HEADS, GLA_DK, GLA_DV), F32), w['g_out'],
                      n_seq=B, seq_len=S, chunk=chunk_p, n_chunks=n_chunks,
                      seqs_per_step=math.gcd(B, GLA_PROMPT_SEQS))
    tq = math.gcd(S, ATTN_TILE)
    om_p = _prompt_attn(qa, ka, vaT, n_seq=B, seq_len=S, tq=tq, tk=tq)
    y_p, tail_p = _post(xp2, og_p, om_p, p_p.reshape(B * S, PLE_DIM), w, tm=tm, tiles_per_seq=tps, seq_rows=None)
    conv_p = tail_p.reshape(B, tps, SUBLANES, D_FF)[:, -1, SUBLANES - (CONV_W - 1):, :]

    Ts = Bd * Td
    pos_s = PAST_LEN + jnp.arange(Td)
    cos_s, sin_s = _rope_tables(jnp.tile(pos_s, Bd))
    xs2 = x_s.reshape(Ts, D_MODEL)
    gq, gk, gv, gg, la, qa, ckv_s, kr_s, qabs = _in_proj(xs2, cos_s, sin_s, w, tm=Ts, tiles_per_seq=1, sample=True)
    chunk_s = math.gcd(Td, GLA_CHUNK)
    og_s, gla_s = _gla(gq, gk, gv, gg, la, state_gla, w['g_out'],
                      n_seq=Bd, seq_len=Td, chunk=chunk_s, n_chunks=Td // chunk_s,
                      seqs_per_step=math.gcd(Bd, GLA_SAMPLE_SEQS))
    rows = Td * MLA_HEADS
    qrope = qa.reshape(Ts, MLA_HEADS, HEAD_SLAB)[:, :, ROPE_LO:ROPE_LO + MLA_ROPE].reshape(Bd, rows, MLA_ROPE)
    krT_new = jnp.pad(jnp.swapaxes(kr_s.reshape(Bd, Td, MLA_ROPE), 1, 2), ((0, 0), (0, 0), (0, PAGE_SIZE - Td)))
    om_s = _sample_attn(page_table, qabs.reshape(Bd, rows, MLA_KV_LORA), qrope,
                        ckv_s.reshape(Bd, Td, MLA_KV_LORA), krT_new,
                        w['w_kT'], w['w_v'], pool_c, jnp.swapaxes(pool_r, 2, 3), layer,
                        pages_per_step=math.gcd(page_table.shape[1], SAMPLE_PAGES_PER_STEP))
    zpad = lambda a, lo: jnp.pad(a, ((0, 0), (lo, Td - lo - a.shape[1]), (0, 0))).reshape(Ts, D_FF)
    prev1 = zpad(state_conv[:, 1:2], 0)
    prev2 = zpad(state_conv, 0)
    y_s, a_s = _post(xs2, og_s, om_s, p_s.reshape(Ts, PLE_DIM), w, prev=(prev1, prev2), tm=Ts, tiles_per_seq=1,
                     seq_rows=Td)
    full = jnp.concatenate([state_conv, a_s.reshape(Bd, Td, D_FF)], axis=1)
    conv_s = full[:, full.shape[1] - (CONV_W - 1):]
    return (y_p.reshape(B, S, D_MODEL), y_s.reshape(Bd, Td, D_MODEL),
            ckv_p.reshape(B, S, MLA_KV_LORA), kr_p.reshape(B, S, MLA_ROPE),
            gla_p, conv_p,
            ckv_s.reshape(Bd, Td, MLA_KV_LORA), kr_s.reshape(Bd, Td, MLA_ROPE),
            gla_s, conv_s)


def kernel(x_prompt, x_sample, cache_ckv, cache_krope, state_gla, state_conv, page_table, p_prompt, p_sample, g_mix, w_in, gla_w_a2, gla_b_a, gla_g_out, mla_g_qa, mla_w_qup, mla_g_qn, mla_g_qr, mla_g_kva, mla_g_kr, mla_w_kvup, mla_g_kn, w_o, g_ffn, ffn_w_gate, ffn_w_up, ffn_conv_w, ffn_conv_b, ffn_w_down, g_ple, ple_w_gate, ple_w_proj):
    depth = w_in.shape[0]
    per_layer = (g_mix, w_in, gla_w_a2, gla_b_a, gla_g_out, mla_g_qa, mla_w_qup, mla_g_qn, mla_g_qr, mla_g_kva,
                 mla_g_kr, mla_w_kvup, mla_g_kn, w_o, g_ffn, ffn_w_gate, ffn_w_up, ffn_conv_w, ffn_conv_b,
                 ffn_w_down, g_ple, ple_w_gate, ple_w_proj)
    y_p, y_s = x_prompt, x_sample
    outs = [[] for _ in range(8)]
    for i in range(depth):
        w = _prep_weights(*(a[i] for a in per_layer))
        res = _layer(w, y_p, y_s, p_prompt[i], p_sample[i], cache_ckv, cache_krope, i, state_gla[i],
                     state_conv[i], page_table)
        y_p, y_s = res[0], res[1]
        for lst, r in zip(outs, res[2:]):
            lst.append(r)
    return (y_p, y_s) + tuple(jnp.stack(lst) for lst in outs)
```

```python
import functools
import math

import jax
import jax.numpy as jnp
import numpy as np
from jax import lax
from jax.experimental import pallas as pl
from jax.experimental.pallas import tpu as pltpu

D_MODEL = 1024
PAST_LEN = 16384
PAGE_SIZE = 128
GLA_HEADS = 4
GLA_DK = 64
GLA_DV = 128
GLA_GATE_RANK = 16
GLA_GATE_NORM = 16.0
GLA_CHUNK = 32
MLA_HEADS = 8
MLA_Q_LORA = 256
MLA_KV_LORA = 128
MLA_NOPE = 64
MLA_ROPE = 32
MLA_V = 64
MLA_SCALE = (MLA_NOPE + MLA_ROPE) ** -0.5
LOG2E = math.log2(math.e)
ROPE_THETA = 10000.0
D_FF = 2816
CONV_W = 3
PLE_DIM = 256
EPS = 1e-6
NEG = -1e30

LANES = 128
SUBLANES = 8
HEAD_SLAB = LANES
ROPE_LO = MLA_NOPE
ROPE_HALF = MLA_ROPE // 2
GLA_QK = GLA_HEADS * GLA_DK
GLA_V = GLA_HEADS * GLA_DV
MLA_SLABS = MLA_HEADS * HEAD_SLAB
MLA_VALL = MLA_HEADS * MLA_V
C_GQ, C_GK, C_GV, C_GG = 0, 256, 512, 1024
C_MQ, C_MKV, C_MISC = 1536, 1792, 1920

TOKEN_TILE = 512
FFN_CHUNKS = (1536, 1280)
ATTN_TILE = 512
ATTN_HEADS_PER_STEP = 8
ATTN_PIPELINE_DEPTH = 1
GLA_CHUNKS_PER_STEP = 8
GLA_PROMPT_SEQS = 4
GLA_SAMPLE_SEQS = 8
SAMPLE_PAGES_PER_STEP = 64
PAGE_RING = 3
VMEM_LIMIT = 56 * 1024 * 1024

BF16 = jnp.bfloat16
F32 = jnp.float32


def _dot(a, b):
    return jnp.dot(a, b, preferred_element_type=F32)


def _dot_nt(a, b):
    return lax.dot_general(a, b, (((1,), (1,)), ((), ())), preferred_element_type=F32)


def _dot_tn(a, b):
    return lax.dot_general(a, b, (((0,), (0,)), ((), ())), preferred_element_type=F32)


def _rms(x, g):
    return x * lax.rsqrt(jnp.mean(x * x, axis=-1, keepdims=True) + EPS) * g


def _const_spec(shape):
    nd = len(shape)
    return pl.BlockSpec(shape, lambda *_: (0,) * nd, pipeline_mode=pl.Buffered(1))


def _rope_slab(y, cos_t, sin_t):
    lane = lax.broadcasted_iota(jnp.int32, y.shape, 1)
    swapped = jnp.where(lane < ROPE_LO + ROPE_HALF,
                        pltpu.roll(y, LANES - ROPE_HALF, axis=1),
                        pltpu.roll(y, ROPE_HALF, axis=1))
    return y * cos_t + swapped * sin_t


def _in_proj_kernel(x_ref, cos_ref, sin_ref, gmix_ref, wgla_ref, wmla_ref, wa2_ref, ba_ref, gqa_ref, wqup_ref,
                    segm_ref, gq_ref, gkva_ref, gkr_ref, *rest, sample):
    if sample:
        (wkabs_ref, gq_o, gk_o, gv_o, gg_o, la_o, qa_o, ckv_o, kr_o, qabs_o) = rest
    else:
        (wk_ref, gk_ref, wvT_ref, gq_o, gk_o, gv_o, gg_o, la_o, qa_o, ckv_o, kr_o, ka_o, vaT_o) = rest
    segm = segm_ref[...]
    q_gain = gq_ref[...]

    cos_t = cos_ref[...]
    sin_t = sin_ref[...]
    n = _rms(x_ref[...], gmix_ref[...]).astype(BF16)

    def proj(lo, width):
        return _dot(n, wgla_ref[:, lo:lo + width])

    um = _dot(n, wmla_ref[...])
    ug = proj(C_GQ, 2 * GLA_QK)
    gq_o[...] = ug[:, :GLA_QK] * (GLA_DK ** -0.5)
    gk_o[...] = ug[:, GLA_QK:]
    misc = um[:, C_MISC - C_MQ:]
    z = _dot(misc.astype(BF16), wa2_ref[...]) + ba_ref[...]
    la_o[...] = (jnp.minimum(z, 0.0) - jnp.log1p(jnp.exp(-jnp.abs(z)))) * (1.0 / GLA_GATE_NORM)
    lane = lax.broadcasted_iota(jnp.int32, misc.shape, 1)
    is_rope = (lane >= ROPE_LO) & (lane < ROPE_LO + MLA_ROPE)
    ms_r = jnp.sum(jnp.where(is_rope, misc * misc, 0.0), axis=-1, keepdims=True) * (1.0 / MLA_ROPE)
    kr_slab = _rope_slab(misc * lax.rsqrt(ms_r + EPS) * gkr_ref[...], cos_t, sin_t)
    kr_o[...] = kr_slab[:, ROPE_LO:ROPE_LO + MLA_ROPE]
    ckv = _rms(um[:, C_MKV - C_MQ:C_MISC - C_MQ], gkva_ref[...])
    ckv_o[...] = ckv
    ckv_b = ckv.astype(BF16)
    cq = _rms(um[:, :MLA_Q_LORA], gqa_ref[...]).astype(BF16)

    if not sample:
        kn = _dot(ckv_b, wk_ref[...])
        vaT_o[...] = _dot_nt(wvT_ref[...], ckv_b).astype(BF16)
    qf = _dot(cq, wqup_ref[...])
    q_cos = cos_t * (q_gain[0:1] * (MLA_SCALE * LOG2E))
    q_sin = sin_t * (q_gain[1:2] * (MLA_SCALE * LOG2E))

    def head_pair(pair):
        lo = pair * 2 * HEAD_SLAB
        q2 = qf[:, lo:lo + 2 * HEAD_SLAB]
        q_inv = lax.rsqrt(_dot((q2 * q2).astype(BF16), segm) + EPS)
        if not sample:
            k2 = kn[:, lo:lo + 2 * HEAD_SLAB]
            k2 = k2 * lax.rsqrt(_dot((k2 * k2).astype(BF16), segm) + EPS) * gk_ref[:, lo:lo + 2 * HEAD_SLAB]
        for j in range(2):
            sl = slice(lo + j * HEAD_SLAB, lo + (j + 1) * HEAD_SLAB)
            swapped = qf[:, MLA_SLABS + lo + j * HEAD_SLAB:MLA_SLABS + lo + (j + 1) * HEAD_SLAB]
            qh = q_inv[:, j * HEAD_SLAB:(j + 1) * HEAD_SLAB] * (qf[:, sl] * q_cos + swapped * q_sin)
            qa_o[:, sl] = qh.astype(BF16)
            if not sample:
                ka_o[:, sl] = (k2[:, j * HEAD_SLAB:(j + 1) * HEAD_SLAB] + kr_slab).astype(BF16)

    half = GLA_V // 2
    gv_o[:, :half] = proj(C_GV, half).astype(BF16)
    head_pair(0)
    gv_o[:, half:] = proj(C_GV + half, half).astype(BF16)
    head_pair(1)
    gg_o[:, :half] = proj(C_GG, half)
    head_pair(2)
    gg_o[:, half:] = proj(C_GG + half, half)
    head_pair(3)
    if sample:
        qabs_o[...] = _dot(qa_o[...], wkabs_ref[...]).astype(BF16)


def _in_proj(x2d, cos_t, sin_t, w, *, tm, tiles_per_seq, sample):
    T = x2d.shape[0]
    nt = T // tm
    row = lambda i: (i, 0)
    pos = lambda i: (i % tiles_per_seq, 0)
    consts = [w['g_mix'], w['w_in_gla'], w['w_in_mla'], w['w_a2'], w['b_a'], w['g_qa'], w['w_qup'], w['segm'], w['gq_slab'],
              w['g_kva'], w['gkr_slab']]
    consts += [w['w_kabs']] if sample else [w['w_kslab'], w['gk_slab'], w['w_vT']]
    in_specs = [pl.BlockSpec((tm, D_MODEL), row), pl.BlockSpec((tm, LANES), pos), pl.BlockSpec((tm, LANES), pos)]
    in_specs += [_const_spec(c.shape) for c in consts]
    outs = [(GLA_QK, F32), (GLA_QK, F32), (GLA_V, BF16), (GLA_V, F32), (GLA_QK, F32),
            (MLA_SLABS, BF16), (MLA_KV_LORA, F32), (MLA_ROPE, F32), (MLA_SLABS, BF16)]
    out_specs = [pl.BlockSpec((tm, c), row) for c, _ in outs]
    out_shape = [jax.ShapeDtypeStruct((T, c), d) for c, d in outs]
    if not sample:
        out_specs.append(pl.BlockSpec((MLA_VALL, tm), lambda i: (0, i)))
        out_shape.append(jax.ShapeDtypeStruct((MLA_VALL, T), BF16))
    return pl.pallas_call(
        functools.partial(_in_proj_kernel, sample=sample),
        grid=(nt,),
        in_specs=in_specs,
        out_specs=out_specs,
        out_shape=out_shape,
        compiler_params=pltpu.CompilerParams(dimension_semantics=("parallel",), vmem_limit_bytes=VMEM_LIMIT),
        name="in_proj_sample" if sample else "in_proj_prompt",
    )(x2d, cos_t, sin_t, *consts)


def _gla_kernel(q_ref, k_ref, v_ref, gg_ref, la_ref, s0_ref, cum_ref, sel_ref, gout_ref,
                o_ref, s_out_ref, st_scr, *, chunk, n_chunks):
    step = pl.program_id(1)
    n_seq = q_ref.shape[0]
    tc = chunk * n_chunks
    heads = range(GLA_HEADS)
    dk = [slice(h * GLA_DK, (h + 1) * GLA_DK) for h in heads]
    dv = [slice(h * GLA_DV, (h + 1) * GLA_DV) for h in heads]
    rows = [slice(ci * chunk, (ci + 1) * chunk) for ci in range(n_chunks)]

    @pl.when(step == 0)
    def _():
        st_scr[...] = s0_ref[...]

    cum = cum_ref[...]
    sel = sel_ref[...]
    r = lax.broadcasted_iota(jnp.int32, (tc, tc), 0)
    c = lax.broadcasted_iota(jnp.int32, (tc, tc), 1)
    causal = (r >= c) & ((r // chunk) == (c // chunk))
    gout = gout_ref[...]

    def decays(i, t):
        la = la_ref[i]
        la_hi = la.astype(BF16)
        la_lo = (la - la_hi.astype(F32)).astype(BF16)
        t['bb'] = _dot(cum, la_hi) + _dot(cum, la_lo)
        t['dec'] = jnp.exp(_dot_tn(la_hi, sel) + _dot_tn(la_lo, sel))

    def scale(i, t):
        b, bl = t['bb'][:tc], t['bb'][tc:]
        q, k = q_ref[i], k_ref[i]
        t['q_dec'] = (q * jnp.exp(b)).astype(BF16)
        t['k_inv'] = (k * jnp.exp(-b)).astype(BF16)
        t['k_end'] = (k * jnp.exp(bl - b)).astype(BF16)

    def intra_scores(i, t):
        t['a'] = [_dot_nt(t['q_dec'][:, dk[h]], t['k_inv'][:, dk[h]]) for h in heads]

    def chunk_outer(i, t):
        v = v_ref[i]
        t['x'] = [[_dot_tn(t['k_end'][rw, dk[h]], v[rw, dv[h]]) for rw in rows] for h in heads]

    def intra_out(i, t):
        v = v_ref[i]
        t['o'] = [_dot(jnp.where(causal, t['a'][h], 0.0).astype(BF16), v[:, dv[h]]) for h in heads]

    def scan(i, t):
        t['s'] = []
        for h in heads:
            s_h = st_scr[i, h]
            starts = []
            for ci in range(n_chunks):
                starts.append(s_h.astype(BF16))
                s_h = s_h * t['dec'][dk[h], ci:ci + 1] + t['x'][h][ci]
            st_scr[i, h] = s_h
            t['s'].append(starts)

    def inter_out(i, t):
        for h in heads:
            parts = [_dot(t['q_dec'][rw, dk[h]], t['s'][h][ci]) for ci, rw in enumerate(rows)]
            t['o'][h] = t['o'][h] + (jnp.concatenate(parts, axis=0) if n_chunks > 1 else parts[0])

    def finish(i, t):
        gg = gg_ref[i]
        gate = gg / (1.0 + jnp.exp(-gg))
        for h in heads:
            o_ref[i, :, dv[h]] = (_rms(t['o'][h], gout) * gate[:, dv[h]]).astype(o_ref.dtype)

    temps = [{} for _ in range(n_seq)]
    for stage in (decays, scale, intra_scores, chunk_outer, intra_out, scan, inter_out, finish):
        for i in range(n_seq):
            stage(i, temps[i])

    @pl.when(step == pl.num_programs(1) - 1)
    def _():
        s_out_ref[...] = st_scr[...]


def _gla(q, k, v, gg, la, s0, g_out, *, n_seq, seq_len, chunk, n_chunks, seqs_per_step):
    tc = chunk * n_chunks
    steps = seq_len // tc
    nb = seqs_per_step
    idx = np.arange(tc)
    same = (idx[:, None] // chunk) == (idx[None, :] // chunk)
    cum = jnp.asarray(np.concatenate([same & (idx[:, None] >= idx[None, :]), same], axis=0), BF16)
    sel = jnp.asarray((idx[:, None] // chunk) == np.arange(LANES)[None, :], BF16)
    tok = lambda width: pl.BlockSpec((nb, tc, width), lambda b, s: (b, s, 0))
    per_seq = lambda a: a.reshape(n_seq, seq_len, a.shape[-1])
    state_spec = pl.BlockSpec((nb, GLA_HEADS, GLA_DK, GLA_DV), lambda b, s: (b, 0, 0, 0))
    o, state = pl.pallas_call(
        functools.partial(_gla_kernel, chunk=chunk, n_chunks=n_chunks),
        grid=(n_seq // nb, steps),
        in_specs=[tok(GLA_QK), tok(GLA_QK), tok(GLA_V), tok(GLA_V), tok(GLA_QK), state_spec,
                  _const_spec((2 * tc, tc)), _const_spec((tc, LANES)), _const_spec((1, GLA_DV))],
        out_specs=[tok(GLA_V), state_spec],
        out_shape=[jax.ShapeDtypeStruct((n_seq, seq_len, GLA_V), BF16),
                   jax.ShapeDtypeStruct((n_seq, GLA_HEADS, GLA_DK, GLA_DV), F32)],
        scratch_shapes=[pltpu.VMEM((nb, GLA_HEADS, GLA_DK, GLA_DV), F32)],
        compiler_params=pltpu.CompilerParams(dimension_semantics=("parallel", "arbitrary"),
                                             vmem_limit_bytes=VMEM_LIMIT),
        name=f"gla_c{chunk}",
    )(per_seq(q), per_seq(k), per_seq(v), per_seq(gg), per_seq(la), s0, cum, sel, g_out)
    return o.reshape(n_seq * seq_len, GLA_V), state


def _prompt_attn_kernel(q_ref, k_ref, vT_ref, o_ref, s_scr, *, tq, tk):
    for qi in range(q_ref.shape[0] // tq):
        _attn_tile(q_ref, k_ref, vT_ref, o_ref, s_scr, qi, tq=tq, tk=tk)


def _attn_tile(q_ref, k_ref, vT_ref, o_ref, s_scr, qi, *, tq, tk):
    assert tq == tk
    key = lax.broadcasted_iota(jnp.int32, (tk, tq), 0)
    qry = lax.broadcasted_iota(jnp.int32, (tk, tq), 1)
    ones_rows = jnp.ones((2 * SUBLANES, tk), BF16)

    def scores(j, kb, masked):
        ks = kb * tk if isinstance(kb, int) else pl.multiple_of(kb * tk, tk)
        s = _dot_nt(k_ref[pl.ds(ks, tk), j * HEAD_SLAB:(j + 1) * HEAD_SLAB],
                    q_ref[qi * tq:(qi + 1) * tq, j * HEAD_SLAB:(j + 1) * HEAD_SLAB])
        if masked:
            s = jnp.where(key <= qry, s, NEG)
        s_scr[j] = s
        return jnp.max(s, axis=0, keepdims=True)

    def update(j, kb, m_blk, state):
        m, acc = state
        ks = kb * tk if isinstance(kb, int) else pl.multiple_of(kb * tk, tk)
        m_new = jnp.maximum(m, m_blk)
        alpha = jnp.exp2(m - m_new)
        p = jnp.exp2(s_scr[j] - m_new)
        v_aug = jnp.concatenate([vT_ref[j * MLA_V:(j + 1) * MLA_V, pl.ds(ks, tk)], ones_rows], axis=0)
        acc = alpha * acc + _dot(v_aug, p.astype(BF16))
        return m_new, acc

    n_heads = q_ref.shape[1] // HEAD_SLAB
    depth = min(ATTN_PIPELINE_DEPTH, n_heads - 1)

    def block(kb, masked, pending, state):
        state = list(state)
        queue = [(n_heads - len(pending) + i, kb + 1, m) for i, m in enumerate(pending)]
        for j in range(n_heads):
            queue.append((j, kb, scores(j, kb, masked)))
            if len(queue) > depth:
                pj, pkb, pm = queue.pop(0)
                state[pj] = update(pj, pkb, pm, state[pj])
        return tuple(m for _, _, m in queue), tuple(state)

    init = (jnp.full((1, tq), NEG, F32), jnp.zeros((MLA_V + 2 * SUBLANES, tq), F32))
    pending, state = block(qi, True, (), (init,) * n_heads)

    def body(t, carry):
        return block(qi - 1 - t, False, *carry)

    pending, state = lax.fori_loop(0, qi, body, (pending, state))
    state = list(state)
    for i, m_blk in enumerate(pending):
        head = n_heads - depth + i
        state[head] = update(head, 0, m_blk, state[head])
    o_ref[qi * tq:(qi + 1) * tq, :] = jnp.concatenate([acc[:MLA_V] / acc[MLA_V:MLA_V + 1] for _, acc in state],
                                 axis=0).T.astype(o_ref.dtype)


def _prompt_attn(qa, ka, va, *, n_seq, seq_len, tq, tk):
    hs = ATTN_HEADS_PER_STEP
    return pl.pallas_call(
        functools.partial(_prompt_attn_kernel, tq=tq, tk=tk),
        grid=(n_seq, MLA_HEADS // hs),
        in_specs=[pl.BlockSpec((seq_len, hs * HEAD_SLAB), lambda b, hp: (b, hp)),
                  pl.BlockSpec((seq_len, hs * HEAD_SLAB), lambda b, hp: (b, hp)),
                  pl.BlockSpec((hs * MLA_V, seq_len), lambda b, hp: (hp, b))],
        out_specs=pl.BlockSpec((seq_len, hs * MLA_V), lambda b, hp: (b, hp)),
        out_shape=jax.ShapeDtypeStruct((n_seq * seq_len, MLA_VALL), BF16),
        scratch_shapes=[pltpu.VMEM((hs, tk, tq), F32)],
        compiler_params=pltpu.CompilerParams(dimension_semantics=("parallel", "parallel"),
                                             vmem_limit_bytes=VMEM_LIMIT),
        name="prompt_attn",
    )(qa, ka, va)


def _sample_attn_kernel(pt_ref, qabs_ref, qrope_ref, cnew_ref, krnew_ref, wkT_ref, wv_ref, poolc_hbm, poolr_hbm,
                        o_ref, lhs_scr, m_scr, l_scr, acc_scr, cb_scr, krT_scr, s_scr, cbuf, rbuf, sem,
                        *, n_tok, pages_per_step, n_groups, layer):
    G = pages_per_step
    b = pl.program_id(0)
    n_all = pl.num_programs(0) * n_groups

    def group_of(s):
        s = jnp.minimum(s, n_all - 1)
        return s // n_groups, s % n_groups
    rows = n_tok * MLA_HEADS
    n_k = MLA_HEADS * MLA_NOPE

    def page_copies(sl, group):
        copies = []
        for j in range(G):
            page = pt_ref[group[0], group[1] * G + j]
            copies.append(pltpu.make_async_copy(poolc_hbm.at[layer, page], cbuf.at[sl, j], sem.at[0, sl]))
            copies.append(pltpu.make_async_copy(poolr_hbm.at[layer, page], rbuf.at[sl, j], sem.at[1, sl]))
        return copies

    def wait_slot(sl):
        pltpu.make_async_copy(cbuf.at[sl], cbuf.at[sl], sem.at[0, sl]).wait()
        pltpu.make_async_copy(rbuf.at[sl], rbuf.at[sl], sem.at[1, sl]).wait()

    @pl.when(b == 0)
    def _():
        for d in range(PAGE_RING - 1):
            for cp in page_copies(d, group_of(d)):
                cp.start()

    def scores(cb, krT):
        nk = cb.shape[0]
        big = _dot_nt(lhs_scr[...], cb)
        kvn = big[:n_k]
        s_nope = big[n_k:n_k + rows]
        ss = jnp.sum((kvn * kvn).reshape(MLA_HEADS, MLA_NOPE, nk), axis=1)
        inv = lax.rsqrt(ss * (1.0 / MLA_NOPE) + EPS)
        s_rope = _dot(qrope_ref[0], krT)
        return (s_nope.reshape(n_tok, MLA_HEADS, nk) * inv[None]).reshape(rows, nk) + s_rope

    def weights(s):
        m = m_scr[...]
        m_new = jnp.maximum(m, jnp.max(s, axis=-1, keepdims=True))
        alpha = jnp.exp2(m - m_new)
        p = jnp.exp2(s - m_new)
        l_scr[...] = alpha * l_scr[...] + jnp.sum(p, axis=-1, keepdims=True)
        m_scr[...] = m_new
        return alpha, p.astype(BF16)

    def accumulate(alpha, p, cb):
        acc_scr[...] = alpha * acc_scr[...] + _dot(p, cb)

    def absorb(s, cb):
        accumulate(*weights(s), cb)

    lhs_scr[:n_k, :] = wkT_ref[...]
    lhs_scr[n_k:n_k + rows, :] = qabs_ref[0]
    m_scr[...] = jnp.full(m_scr.shape, NEG, F32)
    l_scr[...] = jnp.zeros(l_scr.shape, F32)
    acc_scr[...] = jnp.zeros(acc_scr.shape, F32)
    c_new = jnp.concatenate([cnew_ref[0], jnp.zeros((PAGE_SIZE - n_tok, MLA_KV_LORA), F32)], axis=0).astype(BF16)
    key = lax.broadcasted_iota(jnp.int32, (rows, PAGE_SIZE), 1)
    tok = lax.broadcasted_iota(jnp.int32, (rows, PAGE_SIZE), 0) // MLA_HEADS
    absorb(jnp.where(key <= tok, scores(c_new, krnew_ref[0].astype(BF16)), NEG), c_new)

    for gi in range(n_groups):
        group = b * n_groups + gi
        slot = group % PAGE_RING
        cur, prv = gi % 2, 1 - gi % 2
        wait_slot(slot)
        for j in range(G):
            cb_scr[cur, j * PAGE_SIZE:(j + 1) * PAGE_SIZE, :] = cbuf[slot, j].astype(BF16)
            krT_scr[:, j * PAGE_SIZE:(j + 1) * PAGE_SIZE] = rbuf[slot, j].astype(BF16)
        ahead = group + (PAGE_RING - 1)
        for cp in page_copies(ahead % PAGE_RING, group_of(ahead)):
            cp.start()
        if gi == 0:
            s_scr[cur] = scores(cb_scr[cur], krT_scr[...])
        else:
            alpha, p = weights(s_scr[prv])
            s_scr[cur] = scores(cb_scr[cur], krT_scr[...])
            accumulate(alpha, p, cb_scr[prv])
    absorb(s_scr[cur], cb_scr[cur])

    o_lat = (acc_scr[...] / l_scr[...]).astype(BF16)
    full = _dot(o_lat, wv_ref[...])
    full = full.reshape(n_tok, MLA_HEADS, MLA_VALL)
    hr = lax.broadcasted_iota(jnp.int32, (MLA_HEADS, MLA_VALL), 0)
    hc = lax.broadcasted_iota(jnp.int32, (MLA_HEADS, MLA_VALL), 1) // MLA_V
    o_ref[...] = jnp.sum(jnp.where((hr == hc)[None], full, 0.0), axis=1).astype(o_ref.dtype)

    @pl.when(b == pl.num_programs(0) - 1)
    def _():
        for d in range(1, PAGE_RING):
            wait_slot((n_all - 1 + d) % PAGE_RING)


def _sample_attn(page_table, qabs, qrope, c_new, kr_new, w_kT, w_v, pool_c, pool_r, layer, *, pages_per_step):
    n_b, n_pages = page_table.shape
    n_tok = c_new.shape[1]
    rows = n_tok * MLA_HEADS
    G = pages_per_step
    n_k = MLA_HEADS * MLA_NOPE

    per_b = lambda b, pt: (b, 0, 0)
    grid_spec = pltpu.PrefetchScalarGridSpec(
        num_scalar_prefetch=1,
        grid=(n_b,),
        in_specs=[pl.BlockSpec((1, rows, MLA_KV_LORA), per_b), pl.BlockSpec((1, rows, MLA_ROPE), per_b),
                  pl.BlockSpec((1, n_tok, MLA_KV_LORA), per_b), pl.BlockSpec((1, MLA_ROPE, PAGE_SIZE), per_b),
                  pl.BlockSpec((n_k, MLA_KV_LORA), lambda b, pt: (0, 0)),
                  pl.BlockSpec((MLA_KV_LORA, MLA_VALL), lambda b, pt: (0, 0)),
                  pl.BlockSpec(memory_space=pl.ANY), pl.BlockSpec(memory_space=pl.ANY)],
        out_specs=pl.BlockSpec((n_tok, MLA_VALL), lambda b, pt: (b, 0)),
        scratch_shapes=[pltpu.VMEM((n_k + rows, MLA_KV_LORA), BF16), pltpu.VMEM((rows, 1), F32),
                        pltpu.VMEM((rows, 1), F32), pltpu.VMEM((rows, MLA_KV_LORA), F32),
                        pltpu.VMEM((2, G * PAGE_SIZE, MLA_KV_LORA), BF16), pltpu.VMEM((MLA_ROPE, G * PAGE_SIZE), BF16),
                        pltpu.VMEM((2, rows, G * PAGE_SIZE), F32),
                        pltpu.VMEM((PAGE_RING, G, PAGE_SIZE, MLA_KV_LORA), F32),
                        pltpu.VMEM((PAGE_RING, G, MLA_ROPE, PAGE_SIZE), F32),
                        pltpu.SemaphoreType.DMA((2, PAGE_RING))])
    return pl.pallas_call(
        functools.partial(_sample_attn_kernel, n_tok=n_tok, pages_per_step=G, n_groups=n_pages // G, layer=layer),
        grid_spec=grid_spec,
        out_shape=jax.ShapeDtypeStruct((n_b * n_tok, MLA_VALL), BF16),
        compiler_params=pltpu.CompilerParams(dimension_semantics=("arbitrary",),
                                             vmem_limit_bytes=VMEM_LIMIT),
        name="sample_attn",
    )(page_table, qabs, qrope, c_new, kr_new, w_kT, w_v, pool_c, pool_r)


def _post_kernel(x_ref, og_ref, om_ref, p_ref, wo_ref, gffn_ref, wg_ref, wu_ref, cw_ref, cb_ref, wd_ref,
                 gple_ref, wpg_ref, wpp_ref, *rest, tiles_per_seq, seq_rows):
    paged_prev = seq_rows is not None
    if paged_prev:
        p1_ref, p2_ref, y_ref, tail_ref, acc_scr = rest
    else:
        y_ref, tail_ref, acc_scr, carry_scr = rest
    tm = x_ref.shape[0]
    i = pl.program_id(0)
    h1 = x_ref[...] + _dot(og_ref[...], wo_ref[:GLA_V, :]) + _dot(om_ref[...], wo_ref[GLA_V:, :])
    n2 = _rms(h1, gffn_ref[...]).astype(BF16)
    if not paged_prev:
        @pl.when(i % tiles_per_seq == 0)
        def _():
            carry_scr[...] = jnp.zeros(carry_scr.shape, F32)

    bounds = np.concatenate([[0], np.cumsum(FFN_CHUNKS)]).tolist()
    chunks = [slice(lo, hi) for lo, hi in zip(bounds[:-1], bounds[1:])]
    lo = bounds[-1]

    def gate_up(cols):
        return _dot(n2, wg_ref[:, cols]), _dot(n2, wu_ref[:, cols])

    nxt = gate_up(chunks[0])
    for ci, cols in enumerate(chunks):
        a, up = nxt
        if ci + 1 < len(chunks):
            nxt = gate_up(chunks[ci + 1])
        row =lax.broadcasted_iota(jnp.int32, a.shape, 0)
        slabs = [a[:, c0:c0 + 2 * LANES] for c0 in range(0, a.shape[1], 2 * LANES)]
        r1 = jnp.concatenate([pltpu.roll(sl, 1, axis=0) for sl in slabs], axis=1)
        r2 = jnp.concatenate([pltpu.roll(sl, 2, axis=0) for sl in slabs], axis=1)
        if paged_prev:
            t = row % seq_rows
            a1 = jnp.where(t >= 1, r1, p1_ref[:, cols])
            a2 = jnp.where(t >= 2, r2, p2_ref[:, cols])
            tail_ref[:, cols] = a
        else:
            prev = carry_scr[:, cols]
            pm1 = prev[SUBLANES - 1:SUBLANES, :]
            pm2 = prev[SUBLANES - 2:SUBLANES - 1, :]
            a1 = jnp.where(row >= 1, r1, pm1)
            a2 = jnp.where(row >= 2, r2, jnp.where(row == 0, pm2, pm1))
            carry_scr[:, cols] = a[tm - SUBLANES:, :]
            tail_ref[:, cols] = a[tm - SUBLANES:, :]
        conv = cb_ref[:, cols] + cw_ref[0:1, cols] * a2 + cw_ref[1:2, cols] * a1 + cw_ref[2:3, cols] * a
        gact = (conv / (1.0 + jnp.exp(-conv)) * up).astype(BF16)
        down = _dot(gact, wd_ref[cols, :])
        if ci == 0:
            acc_scr[...] = down
        else:
            acc_scr[...] += down
    assert lo == D_FF
    h2 = h1 + acc_scr[...]
    n3 =_rms(h2, gple_ref[...]).astype(BF16)
    gate = 1.0 / (1.0 + jnp.exp(-_dot(n3, wpg_ref[...])))
    y_ref[...] = h2 + _dot(p_ref[...].astype(BF16), wpp_ref[...]) * gate


def _post(x2d, og, om, p2d, w, prev=None, *, tm, tiles_per_seq, seq_rows):
    T = x2d.shape[0]
    nt = T // tm
    row = lambda i: (i, 0)
    consts = [w['w_o'], w['g_ffn'], w['w_gate'], w['w_up'], w['conv_w'], w['conv_b'], w['w_down'],
              w['g_ple'], w['w_pgate'], w['w_pproj']]
    in_specs = [pl.BlockSpec((tm, D_MODEL), row), pl.BlockSpec((tm, GLA_V), row),
                pl.BlockSpec((tm, MLA_VALL), row), pl.BlockSpec((tm, PLE_DIM), row)]
    in_specs += [_const_spec(c.shape) for c in consts]
    args = [x2d, og, om, p2d, *consts]
    scratch = [pltpu.VMEM((tm, D_MODEL), F32)]
    if prev is not None:
        in_specs += [pl.BlockSpec((tm, D_FF), row)] * 2
        args += list(prev)
        tail_rows = tm
    else:
        scratch.append(pltpu.VMEM((SUBLANES, D_FF), F32))
        tail_rows = SUBLANES
    return pl.pallas_call(
        functools.partial(_post_kernel, tiles_per_seq=tiles_per_seq, seq_rows=seq_rows),
        grid=(nt,),
        in_specs=in_specs,
        out_specs=[pl.BlockSpec((tm, D_MODEL), row), pl.BlockSpec((tail_rows, D_FF), row)],
        out_shape=[jax.ShapeDtypeStruct((T, D_MODEL), F32), jax.ShapeDtypeStruct((nt * tail_rows, D_FF), F32)],
        scratch_shapes=scratch,
        compiler_params=pltpu.CompilerParams(dimension_semantics=("arbitrary",), vmem_limit_bytes=VMEM_LIMIT),
        name="post_sample" if prev is not None else "post_prompt",
    )(*args)


def _prep_weights(g_mix, w_in, gla_w_a2, gla_b_a, gla_g_out, mla_g_qa, mla_w_qup, mla_g_qn, mla_g_qr,
                  mla_g_kva, mla_g_kr, mla_w_kvup, mla_g_kn, w_o, g_ffn, ffn_w_gate, ffn_w_up, ffn_conv_w,
                  ffn_conv_b, ffn_w_down, g_ple, ple_w_gate, ple_w_proj):
    sizes = (GLA_QK, GLA_QK, GLA_V, GLA_V, GLA_GATE_RANK, MLA_Q_LORA, MLA_KV_LORA, MLA_ROPE)
    offs = np.concatenate([[0], np.cumsum(sizes)])
    piece = lambda i: w_in[:, offs[i]:offs[i + 1]]
    zeros = lambda n: jnp.zeros((D_MODEL, n), w_in.dtype)
    misc = jnp.concatenate([piece(4), zeros(ROPE_LO - GLA_GATE_RANK), piece(7),
                            zeros(LANES - ROPE_LO - MLA_ROPE)], axis=1)
    w_in_gla = w_in[:, :C_MQ]
    w_in_mla = jnp.concatenate([piece(5), piece(6), misc], axis=1)
    w_a2 = jnp.concatenate([gla_w_a2, jnp.zeros((LANES - GLA_GATE_RANK, GLA_QK), gla_w_a2.dtype)], axis=0)

    def slab_vec(nope, rope_):
        one = jnp.concatenate([nope, rope_, jnp.zeros((HEAD_SLAB - MLA_NOPE - MLA_ROPE,), F32)])
        return jnp.tile(one, MLA_HEADS)[None, :]

    wq = mla_w_qup.reshape(MLA_Q_LORA, MLA_HEADS, MLA_NOPE + MLA_ROPE)
    swap = np.arange(MLA_NOPE + MLA_ROPE)
    swap[MLA_NOPE:] = np.concatenate([swap[MLA_NOPE + ROPE_HALF:], swap[MLA_NOPE:MLA_NOPE + ROPE_HALF]])
    slab_pad = ((0, 0), (0, 0), (0, HEAD_SLAB - MLA_NOPE - MLA_ROPE))
    wq = jnp.concatenate([jnp.pad(wq, slab_pad).reshape(MLA_Q_LORA, MLA_SLABS),
                          jnp.pad(wq[:, :, swap], slab_pad).reshape(MLA_Q_LORA, MLA_SLABS)], axis=1)
    g_q = jnp.concatenate([mla_g_qn, mla_g_qr])
    gq_slab = jnp.pad(jnp.stack([g_q, g_q[swap]]), ((0, 0), (0, HEAD_SLAB - MLA_NOPE - MLA_ROPE)))
    wkv = mla_w_kvup.reshape(MLA_KV_LORA, MLA_HEADS, MLA_NOPE + MLA_V)
    wk = wkv[:, :, :MLA_NOPE]
    wv = wkv[:, :, MLA_NOPE:].reshape(MLA_KV_LORA, MLA_VALL)
    wk_slab = jnp.pad(wk, ((0, 0), (0, 0), (0, HEAD_SLAB - MLA_NOPE))).reshape(MLA_KV_LORA, MLA_SLABS)
    wk_g = jnp.pad(wk * mla_g_kn[None, None, :], ((0, 0), (0, 0), (0, HEAD_SLAB - MLA_NOPE)))
    eye = jnp.eye(MLA_HEADS, dtype=F32)
    w_kabs = jnp.einsum('nhd,hg->hdgn', wk_g, eye).reshape(MLA_SLABS, MLA_HEADS * MLA_KV_LORA)
    w_kT = wk.transpose(1, 2, 0).reshape(MLA_HEADS * MLA_NOPE, MLA_KV_LORA)

    lane = np.arange(2 * HEAD_SLAB)
    seg = np.where(lane % HEAD_SLAB < MLA_NOPE, 0, np.where(lane % HEAD_SLAB < MLA_NOPE + MLA_ROPE, 1, 2))
    same = (lane[:, None] // HEAD_SLAB == lane[None, :] // HEAD_SLAB) & (seg[:, None] == seg[None, :])
    segm = np.where(same & (seg[:, None] == 0), 1.0 / MLA_NOPE, np.where(same & (seg[:, None] == 1), 1.0 / MLA_ROPE, 0.0))

    gkr = jnp.concatenate([jnp.zeros((ROPE_LO,), F32), mla_g_kr, jnp.zeros((LANES - ROPE_LO - MLA_ROPE,), F32)])
    return dict(
        g_mix=g_mix[None, :], w_in_gla=w_in_gla.astype(BF16), w_in_mla=w_in_mla.astype(BF16), w_a2=w_a2.astype(BF16), b_a=gla_b_a[None, :],
        g_out=gla_g_out[None, :], g_qa=mla_g_qa[None, :], w_qup=wq.astype(BF16), segm=jnp.asarray(segm, BF16),
        gq_slab=gq_slab, g_kva=mla_g_kva[None, :], w_kslab=wk_slab.astype(BF16),
        w_vT=wv.T.astype(BF16),
        gk_slab=slab_vec(mla_g_kn, jnp.zeros((MLA_ROPE,), F32)),
        gkr_slab=gkr[None, :], w_kabs=w_kabs.astype(BF16), w_kT=w_kT.astype(BF16),
        w_v=wv.astype(BF16), w_o=w_o.astype(BF16), g_ffn=g_ffn[None, :], w_gate=ffn_w_gate.astype(BF16),
        w_up=ffn_w_up.astype(BF16), conv_w=ffn_conv_w, conv_b=ffn_conv_b[None, :],
        w_down=ffn_w_down.astype(BF16), g_ple=g_ple[None, :], w_pgate=ple_w_gate.astype(BF16),
        w_pproj=ple_w_proj.astype(BF16))


def _rope_tables(pos):
    inv = ROPE_THETA ** (-jnp.arange(ROPE_HALF, dtype=F32) * 2.0 / MLA_ROPE)
    ang = pos.astype(F32)[:, None] * inv[None, :]
    cos, sin = jnp.cos(ang), jnp.sin(ang)
    T = pos.shape[0]
    pad = jnp.zeros((T, HEAD_SLAB - MLA_NOPE - MLA_ROPE), F32)
    cos_t = jnp.concatenate([jnp.ones((T, MLA_NOPE), F32), cos, cos, pad], axis=1)
    sin_t = jnp.concatenate([jnp.zeros((T, MLA_NOPE), F32), -sin, sin, pad], axis=1)
    return cos_t, sin_t


def _layer(w, x_p, x_s, p_p, p_s, pool_c, pool_r, layer, state_gla, state_conv, page_table):
    B, S, _ = x_p.shape
    Bd, Td, _ = x_s.shape
    tm = math.gcd(S, TOKEN_TILE)
    tps = S // tm
    cos_p, sin_p = _rope_tables(jnp.arange(S))
    xp2 = x_p.reshape(B * S, D_MODEL)
    gq, gk, gv, gg, la, qa, ckv_p, kr_p, ka, vaT = _in_proj(xp2, cos_p, sin_p, w, tm=tm, tiles_per_seq=tps,
                                                            sample=False)
    chunk_p = math.gcd(S, GLA_CHUNK)
    n_chunks = math.gcd(S // chunk_p, GLA_CHUNKS_PER_STEP)
    og_p, gla_p = _gla(gq, gk, gv, gg, la, jnp.zeros((B, GLA_HEADS, GLA_DK, GLA_DV), F32), w['g_out'],
                      n_seq=B, seq_len=S, chunk=chunk_p, n_chunks=n_chunks,
                      seqs_per_step=math.gcd(B, GLA_PROMPT_SEQS))
    tq = math.gcd(S, ATTN_TILE)
    om_p = _prompt_attn(qa, ka, vaT, n_seq=B, seq_len=S, tq=tq, tk=tq)
    y_p, tail_p = _post(xp2, og_p, om_p, p_p.reshape(B * S, PLE_DIM), w, tm=tm, tiles_per_seq=tps, seq_rows=None)
    conv_p = tail_p.reshape(B, tps, SUBLANES, D_FF)[:, -1, SUBLANES - (CONV_W - 1):, :]

    Ts = Bd * Td
    pos_s = PAST_LEN + jnp.arange(Td)
    cos_s, sin_s = _rope_tables(jnp.tile(pos_s, Bd))
    xs2 = x_s.reshape(Ts, D_MODEL)
    gq, gk, gv, gg, la, qa, ckv_s, kr_s, qabs = _in_proj(xs2, cos_s, sin_s, w, tm=Ts, tiles_per_seq=1, sample=True)
    chunk_s = math.gcd(Td, GLA_CHUNK)
    og_s, gla_s = _gla(gq, gk, gv, gg, la, state_gla, w['g_out'],
                      n_seq=Bd, seq_len=Td, chunk=chunk_s, n_chunks=Td // chunk_s,
                      seqs_per_step=math.gcd(Bd, GLA_SAMPLE_SEQS))
    rows = Td * MLA_HEADS
    qrope = qa.reshape(Ts, MLA_HEADS, HEAD_SLAB)[:, :, ROPE_LO:ROPE_LO + MLA_ROPE].reshape(Bd, rows, MLA_ROPE)
    krT_new = jnp.pad(jnp.swapaxes(kr_s.reshape(Bd, Td, MLA_ROPE), 1, 2), ((0, 0), (0, 0), (0, PAGE_SIZE - Td)))
    om_s = _sample_attn(page_table, qabs.reshape(Bd, rows, MLA_KV_LORA), qrope,
                        ckv_s.reshape(Bd, Td, MLA_KV_LORA), krT_new,
                        w['w_kT'], w['w_v'], pool_c, jnp.swapaxes(pool_r, 2, 3), layer,
                        pages_per_step=math.gcd(page_table.shape[1], SAMPLE_PAGES_PER_STEP))
    zpad = lambda a, lo: jnp.pad(a, ((0, 0), (lo, Td - lo - a.shape[1]), (0, 0))).reshape(Ts, D_FF)
    prev1 = zpad(state_conv[:, 1:2], 0)
    prev2 = zpad(state_conv, 0)
    y_s, a_s = _post(xs2, og_s, om_s, p_s.reshape(Ts, PLE_DIM), w, prev=(prev1, prev2), tm=Ts, tiles_per_seq=1,
                     seq_rows=Td)
    full = jnp.concatenate([state_conv, a_s.reshape(Bd, Td, D_FF)], axis=1)
    conv_s = full[:, full.shape[1] - (CONV_W - 1):]
    return (y_p.reshape(B, S, D_MODEL), y_s.reshape(Bd, Td, D_MODEL),
            ckv_p.reshape(B, S, MLA_KV_LORA), kr_p.reshape(B, S, MLA_ROPE),
            gla_p, conv_p,
            ckv_s.reshape(Bd, Td, MLA_KV_LORA), kr_s.reshape(Bd, Td, MLA_ROPE),
            gla_s, conv_s)


def kernel(x_prompt, x_sample, cache_ckv, cache_krope, state_gla, state_conv, page_table, p_prompt, p_sample, g_mix, w_in, gla_w_a2, gla_b_a, gla_g_out, mla_g_qa, mla_w_qup, mla_g_qn, mla_g_qr, mla_g_kva, mla_g_kr, mla_w_kvup, mla_g_kn, w_o, g_ffn, ffn_w_gate, ffn_w_up, ffn_conv_w, ffn_conv_b, ffn_w_down, g_ple, ple_w_gate, ple_w_proj):
    depth = w_in.shape[0]
    per_layer = (g_mix, w_in, gla_w_a2, gla_b_a, gla_g_out, mla_g_qa, mla_w_qup, mla_g_qn, mla_g_qr, mla_g_kva,
                 mla_g_kr, mla_w_kvup, mla_g_kn, w_o, g_ffn, ffn_w_gate, ffn_w_up, ffn_conv_w, ffn_conv_b,
                 ffn_w_down, g_ple, ple_w_gate, ple_w_proj)
    y_p, y_s = x_prompt, x_sample
    outs = [[] for _ in range(8)]
    for i in range(depth):
        w = _prep_weights(*(a[i] for a in per_layer))
        res = _layer(w, y_p, y_s, p_prompt[i], p_sample[i], cache_ckv, cache_krope, i, state_gla[i],
                     state_conv[i], page_table)
        y_p, y_s = res[0], res[1]
        for lst, r in zip(outs, res[2:]):
            lst.append(r)
    return (y_p, y_s) + tuple(jnp.stack(lst) for lst in outs)
```
